```python
import jax, jax.numpy as jnp
from jax import lax
import numpy as np

D_MODEL = 2048
BATCH = 8
SEQ = 4096
DEPTH = 4

N_MIXERS = 3
EPS = 1e-6
SB_HEADS = 16
SB_HEAD_DIM = D_MODEL // SB_HEADS
SB_WIDTH = SB_HEADS * SB_HEAD_DIM
Q_BLOCK = 128
GM_CHUNK = 128
GM_GROUPS = 16
GM_WIDTH = D_MODEL
GM_GROUP_DIM = GM_WIDTH // GM_GROUPS
SSD_INNER = 2 * D_MODEL
SSD_HEAD_DIM = 64
SSD_HEADS = SSD_INNER // SSD_HEAD_DIM
SSD_GROUPS = 8
SSD_HPG = SSD_HEADS // SSD_GROUPS
SSD_STATE = 128
SSD_CONV = 4
SSD_CHUNK = 128
SSD_CONV_DIM = SSD_INNER + 2 * SSD_GROUPS * SSD_STATE
SSD_PROJ = SSD_INNER + SSD_CONV_DIM + SSD_HEADS
MLP_HIDDEN = 4 * D_MODEL
N_A = len(range(0, DEPTH, N_MIXERS))
N_B = len(range(1, DEPTH, N_MIXERS))
N_C = len(range(2, DEPTH, N_MIXERS))

kernel_name = "hybrid_sb_gmlp_ssd_trunk"


def rms_norm(x, g):
    xf = x.astype(jnp.float32)
    y = xf * lax.rsqrt(jnp.mean(xf * xf, axis=-1, keepdims=True) + EPS)
    return (y * g.astype(jnp.float32)).astype(x.dtype)


def stick_breaking_attention(h, w_qkv, q_g, k_g, w_o):
    b, s, _ = h.shape
    q, k, v = jnp.split(h @ w_qkv, 3, axis=-1)
    q = rms_norm(q.reshape(b, s, SB_HEADS, SB_HEAD_DIM), q_g).transpose(0, 2, 1, 3)
    k = rms_norm(k.reshape(b, s, SB_HEADS, SB_HEAD_DIM), k_g).transpose(0, 2, 1, 3)
    v = v.reshape(b, s, SB_HEADS, SB_HEAD_DIM).transpose(0, 2, 1, 3)
    scale = SB_HEAD_DIM ** -0.5
    n_blk = s // Q_BLOCK
    q_blocks = q.reshape(b, SB_HEADS, n_blk, Q_BLOCK, SB_HEAD_DIM).transpose(2, 0, 1, 3, 4)
    k_pos = jnp.arange(s)

    def one_block(args):
        qb, start = args
        z = jnp.einsum('bhqd,bhkd->bhqk', qb, k).astype(jnp.float32) * scale
        q_pos = start + jnp.arange(Q_BLOCK)
        mask = k_pos[None, :] < q_pos[:, None]
        log_beta = jax.nn.log_sigmoid(z)
        log_1m = jnp.where(mask, jax.nn.log_sigmoid(-z), 0.0)
        suffix = lax.cumsum(log_1m, axis=3, reverse=True) - log_1m
        a = jnp.where(mask, jnp.exp(log_beta + suffix), 0.0)
        return jnp.einsum('bhqk,bhkd->bhqd', a.astype(v.dtype), v)

    starts = jnp.arange(n_blk, dtype=jnp.int32) * Q_BLOCK
    o = lax.map(one_block, (q_blocks, starts))
    o = o.transpose(1, 0, 3, 2, 4).reshape(b, s, SB_WIDTH)
    return o @ w_o


def chunked_spatial_gating(h, w_in, v_g, w_s, b_s, w_o):
    b, s, _ = h.shape
    u, v = jnp.split(jax.nn.gelu(h @ w_in, approximate=False), 2, axis=-1)
    v = rms_norm(v, v_g)
    n_chunk = s // GM_CHUNK
    v = v.reshape(b, n_chunk, GM_CHUNK, GM_GROUPS, GM_GROUP_DIM)
    u = u.reshape(b, n_chunk, GM_CHUNK, GM_GROUPS, GM_GROUP_DIM)
    causal = jnp.tril(jnp.ones((GM_CHUNK, GM_CHUNK), dtype=bool))
    w = jnp.where(causal[None], w_s, 0.0)
    mixed = jnp.einsum('gts,bcsgd->bctgd', w.astype(v.dtype), v) + b_s.T[:, :, None]
    y = (u * mixed).reshape(b, s, GM_WIDTH)
    return y @ w_o


def ssd_chunked_scan(x, dt, a, bm, cm):
    b, s = x.shape[:2]
    c, L = s // SSD_CHUNK, SSD_CHUNK
    xs = (x * dt[..., None]).reshape(b, c, L, SSD_GROUPS, SSD_HPG, SSD_HEAD_DIM)
    da = (dt * a).reshape(b, c, L, SSD_GROUPS, SSD_HPG).transpose(0, 3, 4, 1, 2)
    bc = bm.reshape(b, c, L, SSD_GROUPS, SSD_STATE)
    cc = cm.reshape(b, c, L, SSD_GROUPS, SSD_STATE)
    a_cum = jnp.cumsum(da, axis=-1)
    seg = a_cum[..., :, None] - a_cum[..., None, :]
    tri = jnp.tril(jnp.ones((L, L), dtype=bool))
    decay = jnp.exp(jnp.where(tri, seg, -jnp.inf))
    cb = jnp.einsum('bclgn,bcsgn->bgcls', cc, bc)
    y_diag = jnp.einsum('bgcls,bgrcls,bcsgrp->bclgrp', cb, decay, xs)
    decay_states = jnp.exp(a_cum[..., -1:] - a_cum)
    states = jnp.einsum('bclgn,bgrcl,bclgrp->cbgrpn', bc, decay_states, xs)
    chunk_decay = jnp.exp(a_cum[..., -1]).transpose(3, 0, 1, 2)

    def step(carry, inp):
        st, dec = inp
        return carry * dec[..., None, None] + st, carry

    _, prev = lax.scan(step, jnp.zeros_like(states[0]), (states, chunk_decay))
    y_off = jnp.einsum('bclgn,cbgrpn,bgrcl->bclgrp', cc, prev, jnp.exp(a_cum))
    return (y_diag + y_off).reshape(b, s, SSD_GROUPS, SSD_HPG, SSD_HEAD_DIM)


def ssd_mixer(h, w_in, conv_w, conv_b, dt_bias, a_log, d_skip, norm_g, w_o):
    b, s, _ = h.shape
    z, xbc, dt = jnp.split(h @ w_in, [SSD_INNER, SSD_INNER + SSD_CONV_DIM], axis=-1)
    xbc = lax.conv_general_dilated(xbc, conv_w[:, None, :].astype(xbc.dtype), window_strides=(1,),
                                   padding=[(SSD_CONV - 1, 0)],
                                   dimension_numbers=('NWC', 'WIO', 'NWC'),
                                   feature_group_count=SSD_CONV_DIM) + conv_b
    xbc = jax.nn.silu(xbc)
    xi, bm, cm = jnp.split(xbc, [SSD_INNER, SSD_INNER + SSD_GROUPS * SSD_STATE], axis=-1)
    f32 = jnp.float32
    dt = jax.nn.softplus(dt.astype(f32) + dt_bias.astype(f32))
    a = -jnp.exp(a_log.astype(f32))
    xi = xi.astype(f32).reshape(b, s, SSD_GROUPS, SSD_HPG, SSD_HEAD_DIM)
    y = ssd_chunked_scan(xi, dt.reshape(b, s, SSD_GROUPS, SSD_HPG), a.reshape(SSD_GROUPS, SSD_HPG),
                         bm.astype(f32).reshape(b, s, SSD_GROUPS, SSD_STATE),
                         cm.astype(f32).reshape(b, s, SSD_GROUPS, SSD_STATE))
    y = y + d_skip.astype(f32).reshape(SSD_GROUPS, SSD_HPG)[:, :, None] * xi
    y = y.reshape(b, s, SSD_INNER).astype(h.dtype) * jax.nn.silu(z)
    y = rms_norm(y.reshape(b, s, SSD_GROUPS, SSD_INNER // SSD_GROUPS),
                 norm_g.reshape(SSD_GROUPS, SSD_INNER // SSD_GROUPS)).reshape(b, s, SSD_INNER)
    return y @ w_o


def squared_relu_mlp(h, w_in, w_out):
    return jnp.square(jax.nn.relu(h @ w_in)) @ w_out


def _fwd_setup_inputs(seed: int = 0) -> dict:
    key = jax.random.key(seed)
    ks = jax.random.split(key, 24)
    nrm = jax.random.normal
    f32 = jnp.float32
    d = D_MODEL
    dt = jnp.exp(jax.random.uniform(ks[14], (N_C, SSD_HEADS), f32, np.log(1e-3), np.log(1e-1)))
    return {
        "x": nrm(ks[0], (BATCH, SEQ, d), f32),
        "norm_mix_g": 1.0 + 0.02 * nrm(ks[1], (DEPTH, d), f32),
        "norm_mlp_g": 1.0 + 0.02 * nrm(ks[2], (DEPTH, d), f32),
        "sb_w_qkv": nrm(ks[3], (N_A, d, 3 * SB_WIDTH), f32) * d ** -0.5,
        "sb_q_norm_g": 1.0 + 0.02 * nrm(ks[4], (N_A, SB_HEAD_DIM), f32),
        "sb_k_norm_g": 1.0 + 0.02 * nrm(ks[5], (N_A, SB_HEAD_DIM), f32),
        "sb_w_o": nrm(ks[6], (N_A, SB_WIDTH, d), f32) * SB_WIDTH ** -0.5,
        "gm_w_in": nrm(ks[7], (N_B, d, 2 * GM_WIDTH), f32) * d ** -0.5,
        "gm_v_norm_g": 1.0 + 0.02 * nrm(ks[8], (N_B, GM_WIDTH), f32),
        "gm_w_s": nrm(ks[9], (N_B, GM_GROUPS, GM_CHUNK, GM_CHUNK), f32) * (1.0 / GM_CHUNK),
        "gm_b_s": 1.0 + 0.02 * nrm(ks[10], (N_B, GM_GROUPS, GM_CHUNK), f32),
        "gm_w_o": nrm(ks[11], (N_B, GM_WIDTH, d), f32) * GM_WIDTH ** -0.5,
        "ssd_w_in": nrm(ks[12], (N_C, d, SSD_PROJ), f32) * d ** -0.5,
        "ssd_conv_w": nrm(ks[13], (N_C, SSD_CONV, SSD_CONV_DIM), f32) * SSD_CONV ** -0.5,
        "ssd_conv_b": 0.02 * nrm(ks[15], (N_C, SSD_CONV_DIM), f32),
        "ssd_dt_bias": dt + jnp.log(-jnp.expm1(-dt)),
        "ssd_a_log": jnp.log(jax.random.uniform(ks[16], (N_C, SSD_HEADS), f32, 1.0, 16.0)),
        "ssd_d": 1.0 + 0.02 * nrm(ks[17], (N_C, SSD_HEADS), f32),
        "ssd_norm_g": 1.0 + 0.02 * nrm(ks[18], (N_C, SSD_INNER), f32),
        "ssd_w_o": nrm(ks[19], (N_C, SSD_INNER, d), f32) * SSD_INNER ** -0.5,
        "mlp_w_in": nrm(ks[20], (DEPTH, d, MLP_HIDDEN), f32) * d ** -0.5,
        "mlp_w_out": nrm(ks[21], (DEPTH, MLP_HIDDEN, d), f32) * MLP_HIDDEN ** -0.5,
    }


def _fwd_reference(x, norm_mix_g, norm_mlp_g, sb_w_qkv, sb_q_norm_g, sb_k_norm_g, sb_w_o,
              gm_w_in, gm_v_norm_g, gm_w_s, gm_b_s, gm_w_o,
              ssd_w_in, ssd_conv_w, ssd_conv_b, ssd_dt_bias, ssd_a_log, ssd_d, ssd_norm_g, ssd_w_o,
              mlp_w_in, mlp_w_out):
    h = x
    for i in range(DEPTH):
        kind, j = i % N_MIXERS, i // N_MIXERS
        hn = rms_norm(h, norm_mix_g[i])
        if kind == 0:
            mix = stick_breaking_attention(hn, sb_w_qkv[j], sb_q_norm_g[j], sb_k_norm_g[j], sb_w_o[j])
        elif kind == 1:
            mix = chunked_spatial_gating(hn, gm_w_in[j], gm_v_norm_g[j], gm_w_s[j], gm_b_s[j], gm_w_o[j])
        else:
            mix = ssd_mixer(hn, ssd_w_in[j], ssd_conv_w[j], ssd_conv_b[j], ssd_dt_bias[j],
                            ssd_a_log[j], ssd_d[j], ssd_norm_g[j], ssd_w_o[j])
        h = h + mix
        h = h + squared_relu_mlp(rms_norm(h, norm_mlp_g[i]), mlp_w_in[i], mlp_w_out[i])
    return h


import jax as _jax
import jax.numpy as _jnp

TWIN_FORMAT = 'train_step'
FWD_PARAMS = ['x', 'norm_mix_g', 'norm_mlp_g', 'sb_w_qkv', 'sb_q_norm_g', 'sb_k_norm_g', 'sb_w_o', 'gm_w_in', 'gm_v_norm_g', 'gm_w_s', 'gm_b_s', 'gm_w_o', 'ssd_w_in', 'ssd_conv_w', 'ssd_conv_b', 'ssd_dt_bias', 'ssd_a_log', 'ssd_d', 'ssd_norm_g', 'ssd_w_o', 'mlp_w_in', 'mlp_w_out']
TWIN_WEIGHTS = ['norm_mix_g', 'norm_mlp_g', 'sb_w_qkv', 'sb_q_norm_g', 'sb_k_norm_g', 'sb_w_o', 'gm_w_in', 'gm_v_norm_g', 'gm_w_s', 'gm_b_s', 'gm_w_o', 'ssd_w_in', 'ssd_conv_w', 'ssd_conv_b', 'ssd_dt_bias', 'ssd_a_log', 'ssd_d', 'ssd_norm_g', 'ssd_w_o', 'mlp_w_in', 'mlp_w_out']
TWIN_DIFF_INPUT = 'x'
TWIN_INPUTS = ['x', 'norm_mix_g', 'norm_mlp_g', 'sb_w_qkv', 'sb_q_norm_g', 'sb_k_norm_g', 'sb_w_o', 'gm_w_in', 'gm_v_norm_g', 'gm_w_s', 'gm_b_s', 'gm_w_o', 'ssd_w_in', 'ssd_conv_w', 'ssd_conv_b', 'ssd_dt_bias', 'ssd_a_log', 'ssd_d', 'ssd_norm_g', 'ssd_w_o', 'mlp_w_in', 'mlp_w_out', 'loss_target', 'm_norm_mix_g', 'm_norm_mlp_g', 'm_sb_w_qkv', 'm_sb_q_norm_g', 'm_sb_k_norm_g', 'm_sb_w_o', 'm_gm_w_in', 'm_gm_v_norm_g', 'm_gm_w_s', 'm_gm_b_s', 'm_gm_w_o', 'm_ssd_w_in', 'm_ssd_conv_w', 'm_ssd_conv_b', 'm_ssd_dt_bias', 'm_ssd_a_log', 'm_ssd_d', 'm_ssd_norm_g', 'm_ssd_w_o', 'm_mlp_w_in', 'm_mlp_w_out', 'v_norm_mix_g', 'v_norm_mlp_g', 'v_sb_w_qkv', 'v_sb_q_norm_g', 'v_sb_k_norm_g', 'v_sb_w_o', 'v_gm_w_in', 'v_gm_v_norm_g', 'v_gm_w_s', 'v_gm_b_s', 'v_gm_w_o', 'v_ssd_w_in', 'v_ssd_conv_w', 'v_ssd_conv_b', 'v_ssd_dt_bias', 'v_ssd_a_log', 'v_ssd_d', 'v_ssd_norm_g', 'v_ssd_w_o', 'v_mlp_w_in', 'v_mlp_w_out']
TWIN_OUTPUTS = ['loss', 'grad_x', 'grad_norm_mix_g', 'grad_norm_mlp_g', 'grad_sb_w_qkv', 'grad_sb_q_norm_g', 'grad_sb_k_norm_g', 'grad_sb_w_o', 'grad_gm_w_in', 'grad_gm_v_norm_g', 'grad_gm_w_s', 'grad_gm_b_s', 'grad_gm_w_o', 'grad_ssd_w_in', 'grad_ssd_conv_w', 'grad_ssd_conv_b', 'grad_ssd_dt_bias', 'grad_ssd_a_log', 'grad_ssd_d', 'grad_ssd_norm_g', 'grad_ssd_w_o', 'grad_mlp_w_in', 'grad_mlp_w_out', 'delta_norm_mix_g', 'delta_norm_mlp_g', 'delta_sb_w_qkv', 'delta_sb_q_norm_g', 'delta_sb_k_norm_g', 'delta_sb_w_o', 'delta_gm_w_in', 'delta_gm_v_norm_g', 'delta_gm_w_s', 'delta_gm_b_s', 'delta_gm_w_o', 'delta_ssd_w_in', 'delta_ssd_conv_w', 'delta_ssd_conv_b', 'delta_ssd_dt_bias', 'delta_ssd_a_log', 'delta_ssd_d', 'delta_ssd_norm_g', 'delta_ssd_w_o', 'delta_mlp_w_in', 'delta_mlp_w_out', 'new_m_norm_mix_g', 'new_m_norm_mlp_g', 'new_m_sb_w_qkv', 'new_m_sb_q_norm_g', 'new_m_sb_k_norm_g', 'new_m_sb_w_o', 'new_m_gm_w_in', 'new_m_gm_v_norm_g', 'new_m_gm_w_s', 'new_m_gm_b_s', 'new_m_gm_w_o', 'new_m_ssd_w_in', 'new_m_ssd_conv_w', 'new_m_ssd_conv_b', 'new_m_ssd_dt_bias', 'new_m_ssd_a_log', 'new_m_ssd_d', 'new_m_ssd_norm_g', 'new_m_ssd_w_o', 'new_m_mlp_w_in', 'new_m_mlp_w_out', 'new_v_norm_mix_g', 'new_v_norm_mlp_g', 'new_v_sb_w_qkv', 'new_v_sb_q_norm_g', 'new_v_sb_k_norm_g', 'new_v_sb_w_o', 'new_v_gm_w_in', 'new_v_gm_v_norm_g', 'new_v_gm_w_s', 'new_v_gm_b_s', 'new_v_gm_w_o', 'new_v_ssd_w_in', 'new_v_ssd_conv_w', 'new_v_ssd_conv_b', 'new_v_ssd_dt_bias', 'new_v_ssd_a_log', 'new_v_ssd_d', 'new_v_ssd_norm_g', 'new_v_ssd_w_o', 'new_v_mlp_w_in', 'new_v_mlp_w_out']
TWIN_LEAF_KINDS = {'loss': 'loss', 'grad_x': 'grad_x', 'grad_norm_mix_g': 'grad_w', 'grad_norm_mlp_g': 'grad_w', 'grad_sb_w_qkv': 'grad_w', 'grad_sb_q_norm_g': 'grad_w', 'grad_sb_k_norm_g': 'grad_w', 'grad_sb_w_o': 'grad_w', 'grad_gm_w_in': 'grad_w', 'grad_gm_v_norm_g': 'grad_w', 'grad_gm_w_s': 'grad_w', 'grad_gm_b_s': 'grad_w', 'grad_gm_w_o': 'grad_w', 'grad_ssd_w_in': 'grad_w', 'grad_ssd_conv_w': 'grad_w', 'grad_ssd_conv_b': 'grad_w', 'grad_ssd_dt_bias': 'grad_w', 'grad_ssd_a_log': 'grad_w', 'grad_ssd_d': 'grad_w', 'grad_ssd_norm_g': 'grad_w', 'grad_ssd_w_o': 'grad_w', 'grad_mlp_w_in': 'grad_w', 'grad_mlp_w_out': 'grad_w', 'delta_norm_mix_g': 'delta_w', 'delta_norm_mlp_g': 'delta_w', 'delta_sb_w_qkv': 'delta_w', 'delta_sb_q_norm_g': 'delta_w', 'delta_sb_k_norm_g': 'delta_w', 'delta_sb_w_o': 'delta_w', 'delta_gm_w_in': 'delta_w', 'delta_gm_v_norm_g': 'delta_w', 'delta_gm_w_s': 'delta_w', 'delta_gm_b_s': 'delta_w', 'delta_gm_w_o': 'delta_w', 'delta_ssd_w_in': 'delta_w', 'delta_ssd_conv_w': 'delta_w', 'delta_ssd_conv_b': 'delta_w', 'delta_ssd_dt_bias': 'delta_w', 'delta_ssd_a_log': 'delta_w', 'delta_ssd_d': 'delta_w', 'delta_ssd_norm_g': 'delta_w', 'delta_ssd_w_o': 'delta_w', 'delta_mlp_w_in': 'delta_w', 'delta_mlp_w_out': 'delta_w', 'new_m_norm_mix_g': 'new_m', 'new_m_norm_mlp_g': 'new_m', 'new_m_sb_w_qkv': 'new_m', 'new_m_sb_q_norm_g': 'new_m', 'new_m_sb_k_norm_g': 'new_m', 'new_m_sb_w_o': 'new_m', 'new_m_gm_w_in': 'new_m', 'new_m_gm_v_norm_g': 'new_m', 'new_m_gm_w_s': 'new_m', 'new_m_gm_b_s': 'new_m', 'new_m_gm_w_o': 'new_m', 'new_m_ssd_w_in': 'new_m', 'new_m_ssd_conv_w': 'new_m', 'new_m_ssd_conv_b': 'new_m', 'new_m_ssd_dt_bias': 'new_m', 'new_m_ssd_a_log': 'new_m', 'new_m_ssd_d': 'new_m', 'new_m_ssd_norm_g': 'new_m', 'new_m_ssd_w_o': 'new_m', 'new_m_mlp_w_in': 'new_m', 'new_m_mlp_w_out': 'new_m', 'new_v_norm_mix_g': 'new_v', 'new_v_norm_mlp_g': 'new_v', 'new_v_sb_w_qkv': 'new_v', 'new_v_sb_q_norm_g': 'new_v', 'new_v_sb_k_norm_g': 'new_v', 'new_v_sb_w_o': 'new_v', 'new_v_gm_w_in': 'new_v', 'new_v_gm_v_norm_g': 'new_v', 'new_v_gm_w_s': 'new_v', 'new_v_gm_b_s': 'new_v', 'new_v_gm_w_o': 'new_v', 'new_v_ssd_w_in': 'new_v', 'new_v_ssd_conv_w': 'new_v', 'new_v_ssd_conv_b': 'new_v', 'new_v_ssd_dt_bias': 'new_v', 'new_v_ssd_a_log': 'new_v', 'new_v_ssd_d': 'new_v', 'new_v_ssd_norm_g': 'new_v', 'new_v_ssd_w_o': 'new_v', 'new_v_mlp_w_in': 'new_v', 'new_v_mlp_w_out': 'new_v'}


def _forward(args):
    return _fwd_reference(*[args[k] for k in FWD_PARAMS])


def _output_shape():
    def fwd():
        inp = _fwd_setup_inputs(0)
        return _fwd_reference(*[inp[k] for k in FWD_PARAMS])
    out = _jax.eval_shape(fwd)
    return out.shape, out.dtype

N_MICROBATCH = 1
ADAM_LR = 0.001
ADAM_B1 = 0.9
ADAM_B2 = 0.999
ADAM_EPS = 1e-08
ADAM_WD = 0.01
ADAM_STEP = 10
PER_EXAMPLE_BATCH_AXIS = {'x': 0, 'loss_target': 0}
SHARED_INPUTS = []
_WEIGHT_DTYPES = {'norm_mix_g': _jnp.float32, 'norm_mlp_g': _jnp.float32, 'sb_w_qkv': _jnp.float32, 'sb_q_norm_g': _jnp.float32, 'sb_k_norm_g': _jnp.float32, 'sb_w_o': _jnp.float32, 'gm_w_in': _jnp.float32, 'gm_v_norm_g': _jnp.float32, 'gm_w_s': _jnp.float32, 'gm_b_s': _jnp.float32, 'gm_w_o': _jnp.float32, 'ssd_w_in': _jnp.float32, 'ssd_conv_w': _jnp.float32, 'ssd_conv_b': _jnp.float32, 'ssd_dt_bias': _jnp.float32, 'ssd_a_log': _jnp.float32, 'ssd_d': _jnp.float32, 'ssd_norm_g': _jnp.float32, 'ssd_w_o': _jnp.float32, 'mlp_w_in': _jnp.float32, 'mlp_w_out': _jnp.float32}
MOMENT_SCALE = {'norm_mix_g': 9.627301e+00, 'norm_mlp_g': 5.007076e+01, 'sb_w_qkv': 3.923209e+00, 'sb_q_norm_g': 1.497080e+01, 'sb_k_norm_g': 1.494567e+01, 'sb_w_o': 6.305119e+00, 'gm_w_in': 3.691594e+00, 'gm_v_norm_g': 4.824834e-02, 'gm_w_s': 1.893802e+00, 'gm_b_s': 6.511142e+00, 'gm_w_o': 1.051964e+01, 'ssd_w_in': 3.793324e+00, 'ssd_conv_w': 4.622378e+00, 'ssd_conv_b': 8.888876e+00, 'ssd_dt_bias': 2.896621e+00, 'ssd_a_log': 1.692246e+01, 'ssd_d': 1.932814e+01, 'ssd_norm_g': 2.153156e+01, 'ssd_w_o': 9.080528e+00, 'mlp_w_in': 4.292288e+00, 'mlp_w_out': 1.543633e+01}


def _to_microbatches(a, axis):
    t = _jnp.moveaxis(a, axis, 0)
    t = t.reshape((N_MICROBATCH, t.shape[0] // N_MICROBATCH) + t.shape[1:])
    return _jnp.moveaxis(t, 1, axis + 1)


def setup_inputs(seed: int = 0) -> dict:
    inp = _fwd_setup_inputs(seed)
    key = _jax.random.fold_in(_jax.random.key(seed), 7919)
    shape, _ = _output_shape()
    out = dict(inp)
    out["loss_target"] = _jax.random.normal(_jax.random.fold_in(key, 0), shape, _jnp.float32)
    for i, name in enumerate(TWIN_WEIGHTS):
        w = inp[name].astype(_jnp.float32)
        if MOMENT_SCALE is None:
            s = _jnp.sqrt(_jnp.mean(_jnp.square(w)) + 1e-30)
        else:
            s = MOMENT_SCALE[name]
        km, kv = _jax.random.split(_jax.random.fold_in(key, i + 1))
        out[name] = w
        out["m_" + name] = s * _jax.random.normal(km, w.shape, _jnp.float32)
        out["v_" + name] = (s * s) * _jax.random.uniform(kv, w.shape, _jnp.float32, 0.5, 1.5)
    if N_MICROBATCH > 1:
        for name, axis in PER_EXAMPLE_BATCH_AXIS.items():
            out[name] = _to_microbatches(out[name], axis)
    return {'x': out['x'], 'norm_mix_g': out['norm_mix_g'], 'norm_mlp_g': out['norm_mlp_g'], 'sb_w_qkv': out['sb_w_qkv'], 'sb_q_norm_g': out['sb_q_norm_g'], 'sb_k_norm_g': out['sb_k_norm_g'], 'sb_w_o': out['sb_w_o'], 'gm_w_in': out['gm_w_in'], 'gm_v_norm_g': out['gm_v_norm_g'], 'gm_w_s': out['gm_w_s'], 'gm_b_s': out['gm_b_s'], 'gm_w_o': out['gm_w_o'], 'ssd_w_in': out['ssd_w_in'], 'ssd_conv_w': out['ssd_conv_w'], 'ssd_conv_b': out['ssd_conv_b'], 'ssd_dt_bias': out['ssd_dt_bias'], 'ssd_a_log': out['ssd_a_log'], 'ssd_d': out['ssd_d'], 'ssd_norm_g': out['ssd_norm_g'], 'ssd_w_o': out['ssd_w_o'], 'mlp_w_in': out['mlp_w_in'], 'mlp_w_out': out['mlp_w_out'], 'loss_target': out['loss_target'], 'm_norm_mix_g': out['m_norm_mix_g'], 'm_norm_mlp_g': out['m_norm_mlp_g'], 'm_sb_w_qkv': out['m_sb_w_qkv'], 'm_sb_q_norm_g': out['m_sb_q_norm_g'], 'm_sb_k_norm_g': out['m_sb_k_norm_g'], 'm_sb_w_o': out['m_sb_w_o'], 'm_gm_w_in': out['m_gm_w_in'], 'm_gm_v_norm_g': out['m_gm_v_norm_g'], 'm_gm_w_s': out['m_gm_w_s'], 'm_gm_b_s': out['m_gm_b_s'], 'm_gm_w_o': out['m_gm_w_o'], 'm_ssd_w_in': out['m_ssd_w_in'], 'm_ssd_conv_w': out['m_ssd_conv_w'], 'm_ssd_conv_b': out['m_ssd_conv_b'], 'm_ssd_dt_bias': out['m_ssd_dt_bias'], 'm_ssd_a_log': out['m_ssd_a_log'], 'm_ssd_d': out['m_ssd_d'], 'm_ssd_norm_g': out['m_ssd_norm_g'], 'm_ssd_w_o': out['m_ssd_w_o'], 'm_mlp_w_in': out['m_mlp_w_in'], 'm_mlp_w_out': out['m_mlp_w_out'], 'v_norm_mix_g': out['v_norm_mix_g'], 'v_norm_mlp_g': out['v_norm_mlp_g'], 'v_sb_w_qkv': out['v_sb_w_qkv'], 'v_sb_q_norm_g': out['v_sb_q_norm_g'], 'v_sb_k_norm_g': out['v_sb_k_norm_g'], 'v_sb_w_o': out['v_sb_w_o'], 'v_gm_w_in': out['v_gm_w_in'], 'v_gm_v_norm_g': out['v_gm_v_norm_g'], 'v_gm_w_s': out['v_gm_w_s'], 'v_gm_b_s': out['v_gm_b_s'], 'v_gm_w_o': out['v_gm_w_o'], 'v_ssd_w_in': out['v_ssd_w_in'], 'v_ssd_conv_w': out['v_ssd_conv_w'], 'v_ssd_conv_b': out['v_ssd_conv_b'], 'v_ssd_dt_bias': out['v_ssd_dt_bias'], 'v_ssd_a_log': out['v_ssd_a_log'], 'v_ssd_d': out['v_ssd_d'], 'v_ssd_norm_g': out['v_ssd_norm_g'], 'v_ssd_w_o': out['v_ssd_w_o'], 'v_mlp_w_in': out['v_mlp_w_in'], 'v_mlp_w_out': out['v_mlp_w_out']}


def _loss(weights, diff, rest, loss_target):
    with _jax.named_scope("forward"):
        args = {**rest, TWIN_DIFF_INPUT: diff, **{k: w.astype(_WEIGHT_DTYPES[k]) for k, w in weights.items()}}
        y = _forward(args)
    with _jax.named_scope("loss_head"):
        err = _jnp.square(y.astype(_jnp.float32) - loss_target)
        return 0.5 * _jnp.sum(_jnp.mean(err, axis=-1)) if err.ndim else 0.5 * err


def _adamw(w, g, m, v):
    m = ADAM_B1 * m + (1.0 - ADAM_B1) * g
    v = ADAM_B2 * v + (1.0 - ADAM_B2) * _jnp.square(g)
    m_hat = m / (1.0 - ADAM_B1 ** ADAM_STEP)
    v_hat = v / (1.0 - ADAM_B2 ** ADAM_STEP)
    delta = -ADAM_LR * (m_hat / (_jnp.sqrt(v_hat) + ADAM_EPS) + ADAM_WD * w)
    return delta, m, v


def reference(x, norm_mix_g, norm_mlp_g, sb_w_qkv, sb_q_norm_g, sb_k_norm_g, sb_w_o, gm_w_in, gm_v_norm_g, gm_w_s, gm_b_s, gm_w_o, ssd_w_in, ssd_conv_w, ssd_conv_b, ssd_dt_bias, ssd_a_log, ssd_d, ssd_norm_g, ssd_w_o, mlp_w_in, mlp_w_out, loss_target, m_norm_mix_g, m_norm_mlp_g, m_sb_w_qkv, m_sb_q_norm_g, m_sb_k_norm_g, m_sb_w_o, m_gm_w_in, m_gm_v_norm_g, m_gm_w_s, m_gm_b_s, m_gm_w_o, m_ssd_w_in, m_ssd_conv_w, m_ssd_conv_b, m_ssd_dt_bias, m_ssd_a_log, m_ssd_d, m_ssd_norm_g, m_ssd_w_o, m_mlp_w_in, m_mlp_w_out, v_norm_mix_g, v_norm_mlp_g, v_sb_w_qkv, v_sb_q_norm_g, v_sb_k_norm_g, v_sb_w_o, v_gm_w_in, v_gm_v_norm_g, v_gm_w_s, v_gm_b_s, v_gm_w_o, v_ssd_w_in, v_ssd_conv_w, v_ssd_conv_b, v_ssd_dt_bias, v_ssd_a_log, v_ssd_d, v_ssd_norm_g, v_ssd_w_o, v_mlp_w_in, v_mlp_w_out):
    given = dict(x=x, norm_mix_g=norm_mix_g, norm_mlp_g=norm_mlp_g, sb_w_qkv=sb_w_qkv, sb_q_norm_g=sb_q_norm_g, sb_k_norm_g=sb_k_norm_g, sb_w_o=sb_w_o, gm_w_in=gm_w_in, gm_v_norm_g=gm_v_norm_g, gm_w_s=gm_w_s, gm_b_s=gm_b_s, gm_w_o=gm_w_o, ssd_w_in=ssd_w_in, ssd_conv_w=ssd_conv_w, ssd_conv_b=ssd_conv_b, ssd_dt_bias=ssd_dt_bias, ssd_a_log=ssd_a_log, ssd_d=ssd_d, ssd_norm_g=ssd_norm_g, ssd_w_o=ssd_w_o, mlp_w_in=mlp_w_in, mlp_w_out=mlp_w_out, loss_target=loss_target, m_norm_mix_g=m_norm_mix_g, m_norm_mlp_g=m_norm_mlp_g, m_sb_w_qkv=m_sb_w_qkv, m_sb_q_norm_g=m_sb_q_norm_g, m_sb_k_norm_g=m_sb_k_norm_g, m_sb_w_o=m_sb_w_o, m_gm_w_in=m_gm_w_in, m_gm_v_norm_g=m_gm_v_norm_g, m_gm_w_s=m_gm_w_s, m_gm_b_s=m_gm_b_s, m_gm_w_o=m_gm_w_o, m_ssd_w_in=m_ssd_w_in, m_ssd_conv_w=m_ssd_conv_w, m_ssd_conv_b=m_ssd_conv_b, m_ssd_dt_bias=m_ssd_dt_bias, m_ssd_a_log=m_ssd_a_log, m_ssd_d=m_ssd_d, m_ssd_norm_g=m_ssd_norm_g, m_ssd_w_o=m_ssd_w_o, m_mlp_w_in=m_mlp_w_in, m_mlp_w_out=m_mlp_w_out, v_norm_mix_g=v_norm_mix_g, v_norm_mlp_g=v_norm_mlp_g, v_sb_w_qkv=v_sb_w_qkv, v_sb_q_norm_g=v_sb_q_norm_g, v_sb_k_norm_g=v_sb_k_norm_g, v_sb_w_o=v_sb_w_o, v_gm_w_in=v_gm_w_in, v_gm_v_norm_g=v_gm_v_norm_g, v_gm_w_s=v_gm_w_s, v_gm_b_s=v_gm_b_s, v_gm_w_o=v_gm_w_o, v_ssd_w_in=v_ssd_w_in, v_ssd_conv_w=v_ssd_conv_w, v_ssd_conv_b=v_ssd_conv_b, v_ssd_dt_bias=v_ssd_dt_bias, v_ssd_a_log=v_ssd_a_log, v_ssd_d=v_ssd_d, v_ssd_norm_g=v_ssd_norm_g, v_ssd_w_o=v_ssd_w_o, v_mlp_w_in=v_mlp_w_in, v_mlp_w_out=v_mlp_w_out)
    weights = {n: given[n] for n in TWIN_WEIGHTS}
    shared = {n: given[n] for n in SHARED_INPUTS}
    per_example = {n: given[n] for n in ['x']}
    grad_fn = _jax.value_and_grad(_loss, argnums=(0, 1))

    def one_microbatch(ex, loss_target):
        ex = dict(ex)
        diff = ex.pop(TWIN_DIFF_INPUT)
        return grad_fn(weights, diff, {**shared, **ex}, loss_target)

    if N_MICROBATCH == 1:
        loss, (grad_w, grad_x) = one_microbatch(per_example, given["loss_target"])
    else:
        def body(carry, xs):
            loss_sum, grad_sum = carry
            l_k, (gw_k, gx_k) = one_microbatch(xs[0], xs[1])
            with _jax.named_scope("update"):
                return (loss_sum + l_k, _jax.tree.map(_jnp.add, grad_sum, gw_k)), gx_k

        init = (_jnp.zeros((), _jnp.float32), _jax.tree.map(_jnp.zeros_like, weights))
        (loss, grad_w), grad_x = _jax.lax.scan(body, init, (per_example, given["loss_target"]))
    with _jax.named_scope("update"):
        delta_w, new_m, new_v = {}, {}, {}
        for n in TWIN_WEIGHTS:
            delta_w[n], new_m[n], new_v[n] = _adamw(weights[n], grad_w[n], given["m_" + n], given["v_" + n])
    return (loss, grad_x, *[grad_w[n] for n in TWIN_WEIGHTS], *[delta_w[n] for n in TWIN_WEIGHTS],
            *[new_m[n] for n in TWIN_WEIGHTS], *[new_v[n] for n in TWIN_WEIGHTS])
```

```python
import functools
import math

import jax
import jax.numpy as jnp
from jax import lax
from jax.experimental import pallas as pl
from jax.experimental.pallas import tpu as pltpu

F32 = jnp.float32
BF16 = jnp.bfloat16
_MXU = jnp.bfloat16
_ACT = jnp.bfloat16
GRAD_DT = jnp.bfloat16
_VMEM_LIMIT = 56 * 1024 * 1024
EPS = 1e-6
LANES = 128
CHUNK = 128
SSD_HEAD_DIM = 64
SSD_STATE = 128
SSD_CONV = 4
ADAM_LR, ADAM_B1, ADAM_B2, ADAM_EPS, ADAM_WD, ADAM_STEP = 1e-3, 0.9, 0.999, 1e-8, 0.01, 10
MESH = pl.DeviceIdType.MESH

NN = (((1,), (0,)), ((), ()))
NT = (((1,), (1,)), ((), ()))
TN = (((0,), (0,)), ((), ()))


def _dot(a, b, dims=NN):
    return lax.dot_general(a.astype(_MXU), b.astype(_MXU), dims, preferred_element_type=F32)


def _params(sem):
    return pltpu.CompilerParams(dimension_semantics=sem, vmem_limit_bytes=_VMEM_LIMIT)


def _pick(n, *cands):
    for c in cands:
        if n % c == 0:
            return c
    return n


def matmul(a, b, mode, out_dtypes, name, epilogue=None, extras=(), a_l=None, b_l=None):
    ash, bsh = a.shape[-2:], b.shape[-2:]
    if mode == "nn":
        (M, K), (K2, N) = ash, bsh
    elif mode == "nt":
        (M, K), (N, K2) = ash, bsh
    else:
        (K, M), (K2, N) = ash, bsh
    assert K == K2, (mode, a.shape, b.shape)
    tm = _pick(M, 1024, 512, 256, 128)
    tn = _pick(N, 1024, 512, 256, 128)
    tk = _pick(K, 512, 256, 128)
    nk = K // tk
    dims = {"nn": NN, "nt": NT, "tn": TN}[mode]
    single = not isinstance(out_dtypes, (tuple, list))
    odt = (out_dtypes,) if single else tuple(out_dtypes)
    n_ex = len(extras)

    def lead(l, spec_shape, imap):
        if l is None:
            return pl.BlockSpec(spec_shape, imap)
        return pl.BlockSpec((None,) + spec_shape, lambda i, j, k: (l,) + imap(i, j, k))

    if mode == "tn":
        a_spec = lead(a_l, (tk, tm), lambda i, j, k: (k, i))
    else:
        a_spec = lead(a_l, (tm, tk), lambda i, j, k: (i, k))
    if mode == "nt":
        b_spec = lead(b_l, (tn, tk), lambda i, j, k: (j, k))
    else:
        b_spec = lead(b_l, (tk, tn), lambda i, j, k: (k, j))
    mn_spec = pl.BlockSpec((tm, tn), lambda i, j, k: (i, j))

    def body(*refs):
        a_ref, b_ref = refs[0], refs[1]
        ex = refs[2:2 + n_ex]
        outs = refs[2 + n_ex:2 + n_ex + len(odt)]
        acc = refs[-1]
        k = pl.program_id(2)

        @pl.when(k == 0)
        def _():
            acc[...] = jnp.zeros_like(acc)

        acc[...] += _dot(a_ref[...], b_ref[...], dims)

        @pl.when(k == nk - 1)
        def _():
            r = acc[...]
            res = (r,) if epilogue is None else epilogue(r, *[e[...] for e in ex])
            for o, v in zip(outs, res):
                o[...] = v.astype(o.dtype)

    out = pl.pallas_call(
        body,
        out_shape=tuple(jax.ShapeDtypeStruct((M, N), d) for d in odt),
        grid=(M // tm, N // tn, nk),
        in_specs=[a_spec, b_spec] + [mn_spec] * n_ex,
        out_specs=tuple(mn_spec for _ in odt),
        scratch_shapes=[pltpu.VMEM((tm, tn), F32)],
        compiler_params=_params(("parallel", "parallel", "arbitrary")),
        name=name,
    )(a, b, *extras)
    return out[0] if single else out


class Op:
    def __init__(self, arr, block, imap, kind="tile", grad=True, gshape=None, gimap=None):
        self.arr, self.block, self.imap, self.kind, self.grad = arr, block, imap, kind, grad
        self.gshape = gshape or arr.shape
        self.gimap = gimap or imap

    def spec(self):
        return pl.BlockSpec(self.block, self.imap)


def tmap(f, grid, ins, outs, name):
    n_in = len(ins)

    def body(*refs):
        res = f(*[r[...] for r in refs[:n_in]])
        for o, v in zip(refs[n_in:], res):
            o[...] = v.astype(o.dtype)

    return pl.pallas_call(
        body,
        out_shape=tuple(jax.ShapeDtypeStruct(s, d) for s, d, _, _ in outs),
        grid=grid,
        in_specs=[o.spec() for o in ins],
        out_specs=tuple(pl.BlockSpec(b, m) for _, _, b, m in outs),
        compiler_params=_params(("parallel", "parallel")),
        name=name,
    )(*[o.arr for o in ins])


def tmap_vjp(f, grid, ins, cts, name, grad_dtypes=None):
    n_in, n_ct = len(ins), len(cts)
    gidx = [i for i, o in enumerate(ins) if o.grad]
    gdt = grad_dtypes or {}

    def body(*refs):
        in_refs, ct_refs, g_refs = refs[:n_in], refs[n_in:n_in + n_ct], refs[n_in + n_ct:]
        vals = [r[...] for r in in_refs]

        def g_only(*diff):
            full = list(vals)
            for i, v in zip(gidx, diff):
                full[i] = v
            return f(*full)

        res, vjp = jax.vjp(g_only, *[vals[i].astype(F32) for i in gidx])
        grads = vjp(tuple(c[...].astype(r.dtype) for c, r in zip(ct_refs, res)))
        inner = pl.program_id(1)
        for i, g, gr in zip(gidx, grads, g_refs):
            if ins[i].kind == "tile":
                gr[...] = g.astype(gr.dtype)
            else:
                @pl.when(inner == 0)
                def _(gr=gr, g=g):
                    gr[...] = g.astype(gr.dtype)

                @pl.when(inner != 0)
                def _(gr=gr, g=g):
                    gr[...] += g.astype(gr.dtype)

    out_shape = tuple(jax.ShapeDtypeStruct(ins[i].gshape, gdt.get(i, F32)) for i in gidx)
    return pl.pallas_call(
        body,
        out_shape=out_shape,
        grid=grid,
        in_specs=[o.spec() for o in ins] + [o.spec() for o in cts],
        out_specs=tuple(pl.BlockSpec(ins[i].block, ins[i].gimap) for i in gidx),
        compiler_params=_params(("parallel", "arbitrary")),
        name=name,
    )(*[o.arr for o in ins], *[o.arr for o in cts])


def _rms(x, g):
    return x * lax.rsqrt(jnp.mean(x * x, axis=-1, keepdims=True) + EPS) * g


def _row_ops(arrs, tm, grads=None):
    grads = grads or [True] * len(arrs)
    return [Op(a, (tm, a.shape[1]), lambda o, i: (i, 0), "tile", g) for a, g in zip(arrs, grads)]


def _vec_op(v, grad=True):
    return Op(v, (1, v.shape[1]), lambda o, i: (0, 0), "param", grad)


def rmsnorm_fwd(h, g, name):
    T, D = h.shape
    tm = _pick(T, 512, 256, 128)
    f = lambda x, gg: (_rms(x, gg),)
    return tmap(f, (1, T // tm), _row_ops([h], tm) + [_vec_op(g)],
                [((T, D), _ACT, (tm, D), lambda o, i: (i, 0))], name)[0]


def rmsnorm_bwd(h, g, dhn, dres, name):
    T, D = h.shape
    tm = _pick(T, 512, 256, 128)
    f = lambda x, gg: (_rms(x, gg), x)
    return tmap_vjp(f, (1, T // tm), _row_ops([h], tm) + [_vec_op(g)], _row_ops([dhn, dres], tm), name)


def mlp_fwd(h, g_row, w_in, w_out, l, tag):
    hn = rmsnorm_fwd(h, g_row, f"{tag}_norm")
    a, r2 = matmul(hn, w_in, "nn", (F32, _ACT), f"{tag}_in", b_l=l,
                   epilogue=lambda acc: (acc, jnp.square(jnp.maximum(acc, 0.0))))
    out = matmul(r2, w_out, "nn", F32, f"{tag}_out", b_l=l, epilogue=lambda acc, hh: (acc + hh,), extras=(h,))
    return out, (h, hn, a, r2)


def mlp_bwd(dout, saved, g_row, w_in, w_out, l, tag):
    h, hn, a, r2 = saved
    da = matmul(dout, w_out, "nt", _ACT, f"{tag}_dact", b_l=l,
                epilogue=lambda acc, aa: (acc * (2.0 * jnp.maximum(aa, 0.0)),), extras=(a,))
    dw_out = matmul(r2, dout, "tn", GRAD_DT, f"{tag}_dwout")
    dw_in = matmul(hn, da, "tn", GRAD_DT, f"{tag}_dwin")
    dhn = matmul(da, w_in, "nt", F32, f"{tag}_dhn", b_l=l)
    dh, dg = rmsnorm_bwd(h, g_row, dhn, dout, f"{tag}_dnorm")
    return dh, dg, dw_in, dw_out


def _split3(x):
    hi = x.astype(BF16)
    r = x - hi.astype(F32)
    mid = r.astype(BF16)
    lo = (r - mid.astype(F32)).astype(BF16)
    return hi, mid, lo


def _cumdot(x, tri):
    return sum(lax.dot_general(p, tri, NN, preferred_element_type=F32) for p in _split3(x))


def _iotas():
    row = lax.broadcasted_iota(jnp.int32, (CHUNK, CHUNK), 0)
    col = lax.broadcasted_iota(jnp.int32, (CHUNK, CHUNK), 1)
    return row, col


def _sb_block(q, kblk, qi, kb, scale, row, col):
    z = _dot(q, kblk, NT) * scale
    e = jnp.exp(-jnp.abs(z))
    den = 1.0 + e
    sp = jnp.maximum(z, 0.0) + jnp.log(den)
    mask = (col + kb * CHUNK) < (row + qi * CHUNK)
    lg = jnp.where(mask, -sp, 0.0)
    beta = jnp.where(z >= 0, 1.0, e) / den
    return z, mask, lg, beta


def attn_fwd(qn, kn, v, name):
    T, W = qn.shape
    H, NQ = W // LANES, T // CHUNK
    assert NQ <= LANES
    scale = LANES ** -0.5

    def body(q_ref, k_ref, v_ref, o_ref, r_ref, acc_ref, run_ref):
        qi = pl.program_id(1)
        q = q_ref[...]
        row, col = _iotas()
        suffix = (row >= col).astype(_MXU)
        acc_ref[...] = jnp.zeros_like(acc_ref)
        run_ref[...] = jnp.zeros_like(run_ref)
        r_ref[...] = jnp.zeros_like(r_ref)

        @pl.loop(0, qi + 1)
        def _(it):
            kb = qi - it
            off = pl.multiple_of(kb * CHUNK, CHUNK)
            run = run_ref[...]
            z, mask, lg, _ = _sb_block(q, k_ref[pl.ds(off, CHUNK), :], qi, kb, scale, row, col)
            r_ref[...] = jnp.where(col == kb, run, r_ref[...])
            cl = _cumdot(lg, suffix) + run
            a = jnp.exp(jnp.where(mask, z + cl, -1e30))
            acc_ref[...] += _dot(a, v_ref[pl.ds(off, CHUNK), :])
            run_ref[...] = run + jnp.sum(lg, axis=1, keepdims=True)

        o_ref[...] = acc_ref[...].astype(o_ref.dtype)

    qspec = pl.BlockSpec((CHUNK, LANES), lambda h, i: (i, h))
    kvspec = pl.BlockSpec((T, LANES), lambda h, i: (0, h))
    return pl.pallas_call(
        body,
        out_shape=(jax.ShapeDtypeStruct((T, W), _ACT), jax.ShapeDtypeStruct((H, T, LANES), F32)),
        grid=(H, NQ),
        in_specs=[qspec, kvspec, kvspec],
        out_specs=(qspec, pl.BlockSpec((None, CHUNK, LANES), lambda h, i: (h, i, 0))),
        scratch_shapes=[pltpu.VMEM((CHUNK, LANES), F32), pltpu.VMEM((CHUNK, 1), F32)],
        compiler_params=_params(("parallel", "parallel")),
        name=name,
    )(qn, kn, v)


def attn_bwd(qn, kn, v, do, rblk, name):
    T, W = qn.shape
    H, NQ = W // LANES, T // CHUNK
    scale = LANES ** -0.5

    def body(q_ref, k_ref, v_ref, do_ref, r_ref, dq_ref, dk_ref, dv_ref, g_ref):
        qi = pl.program_id(1)

        @pl.when(qi == 0)
        def _():
            dk_ref[...] = jnp.zeros_like(dk_ref)
            dv_ref[...] = jnp.zeros_like(dv_ref)

        q = q_ref[...]
        dout = do_ref[...]
        rt = r_ref[...]
        row, col = _iotas()
        suffix = (row >= col).astype(_MXU)
        prefix = (row <= col).astype(_MXU)

        dq_ref[...] = jnp.zeros_like(dq_ref)
        g_ref[...] = jnp.zeros_like(g_ref)

        @pl.loop(0, qi + 1)
        def _(kb):
            gsum = g_ref[...]
            off = pl.multiple_of(kb * CHUNK, CHUNK)
            kblk = k_ref[pl.ds(off, CHUNK), :]
            vblk = v_ref[pl.ds(off, CHUNK), :]
            z, mask, lg, beta = _sb_block(q, kblk, qi, kb, scale, row, col)
            run = jnp.sum(jnp.where(col == kb, rt, 0.0), axis=1, keepdims=True)
            cl = _cumdot(lg, suffix) + run
            a = jnp.exp(jnp.where(mask, z + cl, -1e30))
            e = _dot(dout, vblk, NT) * a
            f = _cumdot(e, prefix) + gsum
            dz = jnp.where(mask, e - beta * f, 0.0) * scale
            dk_ref[pl.ds(off, CHUNK), :] += _dot(dz, q, TN)
            dv_ref[pl.ds(off, CHUNK), :] += _dot(a, dout, TN)
            dq_ref[...] += _dot(dz, kblk)
            g_ref[...] = gsum + jnp.sum(e, axis=1, keepdims=True)

    qspec = pl.BlockSpec((CHUNK, LANES), lambda h, i: (i, h))
    kvspec = pl.BlockSpec((T, LANES), lambda h, i: (0, h))
    big = jax.ShapeDtypeStruct((T, W), F32)
    return pl.pallas_call(
        body,
        out_shape=(big, big, big),
        grid=(H, NQ),
        in_specs=[qspec, kvspec, kvspec, qspec, pl.BlockSpec((None, CHUNK, LANES), lambda h, i: (h, i, 0))],
        out_specs=(qspec, kvspec, kvspec),
        scratch_shapes=[pltpu.VMEM((CHUNK, 1), F32)],
        compiler_params=_params(("parallel", "arbitrary")),
        name=name,
    )(qn, kn, v, do, rblk)


def _qk_ops(qkv, qg, kg, tm, grad):
    T, W3 = qkv.shape
    H, NT_ = W3 // (3 * LANES), T // tm
    W = H * LANES

    def part(p):
        return Op(qkv, (tm, LANES), lambda o, n: (lax.rem(n, NT_), p * H + lax.div(n, NT_)), "tile", grad,
                  gshape=(T, W), gimap=lambda o, n: (lax.rem(n, NT_), lax.div(n, NT_)))

    vec = lambda g: Op(g, (1, LANES), lambda o, n: (0, 0), "param", grad)
    return [part(0), part(1), part(2), vec(qg), vec(kg)], (1, H * NT_), H, NT_, W


def _qk_f(q, k, v, qg, kg):
    return _rms(q, qg), _rms(k, kg), v


def qknorm_fwd(qkv, qg, kg, name):
    T = qkv.shape[0]
    tm = _pick(T, 512, 256, 128)
    ins, grid, H, NT_, W = _qk_ops(qkv, qg, kg, tm, False)
    out = ((T, W), _ACT, (tm, LANES), lambda o, n: (lax.rem(n, NT_), lax.div(n, NT_)))
    return tmap(_qk_f, grid, ins, [out, out, out], name)


def qknorm_bwd(qkv, qg, kg, dq, dk, dv, name):
    T = qkv.shape[0]
    tm = _pick(T, 512, 256, 128)
    ins, grid, H, NT_, W = _qk_ops(qkv, qg, kg, tm, True)
    cts = [Op(c, (tm, LANES), lambda o, n: (lax.rem(n, NT_), lax.div(n, NT_))) for c in (dq, dk, dv)]
    return tmap_vjp(_qk_f, grid, ins, cts, name, grad_dtypes={0: _ACT, 1: _ACT, 2: _ACT})


def sb_fwd(h, g_row, w_qkv, qg, kg, w_o, l, tag):
    hn = rmsnorm_fwd(h, g_row, f"{tag}_norm")
    qkv = matmul(hn, w_qkv, "nn", F32, f"{tag}_qkv", b_l=l)
    qn, kn, v = qknorm_fwd(qkv, qg, kg, f"{tag}_qknorm")
    o, rblk = attn_fwd(qn, kn, v, f"{tag}_attn")
    out = matmul(o, w_o, "nn", F32, f"{tag}_wo", b_l=l, epilogue=lambda acc, hh: (acc + hh,), extras=(h,))
    return out, (h, hn, qkv, qn, kn, v, o, rblk)


def sb_bwd(dout, saved, g_row, w_qkv, qg, kg, w_o, l, tag):
    h, hn, qkv, qn, kn, v, o, rblk = saved
    do = matmul(dout, w_o, "nt", _ACT, f"{tag}_do", b_l=l)
    dw_o = matmul(o, dout, "tn", GRAD_DT, f"{tag}_dwo")
    dqn, dkn, dv = attn_bwd(qn, kn, v, do, rblk, f"{tag}_dattn")
    dq, dk, dvv, dqg, dkg = qknorm_bwd(qkv, qg, kg, dqn, dkn, dv, f"{tag}_dqknorm")
    dqkv = jnp.concatenate([dq, dk, dvv], axis=1)
    dw_qkv = matmul(hn, dqkv, "tn", GRAD_DT, f"{tag}_dwqkv")
    dhn = matmul(dqkv, w_qkv, "nt", F32, f"{tag}_dhn", b_l=l)
    dh, dg = rmsnorm_bwd(h, g_row, dhn, dout, f"{tag}_dnorm")
    return dh, dg, dw_qkv, dqg, dkg, dw_o


@functools.partial(jax.custom_vjp, nondiff_argnums=(2,))
def _dotv(a, b, mode):
    return _dot(a, b, {"nn": NN, "nt": NT, "tn": TN}[mode])


def _dotv_fwd(a, b, mode):
    return _dotv(a, b, mode), (a, b)


def _dotv_bwd(mode, res, g):
    a, b = res
    if mode == "nn":
        return _dot(g, b, NT), _dot(a, g, TN)
    if mode == "nt":
        return _dot(g, b, NN), _dot(g, a, TN)
    return _dot(b, g, NT), _dot(a, g, NN)


_dotv.defvjp(_dotv_fwd, _dotv_bwd)


def _gelu(x):
    return 0.5 * x * (1.0 + lax.erf(x * (2.0 ** -0.5)))


def _gm1_f(au, av, vg):
    return _gelu(au), _rms(_gelu(av), vg)


def _gm1_ops(a, vg, tm, grad):
    T, W2 = a.shape
    W = W2 // 2
    part = lambda p: Op(a, (tm, W), lambda o, i: (i, p), "tile", grad, gshape=(T, W), gimap=lambda o, i: (i, 0))
    return [part(0), part(1), _vec_op(vg, grad)], (1, T // tm), W


def _gm2_f(u, vn, ws, bcol):
    row, col = _iotas()
    w = jnp.where(row >= col, ws, 0.0)
    return (u * (_dotv(w, vn, "nn") + bcol),)


def _gm2_ops(u, vn, ws, bcol, grad):
    T, W = u.shape
    blk = lambda x: Op(x, (CHUNK, LANES), lambda g, c: (c, g), "tile", grad)
    return [blk(u), blk(vn),
            Op(ws, (None, CHUNK, CHUNK), lambda g, c: (g, 0, 0), "param", grad),
            Op(bcol, (None, CHUNK, 1), lambda g, c: (g, 0, 0), "param", grad)], (W // LANES, T // CHUNK)


def gm_fwd(h, g_row, w_in, vg, ws, bcol, w_o, l, tag):
    T = h.shape[0]
    tm = _pick(T, 256, 128)
    hn = rmsnorm_fwd(h, g_row, f"{tag}_norm")
    a = matmul(hn, w_in, "nn", F32, f"{tag}_in", b_l=l)
    ins, grid, W = _gm1_ops(a, vg, tm, False)
    rows = lambda dt: ((T, W), dt, (tm, W), lambda o, i: (i, 0))
    u, vn = tmap(_gm1_f, grid, ins, [rows(F32), rows(_ACT)], f"{tag}_act")
    ins2, grid2 = _gm2_ops(u, vn, ws, bcol, False)
    y = tmap(_gm2_f, grid2, ins2, [((T, W), _ACT, (CHUNK, LANES), lambda g, c: (c, g))], f"{tag}_mix")[0]
    out = matmul(y, w_o, "nn", F32, f"{tag}_wo", b_l=l, epilogue=lambda acc, hh: (acc + hh,), extras=(h,))
    return out, (h, hn, a, u, vn, y)


def gm_bwd(dout, saved, g_row, w_in, vg, ws, bcol, w_o, l, tag):
    h, hn, a, u, vn, y = saved
    T = h.shape[0]
    tm = _pick(T, 256, 128)
    dy = matmul(dout, w_o, "nt", F32, f"{tag}_dy", b_l=l)
    dw_o = matmul(y, dout, "tn", GRAD_DT, f"{tag}_dwo")
    ins2, grid2 = _gm2_ops(u, vn, ws, bcol, True)
    du, dvn, dws, dbcol = tmap_vjp(_gm2_f, grid2, ins2, [Op(dy, (CHUNK, LANES), lambda g, c: (c, g))], f"{tag}_dmix")
    ins, grid, W = _gm1_ops(a, vg, tm, True)
    dau, dav, dvg = tmap_vjp(_gm1_f, grid, ins, _row_ops([du, dvn], tm), f"{tag}_dact", grad_dtypes={0: _ACT, 1: _ACT})
    da = jnp.concatenate([dau, dav], axis=1)
    dw_in = matmul(hn, da, "tn", GRAD_DT, f"{tag}_dwin")
    dhn = matmul(da, w_in, "nt", F32, f"{tag}_dhn", b_l=l)
    dh, dg = rmsnorm_bwd(h, g_row, dhn, dout, f"{tag}_dnorm")
    return dh, dg, dw_in, dvg, dws, dbcol, dw_o


@jax.custom_vjp
def _softplus(x):
    return jnp.maximum(x, 0.0) + jnp.log(1.0 + jnp.exp(-jnp.abs(x)))


_softplus.defvjp(lambda x: (_softplus(x), x), lambda x, g: (g * lax.logistic(x),))


def _silu(x):
    return x * lax.logistic(x)


def _shift_impl(x, s, down):
    n = x.shape[0]
    r = lax.broadcasted_iota(jnp.int32, x.shape, 0)
    if down:
        return jnp.where(r >= s, pltpu.roll(x, s, 0), 0.0)
    return jnp.where(r < n - s, pltpu.roll(x, n - s, 0), 0.0)


@functools.partial(jax.custom_vjp, nondiff_argnums=(1,))
def _shift_down(x, s):
    return _shift_impl(x, s, True)


_shift_down.defvjp(lambda x, s: (_shift_impl(x, s, True), None), lambda s, _, g: (_shift_impl(g, s, False),))


def _conv_f(x, w, b):
    k_id = lax.broadcasted_iota(jnp.int32, w.shape, 0)
    y = b + jnp.sum(jnp.where(k_id == SSD_CONV - 1, w, 0.0), axis=0, keepdims=True) * x
    for k in range(SSD_CONV - 1):
        wk = jnp.sum(jnp.where(k_id == k, w, 0.0), axis=0, keepdims=True)
        y = y + wk * _shift_down(x, SSD_CONV - 1 - k)
    return (_silu(y),)


def _conv_ops(zx, conv_w, conv_b, wi, grad):
    T = zx.shape[0]
    cd = conv_w.shape[1]
    cw = LANES
    off = wi // cw
    return [Op(zx, (T, cw), lambda o, j: (0, off + j), "tile", grad, gshape=(T, cd), gimap=lambda o, j: (0, j)),
            Op(conv_w, (SSD_CONV, cw), lambda o, j: (0, j), "tile", grad),
            Op(conv_b, (1, cw), lambda o, j: (0, j), "tile", grad)], (1, cd // cw), (T, cw)


def _dt_f(dtr, bias):
    return (_softplus(dtr + bias),)


def _cumdot_left(tri, x):
    return sum(lax.dot_general(tri, p, NN, preferred_element_type=F32) for p in _split3(x))


@jax.custom_vjp
def _cumsum_rows(x):
    row, col = _iotas()
    return _cumdot_left((row >= col).astype(_MXU), x)


def _cumsum_rows_bwd(_, g):
    row, col = _iotas()
    return (_cumdot_left((row <= col).astype(_MXU), g),)


_cumsum_rows.defvjp(lambda x: (_cumsum_rows(x), None), _cumsum_rows_bwd)


def _ssd_chunk(xps, dt, bm, cm, sps, alog, hid_base):
    row, col = _iotas()
    half = SSD_HEAD_DIM
    rcol = lax.broadcasted_iota(jnp.int32, (CHUNK, 1), 0)
    colpick = lambda m, hid: jnp.sum(jnp.where(col == hid, m, 0.0), axis=1, keepdims=True)
    rowpick = lambda m, hid: jnp.sum(jnp.where(row == hid, m, 0.0), axis=0, keepdims=True)
    last = lambda v: jnp.sum(jnp.where(rcol == CHUNK - 1, v, 0.0), axis=0, keepdims=True)
    acum = _cumsum_rows(dt * (-jnp.exp(alog)))
    acum_t = acum.T
    cb = _dotv(cm, bm, "nt")
    tri = row >= col
    lo = col < half
    ys, snews = [], []
    for p, (xp, sp) in enumerate(zip(xps, sps)):
        h0, h1 = hid_base + 2 * p, hid_base + 2 * p + 1
        ac0, ac1 = colpick(acum, h0), colpick(acum, h1)
        m0 = cb * jnp.exp(jnp.where(tri, ac0 - rowpick(acum_t, h0), -1e30))
        m1 = cb * jnp.exp(jnp.where(tri, ac1 - rowpick(acum_t, h1), -1e30))
        xs = xp * jnp.where(lo, colpick(dt, h0), colpick(dt, h1))
        ydiag = jnp.where(lo, _dotv(m0, xs, "nn"), _dotv(m1, xs, "nn"))
        yoff = jnp.where(lo, jnp.exp(ac0), jnp.exp(ac1)) * _dotv(cm, sp, "nt")
        al0, al1 = last(ac0), last(ac1)
        xsd = xs * jnp.where(lo, jnp.exp(al0 - ac0), jnp.exp(al1 - ac1))
        snew = jnp.where(row < half, jnp.exp(al0), jnp.exp(al1)) * sp + _dotv(xsd, bm, "tn")
        ys.append(ydiag + yoff)
        snews.append(snew)
    return ys, snews


def _ssd_dims(xact, wi):
    T, cd = xact.shape
    G = (cd - wi) // (2 * SSD_STATE)
    hpg = wi // SSD_HEAD_DIM // G
    assert hpg % 2 == 0 and SSD_STATE == LANES
    return T, G, hpg, hpg // 2, T // CHUNK, wi // G


def ssd_scan_fwd(xact, dt, alog, wi, name):
    T, G, hpg, NP, NC, gw = _ssd_dims(xact, wi)
    bo, co = wi // LANES, wi // LANES + G

    def body(x_ref, b_ref, c_ref, dt_ref, al_ref, y_ref, st_ref, s_ref):
        g, c = pl.program_id(0), pl.program_id(1)

        @pl.when(c == 0)
        def _():
            s_ref[...] = jnp.zeros_like(s_ref)

        st_ref[...] = s_ref[...]
        xps = [x_ref[:, p * LANES:(p + 1) * LANES] for p in range(NP)]
        sps = [s_ref[p] for p in range(NP)]
        ys, snews = _ssd_chunk(xps, dt_ref[...], b_ref[...], c_ref[...], sps, al_ref[...], g * hpg)
        for p in range(NP):
            y_ref[:, p * LANES:(p + 1) * LANES] = ys[p]
            s_ref[p] = snews[p]

    return pl.pallas_call(
        body,
        out_shape=(jax.ShapeDtypeStruct((T, wi), F32), jax.ShapeDtypeStruct((G, NC, NP, LANES, SSD_STATE), F32)),
        grid=(G, NC),
        in_specs=[pl.BlockSpec((CHUNK, gw), lambda g, c: (c, g)),
                  pl.BlockSpec((CHUNK, LANES), lambda g, c: (c, bo + g)),
                  pl.BlockSpec((CHUNK, LANES), lambda g, c: (c, co + g)),
                  pl.BlockSpec((CHUNK, LANES), lambda g, c: (c, 0)),
                  pl.BlockSpec((1, LANES), lambda g, c: (0, 0))],
        out_specs=(pl.BlockSpec((CHUNK, gw), lambda g, c: (c, g)),
                   pl.BlockSpec((None, None, NP, LANES, SSD_STATE), lambda g, c: (g, c, 0, 0, 0))),
        scratch_shapes=[pltpu.VMEM((NP, LANES, SSD_STATE), F32)],
        compiler_params=_params(("parallel", "arbitrary")),
        name=name,
    )(xact, xact, xact, dt, alog)


def ssd_scan_bwd(xact, dt, alog, states, dy, wi, name):
    T, G, hpg, NP, NC, gw = _ssd_dims(xact, wi)
    bo, co = wi // LANES, wi // LANES + G
    rev = lambda c: NC - 1 - c

    def body(x_ref, b_ref, c_ref, dt_ref, al_ref, st_ref, dy_ref, dx_ref, db_ref, dc_ref, ddt_ref, dal_ref, ds_ref):
        g, c = pl.program_id(0), pl.program_id(1)

        @pl.when(c == 0)
        def _():
            ds_ref[...] = jnp.zeros_like(ds_ref)

        xps = [x_ref[:, p * LANES:(p + 1) * LANES] for p in range(NP)]
        sps = [st_ref[p] for p in range(NP)]
        f = lambda xps_, dt_, bm_, cm_, sps_, al_: _ssd_chunk(xps_, dt_, bm_, cm_, sps_, al_, g * hpg)
        _, vjp = jax.vjp(f, xps, dt_ref[...], b_ref[...], c_ref[...], sps, al_ref[...])
        dys = [dy_ref[:, p * LANES:(p + 1) * LANES] for p in range(NP)]
        dxps, ddt, dbm, dcm, dsps, dal = vjp((dys, [ds_ref[p] for p in range(NP)]))
        for p in range(NP):
            dx_ref[:, p * LANES:(p + 1) * LANES] = dxps[p]
            ds_ref[p] = dsps[p]
        db_ref[...] = dbm
        dc_ref[...] = dcm
        ddt_ref[...] = ddt

        @pl.when(c == 0)
        def _():
            dal_ref[...] = dal

        @pl.when(c != 0)
        def _():
            dal_ref[...] += dal

    gb = G * SSD_STATE
    return pl.pallas_call(
        body,
        out_shape=(jax.ShapeDtypeStruct((T, wi), F32), jax.ShapeDtypeStruct((T, gb), F32), jax.ShapeDtypeStruct((T, gb), F32),
                   jax.ShapeDtypeStruct((G, T, LANES), F32), jax.ShapeDtypeStruct((G, 1, LANES), F32)),
        grid=(G, NC),
        in_specs=[pl.BlockSpec((CHUNK, gw), lambda g, c: (rev(c), g)),
                  pl.BlockSpec((CHUNK, LANES), lambda g, c: (rev(c), bo + g)),
                  pl.BlockSpec((CHUNK, LANES), lambda g, c: (rev(c), co + g)),
                  pl.BlockSpec((CHUNK, LANES), lambda g, c: (rev(c), 0)),
                  pl.BlockSpec((1, LANES), lambda g, c: (0, 0)),
                  pl.BlockSpec((None, None, NP, LANES, SSD_STATE), lambda g, c: (g, rev(c), 0, 0, 0)),
                  pl.BlockSpec((CHUNK, gw), lambda g, c: (rev(c), g))],
        out_specs=(pl.BlockSpec((CHUNK, gw), lambda g, c: (rev(c), g)),
                   pl.BlockSpec((CHUNK, LANES), lambda g, c: (rev(c), g)),
                   pl.BlockSpec((CHUNK, LANES), lambda g, c: (rev(c), g)),
                   pl.BlockSpec((None, CHUNK, LANES), lambda g, c: (g, rev(c), 0)),
                   pl.BlockSpec((None, 1, LANES), lambda g, c: (g, 0, 0))),
        scratch_shapes=[pltpu.VMEM((NP, LANES, SSD_STATE), F32)],
        compiler_params=_params(("parallel", "arbitrary")),
        name=name,
    )(xact, xact, xact, dt, alog, states, dy)


def _post_f(y, x, z, dcol, ng):
    return (_rms((y + dcol * x) * _silu(z), ng),)


def _post_ops(yssd, xact, zx, dcol, ng, G, tm, grad):
    T, wi = yssd.shape
    gw = wi // G
    blk = lambda a: Op(a, (tm, gw), lambda g, i: (i, g), "tile", grad, gshape=(T, wi))
    vec = lambda v: Op(v, (1, gw), lambda g, i: (0, g), "param", grad)
    return [blk(yssd), blk(xact), blk(zx), vec(dcol), vec(ng)], (G, T // tm), gw


def ssd_fwd(h, g_row, w_zx, w_dt, conv_w, conv_b, dtb, alog, dcol, ng, w_o, tag):
    T = h.shape[0]
    wi = ng.shape[1]
    tm = _pick(T, 256, 128)
    hn = rmsnorm_fwd(h, g_row, f"{tag}_norm")
    zx = matmul(hn, w_zx, "nn", F32, f"{tag}_inzx")
    dtr = matmul(hn, w_dt, "nn", F32, f"{tag}_indt")
    ins, grid, blk = _conv_ops(zx, conv_w, conv_b, wi, False)
    cd = conv_w.shape[1]
    xact = tmap(_conv_f, grid, ins, [((T, cd), F32, blk, lambda o, j: (0, j))], f"{tag}_conv")[0]
    dt = tmap(_dt_f, (1, T // tm), _row_ops([dtr], tm) + [_vec_op(dtb)],
              [((T, LANES), F32, (tm, LANES), lambda o, i: (i, 0))], f"{tag}_dt")[0]
    yssd, states = ssd_scan_fwd(xact, dt, alog, wi, f"{tag}_scan")
    G = states.shape[0]
    ins, grid, gw = _post_ops(yssd, xact, zx, dcol, ng, G, tm, False)
    yn = tmap(_post_f, grid, ins, [((T, wi), _ACT, (tm, gw), lambda g, i: (i, g))], f"{tag}_post")[0]
    out = matmul(yn, w_o, "nn", F32, f"{tag}_wo", epilogue=lambda acc, hh: (acc + hh,), extras=(h,))
    return out, (h, hn, zx, dtr, xact, dt, yssd, states, yn)


def ssd_bwd(dout, saved, g_row, w_zx, w_dt, conv_w, conv_b, dtb, alog, dcol, ng, w_o, tag):
    h, hn, zx, dtr, xact, dt, yssd, states, yn = saved
    T = h.shape[0]
    wi = ng.shape[1]
    tm = _pick(T, 256, 128)
    G = states.shape[0]
    dyn = matmul(dout, w_o, "nt", F32, f"{tag}_dyn")
    dw_o = matmul(yn, dout, "tn", GRAD_DT, f"{tag}_dwo")
    ins, grid, gw = _post_ops(yssd, xact, zx, dcol, ng, G, tm, True)
    dyssd, dxi_skip, dz, ddcol, dng = tmap_vjp(_post_f, grid, ins, [Op(dyn, (tm, gw), lambda g, i: (i, g))],
                                                f"{tag}_dpost", grad_dtypes={2: _ACT})
    dxi, dbm, dcm, ddt_g, dalog_g = ssd_scan_bwd(xact, dt, alog, states, dyssd, wi, f"{tag}_dscan")
    dxact = jnp.concatenate([dxi + dxi_skip, dbm, dcm], axis=1)
    ddt = jnp.sum(ddt_g, axis=0)
    dalog = jnp.sum(dalog_g, axis=0)
    ins, grid, blk = _conv_ops(zx, conv_w, conv_b, wi, True)
    dxbc, dconv_w, dconv_b = tmap_vjp(_conv_f, grid, ins, [Op(dxact, blk, lambda o, j: (0, j))], f"{tag}_dconv",
                                      grad_dtypes={0: _ACT})
    ddtr, ddtb = tmap_vjp(_dt_f, (1, T // tm), _row_ops([dtr], tm) + [_vec_op(dtb)], _row_ops([ddt], tm), f"{tag}_ddt",
                          grad_dtypes={0: _ACT})
    dzx = jnp.concatenate([dz, dxbc], axis=1)
    dw_zx = matmul(hn, dzx, "tn", GRAD_DT, f"{tag}_dwzx")
    dw_dt = matmul(hn, ddtr, "tn", GRAD_DT, f"{tag}_dwdt")
    dhn1 = matmul(ddtr, w_dt, "nt", F32, f"{tag}_dhn1")
    dhn = matmul(dzx, w_zx, "nt", F32, f"{tag}_dhn", epilogue=lambda acc, e: (acc + e,), extras=(dhn1,))
    dh, dg = rmsnorm_bwd(h, g_row, dhn, dout, f"{tag}_dnorm")
    return dh, dg, dw_zx, dw_dt, dconv_w, dconv_b, ddtb, dalog, ddcol, dng, dw_o


def loss_head(y, target, name):
    T, D = y.shape
    tm = _pick(T, 512, 256, 128)

    def body(y_ref, t_ref, dy_ref, part_ref):
        d = y_ref[...] - t_ref[...]
        dy_ref[...] = d * (1.0 / D)
        s = jnp.sum(d * d, axis=0, keepdims=True) * (0.5 / D)

        @pl.when(pl.program_id(0) == 0)
        def _():
            part_ref[...] = s

        @pl.when(pl.program_id(0) != 0)
        def _():
            part_ref[...] += s

    rows = pl.BlockSpec((tm, D), lambda i: (i, 0))
    return pl.pallas_call(
        body,
        out_shape=(jax.ShapeDtypeStruct((T, D), F32), jax.ShapeDtypeStruct((1, D), F32)),
        grid=(T // tm,),
        in_specs=[rows, rows],
        out_specs=(rows, pl.BlockSpec((1, D), lambda i: (0, 0))),
        compiler_params=_params(("arbitrary",)),
        name=name,
    )(y, target)


def _row_tile(R, C, itemsize=4, target=1 << 20):
    for t in (1024, 512, 256, 128, 64, 32, 16, 8):
        if R % t == 0 and t * C * itemsize <= target:
            return t
    return R


def adamw(w, g, m, v, name):
    R, C = w.shape
    tr = _row_tile(R, C)
    c1 = 1.0 - ADAM_B1 ** ADAM_STEP
    c2 = 1.0 - ADAM_B2 ** ADAM_STEP

    def body(w_ref, g_ref, m_ref, v_ref, d_ref, nm_ref, nv_ref):
        gg = g_ref[...]
        nm = ADAM_B1 * m_ref[...] + (1.0 - ADAM_B1) * gg
        nv = ADAM_B2 * v_ref[...] + (1.0 - ADAM_B2) * jnp.square(gg)
        d_ref[...] = -ADAM_LR * ((nm / c1) / (jnp.sqrt(nv / c2) + ADAM_EPS) + ADAM_WD * w_ref[...])
        nm_ref[...] = nm
        nv_ref[...] = nv

    spec = pl.BlockSpec((tr, C), lambda i: (i, 0))
    sds = jax.ShapeDtypeStruct((R, C), F32)
    return pl.pallas_call(body, out_shape=(sds, sds, sds), grid=(R // tr,), in_specs=[spec] * 4, out_specs=(spec,) * 3,
                          compiler_params=_params(("parallel",)), name=name)(w, g, m, v)


def _xyc():
    return lax.axis_index("x"), lax.axis_index("y"), lax.axis_index("c")


_REL = ((1, 0), (0, 1), (1, 1))


def _flip(v, f):
    return 1 - v if f else v


def pair_sum(gfull, recv, name):
    _, _, R, C = gfull.shape
    tr = _row_tile(R, C, 2)

    def body(g_ref, p_ref, o_ref):
        c = lax.axis_index("c")
        o_ref[...] = (g_ref[c].astype(F32) + p_ref[...].astype(F32)).astype(o_ref.dtype)

    return pl.pallas_call(
        body,
        out_shape=jax.ShapeDtypeStruct((4, R, C), gfull.dtype),
        grid=(4, R // tr),
        in_specs=[pl.BlockSpec((None, 2, tr, C), lambda p, i: (p, 0, i, 0)), pl.BlockSpec((None, tr, C), lambda p, i: (p, i, 0))],
        out_specs=pl.BlockSpec((None, tr, C), lambda p, i: (p, i, 0)),
        compiler_params=_params(("parallel", "parallel")),
        name=name,
    )(gfull, recv)


def chip_sum(s, recv, name):
    _, R, C = s.shape
    tr = _row_tile(R, C, 2, 1 << 19)

    def body(s_ref, p_ref, o_ref):
        x, y, _ = _xyc()
        acc = s_ref[2 * x + y].astype(F32)
        for r in range(3):
            acc = acc + p_ref[r].astype(F32)
        o_ref[...] = acc

    return pl.pallas_call(
        body,
        out_shape=jax.ShapeDtypeStruct((R, C), F32),
        grid=(R // tr,),
        in_specs=[pl.BlockSpec((4, tr, C), lambda i: (0, i, 0)), pl.BlockSpec((3, tr, C), lambda i: (0, i, 0))],
        out_specs=pl.BlockSpec((tr, C), lambda i: (i, 0)),
        compiler_params=_params(("parallel",)),
        name=name,
    )(s, recv)


def sum8(g, name):
    _, R, C = g.shape

    def body(g_ref, o_ref):
        acc = g_ref[0]
        for d in range(1, 8):
            acc = acc + g_ref[d]
        o_ref[...] = acc

    return pl.pallas_call(body, out_shape=jax.ShapeDtypeStruct((R, C), F32), name=name,
                          compiler_params=pltpu.CompilerParams(vmem_limit_bytes=_VMEM_LIMIT))(g)


_ANY = pl.BlockSpec(memory_space=pl.ANY)


def _remote(src, dst, ssem, rsem, dev):
    return pltpu.make_async_remote_copy(src_ref=src, dst_ref=dst, send_sem=ssem, recv_sem=rsem, device_id=dev,
                                        device_id_type=MESH)


def gather_small(x, name):
    R, C = x.shape

    def body(x_ref, out_ref, send_sems, recv_sems, local_sem):
        x_, y_, c_ = _xyc()
        me, sibling = (x_, y_, c_), (x_, y_, 1 - c_)
        chips = [(_flip(x_, fx), _flip(y_, fy)) for fx, fy in _REL]
        slot = lambda px, py, pc: out_ref.at[4 * px + 2 * py + pc]

        def copy(k, block, to, src=None):
            return _remote(slot(*block) if src is None else src, slot(*block), send_sems.at[k], recv_sems.at[k], to)

        mine = pltpu.make_async_copy(x_ref, slot(*me), local_sem)
        mine.start()
        first = [copy(0, me, sibling, src=x_ref)]
        first += [copy(1 + j, me, (*chip, c_), src=x_ref) for j, chip in enumerate(chips)]
        for cp in first:
            cp.start()
        passed = [copy(4 + j, (*chip, c_), sibling) for j, chip in enumerate(chips)]
        for j, chip in enumerate(chips):
            copy(1 + j, (*chip, c_), me).wait_recv()
            passed[j].start()
        copy(0, sibling, me).wait_recv()
        for j, chip in enumerate(chips):
            copy(4 + j, (*chip, 1 - c_), me).wait_recv()
        for cp in first + passed:
            cp.wait_send()
        mine.wait()

    return pl.pallas_call(
        body,
        out_shape=jax.ShapeDtypeStruct((8, R, C), x.dtype),
        in_specs=[pl.BlockSpec(memory_space=pltpu.VMEM)],
        out_specs=pl.BlockSpec(memory_space=pltpu.VMEM),
        scratch_shapes=[pltpu.SemaphoreType.DMA((7,)), pltpu.SemaphoreType.DMA((7,)), pltpu.SemaphoreType.DMA],
        compiler_params=pltpu.CompilerParams(vmem_limit_bytes=_VMEM_LIMIT),
        name=name,
    )(x)


def gather_weights(halves, name):
    n = len(halves)

    def body(*refs):
        ins, outs = refs[:n], refs[n:2 * n]
        send_sems, recv_sems, local_sems = refs[2 * n:]
        x_, y_, c_ = _xyc()
        sibling = (x_, y_, 1 - c_)
        chips = [(_flip(x_, fx), _flip(y_, fy)) for fx, fy in _REL]
        mine = 2 * x_ + y_
        started, local = [], []
        for i in range(n):
            own, dst = ins[i].at[c_], outs[i].at[mine, c_]
            lc = pltpu.make_async_copy(own, dst, local_sems.at[i])
            lc.start()
            local.append(lc)
            for r, chip in enumerate(chips):
                started.append(_remote(own, dst, send_sems.at[7 * i + r], recv_sems.at[7 * i + r], (*chip, c_)))
                started[-1].start()
            started.append(_remote(own, dst, send_sems.at[7 * i + 3], recv_sems.at[7 * i + 3], sibling))
            started[-1].start()
        for i in range(n):
            for r, (px, py) in enumerate(chips):
                blk = outs[i].at[2 * px + py, c_]
                _remote(blk, blk, send_sems.at[7 * i + r], recv_sems.at[7 * i + r], sibling).wait_recv()
                started.append(_remote(blk, blk, send_sems.at[7 * i + 4 + r], recv_sems.at[7 * i + 4 + r], sibling))
                started[-1].start()
        for i in range(n):
            blk = outs[i].at[mine, 1 - c_]
            _remote(blk, blk, send_sems.at[7 * i + 3], recv_sems.at[7 * i + 3], sibling).wait_recv()
            for r, (px, py) in enumerate(chips):
                blk = outs[i].at[2 * px + py, 1 - c_]
                _remote(blk, blk, send_sems.at[7 * i + 4 + r], recv_sems.at[7 * i + 4 + r], sibling).wait_recv()
        for cp in started:
            cp.wait_send()
        for lc in local:
            lc.wait()

    return pl.pallas_call(
        body,
        out_shape=tuple(jax.ShapeDtypeStruct((4,) + h.shape, h.dtype) for h in halves),
        in_specs=[_ANY] * n,
        out_specs=tuple(_ANY for _ in halves),
        scratch_shapes=[pltpu.SemaphoreType.DMA((7 * n,)), pltpu.SemaphoreType.DMA((7 * n,)), pltpu.SemaphoreType.DMA((n,))],
        name=name,
    )(*halves)


def swap_halves(gfulls, name):
    n = len(gfulls)

    def body(*refs):
        ins, outs = refs[:n], refs[n:2 * n]
        send_sems, recv_sems = refs[2 * n:]
        x_, y_, c_ = _xyc()
        sibling = (x_, y_, 1 - c_)
        started = []
        for i in range(n):
            for p in range(4):
                started.append(_remote(ins[i].at[p, 1 - c_], outs[i].at[p], send_sems.at[4 * i + p], recv_sems.at[4 * i + p], sibling))
                started[-1].start()
        for cp in started:
            cp.wait()

    return pl.pallas_call(
        body,
        out_shape=tuple(jax.ShapeDtypeStruct((4,) + g.shape[2:], g.dtype) for g in gfulls),
        in_specs=[_ANY] * n,
        out_specs=tuple(_ANY for _ in gfulls),
        scratch_shapes=[pltpu.SemaphoreType.DMA((4 * n,)), pltpu.SemaphoreType.DMA((4 * n,))],
        name=name,
    )(*gfulls)


def scatter_chips(sums, name):
    n = len(sums)

    def body(*refs):
        ins, outs = refs[:n], refs[n:2 * n]
        send_sems, recv_sems = refs[2 * n:]
        x_, y_, c_ = _xyc()
        chips = [(_flip(x_, fx), _flip(y_, fy)) for fx, fy in _REL]
        started = []
        for i in range(n):
            for r, (px, py) in enumerate(chips):
                started.append(_remote(ins[i].at[2 * px + py], outs[i].at[r], send_sems.at[3 * i + r], recv_sems.at[3 * i + r], (px, py, c_)))
                started[-1].start()
        for cp in started:
            cp.wait()

    return pl.pallas_call(
        body,
        out_shape=tuple(jax.ShapeDtypeStruct((3,) + s.shape[1:], s.dtype) for s in sums),
        in_specs=[_ANY] * n,
        out_specs=tuple(_ANY for _ in sums),
        scratch_shapes=[pltpu.SemaphoreType.DMA((3 * n,)), pltpu.SemaphoreType.DMA((3 * n,))],
        name=name,
    )(*sums)


def join_halves(halves, name):
    n = len(halves)

    def body(*refs):
        ins, outs = refs[:n], refs[n:2 * n]
        send_sems, recv_sems, local_sems = refs[2 * n:]
        x_, y_, c_ = _xyc()
        sibling = (x_, y_, 1 - c_)
        local, sent = [], []
        for i in range(n):
            local.append(pltpu.make_async_copy(ins[i], outs[i].at[c_], local_sems.at[i]))
            local[-1].start()
            sent.append(_remote(ins[i], outs[i].at[c_], send_sems.at[i], recv_sems.at[i], sibling))
            sent[-1].start()
        for i in range(n):
            _remote(ins[i], outs[i].at[1 - c_], send_sems.at[i], recv_sems.at[i], sibling).wait_recv()
        for cp in sent:
            cp.wait_send()
        for lc in local:
            lc.wait()

    return pl.pallas_call(
        body,
        out_shape=tuple(jax.ShapeDtypeStruct((2,) + h.shape, h.dtype) for h in halves),
        in_specs=[_ANY] * n,
        out_specs=tuple(_ANY for _ in halves),
        scratch_shapes=[pltpu.SemaphoreType.DMA((n,)), pltpu.SemaphoreType.DMA((n,)), pltpu.SemaphoreType.DMA((n,))],
        name=name,
    )(*halves)


def _halves(a):
    return a.reshape(2, -1, a.shape[-1])


def _cols_to_full(g):
    _, L, K, n = g.shape
    return g.transpose(1, 2, 0, 3).reshape(L, K, 4 * n)


def _full_to_cols(w):
    L, K, N = w.shape
    return w.reshape(L, K, 4, N // 4).transpose(2, 0, 1, 3)


def _rows_to_full(g):
    _, L, k, N = g.shape
    return g.transpose(1, 0, 2, 3).reshape(L, 4 * k, N)


def _full_to_rows(w):
    L, K, N = w.shape
    return w.reshape(L, 4, K // 4, N).transpose(1, 0, 2, 3)


_BIG = ("sb_w_qkv", "sb_w_o", "gm_w_in", "gm_w_o", "ssd_w_in", "ssd_w_o", "mlp_w_in", "mlp_w_out")
_COLS = ("sb_w_qkv", "gm_w_in", "ssd_w_in", "mlp_w_in")
_SMALL = ("norm_mix_g", "norm_mlp_g", "sb_q_norm_g", "sb_k_norm_g", "gm_v_norm_g", "gm_w_s", "gm_b_s",
          "ssd_conv_w", "ssd_conv_b", "ssd_dt_bias", "ssd_a_log", "ssd_d", "ssd_norm_g")
_SMALL_SHARDED = ("ssd_conv_w", "ssd_conv_b", "ssd_norm_g")
_WEIGHTS = ("norm_mix_g", "norm_mlp_g", "sb_w_qkv", "sb_q_norm_g", "sb_k_norm_g", "sb_w_o", "gm_w_in", "gm_v_norm_g",
            "gm_w_s", "gm_b_s", "gm_w_o", "ssd_w_in", "ssd_conv_w", "ssd_conv_b", "ssd_dt_bias", "ssd_a_log", "ssd_d",
            "ssd_norm_g", "ssd_w_o", "mlp_w_in", "mlp_w_out")


def _pack(arrs):
    flat = jnp.concatenate([a.reshape(-1).astype(F32) for a in arrs])
    n = flat.shape[0]
    tot = -(-n // (8 * LANES)) * 8 * LANES
    return jnp.pad(flat, (0, tot - n)).reshape(-1, LANES)


def _unpack(buf, shapes):
    flat = buf.reshape(-1)
    out, o = [], 0
    for s in shapes:
        n = math.prod(s)
        out.append(flat[o:o + n].reshape(s))
        o += n
    return out


def _step(x, w, target, m, v):
    depth = w["norm_mix_g"].shape[0]
    xc, yc, _ = _xyc()
    chip = 2 * xc + yc
    Hs = w["ssd_dt_bias"].shape[1]
    wi = 4 * w["ssd_norm_g"].shape[1]
    cd = 4 * w["ssd_conv_b"].shape[1]

    gathered = gather_weights([_halves(w[k].astype(_MXU)) for k in _BIG], "gather_weights")
    full = {}
    for k, g in zip(_BIG, gathered):
        g = g.reshape((4,) + w[k].shape)
        full[k] = _cols_to_full(g) if k in _COLS else _rows_to_full(g)
    w_zx = full["ssd_w_in"][0][:, :wi + cd]
    w_dt = jnp.pad(full["ssd_w_in"][0][:, wi + cd:], ((0, 0), (0, LANES - Hs)))
    small_sh = gather_small(_pack([w[k] for k in _SMALL_SHARDED]), "gather_small_weights")
    parts = [_unpack(small_sh[2 * j], [w[k].shape for k in _SMALL_SHARDED]) for j in range(4)]
    conv_w = jnp.concatenate([p[0][0] for p in parts], axis=1)
    conv_b = jnp.concatenate([p[1] for p in parts], axis=1)
    ssd_ng = jnp.concatenate([p[2] for p in parts], axis=1)
    padh = lambda a: jnp.pad(a, ((0, 0), (0, LANES - Hs)))
    dtb, alog = padh(w["ssd_dt_bias"]), padh(w["ssd_a_log"])
    dcol = jnp.repeat(w["ssd_d"], SSD_HEAD_DIM, axis=1)
    bcol = w["gm_b_s"][0][:, :, None]

    h = x[0]
    tape = []
    for i in range(depth):
        kind, j = i % 3, i // 3
        gmix = w["norm_mix_g"][i:i + 1]
        if kind == 0:
            args = (gmix, full["sb_w_qkv"], w["sb_q_norm_g"][j:j + 1], w["sb_k_norm_g"][j:j + 1], full["sb_w_o"], j, f"sb{j}")
            h, sv = sb_fwd(h, *args)
        elif kind == 1:
            args = (gmix, full["gm_w_in"], w["gm_v_norm_g"][j:j + 1], w["gm_w_s"][j], bcol, full["gm_w_o"], j, f"gm{j}")
            h, sv = gm_fwd(h, *args)
        else:
            args = (gmix, w_zx, w_dt, conv_w, conv_b, dtb, alog, dcol, ssd_ng, full["ssd_w_o"][j], f"ssd{j}")
            h, sv = ssd_fwd(h, *args)
        margs = (w["norm_mlp_g"][i:i + 1], full["mlp_w_in"], full["mlp_w_out"], i, f"mlp{i}")
        h, msv = mlp_fwd(h, *margs)
        tape.append((kind, j, args, sv, margs, msv))

    dh, loss_cols = loss_head(h, target[0], "loss_head")

    gw = {k: [None] * w[k].shape[0] for k in ("norm_mix_g", "norm_mlp_g", "sb_w_qkv", "sb_q_norm_g", "sb_k_norm_g", "sb_w_o",
                                                "mlp_w_in", "mlp_w_out")}
    for i in reversed(range(depth)):
        kind, j, args, sv, margs, msv = tape[i]
        dh, gw["norm_mlp_g"][i], gw["mlp_w_in"][i], gw["mlp_w_out"][i] = mlp_bwd(dh, msv, *margs)
        if kind == 0:
            dh, gw["norm_mix_g"][i], gw["sb_w_qkv"][j], gw["sb_q_norm_g"][j], gw["sb_k_norm_g"][j], gw["sb_w_o"][j] = sb_bwd(dh, sv, *args)
        elif kind == 1:
            dh, gw["norm_mix_g"][i], d_in, d_vg, d_ws, d_bcol, d_wo = gm_bwd(dh, sv, *args)
            gw["gm_w_in"], gw["gm_v_norm_g"], gw["gm_w_s"], gw["gm_b_s"], gw["gm_w_o"] = d_in[None], d_vg, d_ws[None], d_bcol[None, :, :, 0], d_wo[None]
        else:
            dh, gw["norm_mix_g"][i], d_zx, d_dt, d_cw, d_cb, d_dtb, d_al, d_dcol, d_ng, d_wo = ssd_bwd(dh, sv, *args)
            gw["ssd_w_in"] = jnp.concatenate([d_zx, d_dt[:, :Hs]], axis=1)[None]
            gw["ssd_conv_w"], gw["ssd_conv_b"], gw["ssd_norm_g"], gw["ssd_w_o"] = d_cw[None], d_cb, d_ng, d_wo[None]
            gw["ssd_dt_bias"], gw["ssd_a_log"] = d_dtb[:, :Hs], d_al[:, :Hs]
            gw["ssd_d"] = jnp.sum(d_dcol.reshape(Hs, SSD_HEAD_DIM), axis=1)[None]
    for k in gw:
        if isinstance(gw[k], list):
            gw[k] = jnp.stack(gw[k]) if k in _BIG else jnp.concatenate(gw[k], axis=0)
    grad_x = dh[None]

    gfull = []
    for k in _BIG:
        g = _full_to_cols(gw[k]) if k in _COLS else _full_to_rows(gw[k])
        gfull.append(g.reshape(4, 2, -1, g.shape[-1]).astype(GRAD_DT))
    from_sibling = swap_halves(gfull, "grad_swap_halves")
    pair = [pair_sum(g, r, f"pair_sum_{k}") for k, g, r in zip(_BIG, gfull, from_sibling)]
    from_chips = scatter_chips(pair, "grad_scatter_chips")
    mine = [chip_sum(s, r, f"chip_sum_{k}") for k, s, r in zip(_BIG, pair, from_chips)]
    joined = join_halves(mine, "grad_join_halves")
    grads, deltas, new_m, new_v = {}, {}, {}, {}
    for k, g in zip(_BIG, joined):
        C = w[k].shape[-1]
        d_, m_, v_ = adamw(w[k].reshape(-1, C), g.reshape(-1, C), m[k].reshape(-1, C), v[k].reshape(-1, C), f"adamw_{k}")
        grads[k], deltas[k], new_m[k], new_v[k] = (a.reshape(w[k].shape) for a in (g, d_, m_, v_))

    full_shapes = [gw[k].shape for k in _SMALL] + [(1,)]
    loss_local = jnp.sum(loss_cols).reshape(1)
    red = sum8(gather_small(_pack([gw[k] for k in _SMALL] + [loss_local]), "gather_small_grads"), "sum_small_grads")
    red = _unpack(red, full_shapes)
    loss = red[-1][0]
    gsm = dict(zip(_SMALL, red[:-1]))
    for k in _SMALL_SHARDED:
        n = w[k].shape[-1]
        gsm[k] = lax.dynamic_slice_in_dim(gsm[k], chip * n, n, axis=gsm[k].ndim - 1)
    shapes = [w[k].shape for k in _SMALL]
    packed = [_pack([d[k] for k in _SMALL]) for d in (w, gsm, m, v)]
    outs = adamw(*packed, "adamw_small")
    for k, g_, d_, m_, v_ in zip(_SMALL, [gsm[k] for k in _SMALL], *[_unpack(o, shapes) for o in outs]):
        grads[k], deltas[k], new_m[k], new_v[k] = g_.reshape(w[k].shape), d_, m_, v_

    return (loss, grad_x, *[grads[k] for k in _WEIGHTS], *[deltas[k] for k in _WEIGHTS],
            *[new_m[k] for k in _WEIGHTS], *[new_v[k] for k in _WEIGHTS])


def kernel(x, norm_mix_g, norm_mlp_g, sb_w_qkv, sb_q_norm_g, sb_k_norm_g, sb_w_o, gm_w_in, gm_v_norm_g, gm_w_s, gm_b_s, gm_w_o, ssd_w_in, ssd_conv_w, ssd_conv_b, ssd_dt_bias, ssd_a_log, ssd_d, ssd_norm_g, ssd_w_o, mlp_w_in, mlp_w_out, loss_target, m_norm_mix_g, m_norm_mlp_g, m_sb_w_qkv, m_sb_q_norm_g, m_sb_k_norm_g, m_sb_w_o, m_gm_w_in, m_gm_v_norm_g, m_gm_w_s, m_gm_b_s, m_gm_w_o, m_ssd_w_in, m_ssd_conv_w, m_ssd_conv_b, m_ssd_dt_bias, m_ssd_a_log, m_ssd_d, m_ssd_norm_g, m_ssd_w_o, m_mlp_w_in, m_mlp_w_out, v_norm_mix_g, v_norm_mlp_g, v_sb_w_qkv, v_sb_q_norm_g, v_sb_k_norm_g, v_sb_w_o, v_gm_w_in, v_gm_v_norm_g, v_gm_w_s, v_gm_b_s, v_gm_w_o, v_ssd_w_in, v_ssd_conv_w, v_ssd_conv_b, v_ssd_dt_bias, v_ssd_a_log, v_ssd_d, v_ssd_norm_g, v_ssd_w_o, v_mlp_w_in, v_mlp_w_out):
    w = dict(zip(_WEIGHTS, (norm_mix_g, norm_mlp_g, sb_w_qkv, sb_q_norm_g, sb_k_norm_g, sb_w_o, gm_w_in, gm_v_norm_g, gm_w_s,
                            gm_b_s, gm_w_o, ssd_w_in, ssd_conv_w, ssd_conv_b, ssd_dt_bias, ssd_a_log, ssd_d, ssd_norm_g,
                            ssd_w_o, mlp_w_in, mlp_w_out)))
    m = dict(zip(_WEIGHTS, (m_norm_mix_g, m_norm_mlp_g, m_sb_w_qkv, m_sb_q_norm_g, m_sb_k_norm_g, m_sb_w_o, m_gm_w_in,
                            m_gm_v_norm_g, m_gm_w_s, m_gm_b_s, m_gm_w_o, m_ssd_w_in, m_ssd_conv_w, m_ssd_conv_b,
                            m_ssd_dt_bias, m_ssd_a_log, m_ssd_d, m_ssd_norm_g, m_ssd_w_o, m_mlp_w_in, m_mlp_w_out)))
    v = dict(zip(_WEIGHTS, (v_norm_mix_g, v_norm_mlp_g, v_sb_w_qkv, v_sb_q_norm_g, v_sb_k_norm_g, v_sb_w_o, v_gm_w_in,
                            v_gm_v_norm_g, v_gm_w_s, v_gm_b_s, v_gm_w_o, v_ssd_w_in, v_ssd_conv_w, v_ssd_conv_b,
                            v_ssd_dt_bias, v_ssd_a_log, v_ssd_d, v_ssd_norm_g, v_ssd_w_o, v_mlp_w_in, v_mlp_w_out)))
    return _step(x, w, loss_target, m, v)
```

```python
import functools
import math

import jax
import jax.numpy as jnp
from jax import lax
from jax.experimental import pallas as pl
from jax.experimental.pallas import tpu as pltpu

F32 = jnp.float32
BF16 = jnp.bfloat16
_MXU = jnp.bfloat16
_ACT = jnp.bfloat16
GRAD_DT = jnp.bfloat16
_VMEM_LIMIT = 56 * 1024 * 1024
EPS = 1e-6
LANES = 128
CHUNK = 128
SSD_HEAD_DIM = 64
SSD_STATE = 128
SSD_CONV = 4
ADAM_LR, ADAM_B1, ADAM_B2, ADAM_EPS, ADAM_WD, ADAM_STEP = 1e-3, 0.9, 0.999, 1e-8, 0.01, 10
MESH = pl.DeviceIdType.MESH

NN = (((1,), (0,)), ((), ()))
NT = (((1,), (1,)), ((), ()))
TN = (((0,), (0,)), ((), ()))


def _dot(a, b, dims=NN):
    return lax.dot_general(a.astype(_MXU), b.astype(_MXU), dims, preferred_element_type=F32)


def _params(sem):
    return pltpu.CompilerParams(dimension_semantics=sem, vmem_limit_bytes=_VMEM_LIMIT)


def _pick(n, *cands):
    for c in cands:
        if n % c == 0:
            return c
    return n


def matmul(a, b, mode, out_dtypes, name, epilogue=None, extras=(), a_l=None, b_l=None):
    ash, bsh = a.shape[-2:], b.shape[-2:]
    if mode == "nn":
        (M, K), (K2, N) = ash, bsh
    elif mode == "nt":
        (M, K), (N, K2) = ash, bsh
    else:
        (K, M), (K2, N) = ash, bsh
    assert K == K2, (mode, a.shape, b.shape)
    tm = _pick(M, 512, 256, 128)
    tn = _pick(N, 1024, 512, 256, 128)
    tk = _pick(K, 2048, 1024, 512, 256, 128)
    nk = K // tk
    dims = {"nn": NN, "nt": NT, "tn": TN}[mode]
    single = not isinstance(out_dtypes, (tuple, list))
    odt = (out_dtypes,) if single else tuple(out_dtypes)
    n_ex = len(extras)

    def lead(l, spec_shape, imap):
        if l is None:
            return pl.BlockSpec(spec_shape, imap)
        return pl.BlockSpec((None,) + spec_shape, lambda i, j, k: (l,) + imap(i, j, k))

    if mode == "tn":
        a_spec = lead(a_l, (tk, tm), lambda i, j, k: (k, i))
    else:
        a_spec = lead(a_l, (tm, tk), lambda i, j, k: (i, k))
    if mode == "nt":
        b_spec = lead(b_l, (tn, tk), lambda i, j, k: (j, k))
    else:
        b_spec = lead(b_l, (tk, tn), lambda i, j, k: (k, j))
    mn_spec = pl.BlockSpec((tm, tn), lambda i, j, k: (i, j))

    def body(*refs):
        a_ref, b_ref = refs[0], refs[1]
        ex = refs[2:2 + n_ex]
        outs = refs[2 + n_ex:2 + n_ex + len(odt)]
        acc = refs[-1]
        k = pl.program_id(2)

        @pl.when(k == 0)
        def _():
            acc[...] = jnp.zeros_like(acc)

        part = _dot(a_ref[...], b_ref[...], dims)

        @pl.when(k < nk - 1)
        def _():
            acc[...] += part

        @pl.when(k == nk - 1)
        def _():
            r = acc[...] + part
            res = (r,) if epilogue is None else epilogue(r, *[e[...] for e in ex])
            for o, v in zip(outs, res):
                o[...] = v.astype(o.dtype)

    out = pl.pallas_call(
        body,
        out_shape=tuple(jax.ShapeDtypeStruct((M, N), d) for d in odt),
        grid=(M // tm, N // tn, nk),
        in_specs=[a_spec, b_spec] + [mn_spec] * n_ex,
        out_specs=tuple(mn_spec for _ in odt),
        scratch_shapes=[pltpu.VMEM((tm, tn), F32)],
        compiler_params=_params(("parallel", "parallel", "arbitrary")),
        name=name,
    )(a, b, *extras)
    return out[0] if single else out


class Op:
    def __init__(self, arr, block, imap, kind="tile", grad=True, gshape=None, gimap=None):
        self.arr, self.block, self.imap, self.kind, self.grad = arr, block, imap, kind, grad
        self.gshape = gshape or arr.shape
        self.gimap = gimap or imap

    def spec(self):
        return pl.BlockSpec(self.block, self.imap)


def tmap(f, grid, ins, outs, name):
    n_in = len(ins)

    def body(*refs):
        res = f(*[r[...] for r in refs[:n_in]])
        for o, v in zip(refs[n_in:], res):
            o[...] = v.astype(o.dtype)

    return pl.pallas_call(
        body,
        out_shape=tuple(jax.ShapeDtypeStruct(s, d) for s, d, _, _ in outs),
        grid=grid,
        in_specs=[o.spec() for o in ins],
        out_specs=tuple(pl.BlockSpec(b, m) for _, _, b, m in outs),
        compiler_params=_params(("parallel", "parallel")),
        name=name,
    )(*[o.arr for o in ins])


def tmap_vjp(f, grid, ins, cts, name, grad_dtypes=None):
    n_in, n_ct = len(ins), len(cts)
    gidx = [i for i, o in enumerate(ins) if o.grad]
    gdt = grad_dtypes or {}

    def body(*refs):
        in_refs, ct_refs, g_refs = refs[:n_in], refs[n_in:n_in + n_ct], refs[n_in + n_ct:]
        vals = [r[...] for r in in_refs]

        def g_only(*diff):
            full = list(vals)
            for i, v in zip(gidx, diff):
                full[i] = v
            return f(*full)

        res, vjp = jax.vjp(g_only, *[vals[i].astype(F32) for i in gidx])
        grads = vjp(tuple(c[...].astype(r.dtype) for c, r in zip(ct_refs, res)))
        inner = pl.program_id(1)
        for i, g, gr in zip(gidx, grads, g_refs):
            if ins[i].kind == "tile":
                gr[...] = g.astype(gr.dtype)
            else:
                @pl.when(inner == 0)
                def _(gr=gr, g=g):
                    gr[...] = g.astype(gr.dtype)

                @pl.when(inner != 0)
                def _(gr=gr, g=g):
                    gr[...] += g.astype(gr.dtype)

    out_shape = tuple(jax.ShapeDtypeStruct(ins[i].gshape, gdt.get(i, F32)) for i in gidx)
    return pl.pallas_call(
        body,
        out_shape=out_shape,
        grid=grid,
        in_specs=[o.spec() for o in ins] + [o.spec() for o in cts],
        out_specs=tuple(pl.BlockSpec(ins[i].block, ins[i].gimap) for i in gidx),
        compiler_params=_params(("parallel", "arbitrary")),
        name=name,
    )(*[o.arr for o in ins], *[o.arr for o in cts])


def _rms(x, g):
    return x * lax.rsqrt(jnp.mean(x * x, axis=-1, keepdims=True) + EPS) * g


def _row_ops(arrs, tm, grads=None):
    grads = grads or [True] * len(arrs)
    return [Op(a, (tm, a.shape[1]), lambda o, i: (i, 0), "tile", g) for a, g in zip(arrs, grads)]


def _vec_op(v, grad=True):
    return Op(v, (1, v.shape[1]), lambda o, i: (0, 0), "param", grad)


def rmsnorm_fwd(h, g, name):
    T, D = h.shape
    tm = _pick(T, 512, 256, 128)
    f = lambda x, gg: (_rms(x, gg),)
    return tmap(f, (1, T // tm), _row_ops([h], tm) + [_vec_op(g)],
                [((T, D), _ACT, (tm, D), lambda o, i: (i, 0))], name)[0]


def rmsnorm_bwd(h, g, dhn, dres, name):
    T, D = h.shape
    tm = _pick(T, 512, 256, 128)
    f = lambda x, gg: (_rms(x, gg), x)
    return tmap_vjp(f, (1, T // tm), _row_ops([h], tm) + [_vec_op(g)], _row_ops([dhn, dres], tm), name)


def mlp_fwd(h, g_row, w_in, w_out, l, tag):
    hn = rmsnorm_fwd(h, g_row, f"{tag}_norm")
    a, r2 = matmul(hn, w_in, "nn", (F32, _ACT), f"{tag}_in", b_l=l,
                   epilogue=lambda acc: (acc, jnp.square(jnp.maximum(acc, 0.0))))
    out = matmul(r2, w_out, "nn", F32, f"{tag}_out", b_l=l, epilogue=lambda acc, hh: (acc + hh,), extras=(h,))
    return out, (h, hn, a, r2)


def mlp_bwd(dout, saved, g_row, w_in, w_out, l, tag):
    h, hn, a, r2 = saved
    da = matmul(dout, w_out, "nt", _ACT, f"{tag}_dact", b_l=l,
                epilogue=lambda acc, aa: (acc * (2.0 * jnp.maximum(aa, 0.0)),), extras=(a,))
    dw_out = matmul(r2, dout, "tn", GRAD_DT, f"{tag}_dwout")
    dw_in = matmul(hn, da, "tn", GRAD_DT, f"{tag}_dwin")
    dhn = matmul(da, w_in, "nt", F32, f"{tag}_dhn", b_l=l)
    dh, dg = rmsnorm_bwd(h, g_row, dhn, dout, f"{tag}_dnorm")
    return dh, dg, dw_in, dw_out


def _split3(x):
    hi = x.astype(BF16)
    r = x - hi.astype(F32)
    mid = r.astype(BF16)
    lo = (r - mid.astype(F32)).astype(BF16)
    return hi, mid, lo


def _cumdot(x, tri):
    return sum(lax.dot_general(p, tri, NN, preferred_element_type=F32) for p in _split3(x))


def _iotas():
    row = lax.broadcasted_iota(jnp.int32, (CHUNK, CHUNK), 0)
    col = lax.broadcasted_iota(jnp.int32, (CHUNK, CHUNK), 1)
    return row, col


_TQ = 256
_DEAD = -88.0


def _sb_block(q, kblk, q0, kb, scale, row, col):
    z = _dot(q, kblk, NT) * scale
    e = jnp.exp(-jnp.abs(z))
    den = 1.0 + e
    sp = jnp.maximum(z, 0.0) + jnp.log(den)
    mask = (col + kb * CHUNK) < (row + q0)
    lg = jnp.where(mask, -sp, 0.0)
    beta = jnp.where(z >= 0, 1.0, e) / den
    return z, mask, lg, beta


def _attn_iotas(tq):
    row = lax.broadcasted_iota(jnp.int32, (tq, CHUNK), 0)
    col = lax.broadcasted_iota(jnp.int32, (tq, CHUNK), 1)
    r2, c2 = _iotas()
    return row, col, r2, c2


def attn_fwd(qn, kn, v, name):
    T, W = qn.shape
    tq = _pick(T, _TQ, CHUNK)
    H, NQ, per = W // LANES, T // tq, tq // CHUNK
    assert T // CHUNK <= LANES
    scale = LANES ** -0.5

    def body(q_ref, k_ref, v_ref, o_ref, r_ref, acc_ref, run_ref):
        qi = pl.program_id(1)
        q = q_ref[...]
        row, col, r2, c2 = _attn_iotas(tq)
        suffix = (r2 >= c2).astype(_MXU)
        acc_ref[...] = jnp.zeros_like(acc_ref)
        run_ref[...] = jnp.zeros_like(run_ref)
        r_ref[...] = jnp.full(r_ref.shape, -1e30, F32)

        def step(carry):
            kb, _ = carry
            off = pl.multiple_of(kb * CHUNK, CHUNK)
            run = run_ref[...]
            z, mask, lg, _ = _sb_block(q, k_ref[pl.ds(off, CHUNK), :], qi * tq, kb, scale, row, col)
            r_ref[...] = jnp.where(col == kb, run, r_ref[...])
            cl = _cumdot(lg, suffix) + run
            a = jnp.exp(jnp.where(mask, z + cl, -1e30))
            acc_ref[...] += _dot(a, v_ref[pl.ds(off, CHUNK), :])
            run = run + jnp.sum(lg, axis=1, keepdims=True)
            run_ref[...] = run
            return kb - 1, jnp.max(run) > _DEAD

        lax.while_loop(lambda c: (c[0] >= 0) & c[1], step, ((qi + 1) * per - 1, True))
        o_ref[...] = acc_ref[...].astype(o_ref.dtype)

    qspec = pl.BlockSpec((tq, LANES), lambda h, i: (i, h))
    kvspec = pl.BlockSpec((T, LANES), lambda h, i: (0, h))
    return pl.pallas_call(
        body,
        out_shape=(jax.ShapeDtypeStruct((T, W), _ACT), jax.ShapeDtypeStruct((H, T, LANES), F32)),
        grid=(H, NQ),
        in_specs=[qspec, kvspec, kvspec],
        out_specs=(qspec, pl.BlockSpec((None, tq, LANES), lambda h, i: (h, i, 0))),
        scratch_shapes=[pltpu.VMEM((tq, LANES), F32), pltpu.VMEM((tq, 1), F32)],
        compiler_params=_params(("parallel", "parallel")),
        name=name,
    )(qn, kn, v)


def attn_bwd(qn, kn, v, do, rblk, name):
    T, W = qn.shape
    tq = _pick(T, _TQ, CHUNK)
    H, NQ, per = W // LANES, T // tq, tq // CHUNK
    scale = LANES ** -0.5

    def body(q_ref, k_ref, v_ref, do_ref, r_ref, dq_ref, dk_ref, dv_ref, g_ref):
        qi = pl.program_id(1)

        @pl.when(qi == 0)
        def _():
            dk_ref[...] = jnp.zeros_like(dk_ref)
            dv_ref[...] = jnp.zeros_like(dv_ref)

        q = q_ref[...]
        dout = do_ref[...]
        rt = r_ref[...]
        row, col, r2, c2 = _attn_iotas(tq)
        suffix = (r2 >= c2).astype(_MXU)
        prefix = (r2 <= c2).astype(_MXU)
        kend = (qi + 1) * per - 1
        lane = lax.broadcasted_iota(jnp.int32, (1, LANES), 1)
        unvisited = (jnp.max(rt, axis=0, keepdims=True) < -1e29) & (lane <= kend)
        start = jnp.sum(unvisited.astype(jnp.int32))

        dq_ref[...] = jnp.zeros_like(dq_ref)
        g_ref[...] = jnp.zeros_like(g_ref)

        @pl.loop(start, kend + 1)
        def _(kb):
            gsum = g_ref[...]
            off = pl.multiple_of(kb * CHUNK, CHUNK)
            kblk = k_ref[pl.ds(off, CHUNK), :]
            vblk = v_ref[pl.ds(off, CHUNK), :]
            z, mask, lg, beta = _sb_block(q, kblk, qi * tq, kb, scale, row, col)
            run = jnp.sum(jnp.where(col == kb, rt, 0.0), axis=1, keepdims=True)
            cl = _cumdot(lg, suffix) + run
            a = jnp.exp(jnp.where(mask, z + cl, -1e30))
            e = _dot(dout, vblk, NT) * a
            f = _cumdot(e, prefix) + gsum
            dz = jnp.where(mask, e - beta * f, 0.0) * scale
            dk_ref[pl.ds(off, CHUNK), :] += _dot(dz, q, TN)
            dv_ref[pl.ds(off, CHUNK), :] += _dot(a, dout, TN)
            dq_ref[...] += _dot(dz, kblk)
            g_ref[...] = gsum + jnp.sum(e, axis=1, keepdims=True)

    qspec = pl.BlockSpec((tq, LANES), lambda h, i: (i, h))
    kvspec = pl.BlockSpec((T, LANES), lambda h, i: (0, h))
    big = jax.ShapeDtypeStruct((T, W), F32)
    return pl.pallas_call(
        body,
        out_shape=(big, big, big),
        grid=(H, NQ),
        in_specs=[qspec, kvspec, kvspec, qspec, pl.BlockSpec((None, tq, LANES), lambda h, i: (h, i, 0))],
        out_specs=(qspec, kvspec, kvspec),
        scratch_shapes=[pltpu.VMEM((tq, 1), F32)],
        compiler_params=_params(("parallel", "arbitrary")),
        name=name,
    )(qn, kn, v, do, rblk)


def _qk_ops(qkv, qg, kg, tm, grad):
    T, W3 = qkv.shape
    H, NT_ = W3 // (3 * LANES), T // tm
    W = H * LANES

    def part(p):
        return Op(qkv, (tm, LANES), lambda o, n: (lax.rem(n, NT_), p * H + lax.div(n, NT_)), "tile", grad,
                  gshape=(T, W), gimap=lambda o, n: (lax.rem(n, NT_), lax.div(n, NT_)))

    vec = lambda g: Op(g, (1, LANES), lambda o, n: (0, 0), "param", grad)
    return [part(0), part(1), part(2), vec(qg), vec(kg)], (1, H * NT_), H, NT_, W


def _qk_f(q, k, v, qg, kg):
    return _rms(q, qg), _rms(k, kg), v


def qknorm_fwd(qkv, qg, kg, name):
    T = qkv.shape[0]
    tm = _pick(T, 512, 256, 128)
    ins, grid, H, NT_, W = _qk_ops(qkv, qg, kg, tm, False)
    out = ((T, W), _ACT, (tm, LANES), lambda o, n: (lax.rem(n, NT_), lax.div(n, NT_)))
    return tmap(_qk_f, grid, ins, [out, out, out], name)


def qknorm_bwd(qkv, qg, kg, dq, dk, dv, name):
    T = qkv.shape[0]
    tm = _pick(T, 512, 256, 128)
    ins, grid, H, NT_, W = _qk_ops(qkv, qg, kg, tm, True)
    cts = [Op(c, (tm, LANES), lambda o, n: (lax.rem(n, NT_), lax.div(n, NT_))) for c in (dq, dk, dv)]
    return tmap_vjp(_qk_f, grid, ins, cts, name, grad_dtypes={0: _ACT, 1: _ACT, 2: _ACT})


def sb_fwd(h, g_row, w_qkv, qg, kg, w_o, l, tag):
    hn = rmsnorm_fwd(h, g_row, f"{tag}_norm")
    qkv = matmul(hn, w_qkv, "nn", F32, f"{tag}_qkv", b_l=l)
    qn, kn, v = qknorm_fwd(qkv, qg, kg, f"{tag}_qknorm")
    o, rblk = attn_fwd(qn, kn, v, f"{tag}_attn")
    out = matmul(o, w_o, "nn", F32, f"{tag}_wo", b_l=l, epilogue=lambda acc, hh: (acc + hh,), extras=(h,))
    return out, (h, hn, qkv, qn, kn, v, o, rblk)


def sb_bwd(dout, saved, g_row, w_qkv, qg, kg, w_o, l, tag):
    h, hn, qkv, qn, kn, v, o, rblk = saved
    do = matmul(dout, w_o, "nt", _ACT, f"{tag}_do", b_l=l)
    dw_o = matmul(o, dout, "tn", GRAD_DT, f"{tag}_dwo")
    dqn, dkn, dv = attn_bwd(qn, kn, v, do, rblk, f"{tag}_dattn")
    dq, dk, dvv, dqg, dkg = qknorm_bwd(qkv, qg, kg, dqn, dkn, dv, f"{tag}_dqknorm")
    dqkv = jnp.concatenate([dq, dk, dvv], axis=1)
    dw_qkv = matmul(hn, dqkv, "tn", GRAD_DT, f"{tag}_dwqkv")
    dhn = matmul(dqkv, w_qkv, "nt", F32, f"{tag}_dhn", b_l=l)
    dh, dg = rmsnorm_bwd(h, g_row, dhn, dout, f"{tag}_dnorm")
    return dh, dg, dw_qkv, dqg, dkg, dw_o


@functools.partial(jax.custom_vjp, nondiff_argnums=(2,))
def _dotv(a, b, mode):
    return _dot(a, b, {"nn": NN, "nt": NT, "tn": TN}[mode])


def _dotv_fwd(a, b, mode):
    return _dotv(a, b, mode), (a, b)


def _dotv_bwd(mode, res, g):
    a, b = res
    if mode == "nn":
        return _dot(g, b, NT), _dot(a, g, TN)
    if mode == "nt":
        return _dot(g, b, NN), _dot(g, a, TN)
    return _dot(b, g, NT), _dot(a, g, NN)


_dotv.defvjp(_dotv_fwd, _dotv_bwd)


def _gelu(x):
    return 0.5 * x * (1.0 + lax.erf(x * (2.0 ** -0.5)))


def _gm1_f(au, av, vg):
    return _gelu(au), _rms(_gelu(av), vg)


def _gm1_ops(a, vg, tm, grad):
    T, W2 = a.shape
    W = W2 // 2
    part = lambda p: Op(a, (tm, W), lambda o, i: (i, p), "tile", grad, gshape=(T, W), gimap=lambda o, i: (i, 0))
    return [part(0), part(1), _vec_op(vg, grad)], (1, T // tm), W


def _gm2_f(u, vn, ws, bcol):
    row, col = _iotas()
    w = jnp.where(row >= col, ws, 0.0)
    return (u * (_dotv(w, vn, "nn") + bcol),)


def _gm2_ops(u, vn, ws, bcol, grad):
    T, W = u.shape
    blk = lambda x: Op(x, (CHUNK, LANES), lambda g, c: (c, g), "tile", grad)
    return [blk(u), blk(vn),
            Op(ws, (None, CHUNK, CHUNK), lambda g, c: (g, 0, 0), "param", grad),
            Op(bcol, (None, CHUNK, 1), lambda g, c: (g, 0, 0), "param", grad)], (W // LANES, T // CHUNK)


def gm_fwd(h, g_row, w_in, vg, ws, bcol, w_o, l, tag):
    T = h.shape[0]
    tm = _pick(T, 256, 128)
    hn = rmsnorm_fwd(h, g_row, f"{tag}_norm")
    a = matmul(hn, w_in, "nn", F32, f"{tag}_in", b_l=l)
    ins, grid, W = _gm1_ops(a, vg, tm, False)
    rows = lambda dt: ((T, W), dt, (tm, W), lambda o, i: (i, 0))
    u, vn = tmap(_gm1_f, grid, ins, [rows(F32), rows(_ACT)], f"{tag}_act")
    ins2, grid2 = _gm2_ops(u, vn, ws, bcol, False)
    y = tmap(_gm2_f, grid2, ins2, [((T, W), _ACT, (CHUNK, LANES), lambda g, c: (c, g))], f"{tag}_mix")[0]
    out = matmul(y, w_o, "nn", F32, f"{tag}_wo", b_l=l, epilogue=lambda acc, hh: (acc + hh,), extras=(h,))
    return out, (h, hn, a, u, vn, y)


def gm_bwd(dout, saved, g_row, w_in, vg, ws, bcol, w_o, l, tag):
    h, hn, a, u, vn, y = saved
    T = h.shape[0]
    tm = _pick(T, 256, 128)
    dy = matmul(dout, w_o, "nt", F32, f"{tag}_dy", b_l=l)
    dw_o = matmul(y, dout, "tn", GRAD_DT, f"{tag}_dwo")
    ins2, grid2 = _gm2_ops(u, vn, ws, bcol, True)
    du, dvn, dws, dbcol = tmap_vjp(_gm2_f, grid2, ins2, [Op(dy, (CHUNK, LANES), lambda g, c: (c, g))], f"{tag}_dmix")
    ins, grid, W = _gm1_ops(a, vg, tm, True)
    dau, dav, dvg = tmap_vjp(_gm1_f, grid, ins, _row_ops([du, dvn], tm), f"{tag}_dact", grad_dtypes={0: _ACT, 1: _ACT})
    da = jnp.concatenate([dau, dav], axis=1)
    dw_in = matmul(hn, da, "tn", GRAD_DT, f"{tag}_dwin")
    dhn = matmul(da, w_in, "nt", F32, f"{tag}_dhn", b_l=l)
    dh, dg = rmsnorm_bwd(h, g_row, dhn, dout, f"{tag}_dnorm")
    return dh, dg, dw_in, dvg, dws, dbcol, dw_o


@jax.custom_vjp
def _softplus(x):
    return jnp.maximum(x, 0.0) + jnp.log(1.0 + jnp.exp(-jnp.abs(x)))


_softplus.defvjp(lambda x: (_softplus(x), x), lambda x, g: (g * lax.logistic(x),))


def _silu(x):
    return x * lax.logistic(x)


def _shift_impl(x, s, down):
    n = x.shape[0]
    r = lax.broadcasted_iota(jnp.int32, x.shape, 0)
    if down:
        return jnp.where(r >= s, pltpu.roll(x, s, 0), 0.0)
    return jnp.where(r < n - s, pltpu.roll(x, n - s, 0), 0.0)


@functools.partial(jax.custom_vjp, nondiff_argnums=(1,))
def _shift_down(x, s):
    return _shift_impl(x, s, True)


_shift_down.defvjp(lambda x, s: (_shift_impl(x, s, True), None), lambda s, _, g: (_shift_impl(g, s, False),))


def _conv_f(x, w, b):
    k_id = lax.broadcasted_iota(jnp.int32, w.shape, 0)
    y = b + jnp.sum(jnp.where(k_id == SSD_CONV - 1, w, 0.0), axis=0, keepdims=True) * x
    for k in range(SSD_CONV - 1):
        wk = jnp.sum(jnp.where(k_id == k, w, 0.0), axis=0, keepdims=True)
        y = y + wk * _shift_down(x, SSD_CONV - 1 - k)
    return (_silu(y),)


def _conv_ops(zx, conv_w, conv_b, wi, grad):
    T = zx.shape[0]
    cd = conv_w.shape[1]
    cw = LANES
    off = wi // cw
    return [Op(zx, (T, cw), lambda o, j: (0, off + j), "tile", grad, gshape=(T, cd), gimap=lambda o, j: (0, j)),
            Op(conv_w, (SSD_CONV, cw), lambda o, j: (0, j), "tile", grad),
            Op(conv_b, (1, cw), lambda o, j: (0, j), "tile", grad)], (1, cd // cw), (T, cw)


def _dt_f(dtr, bias):
    return (_softplus(dtr + bias),)


def _cumdot_left(tri, x):
    return sum(lax.dot_general(tri, p, NN, preferred_element_type=F32) for p in _split3(x))


@jax.custom_vjp
def _cumsum_rows(x):
    row, col = _iotas()
    return _cumdot_left((row >= col).astype(_MXU), x)


def _cumsum_rows_bwd(_, g):
    row, col = _iotas()
    return (_cumdot_left((row <= col).astype(_MXU), g),)


_cumsum_rows.defvjp(lambda x: (_cumsum_rows(x), None), _cumsum_rows_bwd)


def _ssd_chunk(xps, dt, bm, cm, sps, alog, hid_base):
    row, col = _iotas()
    half = SSD_HEAD_DIM
    rcol = lax.broadcasted_iota(jnp.int32, (CHUNK, 1), 0)
    colpick = lambda m, hid: jnp.sum(jnp.where(col == hid, m, 0.0), axis=1, keepdims=True)
    rowpick = lambda m, hid: jnp.sum(jnp.where(row == hid, m, 0.0), axis=0, keepdims=True)
    last = lambda v: jnp.sum(jnp.where(rcol == CHUNK - 1, v, 0.0), axis=0, keepdims=True)
    acum = _cumsum_rows(dt * (-jnp.exp(alog)))
    acum_t = acum.T
    cb = _dotv(cm, bm, "nt")
    tri = row >= col
    lo = col < half
    ys, snews = [], []
    for p, (xp, sp) in enumerate(zip(xps, sps)):
        h0, h1 = hid_base + 2 * p, hid_base + 2 * p + 1
        ac0, ac1 = colpick(acum, h0), colpick(acum, h1)
        m0 = cb * jnp.exp(jnp.where(tri, ac0 - rowpick(acum_t, h0), -1e30))
        m1 = cb * jnp.exp(jnp.where(tri, ac1 - rowpick(acum_t, h1), -1e30))
        xs = xp * jnp.where(lo, colpick(dt, h0), colpick(dt, h1))
        ydiag = jnp.where(lo, _dotv(m0, xs, "nn"), _dotv(m1, xs, "nn"))
        yoff = jnp.where(lo, jnp.exp(ac0), jnp.exp(ac1)) * _dotv(cm, sp, "nt")
        al0, al1 = last(ac0), last(ac1)
        xsd = xs * jnp.where(lo, jnp.exp(al0 - ac0), jnp.exp(al1 - ac1))
        snew = jnp.where(row < half, jnp.exp(al0), jnp.exp(al1)) * sp + _dotv(xsd, bm, "tn")
        ys.append(ydiag + yoff)
        snews.append(snew)
    return ys, snews


def _ssd_dims(xact, wi):
    T, cd = xact.shape
    G = (cd - wi) // (2 * SSD_STATE)
    hpg = wi // SSD_HEAD_DIM // G
    assert hpg % 2 == 0 and SSD_STATE == LANES
    return T, G, hpg, hpg // 2, T // CHUNK, wi // G


def ssd_scan_fwd(xact, dt, alog, wi, name):
    T, G, hpg, NP, NC, gw = _ssd_dims(xact, wi)
    bo, co = wi // LANES, wi // LANES + G

    def body(x_ref, b_ref, c_ref, dt_ref, al_ref, y_ref, st_ref, s_ref):
        g, c = pl.program_id(0), pl.program_id(1)

        @pl.when(c == 0)
        def _():
            s_ref[...] = jnp.zeros_like(s_ref)

        st_ref[...] = s_ref[...]
        xps = [x_ref[:, p * LANES:(p + 1) * LANES] for p in range(NP)]
        sps = [s_ref[p] for p in range(NP)]
        ys, snews = _ssd_chunk(xps, dt_ref[...], b_ref[...], c_ref[...], sps, al_ref[...], g * hpg)
        for p in range(NP):
            y_ref[:, p * LANES:(p + 1) * LANES] = ys[p]
            s_ref[p] = snews[p]

    return pl.pallas_call(
        body,
        out_shape=(jax.ShapeDtypeStruct((T, wi), F32), jax.ShapeDtypeStruct((G, NC, NP, LANES, SSD_STATE), F32)),
        grid=(G, NC),
        in_specs=[pl.BlockSpec((CHUNK, gw), lambda g, c: (c, g)),
                  pl.BlockSpec((CHUNK, LANES), lambda g, c: (c, bo + g)),
                  pl.BlockSpec((CHUNK, LANES), lambda g, c: (c, co + g)),
                  pl.BlockSpec((CHUNK, LANES), lambda g, c: (c, 0)),
                  pl.BlockSpec((1, LANES), lambda g, c: (0, 0))],
        out_specs=(pl.BlockSpec((CHUNK, gw), lambda g, c: (c, g)),
                   pl.BlockSpec((None, None, NP, LANES, SSD_STATE), lambda g, c: (g, c, 0, 0, 0))),
        scratch_shapes=[pltpu.VMEM((NP, LANES, SSD_STATE), F32)],
        compiler_params=_params(("parallel", "arbitrary")),
        name=name,
    )(xact, xact, xact, dt, alog)


def ssd_scan_bwd(xact, dt, alog, states, dy, wi, name):
    T, G, hpg, NP, NC, gw = _ssd_dims(xact, wi)
    bo, co = wi // LANES, wi // LANES + G
    rev = lambda c: NC - 1 - c

    def body(x_ref, b_ref, c_ref, dt_ref, al_ref, st_ref, dy_ref, dx_ref, db_ref, dc_ref, ddt_ref, dal_ref, ds_ref):
        g, c = pl.program_id(0), pl.program_id(1)

        @pl.when(c == 0)
        def _():
            ds_ref[...] = jnp.zeros_like(ds_ref)

        xps = [x_ref[:, p * LANES:(p + 1) * LANES] for p in range(NP)]
        sps = [st_ref[p] for p in range(NP)]
        f = lambda xps_, dt_, bm_, cm_, sps_, al_: _ssd_chunk(xps_, dt_, bm_, cm_, sps_, al_, g * hpg)
        _, vjp = jax.vjp(f, xps, dt_ref[...], b_ref[...], c_ref[...], sps, al_ref[...])
        dys = [dy_ref[:, p * LANES:(p + 1) * LANES] for p in range(NP)]
        dxps, ddt, dbm, dcm, dsps, dal = vjp((dys, [ds_ref[p] for p in range(NP)]))
        for p in range(NP):
            dx_ref[:, p * LANES:(p + 1) * LANES] = dxps[p]
            ds_ref[p] = dsps[p]
        db_ref[...] = dbm
        dc_ref[...] = dcm
        ddt_ref[...] = ddt

        @pl.when(c == 0)
        def _():
            dal_ref[...] = dal

        @pl.when(c != 0)
        def _():
            dal_ref[...] += dal

    gb = G * SSD_STATE
    return pl.pallas_call(
        body,
        out_shape=(jax.ShapeDtypeStruct((T, wi), F32), jax.ShapeDtypeStruct((T, gb), F32), jax.ShapeDtypeStruct((T, gb), F32),
                   jax.ShapeDtypeStruct((G, T, LANES), F32), jax.ShapeDtypeStruct((G, 1, LANES), F32)),
        grid=(G, NC),
        in_specs=[pl.BlockSpec((CHUNK, gw), lambda g, c: (rev(c), g)),
                  pl.BlockSpec((CHUNK, LANES), lambda g, c: (rev(c), bo + g)),
                  pl.BlockSpec((CHUNK, LANES), lambda g, c: (rev(c), co + g)),
                  pl.BlockSpec((CHUNK, LANES), lambda g, c: (rev(c), 0)),
                  pl.BlockSpec((1, LANES), lambda g, c: (0, 0)),
                  pl.BlockSpec((None, None, NP, LANES, SSD_STATE), lambda g, c: (g, rev(c), 0, 0, 0)),
                  pl.BlockSpec((CHUNK, gw), lambda g, c: (rev(c), g))],
        out_specs=(pl.BlockSpec((CHUNK, gw), lambda g, c: (rev(c), g)),
                   pl.BlockSpec((CHUNK, LANES), lambda g, c: (rev(c), g)),
                   pl.BlockSpec((CHUNK, LANES), lambda g, c: (rev(c), g)),
                   pl.BlockSpec((None, CHUNK, LANES), lambda g, c: (g, rev(c), 0)),
                   pl.BlockSpec((None, 1, LANES), lambda g, c: (g, 0, 0))),
        scratch_shapes=[pltpu.VMEM((NP, LANES, SSD_STATE), F32)],
        compiler_params=_params(("parallel", "arbitrary")),
        name=name,
    )(xact, xact, xact, dt, alog, states, dy)


def _post_f(y, x, z, dcol, ng):
    return (_rms((y + dcol * x) * _silu(z), ng),)


def _post_ops(yssd, xact, zx, dcol, ng, G, tm, grad):
    T, wi = yssd.shape
    gw = wi // G
    blk = lambda a: Op(a, (tm, gw), lambda g, i: (i, g), "tile", grad, gshape=(T, wi))
    vec = lambda v: Op(v, (1, gw), lambda g, i: (0, g), "param", grad)
    return [blk(yssd), blk(xact), blk(zx), vec(dcol), vec(ng)], (G, T // tm), gw


def ssd_fwd(h, g_row, w_zx, w_dt, conv_w, conv_b, dtb, alog, dcol, ng, w_o, tag):
    T = h.shape[0]
    wi = ng.shape[1]
    tm = _pick(T, 256, 128)
    hn = rmsnorm_fwd(h, g_row, f"{tag}_norm")
    zx = matmul(hn, w_zx, "nn", F32, f"{tag}_inzx")
    dtr = matmul(hn, w_dt, "nn", F32, f"{tag}_indt")
    ins, grid, blk = _conv_ops(zx, conv_w, conv_b, wi, False)
    cd = conv_w.shape[1]
    xact = tmap(_conv_f, grid, ins, [((T, cd), F32, blk, lambda o, j: (0, j))], f"{tag}_conv")[0]
    dt = tmap(_dt_f, (1, T // tm), _row_ops([dtr], tm) + [_vec_op(dtb)],
              [((T, LANES), F32, (tm, LANES), lambda o, i: (i, 0))], f"{tag}_dt")[0]
    yssd, states = ssd_scan_fwd(xact, dt, alog, wi, f"{tag}_scan")
    G = states.shape[0]
    ins, grid, gw = _post_ops(yssd, xact, zx, dcol, ng, G, tm, False)
    yn = tmap(_post_f, grid, ins, [((T, wi), _ACT, (tm, gw), lambda g, i: (i, g))], f"{tag}_post")[0]
    out = matmul(yn, w_o, "nn", F32, f"{tag}_wo", epilogue=lambda acc, hh: (acc + hh,), extras=(h,))
    return out, (h, hn, zx, dtr, xact, dt, yssd, states, yn)


def ssd_bwd(dout, saved, g_row, w_zx, w_dt, conv_w, conv_b, dtb, alog, dcol, ng, w_o, tag):
    h, hn, zx, dtr, xact, dt, yssd, states, yn = saved
    T = h.shape[0]
    wi = ng.shape[1]
    tm = _pick(T, 256, 128)
    G = states.shape[0]
    dyn = matmul(dout, w_o, "nt", F32, f"{tag}_dyn")
    dw_o = matmul(yn, dout, "tn", GRAD_DT, f"{tag}_dwo")
    ins, grid, gw = _post_ops(yssd, xact, zx, dcol, ng, G, tm, True)
    dyssd, dxi_skip, dz, ddcol, dng = tmap_vjp(_post_f, grid, ins, [Op(dyn, (tm, gw), lambda g, i: (i, g))],
                                                f"{tag}_dpost", grad_dtypes={2: _ACT})
    dxi, dbm, dcm, ddt_g, dalog_g = ssd_scan_bwd(xact, dt, alog, states, dyssd, wi, f"{tag}_dscan")
    dxact = jnp.concatenate([dxi + dxi_skip, dbm, dcm], axis=1)
    ddt = jnp.sum(ddt_g, axis=0)
    dalog = jnp.sum(dalog_g, axis=0)
    ins, grid, blk = _conv_ops(zx, conv_w, conv_b, wi, True)
    dxbc, dconv_w, dconv_b = tmap_vjp(_conv_f, grid, ins, [Op(dxact, blk, lambda o, j: (0, j))], f"{tag}_dconv",
                                      grad_dtypes={0: _ACT})
    ddtr, ddtb = tmap_vjp(_dt_f, (1, T // tm), _row_ops([dtr], tm) + [_vec_op(dtb)], _row_ops([ddt], tm), f"{tag}_ddt",
                          grad_dtypes={0: _ACT})
    dzx = jnp.concatenate([dz, dxbc], axis=1)
    dw_zx = matmul(hn, dzx, "tn", GRAD_DT, f"{tag}_dwzx")
    dw_dt = matmul(hn, ddtr, "tn", GRAD_DT, f"{tag}_dwdt")
    dhn1 = matmul(ddtr, w_dt, "nt", F32, f"{tag}_dhn1")
    dhn = matmul(dzx, w_zx, "nt", F32, f"{tag}_dhn", epilogue=lambda acc, e: (acc + e,), extras=(dhn1,))
    dh, dg = rmsnorm_bwd(h, g_row, dhn, dout, f"{tag}_dnorm")
    return dh, dg, dw_zx, dw_dt, dconv_w, dconv_b, ddtb, dalog, ddcol, dng, dw_o


def loss_head(y, target, name):
    T, D = y.shape
    tm = _pick(T, 512, 256, 128)

    def body(y_ref, t_ref, dy_ref, part_ref):
        d = y_ref[...] - t_ref[...]
        dy_ref[...] = d * (1.0 / D)
        s = jnp.sum(d * d, axis=0, keepdims=True) * (0.5 / D)

        @pl.when(pl.program_id(0) == 0)
        def _():
            part_ref[...] = s

        @pl.when(pl.program_id(0) != 0)
        def _():
            part_ref[...] += s

    rows = pl.BlockSpec((tm, D), lambda i: (i, 0))
    return pl.pallas_call(
        body,
        out_shape=(jax.ShapeDtypeStruct((T, D), F32), jax.ShapeDtypeStruct((1, D), F32)),
        grid=(T // tm,),
        in_specs=[rows, rows],
        out_specs=(rows, pl.BlockSpec((1, D), lambda i: (0, 0))),
        compiler_params=_params(("arbitrary",)),
        name=name,
    )(y, target)


def _row_tile(R, C, itemsize=4, target=1 << 20):
    for t in (1024, 512, 256, 128, 64, 32, 16, 8):
        if R % t == 0 and t * C * itemsize <= target:
            return t
    return R


def adamw(w, g, m, v, name):
    R, C = w.shape
    tr = _row_tile(R, C)
    c1 = 1.0 - ADAM_B1 ** ADAM_STEP
    c2 = 1.0 - ADAM_B2 ** ADAM_STEP

    def body(w_ref, g_ref, m_ref, v_ref, d_ref, nm_ref, nv_ref):
        gg = g_ref[...]
        nm = ADAM_B1 * m_ref[...] + (1.0 - ADAM_B1) * gg
        nv = ADAM_B2 * v_ref[...] + (1.0 - ADAM_B2) * jnp.square(gg)
        d_ref[...] = -ADAM_LR * ((nm / c1) / (jnp.sqrt(nv / c2) + ADAM_EPS) + ADAM_WD * w_ref[...])
        nm_ref[...] = nm
        nv_ref[...] = nv

    spec = pl.BlockSpec((tr, C), lambda i: (i, 0))
    sds = jax.ShapeDtypeStruct((R, C), F32)
    return pl.pallas_call(body, out_shape=(sds, sds, sds), grid=(R // tr,), in_specs=[spec] * 4, out_specs=(spec,) * 3,
                          compiler_params=_params(("parallel",)), name=name)(w, g, m, v)


def _xyc():
    return lax.axis_index("x"), lax.axis_index("y"), lax.axis_index("c")


_REL = ((1, 0), (0, 1), (1, 1))


def _flip(v, f):
    return 1 - v if f else v


def pair_sum(gfull, recv, name):
    _, _, R, C = gfull.shape
    tr = _row_tile(R, C, 2)

    def body(g_ref, p_ref, o_ref):
        c = lax.axis_index("c")
        o_ref[...] = (g_ref[c].astype(F32) + p_ref[...].astype(F32)).astype(o_ref.dtype)

    return pl.pallas_call(
        body,
        out_shape=jax.ShapeDtypeStruct((4, R, C), gfull.dtype),
        grid=(4, R // tr),
        in_specs=[pl.BlockSpec((None, 2, tr, C), lambda p, i: (p, 0, i, 0)), pl.BlockSpec((None, tr, C), lambda p, i: (p, i, 0))],
        out_specs=pl.BlockSpec((None, tr, C), lambda p, i: (p, i, 0)),
        compiler_params=_params(("parallel", "parallel")),
        name=name,
    )(gfull, recv)


def chip_sum(s, recv, name):
    _, R, C = s.shape
    tr = _row_tile(R, C, 2, 1 << 19)

    def body(c_ref, s_ref, p_ref, o_ref):
        x, y, _ = _xyc()
        acc = s_ref[2 * x + y].astype(F32)
        for r in range(3):
            acc = acc + p_ref[r].astype(F32)
        o_ref[...] = acc

    grid_spec = pltpu.PrefetchScalarGridSpec(
        num_scalar_prefetch=1,
        grid=(R // tr,),
        in_specs=[pl.BlockSpec((4, tr, C), lambda i, c: (0, i, 0)), pl.BlockSpec((3, tr, C), lambda i, c: (0, i, 0))],
        out_specs=pl.BlockSpec((None, tr, C), lambda i, c: (c[0], i, 0)),
    )
    return pl.pallas_call(
        body,
        out_shape=jax.ShapeDtypeStruct((2, R, C), F32),
        grid_spec=grid_spec,
        compiler_params=_params(("arbitrary",)),
        name=name,
    )(lax.axis_index("c").reshape(1).astype(jnp.int32), s, recv)


def sum8(g, name):
    _, R, C = g.shape

    def body(g_ref, o_ref):
        acc = g_ref[0]
        for d in range(1, 8):
            acc = acc + g_ref[d]
        o_ref[...] = acc

    return pl.pallas_call(body, out_shape=jax.ShapeDtypeStruct((R, C), F32), name=name,
                          compiler_params=pltpu.CompilerParams(vmem_limit_bytes=_VMEM_LIMIT))(g)


_ANY = pl.BlockSpec(memory_space=pl.ANY)


def _remote(src, dst, ssem, rsem, dev):
    return pltpu.make_async_remote_copy(src_ref=src, dst_ref=dst, send_sem=ssem, recv_sem=rsem, device_id=dev,
                                        device_id_type=MESH)


def gather_small(x, name):
    R, C = x.shape

    def body(x_ref, out_ref, send_sems, recv_sems, local_sem):
        x_, y_, c_ = _xyc()
        me, sibling = (x_, y_, c_), (x_, y_, 1 - c_)
        chips = [(_flip(x_, fx), _flip(y_, fy)) for fx, fy in _REL]
        slot = lambda px, py, pc: out_ref.at[4 * px + 2 * py + pc]

        def copy(k, block, to, src=None):
            return _remote(slot(*block) if src is None else src, slot(*block), send_sems.at[k], recv_sems.at[k], to)

        mine = pltpu.make_async_copy(x_ref, slot(*me), local_sem)
        mine.start()
        first = [copy(0, me, sibling, src=x_ref)]
        first += [copy(1 + j, me, (*chip, c_), src=x_ref) for j, chip in enumerate(chips)]
        for cp in first:
            cp.start()
        passed = [copy(4 + j, (*chip, c_), sibling) for j, chip in enumerate(chips)]
        for j, chip in enumerate(chips):
            copy(1 + j, (*chip, c_), me).wait_recv()
            passed[j].start()
        copy(0, sibling, me).wait_recv()
        for j, chip in enumerate(chips):
            copy(4 + j, (*chip, 1 - c_), me).wait_recv()
        for cp in first + passed:
            cp.wait_send()
        mine.wait()

    return pl.pallas_call(
        body,
        out_shape=jax.ShapeDtypeStruct((8, R, C), x.dtype),
        in_specs=[pl.BlockSpec(memory_space=pltpu.VMEM)],
        out_specs=pl.BlockSpec(memory_space=pltpu.VMEM),
        scratch_shapes=[pltpu.SemaphoreType.DMA((7,)), pltpu.SemaphoreType.DMA((7,)), pltpu.SemaphoreType.DMA],
        compiler_params=pltpu.CompilerParams(vmem_limit_bytes=_VMEM_LIMIT),
        name=name,
    )(x)


def gather_weights(halves, name):
    n = len(halves)

    def body(*refs):
        ins, outs = refs[:n], refs[n:2 * n]
        send_sems, recv_sems, local_sems = refs[2 * n:]
        x_, y_, c_ = _xyc()
        sibling = (x_, y_, 1 - c_)
        chips = [(_flip(x_, fx), _flip(y_, fy)) for fx, fy in _REL]
        mine = 2 * x_ + y_
        started, local = [], []
        for i in range(n):
            own, dst = ins[i].at[c_], outs[i].at[mine, c_]
            lc = pltpu.make_async_copy(own, dst, local_sems.at[i])
            lc.start()
            local.append(lc)
            for r, chip in enumerate(chips):
                started.append(_remote(own, dst, send_sems.at[7 * i + r], recv_sems.at[7 * i + r], (*chip, c_)))
                started[-1].start()
            started.append(_remote(own, dst, send_sems.at[7 * i + 3], recv_sems.at[7 * i + 3], sibling))
            started[-1].start()
        for i in range(n):
            for r, (px, py) in enumerate(chips):
                blk = outs[i].at[2 * px + py, c_]
                _remote(blk, blk, send_sems.at[7 * i + r], recv_sems.at[7 * i + r], sibling).wait_recv()
                started.append(_remote(blk, blk, send_sems.at[7 * i + 4 + r], recv_sems.at[7 * i + 4 + r], sibling))
                started[-1].start()
        for i in range(n):
            blk = outs[i].at[mine, 1 - c_]
            _remote(blk, blk, send_sems.at[7 * i + 3], recv_sems.at[7 * i + 3], sibling).wait_recv()
            for r, (px, py) in enumerate(chips):
                blk = outs[i].at[2 * px + py, 1 - c_]
                _remote(blk, blk, send_sems.at[7 * i + 4 + r], recv_sems.at[7 * i + 4 + r], sibling).wait_recv()
        for cp in started:
            cp.wait_send()
        for lc in local:
            lc.wait()

    return pl.pallas_call(
        body,
        out_shape=tuple(jax.ShapeDtypeStruct((4,) + h.shape, h.dtype) for h in halves),
        in_specs=[_ANY] * n,
        out_specs=tuple(_ANY for _ in halves),
        scratch_shapes=[pltpu.SemaphoreType.DMA((7 * n,)), pltpu.SemaphoreType.DMA((7 * n,)), pltpu.SemaphoreType.DMA((n,))],
        name=name,
    )(*halves)


def swap_halves(gfulls, name):
    n = len(gfulls)

    def body(*refs):
        ins, outs = refs[:n], refs[n:2 * n]
        send_sems, recv_sems = refs[2 * n:]
        x_, y_, c_ = _xyc()
        sibling = (x_, y_, 1 - c_)
        started = []
        for i in range(n):
            for p in range(4):
                started.append(_remote(ins[i].at[p, 1 - c_], outs[i].at[p], send_sems.at[4 * i + p], recv_sems.at[4 * i + p], sibling))
                started[-1].start()
        for cp in started:
            cp.wait()

    return pl.pallas_call(
        body,
        out_shape=tuple(jax.ShapeDtypeStruct((4,) + g.shape[2:], g.dtype) for g in gfulls),
        in_specs=[_ANY] * n,
        out_specs=tuple(_ANY for _ in gfulls),
        scratch_shapes=[pltpu.SemaphoreType.DMA((4 * n,)), pltpu.SemaphoreType.DMA((4 * n,))],
        name=name,
    )(*gfulls)


def scatter_chips(sums, name):
    n = len(sums)

    def body(*refs):
        ins, outs = refs[:n], refs[n:2 * n]
        send_sems, recv_sems = refs[2 * n:]
        x_, y_, c_ = _xyc()
        chips = [(_flip(x_, fx), _flip(y_, fy)) for fx, fy in _REL]
        started = []
        for i in range(n):
            for r, (px, py) in enumerate(chips):
                started.append(_remote(ins[i].at[2 * px + py], outs[i].at[r], send_sems.at[3 * i + r], recv_sems.at[3 * i + r], (px, py, c_)))
                started[-1].start()
        for cp in started:
            cp.wait()

    return pl.pallas_call(
        body,
        out_shape=tuple(jax.ShapeDtypeStruct((3,) + s.shape[1:], s.dtype) for s in sums),
        in_specs=[_ANY] * n,
        out_specs=tuple(_ANY for _ in sums),
        scratch_shapes=[pltpu.SemaphoreType.DMA((3 * n,)), pltpu.SemaphoreType.DMA((3 * n,))],
        name=name,
    )(*sums)


_JOIN_CHUNK_BYTES = 4 << 20


def join_halves(bufs, name):
    n = len(bufs)
    chunks = []
    for b in bufs:
        _, R, C = b.shape
        k = 1
        while k < 8 and R % (2 * k * 8) == 0 and R * C * 4 // k > _JOIN_CHUNK_BYTES:
            k *= 2
        chunks.append(k)
    base = [sum(chunks[:i]) for i in range(n)]
    total = sum(chunks)

    def body(*refs):
        outs = refs[n:2 * n]
        send_sems, recv_sems = refs[2 * n:]
        x_, y_, c_ = _xyc()
        sibling = (x_, y_, 1 - c_)
        sent = []
        for i in range(n):
            rc = bufs[i].shape[1] // chunks[i]
            for q in range(chunks[i]):
                blk = outs[i].at[c_, pl.ds(q * rc, rc)]
                sent.append(_remote(blk, blk, send_sems.at[base[i] + q], recv_sems.at[base[i] + q], sibling))
                sent[-1].start()
        for i in range(n):
            rc = bufs[i].shape[1] // chunks[i]
            for q in range(chunks[i]):
                blk = outs[i].at[1 - c_, pl.ds(q * rc, rc)]
                _remote(blk, blk, send_sems.at[base[i] + q], recv_sems.at[base[i] + q], sibling).wait_recv()
        for cp in sent:
            cp.wait_send()

    return pl.pallas_call(
        body,
        out_shape=tuple(jax.ShapeDtypeStruct(b.shape, b.dtype) for b in bufs),
        in_specs=[_ANY] * n,
        out_specs=tuple(_ANY for _ in bufs),
        input_output_aliases={i: i for i in range(n)},
        scratch_shapes=[pltpu.SemaphoreType.DMA((total,)), pltpu.SemaphoreType.DMA((total,))],
        name=name,
    )(*bufs)


def _halves(a):
    return a.reshape(2, -1, a.shape[-1])


def _cols_to_full(g):
    _, L, K, n = g.shape
    return g.transpose(1, 2, 0, 3).reshape(L, K, 4 * n)


def _full_to_cols(w):
    L, K, N = w.shape
    return w.reshape(L, K, 4, N // 4).transpose(2, 0, 1, 3)


def _rows_to_full(g):
    _, L, k, N = g.shape
    return g.transpose(1, 0, 2, 3).reshape(L, 4 * k, N)


def _full_to_rows(w):
    L, K, N = w.shape
    return w.reshape(L, 4, K // 4, N).transpose(1, 0, 2, 3)


_BIG = ("sb_w_qkv", "sb_w_o", "gm_w_in", "gm_w_o", "ssd_w_in", "ssd_w_o", "mlp_w_in", "mlp_w_out")
_COLS = ("sb_w_qkv", "gm_w_in", "ssd_w_in", "mlp_w_in")
_SMALL = ("norm_mix_g", "norm_mlp_g", "sb_q_norm_g", "sb_k_norm_g", "gm_v_norm_g", "gm_w_s", "gm_b_s",
          "ssd_conv_w", "ssd_conv_b", "ssd_dt_bias", "ssd_a_log", "ssd_d", "ssd_norm_g")
_SMALL_SHARDED = ("ssd_conv_w", "ssd_conv_b", "ssd_norm_g")
_WEIGHTS = ("norm_mix_g", "norm_mlp_g", "sb_w_qkv", "sb_q_norm_g", "sb_k_norm_g", "sb_w_o", "gm_w_in", "gm_v_norm_g",
            "gm_w_s", "gm_b_s", "gm_w_o", "ssd_w_in", "ssd_conv_w", "ssd_conv_b", "ssd_dt_bias", "ssd_a_log", "ssd_d",
            "ssd_norm_g", "ssd_w_o", "mlp_w_in", "mlp_w_out")


def _pack(arrs):
    flat = jnp.concatenate([a.reshape(-1).astype(F32) for a in arrs])
    n = flat.shape[0]
    tot = -(-n // (8 * LANES)) * 8 * LANES
    return jnp.pad(flat, (0, tot - n)).reshape(-1, LANES)


def _unpack(buf, shapes):
    flat = buf.reshape(-1)
    out, o = [], 0
    for s in shapes:
        n = math.prod(s)
        out.append(flat[o:o + n].reshape(s))
        o += n
    return out


def _step(x, w, target, m, v):
    depth = w["norm_mix_g"].shape[0]
    xc, yc, _ = _xyc()
    chip = 2 * xc + yc
    Hs = w["ssd_dt_bias"].shape[1]
    wi = 4 * w["ssd_norm_g"].shape[1]
    cd = 4 * w["ssd_conv_b"].shape[1]

    gathered = gather_weights([_halves(w[k].astype(_MXU)) for k in _BIG], "gather_weights")
    full = {}
    for k, g in zip(_BIG, gathered):
        g = g.reshape((4,) + w[k].shape)
        full[k] = _cols_to_full(g) if k in _COLS else _rows_to_full(g)
    w_zx = full["ssd_w_in"][0][:, :wi + cd]
    w_dt = jnp.pad(full["ssd_w_in"][0][:, wi + cd:], ((0, 0), (0, LANES - Hs)))
    small_sh = gather_small(_pack([w[k] for k in _SMALL_SHARDED]), "gather_small_weights")
    parts = [_unpack(small_sh[2 * j], [w[k].shape for k in _SMALL_SHARDED]) for j in range(4)]
    conv_w = jnp.concatenate([p[0][0] for p in parts], axis=1)
    conv_b = jnp.concatenate([p[1] for p in parts], axis=1)
    ssd_ng = jnp.concatenate([p[2] for p in parts], axis=1)
    padh = lambda a: jnp.pad(a, ((0, 0), (0, LANES - Hs)))
    dtb, alog = padh(w["ssd_dt_bias"]), padh(w["ssd_a_log"])
    dcol = jnp.repeat(w["ssd_d"], SSD_HEAD_DIM, axis=1)
    bcol = w["gm_b_s"][0][:, :, None]

    h = x[0]
    tape = []
    for i in range(depth):
        kind, j = i % 3, i // 3
        gmix = w["norm_mix_g"][i:i + 1]
        if kind == 0:
            args = (gmix, full["sb_w_qkv"], w["sb_q_norm_g"][j:j + 1], w["sb_k_norm_g"][j:j + 1], full["sb_w_o"], j, f"sb{j}")
            h, sv = sb_fwd(h, *args)
        elif kind == 1:
            args = (gmix, full["gm_w_in"], w["gm_v_norm_g"][j:j + 1], w["gm_w_s"][j], bcol, full["gm_w_o"], j, f"gm{j}")
            h, sv = gm_fwd(h, *args)
        else:
            args = (gmix, w_zx, w_dt, conv_w, conv_b, dtb, alog, dcol, ssd_ng, full["ssd_w_o"][j], f"ssd{j}")
            h, sv = ssd_fwd(h, *args)
        margs = (w["norm_mlp_g"][i:i + 1], full["mlp_w_in"], full["mlp_w_out"], i, f"mlp{i}")
        h, msv = mlp_fwd(h, *margs)
        tape.append((kind, j, args, sv, margs, msv))

    dh, loss_cols = loss_head(h, target[0], "loss_head")

    gw = {k: [None] * w[k].shape[0] for k in ("norm_mix_g", "norm_mlp_g", "sb_w_qkv", "sb_q_norm_g", "sb_k_norm_g", "sb_w_o",
                                                "mlp_w_in", "mlp_w_out")}
    for i in reversed(range(depth)):
        kind, j, args, sv, margs, msv = tape[i]
        dh, gw["norm_mlp_g"][i], gw["mlp_w_in"][i], gw["mlp_w_out"][i] = mlp_bwd(dh, msv, *margs)
        if kind == 0:
            dh, gw["norm_mix_g"][i], gw["sb_w_qkv"][j], gw["sb_q_norm_g"][j], gw["sb_k_norm_g"][j], gw["sb_w_o"][j] = sb_bwd(dh, sv, *args)
        elif kind == 1:
            dh, gw["norm_mix_g"][i], d_in, d_vg, d_ws, d_bcol, d_wo = gm_bwd(dh, sv, *args)
            gw["gm_w_in"], gw["gm_v_norm_g"], gw["gm_w_s"], gw["gm_b_s"], gw["gm_w_o"] = d_in[None], d_vg, d_ws[None], d_bcol[None, :, :, 0], d_wo[None]
        else:
            dh, gw["norm_mix_g"][i], d_zx, d_dt, d_cw, d_cb, d_dtb, d_al, d_dcol, d_ng, d_wo = ssd_bwd(dh, sv, *args)
            gw["ssd_w_in"] = jnp.concatenate([d_zx, d_dt[:, :Hs]], axis=1)[None]
            gw["ssd_conv_w"], gw["ssd_conv_b"], gw["ssd_norm_g"], gw["ssd_w_o"] = d_cw[None], d_cb, d_ng, d_wo[None]
            gw["ssd_dt_bias"], gw["ssd_a_log"] = d_dtb[:, :Hs], d_al[:, :Hs]
            gw["ssd_d"] = jnp.sum(d_dcol.reshape(Hs, SSD_HEAD_DIM), axis=1)[None]
    for k in gw:
        if isinstance(gw[k], list):
            gw[k] = jnp.stack(gw[k]) if k in _BIG else jnp.concatenate(gw[k], axis=0)
    grad_x = dh[None]

    gfull = []
    for k in _BIG:
        g = _full_to_cols(gw[k]) if k in _COLS else _full_to_rows(gw[k])
        gfull.append(g.reshape(4, 2, -1, g.shape[-1]).astype(GRAD_DT))
    from_sibling = swap_halves(gfull, "grad_swap_halves")
    pair = [pair_sum(g, r, f"pair_sum_{k}") for k, g, r in zip(_BIG, gfull, from_sibling)]
    from_chips = scatter_chips(pair, "grad_scatter_chips")
    mine = [chip_sum(s, r, f"chip_sum_{k}") for k, s, r in zip(_BIG, pair, from_chips)]
    joined = join_halves(mine, "grad_join_halves")
    grads, deltas, new_m, new_v = {}, {}, {}, {}
    for k, g in zip(_BIG, joined):
        C = w[k].shape[-1]
        d_, m_, v_ = adamw(w[k].reshape(-1, C), g.reshape(-1, C), m[k].reshape(-1, C), v[k].reshape(-1, C), f"adamw_{k}")
        grads[k], deltas[k], new_m[k], new_v[k] = (a.reshape(w[k].shape) for a in (g, d_, m_, v_))

    full_shapes = [gw[k].shape for k in _SMALL] + [(1,)]
    loss_local = jnp.sum(loss_cols).reshape(1)
    red = sum8(gather_small(_pack([gw[k] for k in _SMALL] + [loss_local]), "gather_small_grads"), "sum_small_grads")
    red = _unpack(red, full_shapes)
    loss = red[-1][0]
    gsm = dict(zip(_SMALL, red[:-1]))
    for k in _SMALL_SHARDED:
        n = w[k].shape[-1]
        gsm[k] = lax.dynamic_slice_in_dim(gsm[k], chip * n, n, axis=gsm[k].ndim - 1)
    shapes = [w[k].shape for k in _SMALL]
    packed = [_pack([d[k] for k in _SMALL]) for d in (w, gsm, m, v)]
    outs = adamw(*packed, "adamw_small")
    for k, g_, d_, m_, v_ in zip(_SMALL, [gsm[k] for k in _SMALL], *[_unpack(o, shapes) for o in outs]):
        grads[k], deltas[k], new_m[k], new_v[k] = g_.reshape(w[k].shape), d_, m_, v_

    return (loss, grad_x, *[grads[k] for k in _WEIGHTS], *[deltas[k] for k in _WEIGHTS],
            *[new_m[k] for k in _WEIGHTS], *[new_v[k] for k in _WEIGHTS])


def kernel(x, norm_mix_g, norm_mlp_g, sb_w_qkv, sb_q_norm_g, sb_k_norm_g, sb_w_o, gm_w_in, gm_v_norm_g, gm_w_s, gm_b_s, gm_w_o, ssd_w_in, ssd_conv_w, ssd_conv_b, ssd_dt_bias, ssd_a_log, ssd_d, ssd_norm_g, ssd_w_o, mlp_w_in, mlp_w_out, loss_target, m_norm_mix_g, m_norm_mlp_g, m_sb_w_qkv, m_sb_q_norm_g, m_sb_k_norm_g, m_sb_w_o, m_gm_w_in, m_gm_v_norm_g, m_gm_w_s, m_gm_b_s, m_gm_w_o, m_ssd_w_in, m_ssd_conv_w, m_ssd_conv_b, m_ssd_dt_bias, m_ssd_a_log, m_ssd_d, m_ssd_norm_g, m_ssd_w_o, m_mlp_w_in, m_mlp_w_out, v_norm_mix_g, v_norm_mlp_g, v_sb_w_qkv, v_sb_q_norm_g, v_sb_k_norm_g, v_sb_w_o, v_gm_w_in, v_gm_v_norm_g, v_gm_w_s, v_gm_b_s, v_gm_w_o, v_ssd_w_in, v_ssd_conv_w, v_ssd_conv_b, v_ssd_dt_bias, v_ssd_a_log, v_ssd_d, v_ssd_norm_g, v_ssd_w_o, v_mlp_w_in, v_mlp_w_out):
    w = dict(zip(_WEIGHTS, (norm_mix_g, norm_mlp_g, sb_w_qkv, sb_q_norm_g, sb_k_norm_g, sb_w_o, gm_w_in, gm_v_norm_g, gm_w_s,
                            gm_b_s, gm_w_o, ssd_w_in, ssd_conv_w, ssd_conv_b, ssd_dt_bias, ssd_a_log, ssd_d, ssd_norm_g,
                            ssd_w_o, mlp_w_in, mlp_w_out)))
    m = dict(zip(_WEIGHTS, (m_norm_mix_g, m_norm_mlp_g, m_sb_w_qkv, m_sb_q_norm_g, m_sb_k_norm_g, m_sb_w_o, m_gm_w_in,
                            m_gm_v_norm_g, m_gm_w_s, m_gm_b_s, m_gm_w_o, m_ssd_w_in, m_ssd_conv_w, m_ssd_conv_b,
                            m_ssd_dt_bias, m_ssd_a_log, m_ssd_d, m_ssd_norm_g, m_ssd_w_o, m_mlp_w_in, m_mlp_w_out)))
    v = dict(zip(_WEIGHTS, (v_norm_mix_g, v_norm_mlp_g, v_sb_w_qkv, v_sb_q_norm_g, v_sb_k_norm_g, v_sb_w_o, v_gm_w_in,
                            v_gm_v_norm_g, v_gm_w_s, v_gm_b_s, v_gm_w_o, v_ssd_w_in, v_ssd_conv_w, v_ssd_conv_b,
                            v_ssd_dt_bias, v_ssd_a_log, v_ssd_d, v_ssd_norm_g, v_ssd_w_o, v_mlp_w_in, v_mlp_w_out)))
    return _step(x, w, loss_target, m, v)
```

```python
import functools
import math

import jax
import jax.numpy as jnp
from jax import lax
from jax.experimental import pallas as pl
from jax.experimental.pallas import tpu as pltpu

F32 = jnp.float32
BF16 = jnp.bfloat16
_MXU = jnp.bfloat16
_ACT = jnp.bfloat16
GRAD_DT = jnp.bfloat16
_VMEM_LIMIT = 56 * 1024 * 1024
EPS = 1e-6
LANES = 128
CHUNK = 128
SSD_HEAD_DIM = 64
SSD_STATE = 128
SSD_CONV = 4
ADAM_LR, ADAM_B1, ADAM_B2, ADAM_EPS, ADAM_WD, ADAM_STEP = 1e-3, 0.9, 0.999, 1e-8, 0.01, 10
MESH = pl.DeviceIdType.MESH

NN = (((1,), (0,)), ((), ()))
NT = (((1,), (1,)), ((), ()))
TN = (((0,), (0,)), ((), ()))


def _dot(a, b, dims=NN):
    return lax.dot_general(a.astype(_MXU), b.astype(_MXU), dims, preferred_element_type=F32)


def _params(sem):
    return pltpu.CompilerParams(dimension_semantics=sem, vmem_limit_bytes=_VMEM_LIMIT)


def _pick(n, *cands):
    for c in cands:
        if n % c == 0:
            return c
    return n


def matmul(a, b, mode, out_dtypes, name, epilogue=None, extras=(), a_l=None, b_l=None):
    ash, bsh = a.shape[-2:], b.shape[-2:]
    if mode == "nn":
        (M, K), (K2, N) = ash, bsh
    elif mode == "nt":
        (M, K), (N, K2) = ash, bsh
    else:
        (K, M), (K2, N) = ash, bsh
    assert K == K2, (mode, a.shape, b.shape)
    tm = _pick(M, 1024, 512, 256, 128) if a.dtype.itemsize == 2 and not extras else _pick(M, 512, 256, 128)
    tn = _pick(N, 1024, 512, 256, 128)
    tk = _pick(K, 2048, 1024, 512, 256, 128)
    nk = K // tk
    dims = {"nn": NN, "nt": NT, "tn": TN}[mode]
    single = not isinstance(out_dtypes, (tuple, list))
    odt = (out_dtypes,) if single else tuple(out_dtypes)
    n_ex = len(extras)

    def lead(l, spec_shape, imap):
        if l is None:
            return pl.BlockSpec(spec_shape, imap)
        return pl.BlockSpec((None,) + spec_shape, lambda i, j, k: (l,) + imap(i, j, k))

    if mode == "tn":
        a_spec = lead(a_l, (tk, tm), lambda i, j, k: (k, i))
    else:
        a_spec = lead(a_l, (tm, tk), lambda i, j, k: (i, k))
    if mode == "nt":
        b_spec = lead(b_l, (tn, tk), lambda i, j, k: (j, k))
    else:
        b_spec = lead(b_l, (tk, tn), lambda i, j, k: (k, j))
    mn_spec = pl.BlockSpec((tm, tn), lambda i, j, k: (i, j))

    def body(*refs):
        a_ref, b_ref = refs[0], refs[1]
        ex = refs[2:2 + n_ex]
        outs = refs[2 + n_ex:2 + n_ex + len(odt)]
        acc = refs[-1]
        k = pl.program_id(2)

        @pl.when(k == 0)
        def _():
            acc[...] = jnp.zeros_like(acc)

        part = _dot(a_ref[...], b_ref[...], dims)

        @pl.when(k < nk - 1)
        def _():
            acc[...] += part

        @pl.when(k == nk - 1)
        def _():
            r = acc[...] + part
            res = (r,) if epilogue is None else epilogue(r, *[e[...] for e in ex])
            for o, v in zip(outs, res):
                o[...] = v.astype(o.dtype)

    out = pl.pallas_call(
        body,
        out_shape=tuple(jax.ShapeDtypeStruct((M, N), d) for d in odt),
        grid=(M // tm, N // tn, nk),
        in_specs=[a_spec, b_spec] + [mn_spec] * n_ex,
        out_specs=tuple(mn_spec for _ in odt),
        scratch_shapes=[pltpu.VMEM((tm, tn), F32)],
        compiler_params=_params(("parallel", "parallel", "arbitrary")),
        name=name,
    )(a, b, *extras)
    return out[0] if single else out


class Op:
    def __init__(self, arr, block, imap, kind="tile", grad=True, gshape=None, gimap=None):
        self.arr, self.block, self.imap, self.kind, self.grad = arr, block, imap, kind, grad
        self.gshape = gshape or arr.shape
        self.gimap = gimap or imap

    def spec(self):
        return pl.BlockSpec(self.block, self.imap)


def tmap(f, grid, ins, outs, name):
    n_in = len(ins)

    def body(*refs):
        res = f(*[r[...] for r in refs[:n_in]])
        for o, v in zip(refs[n_in:], res):
            o[...] = v.astype(o.dtype)

    return pl.pallas_call(
        body,
        out_shape=tuple(jax.ShapeDtypeStruct(s, d) for s, d, _, _ in outs),
        grid=grid,
        in_specs=[o.spec() for o in ins],
        out_specs=tuple(pl.BlockSpec(b, m) for _, _, b, m in outs),
        compiler_params=_params(("parallel", "parallel")),
        name=name,
    )(*[o.arr for o in ins])


def tmap_vjp(f, grid, ins, cts, name, grad_dtypes=None):
    n_in, n_ct = len(ins), len(cts)
    gidx = [i for i, o in enumerate(ins) if o.grad]
    gdt = grad_dtypes or {}

    def body(*refs):
        in_refs, ct_refs, g_refs = refs[:n_in], refs[n_in:n_in + n_ct], refs[n_in + n_ct:]
        vals = [r[...] for r in in_refs]

        def g_only(*diff):
            full = list(vals)
            for i, v in zip(gidx, diff):
                full[i] = v
            return f(*full)

        res, vjp = jax.vjp(g_only, *[vals[i].astype(F32) for i in gidx])
        grads = vjp(tuple(c[...].astype(r.dtype) for c, r in zip(ct_refs, res)))
        inner = pl.program_id(1)
        for i, g, gr in zip(gidx, grads, g_refs):
            if ins[i].kind == "tile":
                gr[...] = g.astype(gr.dtype)
            else:
                @pl.when(inner == 0)
                def _(gr=gr, g=g):
                    gr[...] = g.astype(gr.dtype)

                @pl.when(inner != 0)
                def _(gr=gr, g=g):
                    gr[...] += g.astype(gr.dtype)

    out_shape = tuple(jax.ShapeDtypeStruct(ins[i].gshape, gdt.get(i, F32)) for i in gidx)
    return pl.pallas_call(
        body,
        out_shape=out_shape,
        grid=grid,
        in_specs=[o.spec() for o in ins] + [o.spec() for o in cts],
        out_specs=tuple(pl.BlockSpec(ins[i].block, ins[i].gimap) for i in gidx),
        compiler_params=_params(("parallel", "arbitrary")),
        name=name,
    )(*[o.arr for o in ins], *[o.arr for o in cts])


def _rms(x, g):
    return x * lax.rsqrt(jnp.mean(x * x, axis=-1, keepdims=True) + EPS) * g


def _row_ops(arrs, tm, grads=None):
    grads = grads or [True] * len(arrs)
    return [Op(a, (tm, a.shape[1]), lambda o, i: (i, 0), "tile", g) for a, g in zip(arrs, grads)]


def _vec_op(v, grad=True):
    return Op(v, (1, v.shape[1]), lambda o, i: (0, 0), "param", grad)


def rmsnorm_fwd(h, g, name):
    T, D = h.shape
    tm = _pick(T, 512, 256, 128)
    f = lambda x, gg: (_rms(x, gg),)
    return tmap(f, (1, T // tm), _row_ops([h], tm) + [_vec_op(g)],
                [((T, D), _ACT, (tm, D), lambda o, i: (i, 0))], name)[0]


def rmsnorm_bwd(h, g, dhn, dres, name):
    T, D = h.shape
    tm = _pick(T, 512, 256, 128)
    f = lambda x, gg: (_rms(x, gg), x)
    return tmap_vjp(f, (1, T // tm), _row_ops([h], tm) + [_vec_op(g)], _row_ops([dhn, dres], tm), name)


def mlp_fwd(h, g_row, w_in, w_out, l, tag):
    hn = rmsnorm_fwd(h, g_row, f"{tag}_norm")
    a, r2 = matmul(hn, w_in, "nn", (F32, _ACT), f"{tag}_in", b_l=l,
                   epilogue=lambda acc: (acc, jnp.square(jnp.maximum(acc, 0.0))))
    out = matmul(r2, w_out, "nn", F32, f"{tag}_out", b_l=l, epilogue=lambda acc, hh: (acc + hh,), extras=(h,))
    return out, (h, hn, a, r2)


def mlp_bwd(dout, saved, g_row, w_in, w_out, l, tag):
    h, hn, a, r2 = saved
    da = matmul(dout, w_out, "nt", _ACT, f"{tag}_dact", b_l=l,
                epilogue=lambda acc, aa: (acc * (2.0 * jnp.maximum(aa, 0.0)),), extras=(a,))
    dw_out = matmul(r2, dout, "tn", GRAD_DT, f"{tag}_dwout")
    dw_in = matmul(hn, da, "tn", GRAD_DT, f"{tag}_dwin")
    dhn = matmul(da, w_in, "nt", F32, f"{tag}_dhn", b_l=l)
    dh, dg = rmsnorm_bwd(h, g_row, dhn, dout, f"{tag}_dnorm")
    return dh, dg, dw_in, dw_out


def _split3(x):
    hi = x.astype(BF16)
    r = x - hi.astype(F32)
    mid = r.astype(BF16)
    lo = (r - mid.astype(F32)).astype(BF16)
    return hi, mid, lo


def _cumdot(x, tri):
    return sum(lax.dot_general(p, tri, NN, preferred_element_type=F32) for p in _split3(x))


def _iotas():
    row = lax.broadcasted_iota(jnp.int32, (CHUNK, CHUNK), 0)
    col = lax.broadcasted_iota(jnp.int32, (CHUNK, CHUNK), 1)
    return row, col


_TQ = 256
_TK = 256
_DEAD = -88.0


def _sb_block(q, kblk, q0, k0, scale, row, col):
    z = _dot(q, kblk, NT) * scale
    e = jnp.exp(-jnp.abs(z))
    den = 1.0 + e
    sp = jnp.maximum(z, 0.0) + jnp.log(den)
    mask = (col + k0) < (row + q0)
    lg = jnp.where(mask, -sp, 0.0)
    beta = jnp.where(z >= 0, 1.0, e) / den
    return z, mask, lg, beta


def _attn_iotas(tq, tk):
    row = lax.broadcasted_iota(jnp.int32, (tq, tk), 0)
    col = lax.broadcasted_iota(jnp.int32, (tq, tk), 1)
    r2 = lax.broadcasted_iota(jnp.int32, (tk, tk), 0)
    c2 = lax.broadcasted_iota(jnp.int32, (tk, tk), 1)
    return row, col, r2, c2


def attn_fwd(qn, kn, v, name):
    T, W = qn.shape
    tq, tk = _pick(T, _TQ, CHUNK), _pick(T, _TK, CHUNK)
    H, NQ, per = W // LANES, T // tq, tq // tk
    assert tq % tk == 0 and T // tk <= LANES
    scale = LANES ** -0.5

    def body(q_ref, k_ref, v_ref, o_ref, r_ref, acc_ref, run_ref):
        qi = pl.program_id(1)
        q = q_ref[...]
        row, col, r2, c2 = _attn_iotas(tq, tk)
        suffix = (r2 >= c2).astype(_MXU)
        lane_q = lax.broadcasted_iota(jnp.int32, (tq, LANES), 1)
        acc_ref[...] = jnp.zeros_like(acc_ref)
        run_ref[...] = jnp.zeros_like(run_ref)
        r_ref[...] = jnp.full(r_ref.shape, -1e30, F32)

        def step(carry):
            kb, _ = carry
            off = pl.multiple_of(kb * tk, tk)
            run = run_ref[...]
            z, mask, lg, _ = _sb_block(q, k_ref[pl.ds(off, tk), :], qi * tq, off, scale, row, col)
            r_ref[...] = jnp.where(lane_q == kb, run, r_ref[...])
            cl = _cumdot(lg, suffix) + run
            a = jnp.exp(jnp.where(mask, z + cl, -1e30))
            acc_ref[...] += _dot(a, v_ref[pl.ds(off, tk), :])
            run = run + jnp.sum(lg, axis=1, keepdims=True)
            run_ref[...] = run
            return kb - 1, jnp.max(run) > _DEAD

        lax.while_loop(lambda c: (c[0] >= 0) & c[1], step, ((qi + 1) * per - 1, True))
        o_ref[...] = acc_ref[...].astype(o_ref.dtype)

    qspec = pl.BlockSpec((tq, LANES), lambda h, i: (i, h))
    kvspec = pl.BlockSpec((T, LANES), lambda h, i: (0, h))
    return pl.pallas_call(
        body,
        out_shape=(jax.ShapeDtypeStruct((T, W), _ACT), jax.ShapeDtypeStruct((H, T, LANES), F32)),
        grid=(H, NQ),
        in_specs=[qspec, kvspec, kvspec],
        out_specs=(qspec, pl.BlockSpec((None, tq, LANES), lambda h, i: (h, i, 0))),
        scratch_shapes=[pltpu.VMEM((tq, LANES), F32), pltpu.VMEM((tq, 1), F32)],
        compiler_params=_params(("parallel", "parallel")),
        name=name,
    )(qn, kn, v)


def attn_bwd(qn, kn, v, do, rblk, name):
    T, W = qn.shape
    tq, tk = _pick(T, _TQ, CHUNK), _pick(T, _TK, CHUNK)
    H, NQ, per = W // LANES, T // tq, tq // tk
    scale = LANES ** -0.5

    def body(q_ref, k_ref, v_ref, do_ref, r_ref, dq_ref, dk_ref, dv_ref, g_ref):
        qi = pl.program_id(1)

        @pl.when(qi == 0)
        def _():
            dk_ref[...] = jnp.zeros_like(dk_ref)
            dv_ref[...] = jnp.zeros_like(dv_ref)

        q = q_ref[...]
        dout = do_ref[...]
        rt = r_ref[...]
        row, col, r2, c2 = _attn_iotas(tq, tk)
        suffix = (r2 >= c2).astype(_MXU)
        prefix = (r2 <= c2).astype(_MXU)
        kend = (qi + 1) * per - 1
        lane = lax.broadcasted_iota(jnp.int32, (1, LANES), 1)
        lane_q = lax.broadcasted_iota(jnp.int32, (tq, LANES), 1)
        unvisited = (jnp.max(rt, axis=0, keepdims=True) < -1e29) & (lane <= kend)
        start = jnp.sum(unvisited.astype(jnp.int32))

        dq_ref[...] = jnp.zeros_like(dq_ref)
        g_ref[...] = jnp.zeros_like(g_ref)

        @pl.loop(start, kend + 1)
        def _(kb):
            gsum = g_ref[...]
            off = pl.multiple_of(kb * tk, tk)
            kblk = k_ref[pl.ds(off, tk), :]
            vblk = v_ref[pl.ds(off, tk), :]
            z, mask, lg, beta = _sb_block(q, kblk, qi * tq, off, scale, row, col)
            run = jnp.sum(jnp.where(lane_q == kb, rt, 0.0), axis=1, keepdims=True)
            cl = _cumdot(lg, suffix) + run
            a = jnp.exp(jnp.where(mask, z + cl, -1e30))
            e = _dot(dout, vblk, NT) * a
            f = _cumdot(e, prefix) + gsum
            dz = jnp.where(mask, e - beta * f, 0.0) * scale
            dk_ref[pl.ds(off, tk), :] += _dot(dz, q, TN)
            dv_ref[pl.ds(off, tk), :] += _dot(a, dout, TN)
            dq_ref[...] += _dot(dz, kblk)
            g_ref[...] = gsum + jnp.sum(e, axis=1, keepdims=True)

    qspec = pl.BlockSpec((tq, LANES), lambda h, i: (i, h))
    kvspec = pl.BlockSpec((T, LANES), lambda h, i: (0, h))
    big = jax.ShapeDtypeStruct((T, W), F32)
    return pl.pallas_call(
        body,
        out_shape=(big, big, big),
        grid=(H, NQ),
        in_specs=[qspec, kvspec, kvspec, qspec, pl.BlockSpec((None, tq, LANES), lambda h, i: (h, i, 0))],
        out_specs=(qspec, kvspec, kvspec),
        scratch_shapes=[pltpu.VMEM((tq, 1), F32)],
        compiler_params=_params(("parallel", "arbitrary")),
        name=name,
    )(qn, kn, v, do, rblk)


def _qk_ops(qkv, qg, kg, tm, grad):
    T, W3 = qkv.shape
    H, NT_ = W3 // (3 * LANES), T // tm
    W = H * LANES

    def part(p):
        return Op(qkv, (tm, LANES), lambda o, n: (lax.rem(n, NT_), p * H + lax.div(n, NT_)), "tile", grad,
                  gshape=(T, W), gimap=lambda o, n: (lax.rem(n, NT_), lax.div(n, NT_)))

    vec = lambda g: Op(g, (1, LANES), lambda o, n: (0, 0), "param", grad)
    return [part(0), part(1), part(2), vec(qg), vec(kg)], (1, H * NT_), H, NT_, W


def _qk_f(q, k, v, qg, kg):
    return _rms(q, qg), _rms(k, kg), v


def qknorm_fwd(qkv, qg, kg, name):
    T = qkv.shape[0]
    tm = _pick(T, 512, 256, 128)
    ins, grid, H, NT_, W = _qk_ops(qkv, qg, kg, tm, False)
    out = ((T, W), _ACT, (tm, LANES), lambda o, n: (lax.rem(n, NT_), lax.div(n, NT_)))
    return tmap(_qk_f, grid, ins, [out, out, out], name)


def qknorm_bwd(qkv, qg, kg, dq, dk, dv, name):
    T = qkv.shape[0]
    tm = _pick(T, 512, 256, 128)
    ins, grid, H, NT_, W = _qk_ops(qkv, qg, kg, tm, True)
    cts = [Op(c, (tm, LANES), lambda o, n: (lax.rem(n, NT_), lax.div(n, NT_))) for c in (dq, dk, dv)]
    return tmap_vjp(_qk_f, grid, ins, cts, name, grad_dtypes={0: _ACT, 1: _ACT, 2: _ACT})


def sb_fwd(h, g_row, w_qkv, qg, kg, w_o, l, tag):
    hn = rmsnorm_fwd(h, g_row, f"{tag}_norm")
    qkv = matmul(hn, w_qkv, "nn", F32, f"{tag}_qkv", b_l=l)
    qn, kn, v = qknorm_fwd(qkv, qg, kg, f"{tag}_qknorm")
    o, rblk = attn_fwd(qn, kn, v, f"{tag}_attn")
    out = matmul(o, w_o, "nn", F32, f"{tag}_wo", b_l=l, epilogue=lambda acc, hh: (acc + hh,), extras=(h,))
    return out, (h, hn, qkv, qn, kn, v, o, rblk)


def sb_bwd(dout, saved, g_row, w_qkv, qg, kg, w_o, l, tag):
    h, hn, qkv, qn, kn, v, o, rblk = saved
    do = matmul(dout, w_o, "nt", _ACT, f"{tag}_do", b_l=l)
    dw_o = matmul(o, dout, "tn", GRAD_DT, f"{tag}_dwo")
    dqn, dkn, dv = attn_bwd(qn, kn, v, do, rblk, f"{tag}_dattn")
    dq, dk, dvv, dqg, dkg = qknorm_bwd(qkv, qg, kg, dqn, dkn, dv, f"{tag}_dqknorm")
    dqkv = jnp.concatenate([dq, dk, dvv], axis=1)
    dw_qkv = matmul(hn, dqkv, "tn", GRAD_DT, f"{tag}_dwqkv")
    dhn = matmul(dqkv, w_qkv, "nt", F32, f"{tag}_dhn", b_l=l)
    dh, dg = rmsnorm_bwd(h, g_row, dhn, dout, f"{tag}_dnorm")
    return dh, dg, dw_qkv, dqg, dkg, dw_o


@functools.partial(jax.custom_vjp, nondiff_argnums=(2,))
def _dotv(a, b, mode):
    return _dot(a, b, {"nn": NN, "nt": NT, "tn": TN}[mode])


def _dotv_fwd(a, b, mode):
    return _dotv(a, b, mode), (a, b)


def _dotv_bwd(mode, res, g):
    a, b = res
    if mode == "nn":
        return _dot(g, b, NT), _dot(a, g, TN)
    if mode == "nt":
        return _dot(g, b, NN), _dot(g, a, TN)
    return _dot(b, g, NT), _dot(a, g, NN)


_dotv.defvjp(_dotv_fwd, _dotv_bwd)


def _gelu(x):
    return 0.5 * x * (1.0 + lax.erf(x * (2.0 ** -0.5)))


def _gm1_f(au, av, vg):
    return _gelu(au), _rms(_gelu(av), vg)


def _gm1_ops(a, vg, tm, grad):
    T, W2 = a.shape
    W = W2 // 2
    part = lambda p: Op(a, (tm, W), lambda o, i: (i, p), "tile", grad, gshape=(T, W), gimap=lambda o, i: (i, 0))
    return [part(0), part(1), _vec_op(vg, grad)], (1, T // tm), W


_GM_ROWS = 1024


def _gm_specs(T, W, G):
    rows = _pick(T, _GM_ROWS, 512, 256, CHUNK)
    blk = pl.BlockSpec((rows, LANES), lambda g, c: (c, g))
    wspec = pl.BlockSpec((None, CHUNK, CHUNK), lambda g, c: (g, 0, 0))
    bspec = pl.BlockSpec((None, CHUNK, 1), lambda g, c: (g, 0, 0))
    return rows, blk, wspec, bspec, (G, T // rows)


def gm_mix_fwd(u, vn, ws, bcol, name):
    T, W = u.shape
    rows, blk, wspec, bspec, grid = _gm_specs(T, W, W // LANES)

    def body(u_ref, v_ref, w_ref, b_ref, y_ref):
        r, c = _iotas()
        w = jnp.where(r >= c, w_ref[...], 0.0).astype(_MXU)
        for k in range(rows // CHUNK):
            sl = pl.ds(k * CHUNK, CHUNK)
            y_ref[sl, :] = (u_ref[sl, :] * (_dot(w, v_ref[sl, :]) + b_ref[...])).astype(y_ref.dtype)

    return pl.pallas_call(body, out_shape=jax.ShapeDtypeStruct((T, W), _ACT), grid=grid,
                          in_specs=[blk, blk, wspec, bspec], out_specs=blk,
                          compiler_params=_params(("parallel", "parallel")), name=name)(u, vn, ws, bcol)


def gm_mix_bwd(u, vn, ws, bcol, dy, name):
    T, W = u.shape
    G = W // LANES
    rows, blk, wspec, bspec, grid = _gm_specs(T, W, G)

    def body(u_ref, v_ref, w_ref, b_ref, dy_ref, du_ref, dv_ref, dw_ref, db_ref):
        r, c = _iotas()
        tri = r >= c
        w = jnp.where(tri, w_ref[...], 0.0).astype(_MXU)
        dw = jnp.zeros((CHUNK, CHUNK), F32)
        db = jnp.zeros((CHUNK, 1), F32)
        for k in range(rows // CHUNK):
            sl = pl.ds(k * CHUNK, CHUNK)
            v = v_ref[sl, :]
            g = dy_ref[sl, :]
            du_ref[sl, :] = g * (_dot(w, v) + b_ref[...])
            dm = g * u_ref[sl, :]
            dv_ref[sl, :] = _dot(w, dm, TN)
            dw = dw + _dot(dm, v, NT)
            db = db + jnp.sum(dm, axis=1, keepdims=True)
        dw = jnp.where(tri, dw, 0.0)

        @pl.when(pl.program_id(1) == 0)
        def _():
            dw_ref[...] = dw
            db_ref[...] = db

        @pl.when(pl.program_id(1) != 0)
        def _():
            dw_ref[...] += dw
            db_ref[...] += db

    big = jax.ShapeDtypeStruct((T, W), F32)
    return pl.pallas_call(
        body,
        out_shape=(big, big, jax.ShapeDtypeStruct((G, CHUNK, CHUNK), F32), jax.ShapeDtypeStruct((G, CHUNK, 1), F32)),
        grid=grid, in_specs=[blk, blk, wspec, bspec, blk], out_specs=(blk, blk, wspec, bspec),
        compiler_params=_params(("parallel", "arbitrary")), name=name)(u, vn, ws, bcol, dy)


def gm_fwd(h, g_row, w_in, vg, ws, bcol, w_o, l, tag):
    T = h.shape[0]
    tm = _pick(T, 256, 128)
    hn = rmsnorm_fwd(h, g_row, f"{tag}_norm")
    a = matmul(hn, w_in, "nn", F32, f"{tag}_in", b_l=l)
    ins, grid, W = _gm1_ops(a, vg, tm, False)
    rows = lambda dt: ((T, W), dt, (tm, W), lambda o, i: (i, 0))
    u, vn = tmap(_gm1_f, grid, ins, [rows(F32), rows(_ACT)], f"{tag}_act")
    y = gm_mix_fwd(u, vn, ws, bcol, f"{tag}_mix")
    out = matmul(y, w_o, "nn", F32, f"{tag}_wo", b_l=l, epilogue=lambda acc, hh: (acc + hh,), extras=(h,))
    return out, (h, hn, a, u, vn, y)


def gm_bwd(dout, saved, g_row, w_in, vg, ws, bcol, w_o, l, tag):
    h, hn, a, u, vn, y = saved
    T = h.shape[0]
    tm = _pick(T, 256, 128)
    dy = matmul(dout, w_o, "nt", F32, f"{tag}_dy", b_l=l)
    dw_o = matmul(y, dout, "tn", GRAD_DT, f"{tag}_dwo")
    du, dvn, dws, dbcol = gm_mix_bwd(u, vn, ws, bcol, dy, f"{tag}_dmix")
    ins, grid, W = _gm1_ops(a, vg, tm, True)
    dau, dav, dvg = tmap_vjp(_gm1_f, grid, ins, _row_ops([du, dvn], tm), f"{tag}_dact", grad_dtypes={0: _ACT, 1: _ACT})
    da = jnp.concatenate([dau, dav], axis=1)
    dw_in = matmul(hn, da, "tn", GRAD_DT, f"{tag}_dwin")
    dhn = matmul(da, w_in, "nt", F32, f"{tag}_dhn", b_l=l)
    dh, dg = rmsnorm_bwd(h, g_row, dhn, dout, f"{tag}_dnorm")
    return dh, dg, dw_in, dvg, dws, dbcol, dw_o


@jax.custom_vjp
def _softplus(x):
    return jnp.maximum(x, 0.0) + jnp.log(1.0 + jnp.exp(-jnp.abs(x)))


_softplus.defvjp(lambda x: (_softplus(x), x), lambda x, g: (g * lax.logistic(x),))


def _silu(x):
    return x * lax.logistic(x)


def _shift_impl(x, s, down):
    n = x.shape[0]
    r = lax.broadcasted_iota(jnp.int32, x.shape, 0)
    if down:
        return jnp.where(r >= s, pltpu.roll(x, s, 0), 0.0)
    return jnp.where(r < n - s, pltpu.roll(x, n - s, 0), 0.0)


@functools.partial(jax.custom_vjp, nondiff_argnums=(1,))
def _shift_down(x, s):
    return _shift_impl(x, s, True)


_shift_down.defvjp(lambda x, s: (_shift_impl(x, s, True), None), lambda s, _, g: (_shift_impl(g, s, False),))


def _conv_f(x, w, b):
    k_id = lax.broadcasted_iota(jnp.int32, w.shape, 0)
    y = b + jnp.sum(jnp.where(k_id == SSD_CONV - 1, w, 0.0), axis=0, keepdims=True) * x
    for k in range(SSD_CONV - 1):
        wk = jnp.sum(jnp.where(k_id == k, w, 0.0), axis=0, keepdims=True)
        y = y + wk * _shift_down(x, SSD_CONV - 1 - k)
    return (_silu(y),)


def _conv_ops(zx, conv_w, conv_b, wi, grad):
    T = zx.shape[0]
    cd = conv_w.shape[1]
    cw = LANES
    off = wi // cw
    return [Op(zx, (T, cw), lambda o, j: (0, off + j), "tile", grad, gshape=(T, cd), gimap=lambda o, j: (0, j)),
            Op(conv_w, (SSD_CONV, cw), lambda o, j: (0, j), "tile", grad),
            Op(conv_b, (1, cw), lambda o, j: (0, j), "tile", grad)], (1, cd // cw), (T, cw)


def _dt_f(dtr, bias):
    return (_softplus(dtr + bias),)


def _cumdot_left(tri, x):
    return sum(lax.dot_general(tri, p, NN, preferred_element_type=F32) for p in _split3(x))


@jax.custom_vjp
def _cumsum_rows(x):
    row, col = _iotas()
    return _cumdot_left((row >= col).astype(_MXU), x)


def _cumsum_rows_bwd(_, g):
    row, col = _iotas()
    return (_cumdot_left((row <= col).astype(_MXU), g),)


_cumsum_rows.defvjp(lambda x: (_cumsum_rows(x), None), _cumsum_rows_bwd)


def _ssd_chunk(xps, dt, bm, cm, sps, alog, hid_base):
    row, col = _iotas()
    half = SSD_HEAD_DIM
    rcol = lax.broadcasted_iota(jnp.int32, (CHUNK, 1), 0)
    colpick = lambda m, hid: jnp.sum(jnp.where(col == hid, m, 0.0), axis=1, keepdims=True)
    rowpick = lambda m, hid: jnp.sum(jnp.where(row == hid, m, 0.0), axis=0, keepdims=True)
    last = lambda v: jnp.sum(jnp.where(rcol == CHUNK - 1, v, 0.0), axis=0, keepdims=True)
    acum = _cumsum_rows(dt * (-jnp.exp(alog)))
    acum_t = acum.T
    cb = _dotv(cm, bm, "nt")
    tri = row >= col
    lo = col < half
    ys, snews = [], []
    for p, (xp, sp) in enumerate(zip(xps, sps)):
        h0, h1 = hid_base + 2 * p, hid_base + 2 * p + 1
        ac0, ac1 = colpick(acum, h0), colpick(acum, h1)
        m0 = cb * jnp.exp(jnp.where(tri, ac0 - rowpick(acum_t, h0), -1e30))
        m1 = cb * jnp.exp(jnp.where(tri, ac1 - rowpick(acum_t, h1), -1e30))
        xs = xp * jnp.where(lo, colpick(dt, h0), colpick(dt, h1))
        ydiag = jnp.where(lo, _dotv(m0, xs, "nn"), _dotv(m1, xs, "nn"))
        yoff = jnp.where(lo, jnp.exp(ac0), jnp.exp(ac1)) * _dotv(cm, sp, "nt")
        al0, al1 = last(ac0), last(ac1)
        xsd = xs * jnp.where(lo, jnp.exp(al0 - ac0), jnp.exp(al1 - ac1))
        snew = jnp.where(row < half, jnp.exp(al0), jnp.exp(al1)) * sp + _dotv(xsd, bm, "tn")
        ys.append(ydiag + yoff)
        snews.append(snew)
    return ys, snews


def _ssd_dims(xact, wi):
    T, cd = xact.shape
    G = (cd - wi) // (2 * SSD_STATE)
    hpg = wi // SSD_HEAD_DIM // G
    assert hpg % 2 == 0 and SSD_STATE == LANES
    return T, G, hpg, hpg // 2, T // CHUNK, wi // G


def ssd_scan_fwd(xact, dt, alog, wi, name):
    T, G, hpg, NP, NC, gw = _ssd_dims(xact, wi)
    bo, co = wi // LANES, wi // LANES + G

    def body(x_ref, b_ref, c_ref, dt_ref, al_ref, y_ref, st_ref, s_ref):
        g, c = pl.program_id(0), pl.program_id(1)

        @pl.when(c == 0)
        def _():
            s_ref[...] = jnp.zeros_like(s_ref)

        st_ref[...] = s_ref[...]
        xps = [x_ref[:, p * LANES:(p + 1) * LANES] for p in range(NP)]
        sps = [s_ref[p] for p in range(NP)]
        ys, snews = _ssd_chunk(xps, dt_ref[...], b_ref[...], c_ref[...], sps, al_ref[...], g * hpg)
        for p in range(NP):
            y_ref[:, p * LANES:(p + 1) * LANES] = ys[p]
            s_ref[p] = snews[p]

    return pl.pallas_call(
        body,
        out_shape=(jax.ShapeDtypeStruct((T, wi), F32), jax.ShapeDtypeStruct((G, NC, NP, LANES, SSD_STATE), F32)),
        grid=(G, NC),
        in_specs=[pl.BlockSpec((CHUNK, gw), lambda g, c: (c, g)),
                  pl.BlockSpec((CHUNK, LANES), lambda g, c: (c, bo + g)),
                  pl.BlockSpec((CHUNK, LANES), lambda g, c: (c, co + g)),
                  pl.BlockSpec((CHUNK, LANES), lambda g, c: (c, 0)),
                  pl.BlockSpec((1, LANES), lambda g, c: (0, 0))],
        out_specs=(pl.BlockSpec((CHUNK, gw), lambda g, c: (c, g)),
                   pl.BlockSpec((None, None, NP, LANES, SSD_STATE), lambda g, c: (g, c, 0, 0, 0))),
        scratch_shapes=[pltpu.VMEM((NP, LANES, SSD_STATE), F32)],
        compiler_params=_params(("parallel", "arbitrary")),
        name=name,
    )(xact, xact, xact, dt, alog)


def ssd_scan_bwd(xact, dt, alog, states, dy, wi, name):
    T, G, hpg, NP, NC, gw = _ssd_dims(xact, wi)
    bo, co = wi // LANES, wi // LANES + G
    rev = lambda c: NC - 1 - c

    def body(x_ref, b_ref, c_ref, dt_ref, al_ref, st_ref, dy_ref, dx_ref, db_ref, dc_ref, ddt_ref, dal_ref, ds_ref):
        g, c = pl.program_id(0), pl.program_id(1)

        @pl.when(c == 0)
        def _():
            ds_ref[...] = jnp.zeros_like(ds_ref)

        xps = [x_ref[:, p * LANES:(p + 1) * LANES] for p in range(NP)]
        sps = [st_ref[p] for p in range(NP)]
        f = lambda xps_, dt_, bm_, cm_, sps_, al_: _ssd_chunk(xps_, dt_, bm_, cm_, sps_, al_, g * hpg)
        _, vjp = jax.vjp(f, xps, dt_ref[...], b_ref[...], c_ref[...], sps, al_ref[...])
        dys = [dy_ref[:, p * LANES:(p + 1) * LANES] for p in range(NP)]
        dxps, ddt, dbm, dcm, dsps, dal = vjp((dys, [ds_ref[p] for p in range(NP)]))
        for p in range(NP):
            dx_ref[:, p * LANES:(p + 1) * LANES] = dxps[p]
            ds_ref[p] = dsps[p]
        db_ref[...] = dbm
        dc_ref[...] = dcm
        ddt_ref[...] = ddt

        @pl.when(c == 0)
        def _():
            dal_ref[...] = dal

        @pl.when(c != 0)
        def _():
            dal_ref[...] += dal

    gb = G * SSD_STATE
    return pl.pallas_call(
        body,
        out_shape=(jax.ShapeDtypeStruct((T, wi), F32), jax.ShapeDtypeStruct((T, gb), F32), jax.ShapeDtypeStruct((T, gb), F32),
                   jax.ShapeDtypeStruct((G, T, LANES), F32), jax.ShapeDtypeStruct((G, 1, LANES), F32)),
        grid=(G, NC),
        in_specs=[pl.BlockSpec((CHUNK, gw), lambda g, c: (rev(c), g)),
                  pl.BlockSpec((CHUNK, LANES), lambda g, c: (rev(c), bo + g)),
                  pl.BlockSpec((CHUNK, LANES), lambda g, c: (rev(c), co + g)),
                  pl.BlockSpec((CHUNK, LANES), lambda g, c: (rev(c), 0)),
                  pl.BlockSpec((1, LANES), lambda g, c: (0, 0)),
                  pl.BlockSpec((None, None, NP, LANES, SSD_STATE), lambda g, c: (g, rev(c), 0, 0, 0)),
                  pl.BlockSpec((CHUNK, gw), lambda g, c: (rev(c), g))],
        out_specs=(pl.BlockSpec((CHUNK, gw), lambda g, c: (rev(c), g)),
                   pl.BlockSpec((CHUNK, LANES), lambda g, c: (rev(c), g)),
                   pl.BlockSpec((CHUNK, LANES), lambda g, c: (rev(c), g)),
                   pl.BlockSpec((None, CHUNK, LANES), lambda g, c: (g, rev(c), 0)),
                   pl.BlockSpec((None, 1, LANES), lambda g, c: (g, 0, 0))),
        scratch_shapes=[pltpu.VMEM((NP, LANES, SSD_STATE), F32)],
        compiler_params=_params(("parallel", "arbitrary")),
        name=name,
    )(xact, xact, xact, dt, alog, states, dy)


def _post_f(y, x, z, dcol, ng):
    return (_rms((y + dcol * x) * _silu(z), ng),)


def _post_ops(yssd, xact, zx, dcol, ng, G, tm, grad):
    T, wi = yssd.shape
    gw = wi // G
    blk = lambda a: Op(a, (tm, gw), lambda g, i: (i, g), "tile", grad, gshape=(T, wi))
    vec = lambda v: Op(v, (1, gw), lambda g, i: (0, g), "param", grad)
    return [blk(yssd), blk(xact), blk(zx), vec(dcol), vec(ng)], (G, T // tm), gw


def ssd_fwd(h, g_row, w_zx, w_dt, conv_w, conv_b, dtb, alog, dcol, ng, w_o, tag):
    T = h.shape[0]
    wi = ng.shape[1]
    tm = _pick(T, 256, 128)
    hn = rmsnorm_fwd(h, g_row, f"{tag}_norm")
    zx = matmul(hn, w_zx, "nn", F32, f"{tag}_inzx")
    dtr = matmul(hn, w_dt, "nn", F32, f"{tag}_indt")
    ins, grid, blk = _conv_ops(zx, conv_w, conv_b, wi, False)
    cd = conv_w.shape[1]
    xact = tmap(_conv_f, grid, ins, [((T, cd), F32, blk, lambda o, j: (0, j))], f"{tag}_conv")[0]
    dt = tmap(_dt_f, (1, T // tm), _row_ops([dtr], tm) + [_vec_op(dtb)],
              [((T, LANES), F32, (tm, LANES), lambda o, i: (i, 0))], f"{tag}_dt")[0]
    yssd, states = ssd_scan_fwd(xact, dt, alog, wi, f"{tag}_scan")
    G = states.shape[0]
    ins, grid, gw = _post_ops(yssd, xact, zx, dcol, ng, G, tm, False)
    yn = tmap(_post_f, grid, ins, [((T, wi), _ACT, (tm, gw), lambda g, i: (i, g))], f"{tag}_post")[0]
    out = matmul(yn, w_o, "nn", F32, f"{tag}_wo", epilogue=lambda acc, hh: (acc + hh,), extras=(h,))
    return out, (h, hn, zx, dtr, xact, dt, yssd, states, yn)


def ssd_bwd(dout, saved, g_row, w_zx, w_dt, conv_w, conv_b, dtb, alog, dcol, ng, w_o, tag):
    h, hn, zx, dtr, xact, dt, yssd, states, yn = saved
    T = h.shape[0]
    wi = ng.shape[1]
    tm = _pick(T, 256, 128)
    G = states.shape[0]
    dyn = matmul(dout, w_o, "nt", F32, f"{tag}_dyn")
    dw_o = matmul(yn, dout, "tn", GRAD_DT, f"{tag}_dwo")
    ins, grid, gw = _post_ops(yssd, xact, zx, dcol, ng, G, tm, True)
    dyssd, dxi_skip, dz, ddcol, dng = tmap_vjp(_post_f, grid, ins, [Op(dyn, (tm, gw), lambda g, i: (i, g))],
                                                f"{tag}_dpost", grad_dtypes={2: _ACT})
    dxi, dbm, dcm, ddt_g, dalog_g = ssd_scan_bwd(xact, dt, alog, states, dyssd, wi, f"{tag}_dscan")
    dxact = jnp.concatenate([dxi + dxi_skip, dbm, dcm], axis=1)
    ddt = jnp.sum(ddt_g, axis=0)
    dalog = jnp.sum(dalog_g, axis=0)
    ins, grid, blk = _conv_ops(zx, conv_w, conv_b, wi, True)
    dxbc, dconv_w, dconv_b = tmap_vjp(_conv_f, grid, ins, [Op(dxact, blk, lambda o, j: (0, j))], f"{tag}_dconv",
                                      grad_dtypes={0: _ACT})
    ddtr, ddtb = tmap_vjp(_dt_f, (1, T // tm), _row_ops([dtr], tm) + [_vec_op(dtb)], _row_ops([ddt], tm), f"{tag}_ddt",
                          grad_dtypes={0: _ACT})
    dzx = jnp.concatenate([dz, dxbc], axis=1)
    dw_zx = matmul(hn, dzx, "tn", GRAD_DT, f"{tag}_dwzx")
    dw_dt = matmul(hn, ddtr, "tn", GRAD_DT, f"{tag}_dwdt")
    dhn1 = matmul(ddtr, w_dt, "nt", F32, f"{tag}_dhn1")
    dhn = matmul(dzx, w_zx, "nt", F32, f"{tag}_dhn", epilogue=lambda acc, e: (acc + e,), extras=(dhn1,))
    dh, dg = rmsnorm_bwd(h, g_row, dhn, dout, f"{tag}_dnorm")
    return dh, dg, dw_zx, dw_dt, dconv_w, dconv_b, ddtb, dalog, ddcol, dng, dw_o


def loss_head(y, target, name):
    T, D = y.shape
    tm = _pick(T, 512, 256, 128)

    def body(y_ref, t_ref, dy_ref, part_ref):
        d = y_ref[...] - t_ref[...]
        dy_ref[...] = d * (1.0 / D)
        s = jnp.sum(d * d, axis=0, keepdims=True) * (0.5 / D)

        @pl.when(pl.program_id(0) == 0)
        def _():
            part_ref[...] = s

        @pl.when(pl.program_id(0) != 0)
        def _():
            part_ref[...] += s

    rows = pl.BlockSpec((tm, D), lambda i: (i, 0))
    return pl.pallas_call(
        body,
        out_shape=(jax.ShapeDtypeStruct((T, D), F32), jax.ShapeDtypeStruct((1, D), F32)),
        grid=(T // tm,),
        in_specs=[rows, rows],
        out_specs=(rows, pl.BlockSpec((1, D), lambda i: (0, 0))),
        compiler_params=_params(("arbitrary",)),
        name=name,
    )(y, target)


def _row_tile(R, C, itemsize=4, target=1 << 20):
    for t in (1024, 512, 256, 128, 64, 32, 16, 8):
        if R % t == 0 and t * C * itemsize <= target:
            return t
    return R


def adamw(w, g, m, v, name):
    R, C = w.shape
    tr = _row_tile(R, C)
    c1 = 1.0 - ADAM_B1 ** ADAM_STEP
    c2 = 1.0 - ADAM_B2 ** ADAM_STEP

    def body(w_ref, g_ref, m_ref, v_ref, d_ref, nm_ref, nv_ref):
        gg = g_ref[...]
        nm = ADAM_B1 * m_ref[...] + (1.0 - ADAM_B1) * gg
        nv = ADAM_B2 * v_ref[...] + (1.0 - ADAM_B2) * jnp.square(gg)
        d_ref[...] = -ADAM_LR * ((nm / c1) / (jnp.sqrt(nv / c2) + ADAM_EPS) + ADAM_WD * w_ref[...])
        nm_ref[...] = nm
        nv_ref[...] = nv

    spec = pl.BlockSpec((tr, C), lambda i: (i, 0))
    sds = jax.ShapeDtypeStruct((R, C), F32)
    return pl.pallas_call(body, out_shape=(sds, sds, sds), grid=(R // tr,), in_specs=[spec] * 4, out_specs=(spec,) * 3,
                          compiler_params=_params(("parallel",)), name=name)(w, g, m, v)


def _xyc():
    return lax.axis_index("x"), lax.axis_index("y"), lax.axis_index("c")


_REL = ((1, 0), (0, 1), (1, 1))


def _flip(v, f):
    return 1 - v if f else v


def pair_sum(gfull, recv, name):
    _, _, R, C = gfull.shape
    tr = _row_tile(R, C, 2)

    def body(g_ref, p_ref, o_ref):
        c = lax.axis_index("c")
        o_ref[...] = (g_ref[c].astype(F32) + p_ref[...].astype(F32)).astype(o_ref.dtype)

    return pl.pallas_call(
        body,
        out_shape=jax.ShapeDtypeStruct((4, R, C), gfull.dtype),
        grid=(4, R // tr),
        in_specs=[pl.BlockSpec((None, 2, tr, C), lambda p, i: (p, 0, i, 0)), pl.BlockSpec((None, tr, C), lambda p, i: (p, i, 0))],
        out_specs=pl.BlockSpec((None, tr, C), lambda p, i: (p, i, 0)),
        compiler_params=_params(("parallel", "parallel")),
        name=name,
    )(gfull, recv)


def chip_sum(s, recv, name):
    _, R, C = s.shape
    tr = _row_tile(R, C, 2, 1 << 19)

    def body(c_ref, s_ref, p_ref, o_ref):
        x, y, _ = _xyc()
        acc = s_ref[2 * x + y].astype(F32)
        for r in range(3):
            acc = acc + p_ref[r].astype(F32)
        o_ref[...] = acc

    grid_spec = pltpu.PrefetchScalarGridSpec(
        num_scalar_prefetch=1,
        grid=(R // tr,),
        in_specs=[pl.BlockSpec((4, tr, C), lambda i, c: (0, i, 0)), pl.BlockSpec((3, tr, C), lambda i, c: (0, i, 0))],
        out_specs=pl.BlockSpec((None, tr, C), lambda i, c: (c[0], i, 0)),
    )
    return pl.pallas_call(
        body,
        out_shape=jax.ShapeDtypeStruct((2, R, C), F32),
        grid_spec=grid_spec,
        compiler_params=_params(("arbitrary",)),
        name=name,
    )(lax.axis_index("c").reshape(1).astype(jnp.int32), s, recv)


def sum8(g, name):
    _, R, C = g.shape

    def body(g_ref, o_ref):
        acc = g_ref[0]
        for d in range(1, 8):
            acc = acc + g_ref[d]
        o_ref[...] = acc

    return pl.pallas_call(body, out_shape=jax.ShapeDtypeStruct((R, C), F32), name=name,
                          compiler_params=pltpu.CompilerParams(vmem_limit_bytes=_VMEM_LIMIT))(g)


_ANY = pl.BlockSpec(memory_space=pl.ANY)


def _remote(src, dst, ssem, rsem, dev):
    return pltpu.make_async_remote_copy(src_ref=src, dst_ref=dst, send_sem=ssem, recv_sem=rsem, device_id=dev,
                                        device_id_type=MESH)


def gather_small(x, name):
    R, C = x.shape

    def body(x_ref, out_ref, send_sems, recv_sems, local_sem):
        x_, y_, c_ = _xyc()
        me, sibling = (x_, y_, c_), (x_, y_, 1 - c_)
        chips = [(_flip(x_, fx), _flip(y_, fy)) for fx, fy in _REL]
        slot = lambda px, py, pc: out_ref.at[4 * px + 2 * py + pc]

        def copy(k, block, to, src=None):
            return _remote(slot(*block) if src is None else src, slot(*block), send_sems.at[k], recv_sems.at[k], to)

        mine = pltpu.make_async_copy(x_ref, slot(*me), local_sem)
        mine.start()
        first = [copy(0, me, sibling, src=x_ref)]
        first += [copy(1 + j, me, (*chip, c_), src=x_ref) for j, chip in enumerate(chips)]
        for cp in first:
            cp.start()
        passed = [copy(4 + j, (*chip, c_), sibling) for j, chip in enumerate(chips)]
        for j, chip in enumerate(chips):
            copy(1 + j, (*chip, c_), me).wait_recv()
            passed[j].start()
        copy(0, sibling, me).wait_recv()
        for j, chip in enumerate(chips):
            copy(4 + j, (*chip, 1 - c_), me).wait_recv()
        for cp in first + passed:
            cp.wait_send()
        mine.wait()

    return pl.pallas_call(
        body,
        out_shape=jax.ShapeDtypeStruct((8, R, C), x.dtype),
        in_specs=[pl.BlockSpec(memory_space=pltpu.VMEM)],
        out_specs=pl.BlockSpec(memory_space=pltpu.VMEM),
        scratch_shapes=[pltpu.SemaphoreType.DMA((7,)), pltpu.SemaphoreType.DMA((7,)), pltpu.SemaphoreType.DMA],
        compiler_params=pltpu.CompilerParams(vmem_limit_bytes=_VMEM_LIMIT),
        name=name,
    )(x)


def gather_weights(halves, name):
    n = len(halves)

    def body(*refs):
        ins, outs = refs[:n], refs[n:2 * n]
        send_sems, recv_sems, local_sems = refs[2 * n:]
        x_, y_, c_ = _xyc()
        sibling = (x_, y_, 1 - c_)
        chips = [(_flip(x_, fx), _flip(y_, fy)) for fx, fy in _REL]
        mine = 2 * x_ + y_
        started, local = [], []
        for i in range(n):
            own, dst = ins[i].at[c_], outs[i].at[mine, c_]
            lc = pltpu.make_async_copy(own, dst, local_sems.at[i])
            lc.start()
            local.append(lc)
            for r, chip in enumerate(chips):
                started.append(_remote(own, dst, send_sems.at[7 * i + r], recv_sems.at[7 * i + r], (*chip, c_)))
                started[-1].start()
            started.append(_remote(own, dst, send_sems.at[7 * i + 3], recv_sems.at[7 * i + 3], sibling))
            started[-1].start()
        for i in range(n):
            for r, (px, py) in enumerate(chips):
                blk = outs[i].at[2 * px + py, c_]
                _remote(blk, blk, send_sems.at[7 * i + r], recv_sems.at[7 * i + r], sibling).wait_recv()
                started.append(_remote(blk, blk, send_sems.at[7 * i + 4 + r], recv_sems.at[7 * i + 4 + r], sibling))
                started[-1].start()
        for i in range(n):
            blk = outs[i].at[mine, 1 - c_]
            _remote(blk, blk, send_sems.at[7 * i + 3], recv_sems.at[7 * i + 3], sibling).wait_recv()
            for r, (px, py) in enumerate(chips):
                blk = outs[i].at[2 * px + py, 1 - c_]
                _remote(blk, blk, send_sems.at[7 * i + 4 + r], recv_sems.at[7 * i + 4 + r], sibling).wait_recv()
        for cp in started:
            cp.wait_send()
        for lc in local:
            lc.wait()

    return pl.pallas_call(
        body,
        out_shape=tuple(jax.ShapeDtypeStruct((4,) + h.shape, h.dtype) for h in halves),
        in_specs=[_ANY] * n,
        out_specs=tuple(_ANY for _ in halves),
        scratch_shapes=[pltpu.SemaphoreType.DMA((7 * n,)), pltpu.SemaphoreType.DMA((7 * n,)), pltpu.SemaphoreType.DMA((n,))],
        name=name,
    )(*halves)


def swap_halves(gfulls, name):
    n = len(gfulls)

    def body(*refs):
        ins, outs = refs[:n], refs[n:2 * n]
        send_sems, recv_sems = refs[2 * n:]
        x_, y_, c_ = _xyc()
        sibling = (x_, y_, 1 - c_)
        started = []
        for i in range(n):
            for p in range(4):
                started.append(_remote(ins[i].at[p, 1 - c_], outs[i].at[p], send_sems.at[4 * i + p], recv_sems.at[4 * i + p], sibling))
                started[-1].start()
        for cp in started:
            cp.wait()

    return pl.pallas_call(
        body,
        out_shape=tuple(jax.ShapeDtypeStruct((4,) + g.shape[2:], g.dtype) for g in gfulls),
        in_specs=[_ANY] * n,
        out_specs=tuple(_ANY for _ in gfulls),
        scratch_shapes=[pltpu.SemaphoreType.DMA((4 * n,)), pltpu.SemaphoreType.DMA((4 * n,))],
        name=name,
    )(*gfulls)


def scatter_chips(sums, name):
    n = len(sums)

    def body(*refs):
        ins, outs = refs[:n], refs[n:2 * n]
        send_sems, recv_sems = refs[2 * n:]
        x_, y_, c_ = _xyc()
        chips = [(_flip(x_, fx), _flip(y_, fy)) for fx, fy in _REL]
        started = []
        for i in range(n):
            for r, (px, py) in enumerate(chips):
                started.append(_remote(ins[i].at[2 * px + py], outs[i].at[r], send_sems.at[3 * i + r], recv_sems.at[3 * i + r], (px, py, c_)))
                started[-1].start()
        for cp in started:
            cp.wait()

    return pl.pallas_call(
        body,
        out_shape=tuple(jax.ShapeDtypeStruct((3,) + s.shape[1:], s.dtype) for s in sums),
        in_specs=[_ANY] * n,
        out_specs=tuple(_ANY for _ in sums),
        scratch_shapes=[pltpu.SemaphoreType.DMA((3 * n,)), pltpu.SemaphoreType.DMA((3 * n,))],
        name=name,
    )(*sums)


_JOIN_CHUNK_BYTES = 4 << 20


def join_halves(bufs, name):
    n = len(bufs)
    chunks = []
    for b in bufs:
        _, R, C = b.shape
        k = 1
        while k < 8 and R % (2 * k * 8) == 0 and R * C * 4 // k > _JOIN_CHUNK_BYTES:
            k *= 2
        chunks.append(k)
    base = [sum(chunks[:i]) for i in range(n)]
    total = sum(chunks)

    def body(*refs):
        outs = refs[n:2 * n]
        send_sems, recv_sems = refs[2 * n:]
        x_, y_, c_ = _xyc()
        sibling = (x_, y_, 1 - c_)
        sent = []
        for i in range(n):
            rc = bufs[i].shape[1] // chunks[i]
            for q in range(chunks[i]):
                blk = outs[i].at[c_, pl.ds(q * rc, rc)]
                sent.append(_remote(blk, blk, send_sems.at[base[i] + q], recv_sems.at[base[i] + q], sibling))
                sent[-1].start()
        for i in range(n):
            rc = bufs[i].shape[1] // chunks[i]
            for q in range(chunks[i]):
                blk = outs[i].at[1 - c_, pl.ds(q * rc, rc)]
                _remote(blk, blk, send_sems.at[base[i] + q], recv_sems.at[base[i] + q], sibling).wait_recv()
        for cp in sent:
            cp.wait_send()

    return pl.pallas_call(
        body,
        out_shape=tuple(jax.ShapeDtypeStruct(b.shape, b.dtype) for b in bufs),
        in_specs=[_ANY] * n,
        out_specs=tuple(_ANY for _ in bufs),
        input_output_aliases={i: i for i in range(n)},
        scratch_shapes=[pltpu.SemaphoreType.DMA((total,)), pltpu.SemaphoreType.DMA((total,))],
        name=name,
    )(*bufs)


def _halves(a):
    return a.reshape(2, -1, a.shape[-1])


def _cols_to_full(g):
    _, L, K, n = g.shape
    return g.transpose(1, 2, 0, 3).reshape(L, K, 4 * n)


def _full_to_cols(w):
    L, K, N = w.shape
    return w.reshape(L, K, 4, N // 4).transpose(2, 0, 1, 3)


def _rows_to_full(g):
    _, L, k, N = g.shape
    return g.transpose(1, 0, 2, 3).reshape(L, 4 * k, N)


def _full_to_rows(w):
    L, K, N = w.shape
    return w.reshape(L, 4, K // 4, N).transpose(1, 0, 2, 3)


_BIG = ("sb_w_qkv", "sb_w_o", "gm_w_in", "gm_w_o", "ssd_w_in", "ssd_w_o", "mlp_w_in", "mlp_w_out")
_COLS = ("sb_w_qkv", "gm_w_in", "ssd_w_in", "mlp_w_in")
_SMALL = ("norm_mix_g", "norm_mlp_g", "sb_q_norm_g", "sb_k_norm_g", "gm_v_norm_g", "gm_w_s", "gm_b_s",
          "ssd_conv_w", "ssd_conv_b", "ssd_dt_bias", "ssd_a_log", "ssd_d", "ssd_norm_g")
_SMALL_SHARDED = ("ssd_conv_w", "ssd_conv_b", "ssd_norm_g")
_WEIGHTS = ("norm_mix_g", "norm_mlp_g", "sb_w_qkv", "sb_q_norm_g", "sb_k_norm_g", "sb_w_o", "gm_w_in", "gm_v_norm_g",
            "gm_w_s", "gm_b_s", "gm_w_o", "ssd_w_in", "ssd_conv_w", "ssd_conv_b", "ssd_dt_bias", "ssd_a_log", "ssd_d",
            "ssd_norm_g", "ssd_w_o", "mlp_w_in", "mlp_w_out")


def _pack(arrs):
    flat = jnp.concatenate([a.reshape(-1).astype(F32) for a in arrs])
    n = flat.shape[0]
    tot = -(-n // (8 * LANES)) * 8 * LANES
    return jnp.pad(flat, (0, tot - n)).reshape(-1, LANES)


def _unpack(buf, shapes):
    flat = buf.reshape(-1)
    out, o = [], 0
    for s in shapes:
        n = math.prod(s)
        out.append(flat[o:o + n].reshape(s))
        o += n
    return out


def _step(x, w, target, m, v):
    depth = w["norm_mix_g"].shape[0]
    xc, yc, _ = _xyc()
    chip = 2 * xc + yc
    Hs = w["ssd_dt_bias"].shape[1]
    wi = 4 * w["ssd_norm_g"].shape[1]
    cd = 4 * w["ssd_conv_b"].shape[1]

    gathered = gather_weights([_halves(w[k].astype(_MXU)) for k in _BIG], "gather_weights")
    full = {}
    for k, g in zip(_BIG, gathered):
        g = g.reshape((4,) + w[k].shape)
        full[k] = _cols_to_full(g) if k in _COLS else _rows_to_full(g)
    w_zx = full["ssd_w_in"][0][:, :wi + cd]
    w_dt = jnp.pad(full["ssd_w_in"][0][:, wi + cd:], ((0, 0), (0, LANES - Hs)))
    small_sh = gather_small(_pack([w[k] for k in _SMALL_SHARDED]), "gather_small_weights")
    parts = [_unpack(small_sh[2 * j], [w[k].shape for k in _SMALL_SHARDED]) for j in range(4)]
    conv_w = jnp.concatenate([p[0][0] for p in parts], axis=1)
    conv_b = jnp.concatenate([p[1] for p in parts], axis=1)
    ssd_ng = jnp.concatenate([p[2] for p in parts], axis=1)
    padh = lambda a: jnp.pad(a, ((0, 0), (0, LANES - Hs)))
    dtb, alog = padh(w["ssd_dt_bias"]), padh(w["ssd_a_log"])
    dcol = jnp.repeat(w["ssd_d"], SSD_HEAD_DIM, axis=1)
    bcol = w["gm_b_s"][0][:, :, None]

    h = x[0]
    tape = []
    for i in range(depth):
        kind, j = i % 3, i // 3
        gmix = w["norm_mix_g"][i:i + 1]
        if kind == 0:
            args = (gmix, full["sb_w_qkv"], w["sb_q_norm_g"][j:j + 1], w["sb_k_norm_g"][j:j + 1], full["sb_w_o"], j, f"sb{j}")
            h, sv = sb_fwd(h, *args)
        elif kind == 1:
            args = (gmix, full["gm_w_in"], w["gm_v_norm_g"][j:j + 1], w["gm_w_s"][j], bcol, full["gm_w_o"], j, f"gm{j}")
            h, sv = gm_fwd(h, *args)
        else:
            args = (gmix, w_zx, w_dt, conv_w, conv_b, dtb, alog, dcol, ssd_ng, full["ssd_w_o"][j], f"ssd{j}")
            h, sv = ssd_fwd(h, *args)
        margs = (w["norm_mlp_g"][i:i + 1], full["mlp_w_in"], full["mlp_w_out"], i, f"mlp{i}")
        h, msv = mlp_fwd(h, *margs)
        tape.append((kind, j, args, sv, margs, msv))

    dh, loss_cols = loss_head(h, target[0], "loss_head")

    gw = {k: [None] * w[k].shape[0] for k in ("norm_mix_g", "norm_mlp_g", "sb_w_qkv", "sb_q_norm_g", "sb_k_norm_g", "sb_w_o",
                                                "mlp_w_in", "mlp_w_out")}
    for i in reversed(range(depth)):
        kind, j, args, sv, margs, msv = tape[i]
        dh, gw["norm_mlp_g"][i], gw["mlp_w_in"][i], gw["mlp_w_out"][i] = mlp_bwd(dh, msv, *margs)
        if kind == 0:
            dh, gw["norm_mix_g"][i], gw["sb_w_qkv"][j], gw["sb_q_norm_g"][j], gw["sb_k_norm_g"][j], gw["sb_w_o"][j] = sb_bwd(dh, sv, *args)
        elif kind == 1:
            dh, gw["norm_mix_g"][i], d_in, d_vg, d_ws, d_bcol, d_wo = gm_bwd(dh, sv, *args)
            gw["gm_w_in"], gw["gm_v_norm_g"], gw["gm_w_s"], gw["gm_b_s"], gw["gm_w_o"] = d_in[None], d_vg, d_ws[None], d_bcol[None, :, :, 0], d_wo[None]
        else:
            dh, gw["norm_mix_g"][i], d_zx, d_dt, d_cw, d_cb, d_dtb, d_al, d_dcol, d_ng, d_wo = ssd_bwd(dh, sv, *args)
            gw["ssd_w_in"] = jnp.concatenate([d_zx, d_dt[:, :Hs]], axis=1)[None]
            gw["ssd_conv_w"], gw["ssd_conv_b"], gw["ssd_norm_g"], gw["ssd_w_o"] = d_cw[None], d_cb, d_ng, d_wo[None]
            gw["ssd_dt_bias"], gw["ssd_a_log"] = d_dtb[:, :Hs], d_al[:, :Hs]
            gw["ssd_d"] = jnp.sum(d_dcol.reshape(Hs, SSD_HEAD_DIM), axis=1)[None]
    for k in gw:
        if isinstance(gw[k], list):
            gw[k] = jnp.stack(gw[k]) if k in _BIG else jnp.concatenate(gw[k], axis=0)
    grad_x = dh[None]

    gfull = []
    for k in _BIG:
        g = _full_to_cols(gw[k]) if k in _COLS else _full_to_rows(gw[k])
        gfull.append(g.reshape(4, 2, -1, g.shape[-1]).astype(GRAD_DT))
    from_sibling = swap_halves(gfull, "grad_swap_halves")
    pair = [pair_sum(g, r, f"pair_sum_{k}") for k, g, r in zip(_BIG, gfull, from_sibling)]
    from_chips = scatter_chips(pair, "grad_scatter_chips")
    mine = [chip_sum(s, r, f"chip_sum_{k}") for k, s, r in zip(_BIG, pair, from_chips)]
    joined = join_halves(mine, "grad_join_halves")
    grads, deltas, new_m, new_v = {}, {}, {}, {}
    for k, g in zip(_BIG, joined):
        C = w[k].shape[-1]
        d_, m_, v_ = adamw(w[k].reshape(-1, C), g.reshape(-1, C), m[k].reshape(-1, C), v[k].reshape(-1, C), f"adamw_{k}")
        grads[k], deltas[k], new_m[k], new_v[k] = (a.reshape(w[k].shape) for a in (g, d_, m_, v_))

    full_shapes = [gw[k].shape for k in _SMALL] + [(1,)]
    loss_local = jnp.sum(loss_cols).reshape(1)
    red = sum8(gather_small(_pack([gw[k] for k in _SMALL] + [loss_local]), "gather_small_grads"), "sum_small_grads")
    red = _unpack(red, full_shapes)
    loss = red[-1][0]
    gsm = dict(zip(_SMALL, red[:-1]))
    for k in _SMALL_SHARDED:
        n = w[k].shape[-1]
        gsm[k] = lax.dynamic_slice_in_dim(gsm[k], chip * n, n, axis=gsm[k].ndim - 1)
    shapes = [w[k].shape for k in _SMALL]
    packed = [_pack([d[k] for k in _SMALL]) for d in (w, gsm, m, v)]
    outs = adamw(*packed, "adamw_small")
    for k, g_, d_, m_, v_ in zip(_SMALL, [gsm[k] for k in _SMALL], *[_unpack(o, shapes) for o in outs]):
        grads[k], deltas[k], new_m[k], new_v[k] = g_.reshape(w[k].shape), d_, m_, v_

    return (loss, grad_x, *[grads[k] for k in _WEIGHTS], *[deltas[k] for k in _WEIGHTS],
            *[new_m[k] for k in _WEIGHTS], *[new_v[k] for k in _WEIGHTS])


def kernel(x, norm_mix_g, norm_mlp_g, sb_w_qkv, sb_q_norm_g, sb_k_norm_g, sb_w_o, gm_w_in, gm_v_norm_g, gm_w_s, gm_b_s, gm_w_o, ssd_w_in, ssd_conv_w, ssd_conv_b, ssd_dt_bias, ssd_a_log, ssd_d, ssd_norm_g, ssd_w_o, mlp_w_in, mlp_w_out, loss_target, m_norm_mix_g, m_norm_mlp_g, m_sb_w_qkv, m_sb_q_norm_g, m_sb_k_norm_g, m_sb_w_o, m_gm_w_in, m_gm_v_norm_g, m_gm_w_s, m_gm_b_s, m_gm_w_o, m_ssd_w_in, m_ssd_conv_w, m_ssd_conv_b, m_ssd_dt_bias, m_ssd_a_log, m_ssd_d, m_ssd_norm_g, m_ssd_w_o, m_mlp_w_in, m_mlp_w_out, v_norm_mix_g, v_norm_mlp_g, v_sb_w_qkv, v_sb_q_norm_g, v_sb_k_norm_g, v_sb_w_o, v_gm_w_in, v_gm_v_norm_g, v_gm_w_s, v_gm_b_s, v_gm_w_o, v_ssd_w_in, v_ssd_conv_w, v_ssd_conv_b, v_ssd_dt_bias, v_ssd_a_log, v_ssd_d, v_ssd_norm_g, v_ssd_w_o, v_mlp_w_in, v_mlp_w_out):
    w = dict(zip(_WEIGHTS, (norm_mix_g, norm_mlp_g, sb_w_qkv, sb_q_norm_g, sb_k_norm_g, sb_w_o, gm_w_in, gm_v_norm_g, gm_w_s,
                            gm_b_s, gm_w_o, ssd_w_in, ssd_conv_w, ssd_conv_b, ssd_dt_bias, ssd_a_log, ssd_d, ssd_norm_g,
                            ssd_w_o, mlp_w_in, mlp_w_out)))
    m = dict(zip(_WEIGHTS, (m_norm_mix_g, m_norm_mlp_g, m_sb_w_qkv, m_sb_q_norm_g, m_sb_k_norm_g, m_sb_w_o, m_gm_w_in,
                            m_gm_v_norm_g, m_gm_w_s, m_gm_b_s, m_gm_w_o, m_ssd_w_in, m_ssd_conv_w, m_ssd_conv_b,
                            m_ssd_dt_bias, m_ssd_a_log, m_ssd_d, m_ssd_norm_g, m_ssd_w_o, m_mlp_w_in, m_mlp_w_out)))
    v = dict(zip(_WEIGHTS, (v_norm_mix_g, v_norm_mlp_g, v_sb_w_qkv, v_sb_q_norm_g, v_sb_k_norm_g, v_sb_w_o, v_gm_w_in,
                            v_gm_v_norm_g, v_gm_w_s, v_gm_b_s, v_gm_w_o, v_ssd_w_in, v_ssd_conv_w, v_ssd_conv_b,
                            v_ssd_dt_bias, v_ssd_a_log, v_ssd_d, v_ssd_norm_g, v_ssd_w_o, v_mlp_w_in, v_mlp_w_out)))
    return _step(x, w, loss_target, m, v)
```

```python
import functools
import math

import jax
import jax.numpy as jnp
from jax import lax
from jax.experimental import pallas as pl
from jax.experimental.pallas import tpu as pltpu

F32 = jnp.float32
BF16 = jnp.bfloat16
_MXU = jnp.bfloat16
_ACT = jnp.bfloat16
GRAD_DT = jnp.bfloat16
_VMEM_LIMIT = 56 * 1024 * 1024
EPS = 1e-6
LANES = 128
CHUNK = 128
SSD_HEAD_DIM = 64
SSD_STATE = 128
SSD_CONV = 4
ADAM_LR, ADAM_B1, ADAM_B2, ADAM_EPS, ADAM_WD, ADAM_STEP = 1e-3, 0.9, 0.999, 1e-8, 0.01, 10
MESH = pl.DeviceIdType.MESH

NN = (((1,), (0,)), ((), ()))
NT = (((1,), (1,)), ((), ()))
TN = (((0,), (0,)), ((), ()))

_ANY = pl.BlockSpec(memory_space=pl.ANY)
_REL = ((1, 0), (0, 1), (1, 1))


def _dot(a, b, dims=NN):
    return lax.dot_general(a.astype(_MXU), b.astype(_MXU), dims, preferred_element_type=F32)


def _params(sem):
    return pltpu.CompilerParams(dimension_semantics=sem, vmem_limit_bytes=_VMEM_LIMIT)


def _pick(n, *cands):
    for c in cands:
        if n % c == 0:
            return c
    return n


def _xyc():
    return lax.axis_index("x"), lax.axis_index("y"), lax.axis_index("c")


def _flip(v, f):
    return 1 - v if f else v


def _remote(src, dst, ssem, rsem, dev):
    return pltpu.make_async_remote_copy(src_ref=src, dst_ref=dst, send_sem=ssem, recv_sem=rsem, device_id=dev,
                                        device_id_type=MESH)


class Carriers:
    def __init__(self):
        self.pending, self.done = [], {}

    def add(self, kind, src, tag):
        self.pending.append((kind, src, tag))


def matmul(a, b, mode, out_dtypes, name, epilogue=None, extras=(), b_cols4=False, out_cols4=False, cq=None):
    ash = a.shape
    bsh = (b.shape[1], 4 * b.shape[2]) if b_cols4 else b.shape
    if mode == "nn":
        (M, K), (K2, N) = ash, bsh
    elif mode == "nt":
        (M, K), (N, K2) = ash, bsh
    else:
        (K, M), (K2, N) = ash, bsh
    assert K == K2, (mode, a.shape, b.shape)
    nsh = N // 4 if (out_cols4 or (b_cols4 and mode == "nn")) else None
    ksh = K // 4 if (b_cols4 and mode == "nt") else None
    fits = lambda n, sh: [c for c in (1024, 512, 256, 128) if sh is None or sh % c == 0] + ([] if sh is None else [sh])
    tm = _pick(M, 1024, 512, 256, 128) if a.dtype.itemsize == 2 and not extras else _pick(M, 512, 256, 128)
    tn = _pick(N, *fits(N, nsh))
    tk = _pick(K, *([2048] if ksh is None or ksh % 2048 == 0 else []), *fits(K, ksh))
    ni, nj, nk = M // tm, N // tn, K // tk
    dims = {"nn": NN, "nt": NT, "tn": TN}[mode]
    single = not isinstance(out_dtypes, (tuple, list))
    odt = (out_dtypes,) if single else tuple(out_dtypes)
    n_ex, n_out = len(extras), len(odt)
    carry = cq.pending.pop(0) if (cq is not None and cq.pending) else None

    if mode == "tn":
        a_spec = pl.BlockSpec((tk, tm), lambda i, j, k: (k, i))
    else:
        a_spec = pl.BlockSpec((tm, tk), lambda i, j, k: (i, k))
    if b_cols4 and mode == "nn":
        b_spec = pl.BlockSpec((None, tk, tn), lambda i, j, k: (lax.div(j * tn, nsh), k, lax.div(lax.rem(j * tn, nsh), tn)))
    elif b_cols4:
        b_spec = pl.BlockSpec((None, tn, tk), lambda i, j, k: (lax.div(k * tk, ksh), j, lax.div(lax.rem(k * tk, ksh), tk)))
    elif mode == "nt":
        b_spec = pl.BlockSpec((tn, tk), lambda i, j, k: (j, k))
    else:
        b_spec = pl.BlockSpec((tk, tn), lambda i, j, k: (k, j))
    mn_spec = pl.BlockSpec((tm, tn), lambda i, j, k: (i, j))
    if out_cols4:
        o_spec = pl.BlockSpec((None, tm, tn), lambda i, j, k: (lax.div(j * tn, nsh), i, lax.div(lax.rem(j * tn, nsh), tn)))
        o_shape = (4, M, N // 4)
    else:
        o_spec, o_shape = mn_spec, (M, N)

    def body(*refs):
        a_ref, b_ref = refs[0], refs[1]
        ex = refs[2:2 + n_ex]
        pos = 2 + n_ex
        src_ref = refs[pos] if carry else None
        pos += 1 if carry else 0
        outs = refs[pos:pos + n_out]
        pos += n_out
        dst_ref = refs[pos] if carry else None
        pos += 1 if carry else 0
        acc = refs[pos]
        i, j, k = pl.program_id(0), pl.program_id(1), pl.program_id(2)

        if carry:
            ssem, rsem = refs[pos + 1], refs[pos + 2]
            x_, y_, c_ = _xyc()
            chips = [(_flip(x_, fx), _flip(y_, fy)) for fx, fy in _REL]

            def copies(arriving):
                out = []
                for r, (px, py) in enumerate(chips):
                    if carry[0] == "gather":
                        s_, d_ = src_ref.at[c_], dst_ref.at[(2 * px + py) if arriving else (2 * x_ + y_), c_]
                    else:
                        s_, d_ = src_ref.at[2 * px + py], dst_ref.at[r]
                    out.append(_remote(s_, d_, ssem.at[r], rsem.at[r], (px, py, c_)))
                return out

            @pl.when((i == 0) & (j == 0) & (k == 0))
            def _():
                for send in copies(False):
                    send.start()

        @pl.when(k == 0)
        def _():
            acc[...] = jnp.zeros_like(acc)

        part = _dot(a_ref[...], b_ref[...], dims)

        @pl.when(k < nk - 1)
        def _():
            acc[...] += part

        @pl.when(k == nk - 1)
        def _():
            r = acc[...] + part
            res = (r,) if epilogue is None else epilogue(r, *[e[...] for e in ex])
            for o, v in zip(outs, res):
                o[...] = v.astype(o.dtype)

        if carry:
            @pl.when((i == ni - 1) & (j == nj - 1) & (k == nk - 1))
            def _():
                for arrive in copies(True):
                    arrive.wait_recv()
                for send in copies(False):
                    send.wait_send()

    in_specs = [a_spec, b_spec] + [mn_spec] * n_ex
    out_shape = [jax.ShapeDtypeStruct(o_shape, d) for d in odt]
    out_specs = [o_spec for _ in odt]
    scratch = [pltpu.VMEM((tm, tn), F32)]
    operands = [a, b, *extras]
    if carry:
        kind, src, tag = carry
        in_specs.append(_ANY)
        operands.append(src)
        out_shape.append(jax.ShapeDtypeStruct(((4,) + src.shape) if kind == "gather" else ((3,) + src.shape[1:]), src.dtype))
        out_specs.append(_ANY)
        scratch += [pltpu.SemaphoreType.DMA((3,)), pltpu.SemaphoreType.DMA((3,))]
    out = pl.pallas_call(
        body,
        out_shape=tuple(out_shape),
        grid=(ni, nj, nk),
        in_specs=in_specs,
        out_specs=tuple(out_specs),
        scratch_shapes=scratch,
        compiler_params=_params(("arbitrary", "arbitrary", "arbitrary") if carry else ("parallel", "parallel", "arbitrary")),
        name=name,
    )(*operands)
    if carry:
        cq.done[carry[2]] = out[-1]
        out = out[:-1]
    return out[0] if single else out


class Op:
    def __init__(self, arr, block, imap, kind="tile", grad=True, gshape=None, gimap=None):
        self.arr, self.block, self.imap, self.kind, self.grad = arr, block, imap, kind, grad
        self.gshape = gshape or arr.shape
        self.gimap = gimap or imap

    def spec(self):
        return pl.BlockSpec(self.block, self.imap)


def tmap(f, grid, ins, outs, name):
    n_in = len(ins)

    def body(*refs):
        res = f(*[r[...] for r in refs[:n_in]])
        for o, v in zip(refs[n_in:], res):
            o[...] = v.astype(o.dtype)

    return pl.pallas_call(
        body,
        out_shape=tuple(jax.ShapeDtypeStruct(s, d) for s, d, _, _ in outs),
        grid=grid,
        in_specs=[o.spec() for o in ins],
        out_specs=tuple(pl.BlockSpec(b, m) for _, _, b, m in outs),
        compiler_params=_params(("parallel", "parallel")),
        name=name,
    )(*[o.arr for o in ins])


def tmap_vjp(f, grid, ins, cts, name, grad_dtypes=None):
    n_in, n_ct = len(ins), len(cts)
    gidx = [i for i, o in enumerate(ins) if o.grad]
    gdt = grad_dtypes or {}

    def body(*refs):
        in_refs, ct_refs, g_refs = refs[:n_in], refs[n_in:n_in + n_ct], refs[n_in + n_ct:]
        vals = [r[...] for r in in_refs]

        def g_only(*diff):
            full = list(vals)
            for i, v in zip(gidx, diff):
                full[i] = v
            return f(*full)

        res, vjp = jax.vjp(g_only, *[vals[i].astype(F32) for i in gidx])
        grads = vjp(tuple(c[...].astype(r.dtype) for c, r in zip(ct_refs, res)))
        inner = pl.program_id(1)
        for i, g, gr in zip(gidx, grads, g_refs):
            if ins[i].kind == "tile":
                gr[...] = g.astype(gr.dtype)
            else:
                @pl.when(inner == 0)
                def _(gr=gr, g=g):
                    gr[...] = g.astype(gr.dtype)

                @pl.when(inner != 0)
                def _(gr=gr, g=g):
                    gr[...] += g.astype(gr.dtype)

    out_shape = tuple(jax.ShapeDtypeStruct(ins[i].gshape, gdt.get(i, F32)) for i in gidx)
    return pl.pallas_call(
        body,
        out_shape=out_shape,
        grid=grid,
        in_specs=[o.spec() for o in ins] + [o.spec() for o in cts],
        out_specs=tuple(pl.BlockSpec(ins[i].block, ins[i].gimap) for i in gidx),
        compiler_params=_params(("parallel", "arbitrary")),
        name=name,
    )(*[o.arr for o in ins], *[o.arr for o in cts])


def _rms(x, g):
    return x * lax.rsqrt(jnp.mean(x * x, axis=-1, keepdims=True) + EPS) * g


def _row_ops(arrs, tm, grads=None):
    grads = grads or [True] * len(arrs)
    return [Op(a, (tm, a.shape[1]), lambda o, i: (i, 0), "tile", g) for a, g in zip(arrs, grads)]


def _vec_op(v, grad=True):
    return Op(v, (1, v.shape[1]), lambda o, i: (0, 0), "param", grad)


def rmsnorm_fwd(h, g, name):
    T, D = h.shape
    tm = _pick(T, 512, 256, 128)
    f = lambda x, gg: (_rms(x, gg),)
    return tmap(f, (1, T // tm), _row_ops([h], tm) + [_vec_op(g)],
                [((T, D), _ACT, (tm, D), lambda o, i: (i, 0))], name)[0]


def rmsnorm_bwd(h, g, dhn, dres, name):
    T, D = h.shape
    tm = _pick(T, 512, 256, 128)
    f = lambda x, gg: (_rms(x, gg), x)
    return tmap_vjp(f, (1, T // tm), _row_ops([h], tm) + [_vec_op(g)], _row_ops([dhn, dres], tm), name)


def mlp_fwd(h, g_row, w_in, w_out, tag, cq=None):
    hn = rmsnorm_fwd(h, g_row, f"{tag}_norm")
    a, r2 = matmul(hn, w_in, "nn", (F32, _ACT), f"{tag}_in", b_cols4=True, cq=cq,
                   epilogue=lambda acc: (acc, jnp.square(jnp.maximum(acc, 0.0))))
    out = matmul(r2, w_out, "nn", F32, f"{tag}_out", cq=cq, epilogue=lambda acc, hh: (acc + hh,), extras=(h,))
    return out, (h, hn, a, r2)


def mlp_bwd(dout, saved, g_row, w_in, w_out, tag, cq=None):
    h, hn, a, r2 = saved
    da = matmul(dout, w_out, "nt", _ACT, f"{tag}_dact", cq=cq,
                epilogue=lambda acc, aa: (acc * (2.0 * jnp.maximum(aa, 0.0)),), extras=(a,))
    dw_out = matmul(r2, dout, "tn", GRAD_DT, f"{tag}_dwout", cq=cq)
    dw_in = matmul(hn, da, "tn", GRAD_DT, f"{tag}_dwin", out_cols4=True, cq=cq)
    dhn = matmul(da, w_in, "nt", F32, f"{tag}_dhn", b_cols4=True, cq=cq)
    dh, dg = rmsnorm_bwd(h, g_row, dhn, dout, f"{tag}_dnorm")
    return dh, dg, dw_in, dw_out


def _split3(x):
    hi = x.astype(BF16)
    r = x - hi.astype(F32)
    mid = r.astype(BF16)
    lo = (r - mid.astype(F32)).astype(BF16)
    return hi, mid, lo


def _cumdot(x, tri):
    return sum(lax.dot_general(p, tri, NN, preferred_element_type=F32) for p in _split3(x))


def _iotas():
    row = lax.broadcasted_iota(jnp.int32, (CHUNK, CHUNK), 0)
    col = lax.broadcasted_iota(jnp.int32, (CHUNK, CHUNK), 1)
    return row, col


_TQ = 256
_TK = 256
_DEAD = -88.0


def _sb_block(q, kblk, q0, k0, scale, row, col):
    z = _dot(q, kblk, NT) * scale
    e = jnp.exp(-jnp.abs(z))
    den = 1.0 + e
    sp = jnp.maximum(z, 0.0) + jnp.log(den)
    mask = (col + k0) < (row + q0)
    lg = jnp.where(mask, -sp, 0.0)
    beta = jnp.where(z >= 0, 1.0, e) / den
    return z, mask, lg, beta


def _attn_iotas(tq, tk):
    row = lax.broadcasted_iota(jnp.int32, (tq, tk), 0)
    col = lax.broadcasted_iota(jnp.int32, (tq, tk), 1)
    r2 = lax.broadcasted_iota(jnp.int32, (tk, tk), 0)
    c2 = lax.broadcasted_iota(jnp.int32, (tk, tk), 1)
    return row, col, r2, c2


def attn_fwd(qn, kn, v, name):
    T, W = qn.shape
    tq, tk = _pick(T, _TQ, CHUNK), _pick(T, _TK, CHUNK)
    H, NQ, per = W // LANES, T // tq, tq // tk
    assert tq % tk == 0 and T // tk <= LANES
    scale = LANES ** -0.5

    def body(q_ref, k_ref, v_ref, o_ref, r_ref, acc_ref, run_ref):
        qi = pl.program_id(1)
        q = q_ref[...]
        row, col, r2, c2 = _attn_iotas(tq, tk)
        suffix = (r2 >= c2).astype(_MXU)
        lane_q = lax.broadcasted_iota(jnp.int32, (tq, LANES), 1)
        acc_ref[...] = jnp.zeros_like(acc_ref)
        run_ref[...] = jnp.zeros_like(run_ref)
        r_ref[...] = jnp.full(r_ref.shape, -1e30, F32)

        def step(carry):
            kb, _ = carry
            off = pl.multiple_of(kb * tk, tk)
            run = run_ref[...]
            z, mask, lg, _ = _sb_block(q, k_ref[pl.ds(off, tk), :], qi * tq, off, scale, row, col)
            r_ref[...] = jnp.where(lane_q == kb, run, r_ref[...])
            cl = _cumdot(lg, suffix) + run
            a = jnp.exp(jnp.where(mask, z + cl, -1e30))
            acc_ref[...] += _dot(a, v_ref[pl.ds(off, tk), :])
            run = run + jnp.sum(lg, axis=1, keepdims=True)
            run_ref[...] = run
            return kb - 1, jnp.max(run) > _DEAD

        lax.while_loop(lambda c: (c[0] >= 0) & c[1], step, ((qi + 1) * per - 1, True))
        o_ref[...] = acc_ref[...].astype(o_ref.dtype)

    qspec = pl.BlockSpec((tq, LANES), lambda h, i: (i, h))
    kvspec = pl.BlockSpec((T, LANES), lambda h, i: (0, h))
    return pl.pallas_call(
        body,
        out_shape=(jax.ShapeDtypeStruct((T, W), _ACT), jax.ShapeDtypeStruct((H, T, LANES), F32)),
        grid=(H, NQ),
        in_specs=[qspec, kvspec, kvspec],
        out_specs=(qspec, pl.BlockSpec((None, tq, LANES), lambda h, i: (h, i, 0))),
        scratch_shapes=[pltpu.VMEM((tq, LANES), F32), pltpu.VMEM((tq, 1), F32)],
        compiler_params=_params(("parallel", "parallel")),
        name=name,
    )(qn, kn, v)


def attn_bwd(qn, kn, v, do, rblk, name):
    T, W = qn.shape
    tq, tk = _pick(T, _TQ, CHUNK), _pick(T, _TK, CHUNK)
    H, NQ, per = W // LANES, T // tq, tq // tk
    scale = LANES ** -0.5

    def body(q_ref, k_ref, v_ref, do_ref, r_ref, dq_ref, dk_ref, dv_ref, g_ref):
        qi = pl.program_id(1)

        @pl.when(qi == 0)
        def _():
            dk_ref[...] = jnp.zeros_like(dk_ref)
            dv_ref[...] = jnp.zeros_like(dv_ref)

        q = q_ref[...]
        dout = do_ref[...]
        rt = r_ref[...]
        row, col, r2, c2 = _attn_iotas(tq, tk)
        suffix = (r2 >= c2).astype(_MXU)
        prefix = (r2 <= c2).astype(_MXU)
        kend = (qi + 1) * per - 1
        lane = lax.broadcasted_iota(jnp.int32, (1, LANES), 1)
        lane_q = lax.broadcasted_iota(jnp.int32, (tq, LANES), 1)
        unvisited = (jnp.max(rt, axis=0, keepdims=True) < -1e29) & (lane <= kend)
        start = jnp.sum(unvisited.astype(jnp.int32))

        dq_ref[...] = jnp.zeros_like(dq_ref)
        g_ref[...] = jnp.zeros_like(g_ref)

        @pl.loop(start, kend + 1)
        def _(kb):
            gsum = g_ref[...]
            off = pl.multiple_of(kb * tk, tk)
            kblk = k_ref[pl.ds(off, tk), :]
            vblk = v_ref[pl.ds(off, tk), :]
            z, mask, lg, beta = _sb_block(q, kblk, qi * tq, off, scale, row, col)
            run = jnp.sum(jnp.where(lane_q == kb, rt, 0.0), axis=1, keepdims=True)
            cl = _cumdot(lg, suffix) + run
            a = jnp.exp(jnp.where(mask, z + cl, -1e30))
            e = _dot(dout, vblk, NT) * a
            f = _cumdot(e, prefix) + gsum
            dz = jnp.where(mask, e - beta * f, 0.0) * scale
            dk_ref[pl.ds(off, tk), :] += _dot(dz, q, TN)
            dv_ref[pl.ds(off, tk), :] += _dot(a, dout, TN)
            dq_ref[...] += _dot(dz, kblk)
            g_ref[...] = gsum + jnp.sum(e, axis=1, keepdims=True)

    qspec = pl.BlockSpec((tq, LANES), lambda h, i: (i, h))
    kvspec = pl.BlockSpec((T, LANES), lambda h, i: (0, h))
    big = jax.ShapeDtypeStruct((T, W), F32)
    return pl.pallas_call(
        body,
        out_shape=(big, big, big),
        grid=(H, NQ),
        in_specs=[qspec, kvspec, kvspec, qspec, pl.BlockSpec((None, tq, LANES), lambda h, i: (h, i, 0))],
        out_specs=(qspec, kvspec, kvspec),
        scratch_shapes=[pltpu.VMEM((tq, 1), F32)],
        compiler_params=_params(("parallel", "arbitrary")),
        name=name,
    )(qn, kn, v, do, rblk)


def _qk_ops(qkv, qg, kg, tm, grad):
    T, W3 = qkv.shape
    H, NT_ = W3 // (3 * LANES), T // tm
    W = H * LANES

    def part(p):
        return Op(qkv, (tm, LANES), lambda o, n: (lax.rem(n, NT_), p * H + lax.div(n, NT_)), "tile", grad,
                  gshape=(T, W), gimap=lambda o, n: (lax.rem(n, NT_), lax.div(n, NT_)))

    vec = lambda g: Op(g, (1, LANES), lambda o, n: (0, 0), "param", grad)
    return [part(0), part(1), part(2), vec(qg), vec(kg)], (1, H * NT_), H, NT_, W


def _qk_f(q, k, v, qg, kg):
    return _rms(q, qg), _rms(k, kg), v


def qknorm_fwd(qkv, qg, kg, name):
    T = qkv.shape[0]
    tm = _pick(T, 512, 256, 128)
    ins, grid, H, NT_, W = _qk_ops(qkv, qg, kg, tm, False)
    out = ((T, W), _ACT, (tm, LANES), lambda o, n: (lax.rem(n, NT_), lax.div(n, NT_)))
    return tmap(_qk_f, grid, ins, [out, out, out], name)


def qknorm_bwd(qkv, qg, kg, dq, dk, dv, name):
    T = qkv.shape[0]
    tm = _pick(T, 512, 256, 128)
    ins, grid, H, NT_, W = _qk_ops(qkv, qg, kg, tm, True)
    cts = [Op(c, (tm, LANES), lambda o, n: (lax.rem(n, NT_), lax.div(n, NT_))) for c in (dq, dk, dv)]
    return tmap_vjp(_qk_f, grid, ins, cts, name, grad_dtypes={0: _ACT, 1: _ACT, 2: _ACT})


def sb_fwd(h, g_row, w_qkv, qg, kg, w_o, tag, cq=None):
    hn = rmsnorm_fwd(h, g_row, f"{tag}_norm")
    qkv = matmul(hn, w_qkv, "nn", F32, f"{tag}_qkv", b_cols4=True, cq=cq)
    qn, kn, v = qknorm_fwd(qkv, qg, kg, f"{tag}_qknorm")
    o, rblk = attn_fwd(qn, kn, v, f"{tag}_attn")
    out = matmul(o, w_o, "nn", F32, f"{tag}_wo", cq=cq, epilogue=lambda acc, hh: (acc + hh,), extras=(h,))
    return out, (h, hn, qkv, qn, kn, v, o, rblk)


def sb_bwd(dout, saved, g_row, w_qkv, qg, kg, w_o, tag, cq=None):
    h, hn, qkv, qn, kn, v, o, rblk = saved
    do = matmul(dout, w_o, "nt", _ACT, f"{tag}_do")
    dw_o = matmul(o, dout, "tn", GRAD_DT, f"{tag}_dwo")
    dqn, dkn, dv = attn_bwd(qn, kn, v, do, rblk, f"{tag}_dattn")
    dq, dk, dvv, dqg, dkg = qknorm_bwd(qkv, qg, kg, dqn, dkn, dv, f"{tag}_dqknorm")
    dqkv = jnp.concatenate([dq, dk, dvv], axis=1)
    dw_qkv = matmul(hn, dqkv, "tn", GRAD_DT, f"{tag}_dwqkv", out_cols4=True)
    dhn = matmul(dqkv, w_qkv, "nt", F32, f"{tag}_dhn", b_cols4=True)
    dh, dg = rmsnorm_bwd(h, g_row, dhn, dout, f"{tag}_dnorm")
    return dh, dg, dw_qkv, dqg, dkg, dw_o


@functools.partial(jax.custom_vjp, nondiff_argnums=(2,))
def _dotv(a, b, mode):
    return _dot(a, b, {"nn": NN, "nt": NT, "tn": TN}[mode])


def _dotv_fwd(a, b, mode):
    return _dotv(a, b, mode), (a, b)


def _dotv_bwd(mode, res, g):
    a, b = res
    if mode == "nn":
        return _dot(g, b, NT), _dot(a, g, TN)
    if mode == "nt":
        return _dot(g, b, NN), _dot(g, a, TN)
    return _dot(b, g, NT), _dot(a, g, NN)


_dotv.defvjp(_dotv_fwd, _dotv_bwd)


def _gelu(x):
    return 0.5 * x * (1.0 + lax.erf(x * (2.0 ** -0.5)))


def _gm1_f(au, av, vg):
    return _gelu(au), _rms(_gelu(av), vg)


def _gm1_ops(a, vg, tm, grad):
    T, W2 = a.shape
    W = W2 // 2
    part = lambda p: Op(a, (tm, W), lambda o, i: (i, p), "tile", grad, gshape=(T, W), gimap=lambda o, i: (i, 0))
    return [part(0), part(1), _vec_op(vg, grad)], (1, T // tm), W


_GM_ROWS = 1024


def _gm_specs(T, W, G):
    rows = _pick(T, _GM_ROWS, 512, 256, CHUNK)
    blk = pl.BlockSpec((rows, LANES), lambda g, c: (c, g))
    wspec = pl.BlockSpec((None, CHUNK, CHUNK), lambda g, c: (g, 0, 0))
    bspec = pl.BlockSpec((None, CHUNK, 1), lambda g, c: (g, 0, 0))
    return rows, blk, wspec, bspec, (G, T // rows)


def gm_mix_fwd(u, vn, ws, bcol, name):
    T, W = u.shape
    rows, blk, wspec, bspec, grid = _gm_specs(T, W, W // LANES)

    def body(u_ref, v_ref, w_ref, b_ref, y_ref):
        r, c = _iotas()
        w = jnp.where(r >= c, w_ref[...], 0.0).astype(_MXU)
        for k in range(rows // CHUNK):
            sl = pl.ds(k * CHUNK, CHUNK)
            y_ref[sl, :] = (u_ref[sl, :] * (_dot(w, v_ref[sl, :]) + b_ref[...])).astype(y_ref.dtype)

    return pl.pallas_call(body, out_shape=jax.ShapeDtypeStruct((T, W), _ACT), grid=grid,
                          in_specs=[blk, blk, wspec, bspec], out_specs=blk,
                          compiler_params=_params(("parallel", "parallel")), name=name)(u, vn, ws, bcol)


def gm_mix_bwd(u, vn, ws, bcol, dy, name):
    T, W = u.shape
    G = W // LANES
    rows, blk, wspec, bspec, grid = _gm_specs(T, W, G)

    def body(u_ref, v_ref, w_ref, b_ref, dy_ref, du_ref, dv_ref, dw_ref, db_ref):
        r, c = _iotas()
        tri = r >= c
        w = jnp.where(tri, w_ref[...], 0.0).astype(_MXU)
        dw = jnp.zeros((CHUNK, CHUNK), F32)
        db = jnp.zeros((CHUNK, 1), F32)
        for k in range(rows // CHUNK):
            sl = pl.ds(k * CHUNK, CHUNK)
            v = v_ref[sl, :]
            g = dy_ref[sl, :]
            du_ref[sl, :] = g * (_dot(w, v) + b_ref[...])
            dm = g * u_ref[sl, :]
            dv_ref[sl, :] = _dot(w, dm, TN)
            dw = dw + _dot(dm, v, NT)
            db = db + jnp.sum(dm, axis=1, keepdims=True)
        dw = jnp.where(tri, dw, 0.0)

        @pl.when(pl.program_id(1) == 0)
        def _():
            dw_ref[...] = dw
            db_ref[...] = db

        @pl.when(pl.program_id(1) != 0)
        def _():
            dw_ref[...] += dw
            db_ref[...] += db

    big = jax.ShapeDtypeStruct((T, W), F32)
    return pl.pallas_call(
        body,
        out_shape=(big, big, jax.ShapeDtypeStruct((G, CHUNK, CHUNK), F32), jax.ShapeDtypeStruct((G, CHUNK, 1), F32)),
        grid=grid, in_specs=[blk, blk, wspec, bspec, blk], out_specs=(blk, blk, wspec, bspec),
        compiler_params=_params(("parallel", "arbitrary")), name=name)(u, vn, ws, bcol, dy)


def gm_fwd(h, g_row, w_in, vg, ws, bcol, w_o, tag, cq=None):
    T = h.shape[0]
    tm = _pick(T, 256, 128)
    hn = rmsnorm_fwd(h, g_row, f"{tag}_norm")
    a = matmul(hn, w_in, "nn", F32, f"{tag}_in", b_cols4=True, cq=cq)
    ins, grid, W = _gm1_ops(a, vg, tm, False)
    rows = lambda dt: ((T, W), dt, (tm, W), lambda o, i: (i, 0))
    u, vn = tmap(_gm1_f, grid, ins, [rows(F32), rows(_ACT)], f"{tag}_act")
    y = gm_mix_fwd(u, vn, ws, bcol, f"{tag}_mix")
    out = matmul(y, w_o, "nn", F32, f"{tag}_wo", cq=cq, epilogue=lambda acc, hh: (acc + hh,), extras=(h,))
    return out, (h, hn, a, u, vn, y)


def gm_bwd(dout, saved, g_row, w_in, vg, ws, bcol, w_o, tag, cq=None):
    h, hn, a, u, vn, y = saved
    T = h.shape[0]
    tm = _pick(T, 256, 128)
    dy = matmul(dout, w_o, "nt", F32, f"{tag}_dy")
    dw_o = matmul(y, dout, "tn", GRAD_DT, f"{tag}_dwo")
    du, dvn, dws, dbcol = gm_mix_bwd(u, vn, ws, bcol, dy, f"{tag}_dmix")
    ins, grid, W = _gm1_ops(a, vg, tm, True)
    dau, dav, dvg = tmap_vjp(_gm1_f, grid, ins, _row_ops([du, dvn], tm), f"{tag}_dact", grad_dtypes={0: _ACT, 1: _ACT})
    da = jnp.concatenate([dau, dav], axis=1)
    dw_in = matmul(hn, da, "tn", GRAD_DT, f"{tag}_dwin", out_cols4=True)
    dhn = matmul(da, w_in, "nt", F32, f"{tag}_dhn", b_cols4=True)
    dh, dg = rmsnorm_bwd(h, g_row, dhn, dout, f"{tag}_dnorm")
    return dh, dg, dw_in, dvg, dws, dbcol, dw_o


@jax.custom_vjp
def _softplus(x):
    return jnp.maximum(x, 0.0) + jnp.log(1.0 + jnp.exp(-jnp.abs(x)))


_softplus.defvjp(lambda x: (_softplus(x), x), lambda x, g: (g * lax.logistic(x),))


def _silu(x):
    return x * lax.logistic(x)


def _shift_impl(x, s, down):
    n = x.shape[0]
    r = lax.broadcasted_iota(jnp.int32, x.shape, 0)
    if down:
        return jnp.where(r >= s, pltpu.roll(x, s, 0), 0.0)
    return jnp.where(r < n - s, pltpu.roll(x, n - s, 0), 0.0)


@functools.partial(jax.custom_vjp, nondiff_argnums=(1,))
def _shift_down(x, s):
    return _shift_impl(x, s, True)


_shift_down.defvjp(lambda x, s: (_shift_impl(x, s, True), None), lambda s, _, g: (_shift_impl(g, s, False),))


def _conv_f(x, w, b):
    k_id = lax.broadcasted_iota(jnp.int32, w.shape, 0)
    y = b + jnp.sum(jnp.where(k_id == SSD_CONV - 1, w, 0.0), axis=0, keepdims=True) * x
    for k in range(SSD_CONV - 1):
        wk = jnp.sum(jnp.where(k_id == k, w, 0.0), axis=0, keepdims=True)
        y = y + wk * _shift_down(x, SSD_CONV - 1 - k)
    return (_silu(y),)


def _conv_ops(zx, conv_w, conv_b, wi, grad):
    T = zx.shape[0]
    cd = conv_w.shape[1]
    cw = LANES
    off = wi // cw
    return [Op(zx, (T, cw), lambda o, j: (0, off + j), "tile", grad, gshape=(T, cd), gimap=lambda o, j: (0, j)),
            Op(conv_w, (SSD_CONV, cw), lambda o, j: (0, j), "tile", grad),
            Op(conv_b, (1, cw), lambda o, j: (0, j), "tile", grad)], (1, cd // cw), (T, cw)


def _dt_f(dtr, bias):
    return (_softplus(dtr + bias),)


def _cumdot_left(tri, x):
    return sum(lax.dot_general(tri, p, NN, preferred_element_type=F32) for p in _split3(x))


@jax.custom_vjp
def _cumsum_rows(x):
    row, col = _iotas()
    return _cumdot_left((row >= col).astype(_MXU), x)


def _cumsum_rows_bwd(_, g):
    row, col = _iotas()
    return (_cumdot_left((row <= col).astype(_MXU), g),)


_cumsum_rows.defvjp(lambda x: (_cumsum_rows(x), None), _cumsum_rows_bwd)


def _ssd_chunk(xps, dt, bm, cm, sps, alog, hid_base):
    row, col = _iotas()
    half = SSD_HEAD_DIM
    rcol = lax.broadcasted_iota(jnp.int32, (CHUNK, 1), 0)
    colpick = lambda m, hid: jnp.sum(jnp.where(col == hid, m, 0.0), axis=1, keepdims=True)
    rowpick = lambda m, hid: jnp.sum(jnp.where(row == hid, m, 0.0), axis=0, keepdims=True)
    last = lambda v: jnp.sum(jnp.where(rcol == CHUNK - 1, v, 0.0), axis=0, keepdims=True)
    acum = _cumsum_rows(dt * (-jnp.exp(alog)))
    acum_t = acum.T
    cb = _dotv(cm, bm, "nt")
    tri = row >= col
    lo = col < half
    ys, snews = [], []
    for p, (xp, sp) in enumerate(zip(xps, sps)):
        h0, h1 = hid_base + 2 * p, hid_base + 2 * p + 1
        ac0, ac1 = colpick(acum, h0), colpick(acum, h1)
        m0 = cb * jnp.exp(jnp.where(tri, ac0 - rowpick(acum_t, h0), -1e30))
        m1 = cb * jnp.exp(jnp.where(tri, ac1 - rowpick(acum_t, h1), -1e30))
        xs = xp * jnp.where(lo, colpick(dt, h0), colpick(dt, h1))
        ydiag = jnp.where(lo, _dotv(m0, xs, "nn"), _dotv(m1, xs, "nn"))
        yoff = jnp.where(lo, jnp.exp(ac0), jnp.exp(ac1)) * _dotv(cm, sp, "nt")
        al0, al1 = last(ac0), last(ac1)
        xsd = xs * jnp.where(lo, jnp.exp(al0 - ac0), jnp.exp(al1 - ac1))
        snew = jnp.where(row < half, jnp.exp(al0), jnp.exp(al1)) * sp + _dotv(xsd, bm, "tn")
        ys.append(ydiag + yoff)
        snews.append(snew)
    return ys, snews


def _ssd_dims(xact, wi):
    T, cd = xact.shape
    G = (cd - wi) // (2 * SSD_STATE)
    hpg = wi // SSD_HEAD_DIM // G
    assert hpg % 2 == 0 and SSD_STATE == LANES
    return T, G, hpg, hpg // 2, T // CHUNK, wi // G


def ssd_scan_fwd(xact, dt, alog, wi, name):
    T, G, hpg, NP, NC, gw = _ssd_dims(xact, wi)
    bo, co = wi // LANES, wi // LANES + G

    def body(x_ref, b_ref, c_ref, dt_ref, al_ref, y_ref, st_ref, s_ref):
        g, c = pl.program_id(0), pl.program_id(1)

        @pl.when(c == 0)
        def _():
            s_ref[...] = jnp.zeros_like(s_ref)

        st_ref[...] = s_ref[...]
        xps = [x_ref[:, p * LANES:(p + 1) * LANES] for p in range(NP)]
        sps = [s_ref[p] for p in range(NP)]
        ys, snews = _ssd_chunk(xps, dt_ref[...], b_ref[...], c_ref[...], sps, al_ref[...], g * hpg)
        for p in range(NP):
            y_ref[:, p * LANES:(p + 1) * LANES] = ys[p]
            s_ref[p] = snews[p]

    return pl.pallas_call(
        body,
        out_shape=(jax.ShapeDtypeStruct((T, wi), F32), jax.ShapeDtypeStruct((G, NC, NP, LANES, SSD_STATE), F32)),
        grid=(G, NC),
        in_specs=[pl.BlockSpec((CHUNK, gw), lambda g, c: (c, g)),
                  pl.BlockSpec((CHUNK, LANES), lambda g, c: (c, bo + g)),
                  pl.BlockSpec((CHUNK, LANES), lambda g, c: (c, co + g)),
                  pl.BlockSpec((CHUNK, LANES), lambda g, c: (c, 0)),
                  pl.BlockSpec((1, LANES), lambda g, c: (0, 0))],
        out_specs=(pl.BlockSpec((CHUNK, gw), lambda g, c: (c, g)),
                   pl.BlockSpec((None, None, NP, LANES, SSD_STATE), lambda g, c: (g, c, 0, 0, 0))),
        scratch_shapes=[pltpu.VMEM((NP, LANES, SSD_STATE), F32)],
        compiler_params=_params(("parallel", "arbitrary")),
        name=name,
    )(xact, xact, xact, dt, alog)


def ssd_scan_bwd(xact, dt, alog, states, dy, wi, name):
    T, G, hpg, NP, NC, gw = _ssd_dims(xact, wi)
    bo, co = wi // LANES, wi // LANES + G
    rev = lambda c: NC - 1 - c

    def body(x_ref, b_ref, c_ref, dt_ref, al_ref, st_ref, dy_ref, dx_ref, db_ref, dc_ref, ddt_ref, dal_ref, ds_ref):
        g, c = pl.program_id(0), pl.program_id(1)

        @pl.when(c == 0)
        def _():
            ds_ref[...] = jnp.zeros_like(ds_ref)

        xps = [x_ref[:, p * LANES:(p + 1) * LANES] for p in range(NP)]
        sps = [st_ref[p] for p in range(NP)]
        f = lambda xps_, dt_, bm_, cm_, sps_, al_: _ssd_chunk(xps_, dt_, bm_, cm_, sps_, al_, g * hpg)
        _, vjp = jax.vjp(f, xps, dt_ref[...], b_ref[...], c_ref[...], sps, al_ref[...])
        dys = [dy_ref[:, p * LANES:(p + 1) * LANES] for p in range(NP)]
        dxps, ddt, dbm, dcm, dsps, dal = vjp((dys, [ds_ref[p] for p in range(NP)]))
        for p in range(NP):
            dx_ref[:, p * LANES:(p + 1) * LANES] = dxps[p]
            ds_ref[p] = dsps[p]
        db_ref[...] = dbm
        dc_ref[...] = dcm
        ddt_ref[...] = ddt

        @pl.when(c == 0)
        def _():
            dal_ref[...] = dal

        @pl.when(c != 0)
        def _():
            dal_ref[...] += dal

    gb = G * SSD_STATE
    return pl.pallas_call(
        body,
        out_shape=(jax.ShapeDtypeStruct((T, wi), F32), jax.ShapeDtypeStruct((T, gb), F32), jax.ShapeDtypeStruct((T, gb), F32),
                   jax.ShapeDtypeStruct((G, T, LANES), F32), jax.ShapeDtypeStruct((G, 1, LANES), F32)),
        grid=(G, NC),
        in_specs=[pl.BlockSpec((CHUNK, gw), lambda g, c: (rev(c), g)),
                  pl.BlockSpec((CHUNK, LANES), lambda g, c: (rev(c), bo + g)),
                  pl.BlockSpec((CHUNK, LANES), lambda g, c: (rev(c), co + g)),
                  pl.BlockSpec((CHUNK, LANES), lambda g, c: (rev(c), 0)),
                  pl.BlockSpec((1, LANES), lambda g, c: (0, 0)),
                  pl.BlockSpec((None, None, NP, LANES, SSD_STATE), lambda g, c: (g, rev(c), 0, 0, 0)),
                  pl.BlockSpec((CHUNK, gw), lambda g, c: (rev(c), g))],
        out_specs=(pl.BlockSpec((CHUNK, gw), lambda g, c: (rev(c), g)),
                   pl.BlockSpec((CHUNK, LANES), lambda g, c: (rev(c), g)),
                   pl.BlockSpec((CHUNK, LANES), lambda g, c: (rev(c), g)),
                   pl.BlockSpec((None, CHUNK, LANES), lambda g, c: (g, rev(c), 0)),
                   pl.BlockSpec((None, 1, LANES), lambda g, c: (g, 0, 0))),
        scratch_shapes=[pltpu.VMEM((NP, LANES, SSD_STATE), F32)],
        compiler_params=_params(("parallel", "arbitrary")),
        name=name,
    )(xact, xact, xact, dt, alog, states, dy)


def _post_f(y, x, z, dcol, ng):
    return (_rms((y + dcol * x) * _silu(z), ng),)


def _post_ops(yssd, xact, zx, dcol, ng, G, tm, grad):
    T, wi = yssd.shape
    gw = wi // G
    blk = lambda a: Op(a, (tm, gw), lambda g, i: (i, g), "tile", grad, gshape=(T, wi))
    vec = lambda v: Op(v, (1, gw), lambda g, i: (0, g), "param", grad)
    return [blk(yssd), blk(xact), blk(zx), vec(dcol), vec(ng)], (G, T // tm), gw


def ssd_fwd(h, g_row, w_zx, w_dt, conv_w, conv_b, dtb, alog, dcol, ng, w_o, tag, cq=None):
    T = h.shape[0]
    wi = ng.shape[1]
    tm = _pick(T, 256, 128)
    hn = rmsnorm_fwd(h, g_row, f"{tag}_norm")
    zx = matmul(hn, w_zx, "nn", F32, f"{tag}_inzx", cq=cq)
    dtr = matmul(hn, w_dt, "nn", F32, f"{tag}_indt")
    ins, grid, blk = _conv_ops(zx, conv_w, conv_b, wi, False)
    cd = conv_w.shape[1]
    xact = tmap(_conv_f, grid, ins, [((T, cd), F32, blk, lambda o, j: (0, j))], f"{tag}_conv")[0]
    dt = tmap(_dt_f, (1, T // tm), _row_ops([dtr], tm) + [_vec_op(dtb)],
              [((T, LANES), F32, (tm, LANES), lambda o, i: (i, 0))], f"{tag}_dt")[0]
    yssd, states = ssd_scan_fwd(xact, dt, alog, wi, f"{tag}_scan")
    G = states.shape[0]
    ins, grid, gw = _post_ops(yssd, xact, zx, dcol, ng, G, tm, False)
    yn = tmap(_post_f, grid, ins, [((T, wi), _ACT, (tm, gw), lambda g, i: (i, g))], f"{tag}_post")[0]
    out = matmul(yn, w_o, "nn", F32, f"{tag}_wo", cq=cq, epilogue=lambda acc, hh: (acc + hh,), extras=(h,))
    return out, (h, hn, zx, dtr, xact, dt, yssd, states, yn)


def ssd_bwd(dout, saved, g_row, w_zx, w_dt, conv_w, conv_b, dtb, alog, dcol, ng, w_o, tag, cq=None):
    h, hn, zx, dtr, xact, dt, yssd, states, yn = saved
    T = h.shape[0]
    wi = ng.shape[1]
    tm = _pick(T, 256, 128)
    G = states.shape[0]
    dyn = matmul(dout, w_o, "nt", F32, f"{tag}_dyn")
    dw_o = matmul(yn, dout, "tn", GRAD_DT, f"{tag}_dwo")
    ins, grid, gw = _post_ops(yssd, xact, zx, dcol, ng, G, tm, True)
    dyssd, dxi_skip, dz, ddcol, dng = tmap_vjp(_post_f, grid, ins, [Op(dyn, (tm, gw), lambda g, i: (i, g))],
                                                f"{tag}_dpost", grad_dtypes={2: _ACT})
    dxi, dbm, dcm, ddt_g, dalog_g = ssd_scan_bwd(xact, dt, alog, states, dyssd, wi, f"{tag}_dscan")
    dxact = jnp.concatenate([dxi + dxi_skip, dbm, dcm], axis=1)
    ddt = jnp.sum(ddt_g, axis=0)
    dalog = jnp.sum(dalog_g, axis=0)
    ins, grid, blk = _conv_ops(zx, conv_w, conv_b, wi, True)
    dxbc, dconv_w, dconv_b = tmap_vjp(_conv_f, grid, ins, [Op(dxact, blk, lambda o, j: (0, j))], f"{tag}_dconv",
                                      grad_dtypes={0: _ACT})
    ddtr, ddtb = tmap_vjp(_dt_f, (1, T // tm), _row_ops([dtr], tm) + [_vec_op(dtb)], _row_ops([ddt], tm), f"{tag}_ddt",
                          grad_dtypes={0: _ACT})
    dzx = jnp.concatenate([dz, dxbc], axis=1)
    dw_zx = matmul(hn, dzx, "tn", GRAD_DT, f"{tag}_dwzx")
    dw_dt = matmul(hn, ddtr, "tn", GRAD_DT, f"{tag}_dwdt")
    dhn1 = matmul(ddtr, w_dt, "nt", F32, f"{tag}_dhn1")
    dhn = matmul(dzx, w_zx, "nt", F32, f"{tag}_dhn", epilogue=lambda acc, e: (acc + e,), extras=(dhn1,))
    dh, dg = rmsnorm_bwd(h, g_row, dhn, dout, f"{tag}_dnorm")
    return dh, dg, dw_zx, dw_dt, dconv_w, dconv_b, ddtb, dalog, ddcol, dng, dw_o


def loss_head(y, target, name):
    T, D = y.shape
    tm = _pick(T, 512, 256, 128)

    def body(y_ref, t_ref, dy_ref, part_ref):
        d = y_ref[...] - t_ref[...]
        dy_ref[...] = d * (1.0 / D)
        s = jnp.sum(d * d, axis=0, keepdims=True) * (0.5 / D)

        @pl.when(pl.program_id(0) == 0)
        def _():
            part_ref[...] = s

        @pl.when(pl.program_id(0) != 0)
        def _():
            part_ref[...] += s

    rows = pl.BlockSpec((tm, D), lambda i: (i, 0))
    return pl.pallas_call(
        body,
        out_shape=(jax.ShapeDtypeStruct((T, D), F32), jax.ShapeDtypeStruct((1, D), F32)),
        grid=(T // tm,),
        in_specs=[rows, rows],
        out_specs=(rows, pl.BlockSpec((1, D), lambda i: (0, 0))),
        compiler_params=_params(("arbitrary",)),
        name=name,
    )(y, target)


def _row_tile(R, C, itemsize=4, target=1 << 20):
    for t in (1024, 512, 256, 128, 64, 32, 16, 8):
        if R % t == 0 and t * C * itemsize <= target:
            return t
    return R


def adamw(w, g, m, v, name):
    R, C = w.shape
    tr = _row_tile(R, C)
    c1 = 1.0 - ADAM_B1 ** ADAM_STEP
    c2 = 1.0 - ADAM_B2 ** ADAM_STEP

    def body(w_ref, g_ref, m_ref, v_ref, d_ref, nm_ref, nv_ref):
        gg = g_ref[...]
        nm = ADAM_B1 * m_ref[...] + (1.0 - ADAM_B1) * gg
        nv = ADAM_B2 * v_ref[...] + (1.0 - ADAM_B2) * jnp.square(gg)
        d_ref[...] = -ADAM_LR * ((nm / c1) / (jnp.sqrt(nv / c2) + ADAM_EPS) + ADAM_WD * w_ref[...])
        nm_ref[...] = nm
        nv_ref[...] = nv

    spec = pl.BlockSpec((tr, C), lambda i: (i, 0))
    sds = jax.ShapeDtypeStruct((R, C), F32)
    return pl.pallas_call(body, out_shape=(sds, sds, sds), grid=(R // tr,), in_specs=[spec] * 4, out_specs=(spec,) * 3,
                          compiler_params=_params(("parallel",)), name=name)(w, g, m, v)


def pair_sum(gfull, recv, name):
    _, _, R, C = gfull.shape
    tr = _row_tile(R, C, 2)

    def body(g_ref, p_ref, o_ref):
        c = lax.axis_index("c")
        o_ref[...] = (g_ref[c].astype(F32) + p_ref[...].astype(F32)).astype(o_ref.dtype)

    return pl.pallas_call(
        body,
        out_shape=jax.ShapeDtypeStruct((4, R, C), gfull.dtype),
        grid=(4, R // tr),
        in_specs=[pl.BlockSpec((None, 2, tr, C), lambda p, i: (p, 0, i, 0)), pl.BlockSpec((None, tr, C), lambda p, i: (p, i, 0))],
        out_specs=pl.BlockSpec((None, tr, C), lambda p, i: (p, i, 0)),
        compiler_params=_params(("parallel", "parallel")),
        name=name,
    )(gfull, recv)


def chip_sum(s, recv, name):
    _, R, C = s.shape
    tr = _row_tile(R, C, 2, 1 << 19)

    def body(c_ref, s_ref, p_ref, o_ref):
        x, y, _ = _xyc()
        acc = s_ref[2 * x + y].astype(F32)
        for r in range(3):
            acc = acc + p_ref[r].astype(F32)
        o_ref[...] = acc

    grid_spec = pltpu.PrefetchScalarGridSpec(
        num_scalar_prefetch=1,
        grid=(R // tr,),
        in_specs=[pl.BlockSpec((4, tr, C), lambda i, c: (0, i, 0)), pl.BlockSpec((3, tr, C), lambda i, c: (0, i, 0))],
        out_specs=pl.BlockSpec((None, tr, C), lambda i, c: (c[0], i, 0)),
    )
    return pl.pallas_call(
        body,
        out_shape=jax.ShapeDtypeStruct((2, R, C), F32),
        grid_spec=grid_spec,
        compiler_params=_params(("arbitrary",)),
        name=name,
    )(lax.axis_index("c").reshape(1).astype(jnp.int32), s, recv)


def sum8(g, name):
    _, R, C = g.shape

    def body(g_ref, o_ref):
        acc = g_ref[0]
        for d in range(1, 8):
            acc = acc + g_ref[d]
        o_ref[...] = acc

    return pl.pallas_call(body, out_shape=jax.ShapeDtypeStruct((R, C), F32), name=name,
                          compiler_params=pltpu.CompilerParams(vmem_limit_bytes=_VMEM_LIMIT))(g)


def gather_small(x, name):
    R, C = x.shape

    def body(x_ref, out_ref, send_sems, recv_sems, local_sem):
        x_, y_, c_ = _xyc()
        me, sibling = (x_, y_, c_), (x_, y_, 1 - c_)
        chips = [(_flip(x_, fx), _flip(y_, fy)) for fx, fy in _REL]
        slot = lambda px, py, pc: out_ref.at[4 * px + 2 * py + pc]

        def copy(k, block, to, src=None):
            return _remote(slot(*block) if src is None else src, slot(*block), send_sems.at[k], recv_sems.at[k], to)

        mine = pltpu.make_async_copy(x_ref, slot(*me), local_sem)
        mine.start()
        first = [copy(0, me, sibling, src=x_ref)]
        first += [copy(1 + j, me, (*chip, c_), src=x_ref) for j, chip in enumerate(chips)]
        for cp in first:
            cp.start()
        passed = [copy(4 + j, (*chip, c_), sibling) for j, chip in enumerate(chips)]
        for j, chip in enumerate(chips):
            copy(1 + j, (*chip, c_), me).wait_recv()
            passed[j].start()
        copy(0, sibling, me).wait_recv()
        for j, chip in enumerate(chips):
            copy(4 + j, (*chip, 1 - c_), me).wait_recv()
        for cp in first + passed:
            cp.wait_send()
        mine.wait()

    return pl.pallas_call(
        body,
        out_shape=jax.ShapeDtypeStruct((8, R, C), x.dtype),
        in_specs=[pl.BlockSpec(memory_space=pltpu.VMEM)],
        out_specs=pl.BlockSpec(memory_space=pltpu.VMEM),
        scratch_shapes=[pltpu.SemaphoreType.DMA((7,)), pltpu.SemaphoreType.DMA((7,)), pltpu.SemaphoreType.DMA],
        compiler_params=pltpu.CompilerParams(vmem_limit_bytes=_VMEM_LIMIT),
        name=name,
    )(x)


def gather_weights(halves, name):
    n = len(halves)

    def body(*refs):
        ins, outs = refs[:n], refs[n:2 * n]
        send_sems, recv_sems, local_sems = refs[2 * n:]
        x_, y_, c_ = _xyc()
        sibling = (x_, y_, 1 - c_)
        chips = [(_flip(x_, fx), _flip(y_, fy)) for fx, fy in _REL]
        mine = 2 * x_ + y_
        started, local = [], []
        for i in range(n):
            own, dst = ins[i].at[c_], outs[i].at[mine, c_]
            lc = pltpu.make_async_copy(own, dst, local_sems.at[i])
            lc.start()
            local.append(lc)
            for r, chip in enumerate(chips):
                started.append(_remote(own, dst, send_sems.at[7 * i + r], recv_sems.at[7 * i + r], (*chip, c_)))
                started[-1].start()
            started.append(_remote(own, dst, send_sems.at[7 * i + 3], recv_sems.at[7 * i + 3], sibling))
            started[-1].start()
        for i in range(n):
            for r, (px, py) in enumerate(chips):
                blk = outs[i].at[2 * px + py, c_]
                _remote(blk, blk, send_sems.at[7 * i + r], recv_sems.at[7 * i + r], sibling).wait_recv()
                started.append(_remote(blk, blk, send_sems.at[7 * i + 4 + r], recv_sems.at[7 * i + 4 + r], sibling))
                started[-1].start()
        for i in range(n):
            blk = outs[i].at[mine, 1 - c_]
            _remote(blk, blk, send_sems.at[7 * i + 3], recv_sems.at[7 * i + 3], sibling).wait_recv()
            for r, (px, py) in enumerate(chips):
                blk = outs[i].at[2 * px + py, 1 - c_]
                _remote(blk, blk, send_sems.at[7 * i + 4 + r], recv_sems.at[7 * i + 4 + r], sibling).wait_recv()
        for cp in started:
            cp.wait_send()
        for lc in local:
            lc.wait()

    return pl.pallas_call(
        body,
        out_shape=tuple(jax.ShapeDtypeStruct((4,) + h.shape, h.dtype) for h in halves),
        in_specs=[_ANY] * n,
        out_specs=tuple(_ANY for _ in halves),
        scratch_shapes=[pltpu.SemaphoreType.DMA((7 * n,)), pltpu.SemaphoreType.DMA((7 * n,)), pltpu.SemaphoreType.DMA((n,))],
        name=name,
    )(*halves)


def share_weights(halves, gathered, name):
    n = len(halves)

    def body(*refs):
        ins = refs[:n]
        outs = refs[2 * n:3 * n]
        send_sems, recv_sems, local_sems = refs[3 * n:]
        x_, y_, c_ = _xyc()
        sibling = (x_, y_, 1 - c_)
        chips = [(x_, y_)] + [(_flip(x_, fx), _flip(y_, fy)) for fx, fy in _REL]
        sent, local = [], []
        for i in range(n):
            own = ins[i].at[c_]
            local.append(pltpu.make_async_copy(own, outs[i].at[2 * x_ + y_, c_], local_sems.at[i]))
            local[-1].start()
            for q, (px, py) in enumerate(chips):
                blk = outs[i].at[2 * px + py, c_]
                sent.append(_remote(own if q == 0 else blk, blk, send_sems.at[4 * i + q], recv_sems.at[4 * i + q], sibling))
                sent[-1].start()
        for i in range(n):
            for q, (px, py) in enumerate(chips):
                blk = outs[i].at[2 * px + py, 1 - c_]
                _remote(blk, blk, send_sems.at[4 * i + q], recv_sems.at[4 * i + q], sibling).wait_recv()
        for cp in sent:
            cp.wait_send()
        for lc in local:
            lc.wait()

    return pl.pallas_call(
        body,
        out_shape=tuple(jax.ShapeDtypeStruct(g.shape, g.dtype) for g in gathered),
        in_specs=[_ANY] * (2 * n),
        out_specs=tuple(_ANY for _ in gathered),
        input_output_aliases={n + i: i for i in range(n)},
        scratch_shapes=[pltpu.SemaphoreType.DMA((4 * n,)), pltpu.SemaphoreType.DMA((4 * n,)), pltpu.SemaphoreType.DMA((n,))],
        name=name,
    )(*halves, *gathered)


def swap_halves(gfulls, name):
    n = len(gfulls)

    def body(*refs):
        ins, outs = refs[:n], refs[n:2 * n]
        send_sems, recv_sems = refs[2 * n:]
        x_, y_, c_ = _xyc()
        sibling = (x_, y_, 1 - c_)
        started = []
        for i in range(n):
            for p in range(4):
                started.append(_remote(ins[i].at[p, 1 - c_], outs[i].at[p], send_sems.at[4 * i + p], recv_sems.at[4 * i + p], sibling))
                started[-1].start()
        for cp in started:
            cp.wait()

    return pl.pallas_call(
        body,
        out_shape=tuple(jax.ShapeDtypeStruct((4,) + g.shape[2:], g.dtype) for g in gfulls),
        in_specs=[_ANY] * n,
        out_specs=tuple(_ANY for _ in gfulls),
        scratch_shapes=[pltpu.SemaphoreType.DMA((4 * n,)), pltpu.SemaphoreType.DMA((4 * n,))],
        name=name,
    )(*gfulls)


def scatter_chips(sums, name):
    n = len(sums)

    def body(*refs):
        ins, outs = refs[:n], refs[n:2 * n]
        send_sems, recv_sems = refs[2 * n:]
        x_, y_, c_ = _xyc()
        chips = [(_flip(x_, fx), _flip(y_, fy)) for fx, fy in _REL]
        started = []
        for i in range(n):
            for r, (px, py) in enumerate(chips):
                started.append(_remote(ins[i].at[2 * px + py], outs[i].at[r], send_sems.at[3 * i + r], recv_sems.at[3 * i + r], (px, py, c_)))
                started[-1].start()
        for cp in started:
            cp.wait()

    return pl.pallas_call(
        body,
        out_shape=tuple(jax.ShapeDtypeStruct((3,) + s.shape[1:], s.dtype) for s in sums),
        in_specs=[_ANY] * n,
        out_specs=tuple(_ANY for _ in sums),
        scratch_shapes=[pltpu.SemaphoreType.DMA((3 * n,)), pltpu.SemaphoreType.DMA((3 * n,))],
        name=name,
    )(*sums)


_JOIN_CHUNK_BYTES = 4 << 20


def join_halves(bufs, name):
    n = len(bufs)
    chunks = []
    for b in bufs:
        _, R, C = b.shape
        k = 1
        while k < 8 and R % (2 * k * 8) == 0 and R * C * 4 // k > _JOIN_CHUNK_BYTES:
            k *= 2
        chunks.append(k)
    base = [sum(chunks[:i]) for i in range(n)]
    total = sum(chunks)

    def body(*refs):
        outs = refs[n:2 * n]
        send_sems, recv_sems = refs[2 * n:]
        x_, y_, c_ = _xyc()
        sibling = (x_, y_, 1 - c_)
        sent = []
        for i in range(n):
            rc = bufs[i].shape[1] // chunks[i]
            for q in range(chunks[i]):
                blk = outs[i].at[c_, pl.ds(q * rc, rc)]
                sent.append(_remote(blk, blk, send_sems.at[base[i] + q], recv_sems.at[base[i] + q], sibling))
                sent[-1].start()
        for i in range(n):
            rc = bufs[i].shape[1] // chunks[i]
            for q in range(chunks[i]):
                blk = outs[i].at[1 - c_, pl.ds(q * rc, rc)]
                _remote(blk, blk, send_sems.at[base[i] + q], recv_sems.at[base[i] + q], sibling).wait_recv()
        for cp in sent:
            cp.wait_send()

    return pl.pallas_call(
        body,
        out_shape=tuple(jax.ShapeDtypeStruct(b.shape, b.dtype) for b in bufs),
        in_specs=[_ANY] * n,
        out_specs=tuple(_ANY for _ in bufs),
        input_output_aliases={i: i for i in range(n)},
        scratch_shapes=[pltpu.SemaphoreType.DMA((total,)), pltpu.SemaphoreType.DMA((total,))],
        name=name,
    )(*bufs)


def _halves(a):
    return a.reshape(2, -1, a.shape[-1])


_BIG = ("sb_w_qkv", "sb_w_o", "gm_w_in", "gm_w_o", "ssd_w_in", "ssd_w_o", "mlp_w_in", "mlp_w_out")
_COLS = ("sb_w_qkv", "gm_w_in", "ssd_w_in", "mlp_w_in")
_SMALL = ("norm_mix_g", "norm_mlp_g", "sb_q_norm_g", "sb_k_norm_g", "gm_v_norm_g", "gm_w_s", "gm_b_s",
          "ssd_conv_w", "ssd_conv_b", "ssd_dt_bias", "ssd_a_log", "ssd_d", "ssd_norm_g")
_SMALL_SHARDED = ("ssd_conv_w", "ssd_conv_b", "ssd_norm_g")
_WEIGHTS = ("norm_mix_g", "norm_mlp_g", "sb_w_qkv", "sb_q_norm_g", "sb_k_norm_g", "sb_w_o", "gm_w_in", "gm_v_norm_g",
            "gm_w_s", "gm_b_s", "gm_w_o", "ssd_w_in", "ssd_conv_w", "ssd_conv_b", "ssd_dt_bias", "ssd_a_log", "ssd_d",
            "ssd_norm_g", "ssd_w_o", "mlp_w_in", "mlp_w_out")


def _pack(arrs):
    flat = jnp.concatenate([a.reshape(-1).astype(F32) for a in arrs])
    n = flat.shape[0]
    tot = -(-n // (8 * LANES)) * 8 * LANES
    return jnp.pad(flat, (0, tot - n)).reshape(-1, LANES)


def _unpack(buf, shapes):
    flat = buf.reshape(-1)
    out, o = [], 0
    for s in shapes:
        n = math.prod(s)
        out.append(flat[o:o + n].reshape(s))
        o += n
    return out


_CARRIER_RANK = {0: (3, 0, 2, 1), 1: (3, 2, 0, 1), 2: (3, 0, 2, 1)}


def _layer_arrays(i):
    kind, j = i % 3, i // 3
    mix = (("sb_w_qkv", "sb_w_o"), ("gm_w_in", "gm_w_o"), ("ssd_w_in", "ssd_w_o"))[kind]
    return [(mix[0], j), (mix[1], j), ("mlp_w_in", i), ("mlp_w_out", i)]


def _step(x, w, target, m, v):
    depth = w["norm_mix_g"].shape[0]
    xc, yc, _ = _xyc()
    chip = 2 * xc + yc
    Hs = w["ssd_dt_bias"].shape[1]
    wi = 4 * w["ssd_norm_g"].shape[1]
    cd = 4 * w["ssd_conv_b"].shape[1]

    halves = {(k, l): _halves(w[k][l].astype(_MXU)) for i in range(depth) for (k, l) in _layer_arrays(i)}
    size = lambda key: math.prod(halves[key].shape)

    def finish(key, g):
        k, l = key
        g = g.reshape((4,) + w[k].shape[1:])
        if k == "ssd_w_in":
            full = g.transpose(1, 0, 2).reshape(g.shape[1], -1)
            return full[:, :wi + cd], jnp.pad(full[:, wi + cd:], ((0, 0), (0, LANES - Hs)))
        return g if k in _COLS else g.reshape(-1, g.shape[-1])

    small_sh = gather_small(_pack([w[k] for k in _SMALL_SHARDED]), "gather_small_weights")
    parts = [_unpack(small_sh[2 * j], [w[k].shape for k in _SMALL_SHARDED]) for j in range(4)]
    conv_w = jnp.concatenate([p[0][0] for p in parts], axis=1)
    conv_b = jnp.concatenate([p[1] for p in parts], axis=1)
    ssd_ng = jnp.concatenate([p[2] for p in parts], axis=1)
    padh = lambda a: jnp.pad(a, ((0, 0), (0, LANES - Hs)))
    dtb, alog = padh(w["ssd_dt_bias"]), padh(w["ssd_a_log"])
    dcol = jnp.repeat(w["ssd_d"], SSD_HEAD_DIM, axis=1)
    bcol = w["gm_b_s"][0][:, :, None]

    keys0 = _layer_arrays(0)
    ready = dict(zip(keys0, gather_weights([halves[k] for k in keys0], "gather_weights_0")))

    h = x[0]
    tape = []
    for i in range(depth):
        kind, j = i % 3, i // 3
        keys = _layer_arrays(i)
        wl = [finish(k, ready[k]) for k in keys]
        cq = Carriers()
        if i + 1 < depth:
            nxt = sorted(_layer_arrays(i + 1), key=size, reverse=True)
            order = [None] * 4
            for rank, key in zip(_CARRIER_RANK[kind], nxt):
                order[rank] = key
            for key in order:
                cq.add("gather", halves[key], key)
        gmix = w["norm_mix_g"][i:i + 1]
        if kind == 0:
            args = (gmix, wl[0], w["sb_q_norm_g"][j:j + 1], w["sb_k_norm_g"][j:j + 1], wl[1], f"sb{j}")
            h, sv = sb_fwd(h, *args, cq=cq)
        elif kind == 1:
            args = (gmix, wl[0], w["gm_v_norm_g"][j:j + 1], w["gm_w_s"][j], bcol, wl[1], f"gm{j}")
            h, sv = gm_fwd(h, *args, cq=cq)
        else:
            args = (gmix, wl[0][0], wl[0][1], conv_w, conv_b, dtb, alog, dcol, ssd_ng, wl[1], f"ssd{j}")
            h, sv = ssd_fwd(h, *args, cq=cq)
        margs = (w["norm_mlp_g"][i:i + 1], wl[2], wl[3], f"mlp{i}")
        h, msv = mlp_fwd(h, *margs, cq=cq)
        tape.append((kind, j, args, sv, margs, msv))
        if i + 1 < depth:
            assert not cq.pending
            nk_ = _layer_arrays(i + 1)
            shared = share_weights([halves[k] for k in nk_], [cq.done[k] for k in nk_], f"share_weights_{i + 1}")
            ready.update(zip(nk_, shared))

    dh, loss_cols = loss_head(h, target[0], "loss_head")

    gsmall = {k: [None] * w[k].shape[0] for k in ("norm_mix_g", "norm_mlp_g", "sb_q_norm_g", "sb_k_norm_g")}
    pairs, from_chips = {}, {}
    cq = Carriers()
    for i in reversed(range(depth)):
        kind, j, args, sv, margs, msv = tape[i]
        keys = _layer_arrays(i)
        dh, gsmall["norm_mlp_g"][i], d_mlp_in, d_mlp_out = mlp_bwd(dh, msv, *margs, cq=cq)
        assert not cq.pending
        from_chips.update(cq.done)
        if kind == 0:
            dh, gsmall["norm_mix_g"][i], d_in, gsmall["sb_q_norm_g"][j], gsmall["sb_k_norm_g"][j], d_out = sb_bwd(dh, sv, *args)
        elif kind == 1:
            dh, gsmall["norm_mix_g"][i], d_in, d_vg, d_ws, d_bcol, d_out = gm_bwd(dh, sv, *args)
            gsmall["gm_v_norm_g"], gsmall["gm_w_s"], gsmall["gm_b_s"] = d_vg, d_ws[None], d_bcol[None, :, :, 0]
        else:
            dh, gsmall["norm_mix_g"][i], d_zx, d_dt, d_cw, d_cb, d_dtb, d_al, d_dcol, d_ng, d_out = ssd_bwd(dh, sv, *args)
            d_full = jnp.concatenate([d_zx, d_dt[:, :Hs]], axis=1)
            d_in = d_full.reshape(d_full.shape[0], 4, -1).transpose(1, 0, 2)
            gsmall["ssd_conv_w"], gsmall["ssd_conv_b"], gsmall["ssd_norm_g"] = d_cw[None], d_cb, d_ng
            gsmall["ssd_dt_bias"], gsmall["ssd_a_log"] = d_dtb[:, :Hs], d_al[:, :Hs]
            gsmall["ssd_d"] = jnp.sum(d_dcol.reshape(Hs, SSD_HEAD_DIM), axis=1)[None]
        gl = [g.reshape(4, 2, -1, g.shape[-1]).astype(GRAD_DT) for g in (d_in, d_out, d_mlp_in, d_mlp_out)]
        got = swap_halves(gl, f"grad_swap_halves_{i}")
        for key, g, r in zip(keys, gl, got):
            pairs[key] = pair_sum(g, r, f"pair_sum_{key[0]}_{key[1]}")
        cq = Carriers()
        if i > 0:
            for key in sorted(keys, key=size, reverse=True):
                cq.add("scatter", pairs[key], key)
        else:
            from_chips.update(zip(keys, scatter_chips([pairs[k] for k in keys], "grad_scatter_chips_0")))
    for k in gsmall:
        if isinstance(gsmall[k], list):
            gsmall[k] = jnp.concatenate(gsmall[k], axis=0)
    grad_x = dh[None]

    allkeys = [key for i in range(depth) for key in _layer_arrays(i)]
    mine = [chip_sum(pairs[key], from_chips[key], f"chip_sum_{key[0]}_{key[1]}") for key in allkeys]
    joined = dict(zip(allkeys, join_halves(mine, "grad_join_halves")))
    grads, deltas, new_m, new_v = {}, {}, {}, {}
    for k in _BIG:
        g = jnp.stack([joined[(k, l)].reshape(w[k].shape[1:]) for l in range(w[k].shape[0])])
        C = w[k].shape[-1]
        d_, m_, v_ = adamw(w[k].reshape(-1, C), g.reshape(-1, C), m[k].reshape(-1, C), v[k].reshape(-1, C), f"adamw_{k}")
        grads[k], deltas[k], new_m[k], new_v[k] = (a.reshape(w[k].shape) for a in (g, d_, m_, v_))

    full_shapes = [gsmall[k].shape for k in _SMALL] + [(1,)]
    loss_local = jnp.sum(loss_cols).reshape(1)
    red = sum8(gather_small(_pack([gsmall[k] for k in _SMALL] + [loss_local]), "gather_small_grads"), "sum_small_grads")
    red = _unpack(red, full_shapes)
    loss = red[-1][0]
    gsm = dict(zip(_SMALL, red[:-1]))
    for k in _SMALL_SHARDED:
        n = w[k].shape[-1]
        gsm[k] = lax.dynamic_slice_in_dim(gsm[k], chip * n, n, axis=gsm[k].ndim - 1)
    shapes = [w[k].shape for k in _SMALL]
    packed = [_pack([d[k] for k in _SMALL]) for d in (w, gsm, m, v)]
    outs = adamw(*packed, "adamw_small")
    for k, g_, d_, m_, v_ in zip(_SMALL, [gsm[k] for k in _SMALL], *[_unpack(o, shapes) for o in outs]):
        grads[k], deltas[k], new_m[k], new_v[k] = g_.reshape(w[k].shape), d_, m_, v_

    return (loss, grad_x, *[grads[k] for k in _WEIGHTS], *[deltas[k] for k in _WEIGHTS],
            *[new_m[k] for k in _WEIGHTS], *[new_v[k] for k in _WEIGHTS])


def kernel(x, norm_mix_g, norm_mlp_g, sb_w_qkv, sb_q_norm_g, sb_k_norm_g, sb_w_o, gm_w_in, gm_v_norm_g, gm_w_s, gm_b_s, gm_w_o, ssd_w_in, ssd_conv_w, ssd_conv_b, ssd_dt_bias, ssd_a_log, ssd_d, ssd_norm_g, ssd_w_o, mlp_w_in, mlp_w_out, loss_target, m_norm_mix_g, m_norm_mlp_g, m_sb_w_qkv, m_sb_q_norm_g, m_sb_k_norm_g, m_sb_w_o, m_gm_w_in, m_gm_v_norm_g, m_gm_w_s, m_gm_b_s, m_gm_w_o, m_ssd_w_in, m_ssd_conv_w, m_ssd_conv_b, m_ssd_dt_bias, m_ssd_a_log, m_ssd_d, m_ssd_norm_g, m_ssd_w_o, m_mlp_w_in, m_mlp_w_out, v_norm_mix_g, v_norm_mlp_g, v_sb_w_qkv, v_sb_q_norm_g, v_sb_k_norm_g, v_sb_w_o, v_gm_w_in, v_gm_v_norm_g, v_gm_w_s, v_gm_b_s, v_gm_w_o, v_ssd_w_in, v_ssd_conv_w, v_ssd_conv_b, v_ssd_dt_bias, v_ssd_a_log, v_ssd_d, v_ssd_norm_g, v_ssd_w_o, v_mlp_w_in, v_mlp_w_out):
    w = dict(zip(_WEIGHTS, (norm_mix_g, norm_mlp_g, sb_w_qkv, sb_q_norm_g, sb_k_norm_g, sb_w_o, gm_w_in, gm_v_norm_g, gm_w_s,
                            gm_b_s, gm_w_o, ssd_w_in, ssd_conv_w, ssd_conv_b, ssd_dt_bias, ssd_a_log, ssd_d, ssd_norm_g,
                            ssd_w_o, mlp_w_in, mlp_w_out)))
    m = dict(zip(_WEIGHTS, (m_norm_mix_g, m_norm_mlp_g, m_sb_w_qkv, m_sb_q_norm_g, m_sb_k_norm_g, m_sb_w_o, m_gm_w_in,
                            m_gm_v_norm_g, m_gm_w_s, m_gm_b_s, m_gm_w_o, m_ssd_w_in, m_ssd_conv_w, m_ssd_conv_b,
                            m_ssd_dt_bias, m_ssd_a_log, m_ssd_d, m_ssd_norm_g, m_ssd_w_o, m_mlp_w_in, m_mlp_w_out)))
    v = dict(zip(_WEIGHTS, (v_norm_mix_g, v_norm_mlp_g, v_sb_w_qkv, v_sb_q_norm_g, v_sb_k_norm_g, v_sb_w_o, v_gm_w_in,
                            v_gm_v_norm_g, v_gm_w_s, v_gm_b_s, v_gm_w_o, v_ssd_w_in, v_ssd_conv_w, v_ssd_conv_b,
                            v_ssd_dt_bias, v_ssd_a_log, v_ssd_d, v_ssd_norm_g, v_ssd_w_o, v_mlp_w_in, v_mlp_w_out)))
    return _step(x, w, loss_target, m, v)
```

```python
import functools
import math

import jax
import jax.numpy as jnp
from jax import lax
from jax.experimental import pallas as pl
from jax.experimental.pallas import tpu as pltpu

F32 = jnp.float32
BF16 = jnp.bfloat16
_MXU = jnp.bfloat16
_ACT = jnp.bfloat16
GRAD_DT = jnp.bfloat16
_VMEM_LIMIT = 56 * 1024 * 1024
EPS = 1e-6
LANES = 128
CHUNK = 128
SSD_HEAD_DIM = 64
SSD_STATE = 128
SSD_CONV = 4
ADAM_LR, ADAM_B1, ADAM_B2, ADAM_EPS, ADAM_WD, ADAM_STEP = 1e-3, 0.9, 0.999, 1e-8, 0.01, 10
MESH = pl.DeviceIdType.MESH

NN = (((1,), (0,)), ((), ()))
NT = (((1,), (1,)), ((), ()))
TN = (((0,), (0,)), ((), ()))

_ANY = pl.BlockSpec(memory_space=pl.ANY)
_REL = ((1, 0), (0, 1), (1, 1))


def _dot(a, b, dims=NN):
    return lax.dot_general(a.astype(_MXU), b.astype(_MXU), dims, preferred_element_type=F32)


def _params(sem):
    return pltpu.CompilerParams(dimension_semantics=sem, vmem_limit_bytes=_VMEM_LIMIT)


def _pick(n, *cands):
    for c in cands:
        if n % c == 0:
            return c
    return n


def _xyc():
    return lax.axis_index("x"), lax.axis_index("y"), lax.axis_index("c")


def _flip(v, f):
    return 1 - v if f else v


def _remote(src, dst, ssem, rsem, dev):
    return pltpu.make_async_remote_copy(src_ref=src, dst_ref=dst, send_sem=ssem, recv_sem=rsem, device_id=dev,
                                        device_id_type=MESH)


class Carriers:
    def __init__(self):
        self.pending, self.done = [], {}

    def add(self, kind, src, tag):
        self.pending.append((kind, src, tag))


def matmul(a, b, mode, out_dtypes, name, epilogue=None, extras=(), b_cols4=False, out_cols4=False, cq=None):
    ash = a.shape
    bsh = (b.shape[1], 4 * b.shape[2]) if b_cols4 else b.shape
    if mode == "nn":
        (M, K), (K2, N) = ash, bsh
    elif mode == "nt":
        (M, K), (N, K2) = ash, bsh
    else:
        (K, M), (K2, N) = ash, bsh
    assert K == K2, (mode, a.shape, b.shape)
    nsh = N // 4 if (out_cols4 or (b_cols4 and mode == "nn")) else None
    ksh = K // 4 if (b_cols4 and mode == "nt") else None
    fits = lambda n, sh: [c for c in (1024, 512, 256, 128) if sh is None or sh % c == 0] + ([] if sh is None else [sh])
    tm = _pick(M, 1024, 512, 256, 128) if a.dtype.itemsize == 2 and not extras else _pick(M, 512, 256, 128)
    tn = _pick(N, *fits(N, nsh))
    tk = _pick(K, *([2048] if ksh is None or ksh % 2048 == 0 else []), *fits(K, ksh))
    ni, nj, nk = M // tm, N // tn, K // tk
    dims = {"nn": NN, "nt": NT, "tn": TN}[mode]
    single = not isinstance(out_dtypes, (tuple, list))
    odt = (out_dtypes,) if single else tuple(out_dtypes)
    n_ex, n_out = len(extras), len(odt)
    carry = cq.pending.pop(0) if (cq is not None and cq.pending) else None

    if mode == "tn":
        a_spec = pl.BlockSpec((tk, tm), lambda i, j, k: (k, i))
    else:
        a_spec = pl.BlockSpec((tm, tk), lambda i, j, k: (i, k))
    if b_cols4 and mode == "nn":
        b_spec = pl.BlockSpec((None, tk, tn), lambda i, j, k: (lax.div(j * tn, nsh), k, lax.div(lax.rem(j * tn, nsh), tn)))
    elif b_cols4:
        b_spec = pl.BlockSpec((None, tn, tk), lambda i, j, k: (lax.div(k * tk, ksh), j, lax.div(lax.rem(k * tk, ksh), tk)))
    elif mode == "nt":
        b_spec = pl.BlockSpec((tn, tk), lambda i, j, k: (j, k))
    else:
        b_spec = pl.BlockSpec((tk, tn), lambda i, j, k: (k, j))
    mn_spec = pl.BlockSpec((tm, tn), lambda i, j, k: (i, j))
    if out_cols4:
        o_spec = pl.BlockSpec((None, tm, tn), lambda i, j, k: (lax.div(j * tn, nsh), i, lax.div(lax.rem(j * tn, nsh), tn)))
        o_shape = (4, M, N // 4)
    else:
        o_spec, o_shape = mn_spec, (M, N)

    def body(*refs):
        a_ref, b_ref = refs[0], refs[1]
        ex = refs[2:2 + n_ex]
        pos = 2 + n_ex
        src_ref = refs[pos] if carry else None
        pos += 1 if carry else 0
        outs = refs[pos:pos + n_out]
        pos += n_out
        dst_ref = refs[pos] if carry else None
        pos += 1 if carry else 0
        acc = refs[pos]
        i, j, k = pl.program_id(0), pl.program_id(1), pl.program_id(2)

        if carry:
            ssem, rsem = refs[pos + 1], refs[pos + 2]
            x_, y_, c_ = _xyc()
            chips = [(_flip(x_, fx), _flip(y_, fy)) for fx, fy in _REL]

            def copies(arriving):
                out = []
                for r, (px, py) in enumerate(chips):
                    if carry[0] == "gather":
                        s_, d_ = src_ref.at[c_], dst_ref.at[(2 * px + py) if arriving else (2 * x_ + y_), c_]
                    else:
                        s_, d_ = src_ref.at[2 * px + py], dst_ref.at[r]
                    out.append(_remote(s_, d_, ssem.at[r], rsem.at[r], (px, py, c_)))
                return out

            @pl.when((i == 0) & (j == 0) & (k == 0))
            def _():
                for send in copies(False):
                    send.start()

        @pl.when(k == 0)
        def _():
            acc[...] = jnp.zeros_like(acc)

        part = _dot(a_ref[...], b_ref[...], dims)

        @pl.when(k < nk - 1)
        def _():
            acc[...] += part

        @pl.when(k == nk - 1)
        def _():
            r = acc[...] + part
            res = (r,) if epilogue is None else epilogue(r, *[e[...] for e in ex])
            for o, v in zip(outs, res):
                o[...] = v.astype(o.dtype)

        if carry:
            @pl.when((i == ni - 1) & (j == nj - 1) & (k == nk - 1))
            def _():
                for arrive in copies(True):
                    arrive.wait_recv()
                for send in copies(False):
                    send.wait_send()

    in_specs = [a_spec, b_spec] + [mn_spec] * n_ex
    out_shape = [jax.ShapeDtypeStruct(o_shape, d) for d in odt]
    out_specs = [o_spec for _ in odt]
    scratch = [pltpu.VMEM((tm, tn), F32)]
    operands = [a, b, *extras]
    if carry:
        kind, src, tag = carry
        in_specs.append(_ANY)
        operands.append(src)
        out_shape.append(jax.ShapeDtypeStruct(((4,) + src.shape) if kind == "gather" else ((3,) + src.shape[1:]), src.dtype))
        out_specs.append(_ANY)
        scratch += [pltpu.SemaphoreType.DMA((3,)), pltpu.SemaphoreType.DMA((3,))]
    out = pl.pallas_call(
        body,
        out_shape=tuple(out_shape),
        grid=(ni, nj, nk),
        in_specs=in_specs,
        out_specs=tuple(out_specs),
        scratch_shapes=scratch,
        compiler_params=_params(("arbitrary", "arbitrary", "arbitrary") if carry else ("parallel", "parallel", "arbitrary")),
        name=name,
    )(*operands)
    if carry:
        cq.done[carry[2]] = out[-1]
        out = out[:-1]
    return out[0] if single else out


class Op:
    def __init__(self, arr, block, imap, kind="tile", grad=True, gshape=None, gimap=None):
        self.arr, self.block, self.imap, self.kind, self.grad = arr, block, imap, kind, grad
        self.gshape = gshape or arr.shape
        self.gimap = gimap or imap

    def spec(self):
        return pl.BlockSpec(self.block, self.imap)


def tmap(f, grid, ins, outs, name):
    n_in = len(ins)

    def body(*refs):
        res = f(*[r[...] for r in refs[:n_in]])
        for o, v in zip(refs[n_in:], res):
            o[...] = v.astype(o.dtype)

    return pl.pallas_call(
        body,
        out_shape=tuple(jax.ShapeDtypeStruct(s, d) for s, d, _, _ in outs),
        grid=grid,
        in_specs=[o.spec() for o in ins],
        out_specs=tuple(pl.BlockSpec(b, m) for _, _, b, m in outs),
        compiler_params=_params(("parallel", "parallel")),
        name=name,
    )(*[o.arr for o in ins])


def tmap_vjp(f, grid, ins, cts, name, grad_dtypes=None):
    n_in, n_ct = len(ins), len(cts)
    gidx = [i for i, o in enumerate(ins) if o.grad]
    gdt = grad_dtypes or {}

    def body(*refs):
        in_refs, ct_refs, g_refs = refs[:n_in], refs[n_in:n_in + n_ct], refs[n_in + n_ct:]
        vals = [r[...] for r in in_refs]

        def g_only(*diff):
            full = list(vals)
            for i, v in zip(gidx, diff):
                full[i] = v
            return f(*full)

        res, vjp = jax.vjp(g_only, *[vals[i].astype(F32) for i in gidx])
        grads = vjp(tuple(c[...].astype(r.dtype) for c, r in zip(ct_refs, res)))
        inner = pl.program_id(1)
        for i, g, gr in zip(gidx, grads, g_refs):
            if ins[i].kind == "tile":
                gr[...] = g.astype(gr.dtype)
            else:
                @pl.when(inner == 0)
                def _(gr=gr, g=g):
                    gr[...] = g.astype(gr.dtype)

                @pl.when(inner != 0)
                def _(gr=gr, g=g):
                    gr[...] += g.astype(gr.dtype)

    out_shape = tuple(jax.ShapeDtypeStruct(ins[i].gshape, gdt.get(i, F32)) for i in gidx)
    return pl.pallas_call(
        body,
        out_shape=out_shape,
        grid=grid,
        in_specs=[o.spec() for o in ins] + [o.spec() for o in cts],
        out_specs=tuple(pl.BlockSpec(ins[i].block, ins[i].gimap) for i in gidx),
        compiler_params=_params(("parallel", "arbitrary")),
        name=name,
    )(*[o.arr for o in ins], *[o.arr for o in cts])


def _rms(x, g):
    return x * lax.rsqrt(jnp.mean(x * x, axis=-1, keepdims=True) + EPS) * g


def _row_ops(arrs, tm, grads=None):
    grads = grads or [True] * len(arrs)
    return [Op(a, (tm, a.shape[1]), lambda o, i: (i, 0), "tile", g) for a, g in zip(arrs, grads)]


def _vec_op(v, grad=True):
    return Op(v, (1, v.shape[1]), lambda o, i: (0, 0), "param", grad)


def rmsnorm_fwd(h, g, name):
    T, D = h.shape
    tm = _pick(T, 512, 256, 128)
    f = lambda x, gg: (_rms(x, gg),)
    return tmap(f, (1, T // tm), _row_ops([h], tm) + [_vec_op(g)],
                [((T, D), _ACT, (tm, D), lambda o, i: (i, 0))], name)[0]


def rmsnorm_bwd(h, g, dhn, dres, name):
    T, D = h.shape
    tm = _pick(T, 512, 256, 128)
    f = lambda x, gg: (_rms(x, gg), x)
    return tmap_vjp(f, (1, T // tm), _row_ops([h], tm) + [_vec_op(g)], _row_ops([dhn, dres], tm), name)


def mlp_fwd(h, g_row, w_in, w_out, tag, cq=None):
    hn = rmsnorm_fwd(h, g_row, f"{tag}_norm")
    a, r2 = matmul(hn, w_in, "nn", (F32, _ACT), f"{tag}_in", b_cols4=True, cq=cq,
                   epilogue=lambda acc: (acc, jnp.square(jnp.maximum(acc, 0.0))))
    out = matmul(r2, w_out, "nn", F32, f"{tag}_out", cq=cq, epilogue=lambda acc, hh: (acc + hh,), extras=(h,))
    return out, (h, hn, a, r2)


def mlp_bwd(dout, saved, g_row, w_in, w_out, tag, cq=None):
    h, hn, a, r2 = saved
    da = matmul(dout, w_out, "nt", _ACT, f"{tag}_dact", cq=cq,
                epilogue=lambda acc, aa: (acc * (2.0 * jnp.maximum(aa, 0.0)),), extras=(a,))
    dw_out = matmul(r2, dout, "tn", GRAD_DT, f"{tag}_dwout", cq=cq)
    dw_in = matmul(hn, da, "tn", GRAD_DT, f"{tag}_dwin", out_cols4=True, cq=cq)
    dhn = matmul(da, w_in, "nt", F32, f"{tag}_dhn", b_cols4=True, cq=cq)
    dh, dg = rmsnorm_bwd(h, g_row, dhn, dout, f"{tag}_dnorm")
    return dh, dg, dw_in, dw_out


def _split3(x):
    hi = x.astype(BF16)
    r = x - hi.astype(F32)
    mid = r.astype(BF16)
    lo = (r - mid.astype(F32)).astype(BF16)
    return hi, mid, lo


def _cumdot(x, tri):
    return sum(lax.dot_general(p, tri, NN, preferred_element_type=F32) for p in _split3(x))


def _iotas():
    row = lax.broadcasted_iota(jnp.int32, (CHUNK, CHUNK), 0)
    col = lax.broadcasted_iota(jnp.int32, (CHUNK, CHUNK), 1)
    return row, col


_TQ = 256
_TK = 256
_DEAD = -88.0


def _sb_block(q, kblk, q0, k0, scale, row, col):
    z = _dot(q, kblk, NT) * scale
    e = jnp.exp(-jnp.abs(z))
    den = 1.0 + e
    sp = jnp.maximum(z, 0.0) + jnp.log(den)
    mask = (col + k0) < (row + q0)
    lg = jnp.where(mask, -sp, 0.0)
    beta = jnp.where(z >= 0, 1.0, e) / den
    return z, mask, lg, beta


def _attn_iotas(tq, tk):
    row = lax.broadcasted_iota(jnp.int32, (tq, tk), 0)
    col = lax.broadcasted_iota(jnp.int32, (tq, tk), 1)
    r2 = lax.broadcasted_iota(jnp.int32, (tk, tk), 0)
    c2 = lax.broadcasted_iota(jnp.int32, (tk, tk), 1)
    return row, col, r2, c2


def attn_fwd(qn, kn, v, name):
    T, W = qn.shape
    tq, tk = _pick(T, _TQ, CHUNK), _pick(T, _TK, CHUNK)
    H, NQ, per = W // LANES, T // tq, tq // tk
    assert tq % tk == 0 and T // tk <= LANES
    scale = LANES ** -0.5

    def body(q_ref, k_ref, v_ref, o_ref, r_ref, acc_ref, run_ref):
        qi = pl.program_id(1)
        q = q_ref[...]
        row, col, r2, c2 = _attn_iotas(tq, tk)
        suffix = (r2 >= c2).astype(_MXU)
        lane_q = lax.broadcasted_iota(jnp.int32, (tq, LANES), 1)
        acc_ref[...] = jnp.zeros_like(acc_ref)
        run_ref[...] = jnp.zeros_like(run_ref)
        r_ref[...] = jnp.full(r_ref.shape, -1e30, F32)

        def step(carry):
            kb, _ = carry
            off = pl.multiple_of(kb * tk, tk)
            run = run_ref[...]
            z, mask, lg, _ = _sb_block(q, k_ref[pl.ds(off, tk), :], qi * tq, off, scale, row, col)
            r_ref[...] = jnp.where(lane_q == kb, run, r_ref[...])
            cl = _cumdot(lg, suffix) + run
            a = jnp.exp(jnp.where(mask, z + cl, -1e30))
            acc_ref[...] += _dot(a, v_ref[pl.ds(off, tk), :])
            run = run + jnp.sum(lg, axis=1, keepdims=True)
            run_ref[...] = run
            return kb - 1, jnp.max(run) > _DEAD

        lax.while_loop(lambda c: (c[0] >= 0) & c[1], step, ((qi + 1) * per - 1, True))
        o_ref[...] = acc_ref[...].astype(o_ref.dtype)

    qspec = pl.BlockSpec((tq, LANES), lambda h, i: (i, h))
    kvspec = pl.BlockSpec((T, LANES), lambda h, i: (0, h))
    return pl.pallas_call(
        body,
        out_shape=(jax.ShapeDtypeStruct((T, W), _ACT), jax.ShapeDtypeStruct((H, T, LANES), F32)),
        grid=(H, NQ),
        in_specs=[qspec, kvspec, kvspec],
        out_specs=(qspec, pl.BlockSpec((None, tq, LANES), lambda h, i: (h, i, 0))),
        scratch_shapes=[pltpu.VMEM((tq, LANES), F32), pltpu.VMEM((tq, 1), F32)],
        compiler_params=_params(("parallel", "parallel")),
        name=name,
    )(qn, kn, v)


def attn_bwd(qn, kn, v, do, rblk, name):
    T, W = qn.shape
    tq, tk = _pick(T, _TQ, CHUNK), _pick(T, _TK, CHUNK)
    H, NQ, per = W // LANES, T // tq, tq // tk
    scale = LANES ** -0.5

    def body(q_ref, k_ref, v_ref, do_ref, r_ref, dq_ref, dk_ref, dv_ref, g_ref):
        qi = pl.program_id(1)

        @pl.when(qi == 0)
        def _():
            dk_ref[...] = jnp.zeros_like(dk_ref)
            dv_ref[...] = jnp.zeros_like(dv_ref)

        q = q_ref[...]
        dout = do_ref[...]
        rt = r_ref[...]
        row, col, r2, c2 = _attn_iotas(tq, tk)
        suffix = (r2 >= c2).astype(_MXU)
        prefix = (r2 <= c2).astype(_MXU)
        kend = (qi + 1) * per - 1
        lane = lax.broadcasted_iota(jnp.int32, (1, LANES), 1)
        lane_q = lax.broadcasted_iota(jnp.int32, (tq, LANES), 1)
        unvisited = (jnp.max(rt, axis=0, keepdims=True) < -1e29) & (lane <= kend)
        start = jnp.sum(unvisited.astype(jnp.int32))

        dq_ref[...] = jnp.zeros_like(dq_ref)
        g_ref[...] = jnp.zeros_like(g_ref)

        @pl.loop(start, kend + 1)
        def _(kb):
            gsum = g_ref[...]
            off = pl.multiple_of(kb * tk, tk)
            kblk = k_ref[pl.ds(off, tk), :]
            vblk = v_ref[pl.ds(off, tk), :]
            z, mask, lg, beta = _sb_block(q, kblk, qi * tq, off, scale, row, col)
            run = jnp.sum(jnp.where(lane_q == kb, rt, 0.0), axis=1, keepdims=True)
            cl = _cumdot(lg, suffix) + run
            a = jnp.exp(jnp.where(mask, z + cl, -1e30))
            e = _dot(dout, vblk, NT) * a
            f = _cumdot(e, prefix) + gsum
            dz = jnp.where(mask, e - beta * f, 0.0) * scale
            dk_ref[pl.ds(off, tk), :] += _dot(dz, q, TN)
            dv_ref[pl.ds(off, tk), :] += _dot(a, dout, TN)
            dq_ref[...] += _dot(dz, kblk)
            g_ref[...] = gsum + jnp.sum(e, axis=1, keepdims=True)

    qspec = pl.BlockSpec((tq, LANES), lambda h, i: (i, h))
    kvspec = pl.BlockSpec((T, LANES), lambda h, i: (0, h))
    big = jax.ShapeDtypeStruct((T, W), F32)
    return pl.pallas_call(
        body,
        out_shape=(big, big, big),
        grid=(H, NQ),
        in_specs=[qspec, kvspec, kvspec, qspec, pl.BlockSpec((None, tq, LANES), lambda h, i: (h, i, 0))],
        out_specs=(qspec, kvspec, kvspec),
        scratch_shapes=[pltpu.VMEM((tq, 1), F32)],
        compiler_params=_params(("parallel", "arbitrary")),
        name=name,
    )(qn, kn, v, do, rblk)


def _qk_ops(qkv, qg, kg, tm, grad):
    T, W3 = qkv.shape
    H, NT_ = W3 // (3 * LANES), T // tm
    W = H * LANES

    def part(p):
        return Op(qkv, (tm, LANES), lambda o, n: (lax.rem(n, NT_), p * H + lax.div(n, NT_)), "tile", grad,
                  gshape=(T, W), gimap=lambda o, n: (lax.rem(n, NT_), lax.div(n, NT_)))

    vec = lambda g: Op(g, (1, LANES), lambda o, n: (0, 0), "param", grad)
    return [part(0), part(1), part(2), vec(qg), vec(kg)], (1, H * NT_), H, NT_, W


def _qk_f(q, k, v, qg, kg):
    return _rms(q, qg), _rms(k, kg), v


def qknorm_fwd(qkv, qg, kg, name):
    T = qkv.shape[0]
    tm = _pick(T, 512, 256, 128)
    ins, grid, H, NT_, W = _qk_ops(qkv, qg, kg, tm, False)
    out = ((T, W), _ACT, (tm, LANES), lambda o, n: (lax.rem(n, NT_), lax.div(n, NT_)))
    return tmap(_qk_f, grid, ins, [out, out, out], name)


def qknorm_bwd(qkv, qg, kg, dq, dk, dv, name):
    T = qkv.shape[0]
    tm = _pick(T, 512, 256, 128)
    ins, grid, H, NT_, W = _qk_ops(qkv, qg, kg, tm, True)
    cts = [Op(c, (tm, LANES), lambda o, n: (lax.rem(n, NT_), lax.div(n, NT_))) for c in (dq, dk, dv)]
    return tmap_vjp(_qk_f, grid, ins, cts, name, grad_dtypes={0: _ACT, 1: _ACT, 2: _ACT})


def sb_fwd(h, g_row, w_qkv, qg, kg, w_o, tag, cq=None):
    hn = rmsnorm_fwd(h, g_row, f"{tag}_norm")
    qkv = matmul(hn, w_qkv, "nn", F32, f"{tag}_qkv", b_cols4=True, cq=cq)
    qn, kn, v = qknorm_fwd(qkv, qg, kg, f"{tag}_qknorm")
    o, rblk = attn_fwd(qn, kn, v, f"{tag}_attn")
    out = matmul(o, w_o, "nn", F32, f"{tag}_wo", cq=cq, epilogue=lambda acc, hh: (acc + hh,), extras=(h,))
    return out, (h, hn, qkv, qn, kn, v, o, rblk)


def sb_bwd(dout, saved, g_row, w_qkv, qg, kg, w_o, tag, cq=None):
    h, hn, qkv, qn, kn, v, o, rblk = saved
    do = matmul(dout, w_o, "nt", _ACT, f"{tag}_do")
    dw_o = matmul(o, dout, "tn", GRAD_DT, f"{tag}_dwo")
    dqn, dkn, dv = attn_bwd(qn, kn, v, do, rblk, f"{tag}_dattn")
    dq, dk, dvv, dqg, dkg = qknorm_bwd(qkv, qg, kg, dqn, dkn, dv, f"{tag}_dqknorm")
    dqkv = jnp.concatenate([dq, dk, dvv], axis=1)
    dw_qkv = matmul(hn, dqkv, "tn", GRAD_DT, f"{tag}_dwqkv", out_cols4=True)
    dhn = matmul(dqkv, w_qkv, "nt", F32, f"{tag}_dhn", b_cols4=True)
    dh, dg = rmsnorm_bwd(h, g_row, dhn, dout, f"{tag}_dnorm")
    return dh, dg, dw_qkv, dqg, dkg, dw_o


@functools.partial(jax.custom_vjp, nondiff_argnums=(2,))
def _dotv(a, b, mode):
    return _dot(a, b, {"nn": NN, "nt": NT, "tn": TN}[mode])


def _dotv_fwd(a, b, mode):
    return _dotv(a, b, mode), (a, b)


def _dotv_bwd(mode, res, g):
    a, b = res
    if mode == "nn":
        return _dot(g, b, NT), _dot(a, g, TN)
    if mode == "nt":
        return _dot(g, b, NN), _dot(g, a, TN)
    return _dot(b, g, NT), _dot(a, g, NN)


_dotv.defvjp(_dotv_fwd, _dotv_bwd)


def _gelu(x):
    return 0.5 * x * (1.0 + lax.erf(x * (2.0 ** -0.5)))


def _gm1_f(au, av, vg):
    return _gelu(au), _rms(_gelu(av), vg)


def _gm1_ops(a, vg, tm, grad):
    T, W2 = a.shape
    W = W2 // 2
    part = lambda p: Op(a, (tm, W), lambda o, i: (i, p), "tile", grad, gshape=(T, W), gimap=lambda o, i: (i, 0))
    return [part(0), part(1), _vec_op(vg, grad)], (1, T // tm), W


_GM_ROWS = 1024


def _gm_specs(T, W, G):
    rows = _pick(T, _GM_ROWS, 512, 256, CHUNK)
    blk = pl.BlockSpec((rows, LANES), lambda g, c: (c, g))
    wspec = pl.BlockSpec((None, CHUNK, CHUNK), lambda g, c: (g, 0, 0))
    bspec = pl.BlockSpec((None, CHUNK, 1), lambda g, c: (g, 0, 0))
    return rows, blk, wspec, bspec, (G, T // rows)


def gm_mix_fwd(u, vn, ws, bcol, name):
    T, W = u.shape
    rows, blk, wspec, bspec, grid = _gm_specs(T, W, W // LANES)

    def body(u_ref, v_ref, w_ref, b_ref, y_ref):
        r, c = _iotas()
        w = jnp.where(r >= c, w_ref[...], 0.0).astype(_MXU)
        for k in range(rows // CHUNK):
            sl = pl.ds(k * CHUNK, CHUNK)
            y_ref[sl, :] = (u_ref[sl, :] * (_dot(w, v_ref[sl, :]) + b_ref[...])).astype(y_ref.dtype)

    return pl.pallas_call(body, out_shape=jax.ShapeDtypeStruct((T, W), _ACT), grid=grid,
                          in_specs=[blk, blk, wspec, bspec], out_specs=blk,
                          compiler_params=_params(("parallel", "parallel")), name=name)(u, vn, ws, bcol)


def gm_mix_bwd(u, vn, ws, bcol, dy, name):
    T, W = u.shape
    G = W // LANES
    rows, blk, wspec, bspec, grid = _gm_specs(T, W, G)

    def body(u_ref, v_ref, w_ref, b_ref, dy_ref, du_ref, dv_ref, dw_ref, db_ref):
        r, c = _iotas()
        tri = r >= c
        w = jnp.where(tri, w_ref[...], 0.0).astype(_MXU)
        dw = jnp.zeros((CHUNK, CHUNK), F32)
        db = jnp.zeros((CHUNK, 1), F32)
        for k in range(rows // CHUNK):
            sl = pl.ds(k * CHUNK, CHUNK)
            v = v_ref[sl, :]
            g = dy_ref[sl, :]
            du_ref[sl, :] = g * (_dot(w, v) + b_ref[...])
            dm = g * u_ref[sl, :]
            dv_ref[sl, :] = _dot(w, dm, TN)
            dw = dw + _dot(dm, v, NT)
            db = db + jnp.sum(dm, axis=1, keepdims=True)
        dw = jnp.where(tri, dw, 0.0)

        @pl.when(pl.program_id(1) == 0)
        def _():
            dw_ref[...] = dw
            db_ref[...] = db

        @pl.when(pl.program_id(1) != 0)
        def _():
            dw_ref[...] += dw
            db_ref[...] += db

    big = jax.ShapeDtypeStruct((T, W), F32)
    return pl.pallas_call(
        body,
        out_shape=(big, big, jax.ShapeDtypeStruct((G, CHUNK, CHUNK), F32), jax.ShapeDtypeStruct((G, CHUNK, 1), F32)),
        grid=grid, in_specs=[blk, blk, wspec, bspec, blk], out_specs=(blk, blk, wspec, bspec),
        compiler_params=_params(("parallel", "arbitrary")), name=name)(u, vn, ws, bcol, dy)


def gm_fwd(h, g_row, w_in, vg, ws, bcol, w_o, tag, cq=None):
    T = h.shape[0]
    tm = _pick(T, 256, 128)
    hn = rmsnorm_fwd(h, g_row, f"{tag}_norm")
    a = matmul(hn, w_in, "nn", F32, f"{tag}_in", b_cols4=True, cq=cq)
    ins, grid, W = _gm1_ops(a, vg, tm, False)
    rows = lambda dt: ((T, W), dt, (tm, W), lambda o, i: (i, 0))
    u, vn = tmap(_gm1_f, grid, ins, [rows(F32), rows(_ACT)], f"{tag}_act")
    y = gm_mix_fwd(u, vn, ws, bcol, f"{tag}_mix")
    out = matmul(y, w_o, "nn", F32, f"{tag}_wo", cq=cq, epilogue=lambda acc, hh: (acc + hh,), extras=(h,))
    return out, (h, hn, a, u, vn, y)


def gm_bwd(dout, saved, g_row, w_in, vg, ws, bcol, w_o, tag, cq=None):
    h, hn, a, u, vn, y = saved
    T = h.shape[0]
    tm = _pick(T, 256, 128)
    dy = matmul(dout, w_o, "nt", F32, f"{tag}_dy")
    dw_o = matmul(y, dout, "tn", GRAD_DT, f"{tag}_dwo")
    du, dvn, dws, dbcol = gm_mix_bwd(u, vn, ws, bcol, dy, f"{tag}_dmix")
    ins, grid, W = _gm1_ops(a, vg, tm, True)
    dau, dav, dvg = tmap_vjp(_gm1_f, grid, ins, _row_ops([du, dvn], tm), f"{tag}_dact", grad_dtypes={0: _ACT, 1: _ACT})
    da = jnp.concatenate([dau, dav], axis=1)
    dw_in = matmul(hn, da, "tn", GRAD_DT, f"{tag}_dwin", out_cols4=True)
    dhn = matmul(da, w_in, "nt", F32, f"{tag}_dhn", b_cols4=True)
    dh, dg = rmsnorm_bwd(h, g_row, dhn, dout, f"{tag}_dnorm")
    return dh, dg, dw_in, dvg, dws, dbcol, dw_o


@jax.custom_vjp
def _softplus(x):
    return jnp.maximum(x, 0.0) + jnp.log(1.0 + jnp.exp(-jnp.abs(x)))


_softplus.defvjp(lambda x: (_softplus(x), x), lambda x, g: (g * lax.logistic(x),))


def _silu(x):
    return x * lax.logistic(x)


def _shift_impl(x, s, down):
    n = x.shape[0]
    r = lax.broadcasted_iota(jnp.int32, x.shape, 0)
    if down:
        return jnp.where(r >= s, pltpu.roll(x, s, 0), 0.0)
    return jnp.where(r < n - s, pltpu.roll(x, n - s, 0), 0.0)


@functools.partial(jax.custom_vjp, nondiff_argnums=(1,))
def _shift_down(x, s):
    return _shift_impl(x, s, True)


_shift_down.defvjp(lambda x, s: (_shift_impl(x, s, True), None), lambda s, _, g: (_shift_impl(g, s, False),))


def _conv_f(x, w, b):
    k_id = lax.broadcasted_iota(jnp.int32, w.shape, 0)
    y = b + jnp.sum(jnp.where(k_id == SSD_CONV - 1, w, 0.0), axis=0, keepdims=True) * x
    for k in range(SSD_CONV - 1):
        wk = jnp.sum(jnp.where(k_id == k, w, 0.0), axis=0, keepdims=True)
        y = y + wk * _shift_down(x, SSD_CONV - 1 - k)
    return (_silu(y),)


def _conv_ops(zx, conv_w, conv_b, wi, grad):
    T = zx.shape[0]
    cd = conv_w.shape[1]
    cw = LANES
    off = wi // cw
    return [Op(zx, (T, cw), lambda o, j: (0, off + j), "tile", grad, gshape=(T, cd), gimap=lambda o, j: (0, j)),
            Op(conv_w, (SSD_CONV, cw), lambda o, j: (0, j), "tile", grad),
            Op(conv_b, (1, cw), lambda o, j: (0, j), "tile", grad)], (1, cd // cw), (T, cw)


def _dt_f(dtr, bias):
    return (_softplus(dtr + bias),)


def _cumdot_left(tri, x):
    return sum(lax.dot_general(tri, p, NN, preferred_element_type=F32) for p in _split3(x))


@jax.custom_vjp
def _cumsum_rows(x):
    row, col = _iotas()
    return _cumdot_left((row >= col).astype(_MXU), x)


def _cumsum_rows_bwd(_, g):
    row, col = _iotas()
    return (_cumdot_left((row <= col).astype(_MXU), g),)


_cumsum_rows.defvjp(lambda x: (_cumsum_rows(x), None), _cumsum_rows_bwd)


def _ssd_chunk(xps, dt, bm, cm, sps, alog, hid_base):
    row, col = _iotas()
    half = SSD_HEAD_DIM
    rcol = lax.broadcasted_iota(jnp.int32, (CHUNK, 1), 0)
    colpick = lambda m, hid: jnp.sum(jnp.where(col == hid, m, 0.0), axis=1, keepdims=True)
    rowpick = lambda m, hid: jnp.sum(jnp.where(row == hid, m, 0.0), axis=0, keepdims=True)
    last = lambda v: jnp.sum(jnp.where(rcol == CHUNK - 1, v, 0.0), axis=0, keepdims=True)
    acum = _cumsum_rows(dt * (-jnp.exp(alog)))
    acum_t = acum.T
    cb = _dotv(cm, bm, "nt")
    tri = row >= col
    lo = col < half
    ys, snews = [], []
    for p, (xp, sp) in enumerate(zip(xps, sps)):
        h0, h1 = hid_base + 2 * p, hid_base + 2 * p + 1
        ac0, ac1 = colpick(acum, h0), colpick(acum, h1)
        m0 = cb * jnp.exp(jnp.where(tri, ac0 - rowpick(acum_t, h0), -1e30))
        m1 = cb * jnp.exp(jnp.where(tri, ac1 - rowpick(acum_t, h1), -1e30))
        xs = xp * jnp.where(lo, colpick(dt, h0), colpick(dt, h1))
        ydiag = jnp.where(lo, _dotv(m0, xs, "nn"), _dotv(m1, xs, "nn"))
        yoff = jnp.where(lo, jnp.exp(ac0), jnp.exp(ac1)) * _dotv(cm, sp, "nt")
        al0, al1 = last(ac0), last(ac1)
        xsd = xs * jnp.where(lo, jnp.exp(al0 - ac0), jnp.exp(al1 - ac1))
        snew = jnp.where(row < half, jnp.exp(al0), jnp.exp(al1)) * sp + _dotv(xsd, bm, "tn")
        ys.append(ydiag + yoff)
        snews.append(snew)
    return ys, snews


def _ssd_dims(xact, wi):
    T, cd = xact.shape
    G = (cd - wi) // (2 * SSD_STATE)
    hpg = wi // SSD_HEAD_DIM // G
    assert hpg % 2 == 0 and SSD_STATE == LANES
    return T, G, hpg, hpg // 2, T // CHUNK, wi // G


def ssd_scan_fwd(xact, dt, alog, wi, name):
    T, G, hpg, NP, NC, gw = _ssd_dims(xact, wi)
    bo, co = wi // LANES, wi // LANES + G

    def body(x_ref, b_ref, c_ref, dt_ref, al_ref, y_ref, st_ref, s_ref):
        g, c = pl.program_id(0), pl.program_id(1)

        @pl.when(c == 0)
        def _():
            s_ref[...] = jnp.zeros_like(s_ref)

        st_ref[...] = s_ref[...]
        xps = [x_ref[:, p * LANES:(p + 1) * LANES] for p in range(NP)]
        sps = [s_ref[p] for p in range(NP)]
        ys, snews = _ssd_chunk(xps, dt_ref[...], b_ref[...], c_ref[...], sps, al_ref[...], g * hpg)
        for p in range(NP):
            y_ref[:, p * LANES:(p + 1) * LANES] = ys[p]
            s_ref[p] = snews[p]

    return pl.pallas_call(
        body,
        out_shape=(jax.ShapeDtypeStruct((T, wi), F32), jax.ShapeDtypeStruct((G, NC, NP, LANES, SSD_STATE), F32)),
        grid=(G, NC),
        in_specs=[pl.BlockSpec((CHUNK, gw), lambda g, c: (c, g)),
                  pl.BlockSpec((CHUNK, LANES), lambda g, c: (c, bo + g)),
                  pl.BlockSpec((CHUNK, LANES), lambda g, c: (c, co + g)),
                  pl.BlockSpec((CHUNK, LANES), lambda g, c: (c, 0)),
                  pl.BlockSpec((1, LANES), lambda g, c: (0, 0))],
        out_specs=(pl.BlockSpec((CHUNK, gw), lambda g, c: (c, g)),
                   pl.BlockSpec((None, None, NP, LANES, SSD_STATE), lambda g, c: (g, c, 0, 0, 0))),
        scratch_shapes=[pltpu.VMEM((NP, LANES, SSD_STATE), F32)],
        compiler_params=_params(("parallel", "arbitrary")),
        name=name,
    )(xact, xact, xact, dt, alog)


def ssd_scan_bwd(xact, dt, alog, states, dy, wi, name):
    T, G, hpg, NP, NC, gw = _ssd_dims(xact, wi)
    bo, co = wi // LANES, wi // LANES + G
    rev = lambda c: NC - 1 - c

    def body(x_ref, b_ref, c_ref, dt_ref, al_ref, st_ref, dy_ref, dx_ref, db_ref, dc_ref, ddt_ref, dal_ref, ds_ref):
        g, c = pl.program_id(0), pl.program_id(1)

        @pl.when(c == 0)
        def _():
            ds_ref[...] = jnp.zeros_like(ds_ref)

        xps = [x_ref[:, p * LANES:(p + 1) * LANES] for p in range(NP)]
        sps = [st_ref[p] for p in range(NP)]
        f = lambda xps_, dt_, bm_, cm_, sps_, al_: _ssd_chunk(xps_, dt_, bm_, cm_, sps_, al_, g * hpg)
        _, vjp = jax.vjp(f, xps, dt_ref[...], b_ref[...], c_ref[...], sps, al_ref[...])
        dys = [dy_ref[:, p * LANES:(p + 1) * LANES] for p in range(NP)]
        dxps, ddt, dbm, dcm, dsps, dal = vjp((dys, [ds_ref[p] for p in range(NP)]))
        for p in range(NP):
            dx_ref[:, p * LANES:(p + 1) * LANES] = dxps[p]
            ds_ref[p] = dsps[p]
        db_ref[...] = dbm
        dc_ref[...] = dcm
        ddt_ref[...] = ddt

        @pl.when(c == 0)
        def _():
            dal_ref[...] = dal

        @pl.when(c != 0)
        def _():
            dal_ref[...] += dal

    gb = G * SSD_STATE
    return pl.pallas_call(
        body,
        out_shape=(jax.ShapeDtypeStruct((T, wi), F32), jax.ShapeDtypeStruct((T, gb), F32), jax.ShapeDtypeStruct((T, gb), F32),
                   jax.ShapeDtypeStruct((G, T, LANES), F32), jax.ShapeDtypeStruct((G, 1, LANES), F32)),
        grid=(G, NC),
        in_specs=[pl.BlockSpec((CHUNK, gw), lambda g, c: (rev(c), g)),
                  pl.BlockSpec((CHUNK, LANES), lambda g, c: (rev(c), bo + g)),
                  pl.BlockSpec((CHUNK, LANES), lambda g, c: (rev(c), co + g)),
                  pl.BlockSpec((CHUNK, LANES), lambda g, c: (rev(c), 0)),
                  pl.BlockSpec((1, LANES), lambda g, c: (0, 0)),
                  pl.BlockSpec((None, None, NP, LANES, SSD_STATE), lambda g, c: (g, rev(c), 0, 0, 0)),
                  pl.BlockSpec((CHUNK, gw), lambda g, c: (rev(c), g))],
        out_specs=(pl.BlockSpec((CHUNK, gw), lambda g, c: (rev(c), g)),
                   pl.BlockSpec((CHUNK, LANES), lambda g, c: (rev(c), g)),
                   pl.BlockSpec((CHUNK, LANES), lambda g, c: (rev(c), g)),
                   pl.BlockSpec((None, CHUNK, LANES), lambda g, c: (g, rev(c), 0)),
                   pl.BlockSpec((None, 1, LANES), lambda g, c: (g, 0, 0))),
        scratch_shapes=[pltpu.VMEM((NP, LANES, SSD_STATE), F32)],
        compiler_params=_params(("parallel", "arbitrary")),
        name=name,
    )(xact, xact, xact, dt, alog, states, dy)


def _post_f(y, x, z, dcol, ng):
    return (_rms((y + dcol * x) * _silu(z), ng),)


def _post_ops(yssd, xact, zx, dcol, ng, G, tm, grad):
    T, wi = yssd.shape
    gw = wi // G
    blk = lambda a: Op(a, (tm, gw), lambda g, i: (i, g), "tile", grad, gshape=(T, wi))
    vec = lambda v: Op(v, (1, gw), lambda g, i: (0, g), "param", grad)
    return [blk(yssd), blk(xact), blk(zx), vec(dcol), vec(ng)], (G, T // tm), gw


def ssd_fwd(h, g_row, w_zx, w_dt, conv_w, conv_b, dtb, alog, dcol, ng, w_o, tag, cq=None):
    T = h.shape[0]
    wi = ng.shape[1]
    tm = _pick(T, 256, 128)
    hn = rmsnorm_fwd(h, g_row, f"{tag}_norm")
    zx = matmul(hn, w_zx, "nn", F32, f"{tag}_inzx", cq=cq)
    dtr = matmul(hn, w_dt, "nn", F32, f"{tag}_indt")
    ins, grid, blk = _conv_ops(zx, conv_w, conv_b, wi, False)
    cd = conv_w.shape[1]
    xact = tmap(_conv_f, grid, ins, [((T, cd), F32, blk, lambda o, j: (0, j))], f"{tag}_conv")[0]
    dt = tmap(_dt_f, (1, T // tm), _row_ops([dtr], tm) + [_vec_op(dtb)],
              [((T, LANES), F32, (tm, LANES), lambda o, i: (i, 0))], f"{tag}_dt")[0]
    yssd, states = ssd_scan_fwd(xact, dt, alog, wi, f"{tag}_scan")
    G = states.shape[0]
    ins, grid, gw = _post_ops(yssd, xact, zx, dcol, ng, G, tm, False)
    yn = tmap(_post_f, grid, ins, [((T, wi), _ACT, (tm, gw), lambda g, i: (i, g))], f"{tag}_post")[0]
    out = matmul(yn, w_o, "nn", F32, f"{tag}_wo", cq=cq, epilogue=lambda acc, hh: (acc + hh,), extras=(h,))
    return out, (h, hn, zx, dtr, xact, dt, yssd, states, yn)


def ssd_bwd(dout, saved, g_row, w_zx, w_dt, conv_w, conv_b, dtb, alog, dcol, ng, w_o, tag, cq=None):
    h, hn, zx, dtr, xact, dt, yssd, states, yn = saved
    T = h.shape[0]
    wi = ng.shape[1]
    tm = _pick(T, 256, 128)
    G = states.shape[0]
    dyn = matmul(dout, w_o, "nt", F32, f"{tag}_dyn")
    dw_o = matmul(yn, dout, "tn", GRAD_DT, f"{tag}_dwo")
    ins, grid, gw = _post_ops(yssd, xact, zx, dcol, ng, G, tm, True)
    dyssd, dxi_skip, dz, ddcol, dng = tmap_vjp(_post_f, grid, ins, [Op(dyn, (tm, gw), lambda g, i: (i, g))],
                                                f"{tag}_dpost", grad_dtypes={2: _ACT})
    dxi, dbm, dcm, ddt_g, dalog_g = ssd_scan_bwd(xact, dt, alog, states, dyssd, wi, f"{tag}_dscan")
    dxact = jnp.concatenate([dxi + dxi_skip, dbm, dcm], axis=1)
    ddt = jnp.sum(ddt_g, axis=0)
    dalog = jnp.sum(dalog_g, axis=0)
    ins, grid, blk = _conv_ops(zx, conv_w, conv_b, wi, True)
    dxbc, dconv_w, dconv_b = tmap_vjp(_conv_f, grid, ins, [Op(dxact, blk, lambda o, j: (0, j))], f"{tag}_dconv",
                                      grad_dtypes={0: _ACT})
    ddtr, ddtb = tmap_vjp(_dt_f, (1, T // tm), _row_ops([dtr], tm) + [_vec_op(dtb)], _row_ops([ddt], tm), f"{tag}_ddt",
                          grad_dtypes={0: _ACT})
    dzx = jnp.concatenate([dz, dxbc], axis=1)
    dw_zx = matmul(hn, dzx, "tn", GRAD_DT, f"{tag}_dwzx")
    dw_dt = matmul(hn, ddtr, "tn", GRAD_DT, f"{tag}_dwdt")
    dhn1 = matmul(ddtr, w_dt, "nt", F32, f"{tag}_dhn1")
    dhn = matmul(dzx, w_zx, "nt", F32, f"{tag}_dhn", epilogue=lambda acc, e: (acc + e,), extras=(dhn1,))
    dh, dg = rmsnorm_bwd(h, g_row, dhn, dout, f"{tag}_dnorm")
    return dh, dg, dw_zx, dw_dt, dconv_w, dconv_b, ddtb, dalog, ddcol, dng, dw_o


def loss_head(y, target, name):
    T, D = y.shape
    tm = _pick(T, 512, 256, 128)

    def body(y_ref, t_ref, dy_ref, part_ref):
        d = y_ref[...] - t_ref[...]
        dy_ref[...] = d * (1.0 / D)
        s = jnp.sum(d * d, axis=0, keepdims=True) * (0.5 / D)

        @pl.when(pl.program_id(0) == 0)
        def _():
            part_ref[...] = s

        @pl.when(pl.program_id(0) != 0)
        def _():
            part_ref[...] += s

    rows = pl.BlockSpec((tm, D), lambda i: (i, 0))
    return pl.pallas_call(
        body,
        out_shape=(jax.ShapeDtypeStruct((T, D), F32), jax.ShapeDtypeStruct((1, D), F32)),
        grid=(T // tm,),
        in_specs=[rows, rows],
        out_specs=(rows, pl.BlockSpec((1, D), lambda i: (0, 0))),
        compiler_params=_params(("arbitrary",)),
        name=name,
    )(y, target)


def _row_tile(R, C, itemsize=4, target=1 << 20):
    for t in (1024, 512, 256, 128, 64, 32, 16, 8):
        if R % t == 0 and t * C * itemsize <= target:
            return t
    return R


def adamw(w, g, m, v, name):
    R, C = w.shape
    tr = _row_tile(R, C)
    c1 = 1.0 - ADAM_B1 ** ADAM_STEP
    c2 = 1.0 - ADAM_B2 ** ADAM_STEP

    def body(w_ref, g_ref, m_ref, v_ref, d_ref, nm_ref, nv_ref):
        gg = g_ref[...]
        nm = ADAM_B1 * m_ref[...] + (1.0 - ADAM_B1) * gg
        nv = ADAM_B2 * v_ref[...] + (1.0 - ADAM_B2) * jnp.square(gg)
        d_ref[...] = -ADAM_LR * ((nm / c1) / (jnp.sqrt(nv / c2) + ADAM_EPS) + ADAM_WD * w_ref[...])
        nm_ref[...] = nm
        nv_ref[...] = nv

    spec = pl.BlockSpec((tr, C), lambda i: (i, 0))
    sds = jax.ShapeDtypeStruct((R, C), F32)
    return pl.pallas_call(body, out_shape=(sds, sds, sds), grid=(R // tr,), in_specs=[spec] * 4, out_specs=(spec,) * 3,
                          compiler_params=_params(("parallel",)), name=name)(w, g, m, v)


def pair_sum(gfull, recv, name):
    _, _, R, C = gfull.shape
    tr = _row_tile(R, C, 2)

    def body(g_ref, p_ref, o_ref):
        c = lax.axis_index("c")
        o_ref[...] = (g_ref[c].astype(F32) + p_ref[...].astype(F32)).astype(o_ref.dtype)

    return pl.pallas_call(
        body,
        out_shape=jax.ShapeDtypeStruct((4, R, C), gfull.dtype),
        grid=(4, R // tr),
        in_specs=[pl.BlockSpec((None, 2, tr, C), lambda p, i: (p, 0, i, 0)), pl.BlockSpec((None, tr, C), lambda p, i: (p, i, 0))],
        out_specs=pl.BlockSpec((None, tr, C), lambda p, i: (p, i, 0)),
        compiler_params=_params(("parallel", "parallel")),
        name=name,
    )(gfull, recv)


def chip_sum(s, recv, name):
    _, R, C = s.shape
    tr = _row_tile(R, C, 2, 1 << 19)

    def body(c_ref, s_ref, p_ref, o_ref):
        x, y, _ = _xyc()
        acc = s_ref[2 * x + y].astype(F32)
        for r in range(3):
            acc = acc + p_ref[r].astype(F32)
        o_ref[...] = acc

    grid_spec = pltpu.PrefetchScalarGridSpec(
        num_scalar_prefetch=1,
        grid=(R // tr,),
        in_specs=[pl.BlockSpec((4, tr, C), lambda i, c: (0, i, 0)), pl.BlockSpec((3, tr, C), lambda i, c: (0, i, 0))],
        out_specs=pl.BlockSpec((None, tr, C), lambda i, c: (c[0], i, 0)),
    )
    return pl.pallas_call(
        body,
        out_shape=jax.ShapeDtypeStruct((2, R, C), F32),
        grid_spec=grid_spec,
        compiler_params=_params(("arbitrary",)),
        name=name,
    )(lax.axis_index("c").reshape(1).astype(jnp.int32), s, recv)


def sum8(g, name):
    _, R, C = g.shape

    def body(g_ref, o_ref):
        acc = g_ref[0]
        for d in range(1, 8):
            acc = acc + g_ref[d]
        o_ref[...] = acc

    return pl.pallas_call(body, out_shape=jax.ShapeDtypeStruct((R, C), F32), name=name,
                          compiler_params=pltpu.CompilerParams(vmem_limit_bytes=_VMEM_LIMIT))(g)


def gather_small(x, name):
    R, C = x.shape

    def body(x_ref, out_ref, send_sems, recv_sems, local_sem):
        x_, y_, c_ = _xyc()
        me, sibling = (x_, y_, c_), (x_, y_, 1 - c_)
        chips = [(_flip(x_, fx), _flip(y_, fy)) for fx, fy in _REL]
        slot = lambda px, py, pc: out_ref.at[4 * px + 2 * py + pc]

        def copy(k, block, to, src=None):
            return _remote(slot(*block) if src is None else src, slot(*block), send_sems.at[k], recv_sems.at[k], to)

        mine = pltpu.make_async_copy(x_ref, slot(*me), local_sem)
        mine.start()
        first = [copy(0, me, sibling, src=x_ref)]
        first += [copy(1 + j, me, (*chip, c_), src=x_ref) for j, chip in enumerate(chips)]
        for cp in first:
            cp.start()
        passed = [copy(4 + j, (*chip, c_), sibling) for j, chip in enumerate(chips)]
        for j, chip in enumerate(chips):
            copy(1 + j, (*chip, c_), me).wait_recv()
            passed[j].start()
        copy(0, sibling, me).wait_recv()
        for j, chip in enumerate(chips):
            copy(4 + j, (*chip, 1 - c_), me).wait_recv()
        for cp in first + passed:
            cp.wait_send()
        mine.wait()

    return pl.pallas_call(
        body,
        out_shape=jax.ShapeDtypeStruct((8, R, C), x.dtype),
        in_specs=[pl.BlockSpec(memory_space=pltpu.VMEM)],
        out_specs=pl.BlockSpec(memory_space=pltpu.VMEM),
        scratch_shapes=[pltpu.SemaphoreType.DMA((7,)), pltpu.SemaphoreType.DMA((7,)), pltpu.SemaphoreType.DMA],
        compiler_params=pltpu.CompilerParams(vmem_limit_bytes=_VMEM_LIMIT),
        name=name,
    )(x)


def gather_weights(halves, name):
    n = len(halves)

    def body(*refs):
        ins, outs = refs[:n], refs[n:2 * n]
        send_sems, recv_sems = refs[2 * n:]
        x_, y_, c_ = _xyc()
        sibling = (x_, y_, 1 - c_)
        chips = [(_flip(x_, fx), _flip(y_, fy)) for fx, fy in _REL]
        started = []
        for i in range(n):
            own, dst = ins[i].at[c_], outs[i].at[2 * x_ + y_, c_]
            for r, chip in enumerate(chips):
                started.append(_remote(own, dst, send_sems.at[6 * i + r], recv_sems.at[6 * i + r], (*chip, c_)))
                started[-1].start()
        for i in range(n):
            for r, (px, py) in enumerate(chips):
                blk = outs[i].at[2 * px + py, c_]
                _remote(blk, blk, send_sems.at[6 * i + r], recv_sems.at[6 * i + r], sibling).wait_recv()
                started.append(_remote(blk, blk, send_sems.at[6 * i + 3 + r], recv_sems.at[6 * i + 3 + r], sibling))
                started[-1].start()
        for i in range(n):
            for r, (px, py) in enumerate(chips):
                blk = outs[i].at[2 * px + py, 1 - c_]
                _remote(blk, blk, send_sems.at[6 * i + 3 + r], recv_sems.at[6 * i + 3 + r], sibling).wait_recv()
        for cp in started:
            cp.wait_send()

    return pl.pallas_call(
        body,
        out_shape=tuple(jax.ShapeDtypeStruct((4,) + h.shape, h.dtype) for h in halves),
        in_specs=[_ANY] * n,
        out_specs=tuple(_ANY for _ in halves),
        scratch_shapes=[pltpu.SemaphoreType.DMA((6 * n,)), pltpu.SemaphoreType.DMA((6 * n,))],
        name=name,
    )(*halves)


def share_weights(gathered, name):
    n = len(gathered)

    def body(*refs):
        outs = refs[n:2 * n]
        send_sems, recv_sems = refs[2 * n:]
        x_, y_, c_ = _xyc()
        sibling = (x_, y_, 1 - c_)
        chips = [(_flip(x_, fx), _flip(y_, fy)) for fx, fy in _REL]
        sent = []
        for i in range(n):
            for q, (px, py) in enumerate(chips):
                blk = outs[i].at[2 * px + py, c_]
                sent.append(_remote(blk, blk, send_sems.at[3 * i + q], recv_sems.at[3 * i + q], sibling))
                sent[-1].start()
        for i in range(n):
            for q, (px, py) in enumerate(chips):
                blk = outs[i].at[2 * px + py, 1 - c_]
                _remote(blk, blk, send_sems.at[3 * i + q], recv_sems.at[3 * i + q], sibling).wait_recv()
        for cp in sent:
            cp.wait_send()

    return pl.pallas_call(
        body,
        out_shape=tuple(jax.ShapeDtypeStruct(g.shape, g.dtype) for g in gathered),
        in_specs=[_ANY] * n,
        out_specs=tuple(_ANY for _ in gathered),
        input_output_aliases={i: i for i in range(n)},
        scratch_shapes=[pltpu.SemaphoreType.DMA((3 * n,)), pltpu.SemaphoreType.DMA((3 * n,))],
        name=name,
    )(*gathered)


def swap_halves(gfulls, name):
    n = len(gfulls)

    def body(*refs):
        ins, outs = refs[:n], refs[n:2 * n]
        send_sems, recv_sems = refs[2 * n:]
        x_, y_, c_ = _xyc()
        sibling = (x_, y_, 1 - c_)
        started = []
        for i in range(n):
            for p in range(4):
                started.append(_remote(ins[i].at[p, 1 - c_], outs[i].at[p], send_sems.at[4 * i + p], recv_sems.at[4 * i + p], sibling))
                started[-1].start()
        for cp in started:
            cp.wait()

    return pl.pallas_call(
        body,
        out_shape=tuple(jax.ShapeDtypeStruct((4,) + g.shape[2:], g.dtype) for g in gfulls),
        in_specs=[_ANY] * n,
        out_specs=tuple(_ANY for _ in gfulls),
        scratch_shapes=[pltpu.SemaphoreType.DMA((4 * n,)), pltpu.SemaphoreType.DMA((4 * n,))],
        name=name,
    )(*gfulls)


def scatter_chips(sums, name):
    n = len(sums)

    def body(*refs):
        ins, outs = refs[:n], refs[n:2 * n]
        send_sems, recv_sems = refs[2 * n:]
        x_, y_, c_ = _xyc()
        chips = [(_flip(x_, fx), _flip(y_, fy)) for fx, fy in _REL]
        started = []
        for i in range(n):
            for r, (px, py) in enumerate(chips):
                started.append(_remote(ins[i].at[2 * px + py], outs[i].at[r], send_sems.at[3 * i + r], recv_sems.at[3 * i + r], (px, py, c_)))
                started[-1].start()
        for cp in started:
            cp.wait()

    return pl.pallas_call(
        body,
        out_shape=tuple(jax.ShapeDtypeStruct((3,) + s.shape[1:], s.dtype) for s in sums),
        in_specs=[_ANY] * n,
        out_specs=tuple(_ANY for _ in sums),
        scratch_shapes=[pltpu.SemaphoreType.DMA((3 * n,)), pltpu.SemaphoreType.DMA((3 * n,))],
        name=name,
    )(*sums)


_JOIN_CHUNK_BYTES = 4 << 20


def join_halves(bufs, name):
    n = len(bufs)
    chunks = []
    for b in bufs:
        _, R, C = b.shape
        k = 1
        while k < 8 and R % (2 * k * 8) == 0 and R * C * 4 // k > _JOIN_CHUNK_BYTES:
            k *= 2
        chunks.append(k)
    base = [sum(chunks[:i]) for i in range(n)]
    total = sum(chunks)

    def body(*refs):
        outs = refs[n:2 * n]
        send_sems, recv_sems = refs[2 * n:]
        x_, y_, c_ = _xyc()
        sibling = (x_, y_, 1 - c_)
        sent = []
        for i in range(n):
            rc = bufs[i].shape[1] // chunks[i]
            for q in range(chunks[i]):
                blk = outs[i].at[c_, pl.ds(q * rc, rc)]
                sent.append(_remote(blk, blk, send_sems.at[base[i] + q], recv_sems.at[base[i] + q], sibling))
                sent[-1].start()
        for i in range(n):
            rc = bufs[i].shape[1] // chunks[i]
            for q in range(chunks[i]):
                blk = outs[i].at[1 - c_, pl.ds(q * rc, rc)]
                _remote(blk, blk, send_sems.at[base[i] + q], recv_sems.at[base[i] + q], sibling).wait_recv()
        for cp in sent:
            cp.wait_send()

    return pl.pallas_call(
        body,
        out_shape=tuple(jax.ShapeDtypeStruct(b.shape, b.dtype) for b in bufs),
        in_specs=[_ANY] * n,
        out_specs=tuple(_ANY for _ in bufs),
        input_output_aliases={i: i for i in range(n)},
        scratch_shapes=[pltpu.SemaphoreType.DMA((total,)), pltpu.SemaphoreType.DMA((total,))],
        name=name,
    )(*bufs)


def _halves(a):
    return a.reshape(2, -1, a.shape[-1])


_BIG = ("sb_w_qkv", "sb_w_o", "gm_w_in", "gm_w_o", "ssd_w_in", "ssd_w_o", "mlp_w_in", "mlp_w_out")
_COLS = ("sb_w_qkv", "gm_w_in", "ssd_w_in", "mlp_w_in")
_SMALL = ("norm_mix_g", "norm_mlp_g", "sb_q_norm_g", "sb_k_norm_g", "gm_v_norm_g", "gm_w_s", "gm_b_s",
          "ssd_conv_w", "ssd_conv_b", "ssd_dt_bias", "ssd_a_log", "ssd_d", "ssd_norm_g")
_SMALL_SHARDED = ("ssd_conv_w", "ssd_conv_b", "ssd_norm_g")
_WEIGHTS = ("norm_mix_g", "norm_mlp_g", "sb_w_qkv", "sb_q_norm_g", "sb_k_norm_g", "sb_w_o", "gm_w_in", "gm_v_norm_g",
            "gm_w_s", "gm_b_s", "gm_w_o", "ssd_w_in", "ssd_conv_w", "ssd_conv_b", "ssd_dt_bias", "ssd_a_log", "ssd_d",
            "ssd_norm_g", "ssd_w_o", "mlp_w_in", "mlp_w_out")


def _pack(arrs):
    flat = jnp.concatenate([a.reshape(-1).astype(F32) for a in arrs])
    n = flat.shape[0]
    tot = -(-n // (8 * LANES)) * 8 * LANES
    return jnp.pad(flat, (0, tot - n)).reshape(-1, LANES)


def _unpack(buf, shapes):
    flat = buf.reshape(-1)
    out, o = [], 0
    for s in shapes:
        n = math.prod(s)
        out.append(flat[o:o + n].reshape(s))
        o += n
    return out


_CARRIER_RANK = {0: (3, 0, 2, 1), 1: (3, 2, 0, 1), 2: (3, 0, 2, 1)}


def _layer_arrays(i):
    kind, j = i % 3, i // 3
    mix = (("sb_w_qkv", "sb_w_o"), ("gm_w_in", "gm_w_o"), ("ssd_w_in", "ssd_w_o"))[kind]
    return [(mix[0], j), (mix[1], j), ("mlp_w_in", i), ("mlp_w_out", i)]


def _step(x, w, target, m, v):
    depth = w["norm_mix_g"].shape[0]
    xc, yc, _ = _xyc()
    chip = 2 * xc + yc
    Hs = w["ssd_dt_bias"].shape[1]
    wi = 4 * w["ssd_norm_g"].shape[1]
    cd = 4 * w["ssd_conv_b"].shape[1]

    halves = {(k, l): _halves(w[k][l].astype(_MXU)) for i in range(depth) for (k, l) in _layer_arrays(i)}
    size = lambda key: math.prod(halves[key].shape)

    def finish(key, g):
        k, l = key
        g = g.reshape((4,) + w[k].shape[1:])
        g = lax.dynamic_update_slice(g, w[k][l].astype(_MXU)[None], (chip, 0, 0))
        if k == "ssd_w_in":
            full = g.transpose(1, 0, 2).reshape(g.shape[1], -1)
            return full[:, :wi + cd], jnp.pad(full[:, wi + cd:], ((0, 0), (0, LANES - Hs)))
        return g if k in _COLS else g.reshape(-1, g.shape[-1])

    small_sh = gather_small(_pack([w[k] for k in _SMALL_SHARDED]), "gather_small_weights")
    parts = [_unpack(small_sh[2 * j], [w[k].shape for k in _SMALL_SHARDED]) for j in range(4)]
    conv_w = jnp.concatenate([p[0][0] for p in parts], axis=1)
    conv_b = jnp.concatenate([p[1] for p in parts], axis=1)
    ssd_ng = jnp.concatenate([p[2] for p in parts], axis=1)
    padh = lambda a: jnp.pad(a, ((0, 0), (0, LANES - Hs)))
    dtb, alog = padh(w["ssd_dt_bias"]), padh(w["ssd_a_log"])
    dcol = jnp.repeat(w["ssd_d"], SSD_HEAD_DIM, axis=1)
    bcol = w["gm_b_s"][0][:, :, None]

    keys0 = _layer_arrays(0)
    ready = dict(zip(keys0, gather_weights([halves[k] for k in keys0], "gather_weights_0")))

    h = x[0]
    tape = []
    for i in range(depth):
        kind, j = i % 3, i // 3
        keys = _layer_arrays(i)
        wl = [finish(k, ready[k]) for k in keys]
        cq = Carriers()
        if i + 1 < depth:
            nxt = sorted(_layer_arrays(i + 1), key=size, reverse=True)
            order = [None] * 4
            for rank, key in zip(_CARRIER_RANK[kind], nxt):
                order[rank] = key
            for key in order:
                cq.add("gather", halves[key], key)
        gmix = w["norm_mix_g"][i:i + 1]
        if kind == 0:
            args = (gmix, wl[0], w["sb_q_norm_g"][j:j + 1], w["sb_k_norm_g"][j:j + 1], wl[1], f"sb{j}")
            h, sv = sb_fwd(h, *args, cq=cq)
        elif kind == 1:
            args = (gmix, wl[0], w["gm_v_norm_g"][j:j + 1], w["gm_w_s"][j], bcol, wl[1], f"gm{j}")
            h, sv = gm_fwd(h, *args, cq=cq)
        else:
            args = (gmix, wl[0][0], wl[0][1], conv_w, conv_b, dtb, alog, dcol, ssd_ng, wl[1], f"ssd{j}")
            h, sv = ssd_fwd(h, *args, cq=cq)
        margs = (w["norm_mlp_g"][i:i + 1], wl[2], wl[3], f"mlp{i}")
        h, msv = mlp_fwd(h, *margs, cq=cq)
        tape.append((kind, j, args, sv, margs, msv))
        if i + 1 < depth:
            assert not cq.pending
            nk_ = _layer_arrays(i + 1)
            shared = share_weights([cq.done[k] for k in nk_], f"share_weights_{i + 1}")
            ready.update(zip(nk_, shared))

    dh, loss_cols = loss_head(h, target[0], "loss_head")

    gsmall = {k: [None] * w[k].shape[0] for k in ("norm_mix_g", "norm_mlp_g", "sb_q_norm_g", "sb_k_norm_g")}
    pairs, from_chips = {}, {}
    cq = Carriers()
    for i in reversed(range(depth)):
        kind, j, args, sv, margs, msv = tape[i]
        keys = _layer_arrays(i)
        dh, gsmall["norm_mlp_g"][i], d_mlp_in, d_mlp_out = mlp_bwd(dh, msv, *margs, cq=cq)
        assert not cq.pending
        from_chips.update(cq.done)
        if kind == 0:
            dh, gsmall["norm_mix_g"][i], d_in, gsmall["sb_q_norm_g"][j], gsmall["sb_k_norm_g"][j], d_out = sb_bwd(dh, sv, *args)
        elif kind == 1:
            dh, gsmall["norm_mix_g"][i], d_in, d_vg, d_ws, d_bcol, d_out = gm_bwd(dh, sv, *args)
            gsmall["gm_v_norm_g"], gsmall["gm_w_s"], gsmall["gm_b_s"] = d_vg, d_ws[None], d_bcol[None, :, :, 0]
        else:
            dh, gsmall["norm_mix_g"][i], d_zx, d_dt, d_cw, d_cb, d_dtb, d_al, d_dcol, d_ng, d_out = ssd_bwd(dh, sv, *args)
            d_full = jnp.concatenate([d_zx, d_dt[:, :Hs]], axis=1)
            d_in = d_full.reshape(d_full.shape[0], 4, -1).transpose(1, 0, 2)
            gsmall["ssd_conv_w"], gsmall["ssd_conv_b"], gsmall["ssd_norm_g"] = d_cw[None], d_cb, d_ng
            gsmall["ssd_dt_bias"], gsmall["ssd_a_log"] = d_dtb[:, :Hs], d_al[:, :Hs]
            gsmall["ssd_d"] = jnp.sum(d_dcol.reshape(Hs, SSD_HEAD_DIM), axis=1)[None]
        gl = [g.reshape(4, 2, -1, g.shape[-1]).astype(GRAD_DT) for g in (d_in, d_out, d_mlp_in, d_mlp_out)]
        got = swap_halves(gl, f"grad_swap_halves_{i}")
        for key, g, r in zip(keys, gl, got):
            pairs[key] = pair_sum(g, r, f"pair_sum_{key[0]}_{key[1]}")
        cq = Carriers()
        if i > 0:
            for key in sorted(keys, key=size, reverse=True):
                cq.add("scatter", pairs[key], key)
        else:
            from_chips.update(zip(keys, scatter_chips([pairs[k] for k in keys], "grad_scatter_chips_0")))
    for k in gsmall:
        if isinstance(gsmall[k], list):
            gsmall[k] = jnp.concatenate(gsmall[k], axis=0)
    grad_x = dh[None]

    allkeys = [key for i in range(depth) for key in _layer_arrays(i)]
    mine = [chip_sum(pairs[key], from_chips[key], f"chip_sum_{key[0]}_{key[1]}") for key in allkeys]
    joined = dict(zip(allkeys, join_halves(mine, "grad_join_halves")))
    grads, deltas, new_m, new_v = {}, {}, {}, {}
    for k in _BIG:
        g = jnp.stack([joined[(k, l)].reshape(w[k].shape[1:]) for l in range(w[k].shape[0])])
        C = w[k].shape[-1]
        d_, m_, v_ = adamw(w[k].reshape(-1, C), g.reshape(-1, C), m[k].reshape(-1, C), v[k].reshape(-1, C), f"adamw_{k}")
        grads[k], deltas[k], new_m[k], new_v[k] = (a.reshape(w[k].shape) for a in (g, d_, m_, v_))

    full_shapes = [gsmall[k].shape for k in _SMALL] + [(1,)]
    loss_local = jnp.sum(loss_cols).reshape(1)
    red = sum8(gather_small(_pack([gsmall[k] for k in _SMALL] + [loss_local]), "gather_small_grads"), "sum_small_grads")
    red = _unpack(red, full_shapes)
    loss = red[-1][0]
    gsm = dict(zip(_SMALL, red[:-1]))
    for k in _SMALL_SHARDED:
        n = w[k].shape[-1]
        gsm[k] = lax.dynamic_slice_in_dim(gsm[k], chip * n, n, axis=gsm[k].ndim - 1)
    shapes = [w[k].shape for k in _SMALL]
    packed = [_pack([d[k] for k in _SMALL]) for d in (w, gsm, m, v)]
    outs = adamw(*packed, "adamw_small")
    for k, g_, d_, m_, v_ in zip(_SMALL, [gsm[k] for k in _SMALL], *[_unpack(o, shapes) for o in outs]):
        grads[k], deltas[k], new_m[k], new_v[k] = g_.reshape(w[k].shape), d_, m_, v_

    return (loss, grad_x, *[grads[k] for k in _WEIGHTS], *[deltas[k] for k in _WEIGHTS],
            *[new_m[k] for k in _WEIGHTS], *[new_v[k] for k in _WEIGHTS])


def kernel(x, norm_mix_g, norm_mlp_g, sb_w_qkv, sb_q_norm_g, sb_k_norm_g, sb_w_o, gm_w_in, gm_v_norm_g, gm_w_s, gm_b_s, gm_w_o, ssd_w_in, ssd_conv_w, ssd_conv_b, ssd_dt_bias, ssd_a_log, ssd_d, ssd_norm_g, ssd_w_o, mlp_w_in, mlp_w_out, loss_target, m_norm_mix_g, m_norm_mlp_g, m_sb_w_qkv, m_sb_q_norm_g, m_sb_k_norm_g, m_sb_w_o, m_gm_w_in, m_gm_v_norm_g, m_gm_w_s, m_gm_b_s, m_gm_w_o, m_ssd_w_in, m_ssd_conv_w, m_ssd_conv_b, m_ssd_dt_bias, m_ssd_a_log, m_ssd_d, m_ssd_norm_g, m_ssd_w_o, m_mlp_w_in, m_mlp_w_out, v_norm_mix_g, v_norm_mlp_g, v_sb_w_qkv, v_sb_q_norm_g, v_sb_k_norm_g, v_sb_w_o, v_gm_w_in, v_gm_v_norm_g, v_gm_w_s, v_gm_b_s, v_gm_w_o, v_ssd_w_in, v_ssd_conv_w, v_ssd_conv_b, v_ssd_dt_bias, v_ssd_a_log, v_ssd_d, v_ssd_norm_g, v_ssd_w_o, v_mlp_w_in, v_mlp_w_out):
    w = dict(zip(_WEIGHTS, (norm_mix_g, norm_mlp_g, sb_w_qkv, sb_q_norm_g, sb_k_norm_g, sb_w_o, gm_w_in, gm_v_norm_g, gm_w_s,
                            gm_b_s, gm_w_o, ssd_w_in, ssd_conv_w, ssd_conv_b, ssd_dt_bias, ssd_a_log, ssd_d, ssd_norm_g,
                            ssd_w_o, mlp_w_in, mlp_w_out)))
    m = dict(zip(_WEIGHTS, (m_norm_mix_g, m_norm_mlp_g, m_sb_w_qkv, m_sb_q_norm_g, m_sb_k_norm_g, m_sb_w_o, m_gm_w_in,
                            m_gm_v_norm_g, m_gm_w_s, m_gm_b_s, m_gm_w_o, m_ssd_w_in, m_ssd_conv_w, m_ssd_conv_b,
                            m_ssd_dt_bias, m_ssd_a_log, m_ssd_d, m_ssd_norm_g, m_ssd_w_o, m_mlp_w_in, m_mlp_w_out)))
    v = dict(zip(_WEIGHTS, (v_norm_mix_g, v_norm_mlp_g, v_sb_w_qkv, v_sb_q_norm_g, v_sb_k_norm_g, v_sb_w_o, v_gm_w_in,
                            v_gm_v_norm_g, v_gm_w_s, v_gm_b_s, v_gm_w_o, v_ssd_w_in, v_ssd_conv_w, v_ssd_conv_b,
                            v_ssd_dt_bias, v_ssd_a_log, v_ssd_d, v_ssd_norm_g, v_ssd_w_o, v_mlp_w_in, v_mlp_w_out)))
    return _step(x, w, loss_target, m, v)
```

```python
import functools
import math

import jax
import jax.numpy as jnp
from jax import lax
from jax.experimental import pallas as pl
from jax.experimental.pallas import tpu as pltpu

F32 = jnp.float32
BF16 = jnp.bfloat16
_MXU = jnp.bfloat16
_ACT = jnp.bfloat16
GRAD_DT = jnp.bfloat16
_VMEM_LIMIT = 56 * 1024 * 1024
_MATMUL_VMEM_BUDGET = 44 * 1024 * 1024
EPS = 1e-6
LANES = 128
CHUNK = 128
SSD_HEAD_DIM = 64
SSD_STATE = 128
SSD_CONV = 4
ADAM_LR, ADAM_B1, ADAM_B2, ADAM_EPS, ADAM_WD, ADAM_STEP = 1e-3, 0.9, 0.999, 1e-8, 0.01, 10
MESH = pl.DeviceIdType.MESH

NN = (((1,), (0,)), ((), ()))
NT = (((1,), (1,)), ((), ()))
TN = (((0,), (0,)), ((), ()))

_ANY = pl.BlockSpec(memory_space=pl.ANY)
_REL = ((1, 0), (0, 1), (1, 1))


def _dot(a, b, dims=NN):
    return lax.dot_general(a.astype(_MXU), b.astype(_MXU), dims, preferred_element_type=F32)


def _params(sem):
    return pltpu.CompilerParams(dimension_semantics=sem, vmem_limit_bytes=_VMEM_LIMIT)


def _pick(n, *cands):
    for c in cands:
        if n % c == 0:
            return c
    return n


def _xyc():
    return lax.axis_index("x"), lax.axis_index("y"), lax.axis_index("c")


def _flip(v, f):
    return 1 - v if f else v


def _remote(src, dst, ssem, rsem, dev):
    return pltpu.make_async_remote_copy(src_ref=src, dst_ref=dst, send_sem=ssem, recv_sem=rsem, device_id=dev,
                                        device_id_type=MESH)


class Carriers:
    def __init__(self):
        self.pending, self.done = [], {}

    def add(self, kind, src, tag):
        self.pending.append((kind, src, tag))


def matmul(a, b, mode, out_dtypes, name, epilogue=None, extras=(), b_cols4=False, out_cols4=False, cq=None):
    ash = a.shape
    bsh = (b.shape[1], 4 * b.shape[2]) if b_cols4 else b.shape
    if mode == "nn":
        (M, K), (K2, N) = ash, bsh
    elif mode == "nt":
        (M, K), (N, K2) = ash, bsh
    else:
        (K, M), (K2, N) = ash, bsh
    assert K == K2, (mode, a.shape, b.shape)
    nsh = N // 4 if (out_cols4 or (b_cols4 and mode == "nn")) else None
    ksh = K // 4 if (b_cols4 and mode == "nt") else None
    single = not isinstance(out_dtypes, (tuple, list))
    odt = (out_dtypes,) if single else tuple(out_dtypes)
    n_ex, n_out = len(extras), len(odt)

    def fits(sh, top):
        whole = [sh] if sh is not None and sh <= top and sh % LANES == 0 else []
        return whole + [c for c in (2048, 1024, 512, 256, 128) if c <= top and (sh is None or sh % c == 0)] + ([] if sh is None else [sh])

    tn = _pick(N, *fits(nsh, 1536))
    tk = _pick(K, *fits(ksh, 2048))

    def vmem_bytes(tm_):
        per_step = tm_ * tk * a.dtype.itemsize + tk * tn * b.dtype.itemsize + n_ex * tm_ * tn * 4
        per_step += sum(tm_ * tn * jnp.dtype(d).itemsize for d in odt)
        return 2 * per_step + tm_ * tn * 4

    tm = next((c for c in (1024, 512, 256, 128) if M % c == 0 and vmem_bytes(c) <= _MATMUL_VMEM_BUDGET), M)
    ni, nj, nk = M // tm, N // tn, K // tk
    dims = {"nn": NN, "nt": NT, "tn": TN}[mode]
    carry = cq.pending.pop(0) if (cq is not None and cq.pending) else None

    if mode == "tn":
        a_spec = pl.BlockSpec((tk, tm), lambda i, j, k: (k, i))
    else:
        a_spec = pl.BlockSpec((tm, tk), lambda i, j, k: (i, k))
    if b_cols4 and mode == "nn":
        b_spec = pl.BlockSpec((None, tk, tn), lambda i, j, k: (lax.div(j * tn, nsh), k, lax.div(lax.rem(j * tn, nsh), tn)))
    elif b_cols4:
        b_spec = pl.BlockSpec((None, tn, tk), lambda i, j, k: (lax.div(k * tk, ksh), j, lax.div(lax.rem(k * tk, ksh), tk)))
    elif mode == "nt":
        b_spec = pl.BlockSpec((tn, tk), lambda i, j, k: (j, k))
    else:
        b_spec = pl.BlockSpec((tk, tn), lambda i, j, k: (k, j))
    mn_spec = pl.BlockSpec((tm, tn), lambda i, j, k: (i, j))
    if out_cols4:
        o_spec = pl.BlockSpec((None, tm, tn), lambda i, j, k: (lax.div(j * tn, nsh), i, lax.div(lax.rem(j * tn, nsh), tn)))
        o_shape = (4, M, N // 4)
    else:
        o_spec, o_shape = mn_spec, (M, N)

    def body(*refs):
        a_ref, b_ref = refs[0], refs[1]
        ex = refs[2:2 + n_ex]
        pos = 2 + n_ex
        src_ref = refs[pos] if carry else None
        pos += 1 if carry else 0
        outs = refs[pos:pos + n_out]
        pos += n_out
        dst_ref = refs[pos] if carry else None
        pos += 1 if carry else 0
        acc = refs[pos]
        i, j, k = pl.program_id(0), pl.program_id(1), pl.program_id(2)

        if carry:
            ssem, rsem = refs[pos + 1], refs[pos + 2]
            x_, y_, c_ = _xyc()
            chips = [(_flip(x_, fx), _flip(y_, fy)) for fx, fy in _REL]

            def copies(arriving):
                out = []
                for r, (px, py) in enumerate(chips):
                    if carry[0] == "gather":
                        s_, d_ = src_ref.at[c_], dst_ref.at[(2 * px + py) if arriving else (2 * x_ + y_), c_]
                    else:
                        s_, d_ = src_ref.at[2 * px + py], dst_ref.at[r]
                    out.append(_remote(s_, d_, ssem.at[r], rsem.at[r], (px, py, c_)))
                return out

            @pl.when((i == 0) & (j == 0) & (k == 0))
            def _():
                for send in copies(False):
                    send.start()

        @pl.when(k == 0)
        def _():
            acc[...] = jnp.zeros_like(acc)

        part = _dot(a_ref[...], b_ref[...], dims)

        @pl.when(k < nk - 1)
        def _():
            acc[...] += part

        @pl.when(k == nk - 1)
        def _():
            r = acc[...] + part
            res = (r,) if epilogue is None else epilogue(r, *[e[...] for e in ex])
            for o, v in zip(outs, res):
                o[...] = v.astype(o.dtype)

        if carry:
            @pl.when((i == ni - 1) & (j == nj - 1) & (k == nk - 1))
            def _():
                for arrive in copies(True):
                    arrive.wait_recv()
                for send in copies(False):
                    send.wait_send()

    in_specs = [a_spec, b_spec] + [mn_spec] * n_ex
    out_shape = [jax.ShapeDtypeStruct(o_shape, d) for d in odt]
    out_specs = [o_spec for _ in odt]
    scratch = [pltpu.VMEM((tm, tn), F32)]
    operands = [a, b, *extras]
    if carry:
        kind, src, tag = carry
        in_specs.append(_ANY)
        operands.append(src)
        out_shape.append(jax.ShapeDtypeStruct(((4,) + src.shape) if kind == "gather" else ((3,) + src.shape[1:]), src.dtype))
        out_specs.append(_ANY)
        scratch += [pltpu.SemaphoreType.DMA((3,)), pltpu.SemaphoreType.DMA((3,))]
    out = pl.pallas_call(
        body,
        out_shape=tuple(out_shape),
        grid=(ni, nj, nk),
        in_specs=in_specs,
        out_specs=tuple(out_specs),
        scratch_shapes=scratch,
        compiler_params=_params(("arbitrary", "arbitrary", "arbitrary") if carry else ("parallel", "parallel", "arbitrary")),
        name=name,
    )(*operands)
    if carry:
        cq.done[carry[2]] = out[-1]
        out = out[:-1]
    return out[0] if single else out


class Op:
    def __init__(self, arr, block, imap, kind="tile", grad=True, gshape=None, gimap=None):
        self.arr, self.block, self.imap, self.kind, self.grad = arr, block, imap, kind, grad
        self.gshape = gshape or arr.shape
        self.gimap = gimap or imap

    def spec(self):
        return pl.BlockSpec(self.block, self.imap)


def tmap(f, grid, ins, outs, name):
    n_in = len(ins)

    def body(*refs):
        res = f(*[r[...] for r in refs[:n_in]])
        for o, v in zip(refs[n_in:], res):
            o[...] = v.astype(o.dtype)

    return pl.pallas_call(
        body,
        out_shape=tuple(jax.ShapeDtypeStruct(s, d) for s, d, _, _ in outs),
        grid=grid,
        in_specs=[o.spec() for o in ins],
        out_specs=tuple(pl.BlockSpec(b, m) for _, _, b, m in outs),
        compiler_params=_params(("parallel", "parallel")),
        name=name,
    )(*[o.arr for o in ins])


def tmap_vjp(f, grid, ins, cts, name, grad_dtypes=None):
    n_in, n_ct = len(ins), len(cts)
    gidx = [i for i, o in enumerate(ins) if o.grad]
    gdt = grad_dtypes or {}

    def body(*refs):
        in_refs, ct_refs, g_refs = refs[:n_in], refs[n_in:n_in + n_ct], refs[n_in + n_ct:]
        vals = [r[...] for r in in_refs]

        def g_only(*diff):
            full = list(vals)
            for i, v in zip(gidx, diff):
                full[i] = v
            return f(*full)

        res, vjp = jax.vjp(g_only, *[vals[i].astype(F32) for i in gidx])
        grads = vjp(tuple(c[...].astype(r.dtype) for c, r in zip(ct_refs, res)))
        inner = pl.program_id(1)
        for i, g, gr in zip(gidx, grads, g_refs):
            if ins[i].kind == "tile":
                gr[...] = g.astype(gr.dtype)
            else:
                @pl.when(inner == 0)
                def _(gr=gr, g=g):
                    gr[...] = g.astype(gr.dtype)

                @pl.when(inner != 0)
                def _(gr=gr, g=g):
                    gr[...] += g.astype(gr.dtype)

    out_shape = tuple(jax.ShapeDtypeStruct(ins[i].gshape, gdt.get(i, F32)) for i in gidx)
    return pl.pallas_call(
        body,
        out_shape=out_shape,
        grid=grid,
        in_specs=[o.spec() for o in ins] + [o.spec() for o in cts],
        out_specs=tuple(pl.BlockSpec(ins[i].block, ins[i].gimap) for i in gidx),
        compiler_params=_params(("parallel", "arbitrary")),
        name=name,
    )(*[o.arr for o in ins], *[o.arr for o in cts])


def _rms(x, g):
    return x * lax.rsqrt(jnp.mean(x * x, axis=-1, keepdims=True) + EPS) * g


def _row_ops(arrs, tm, grads=None):
    grads = grads or [True] * len(arrs)
    return [Op(a, (tm, a.shape[1]), lambda o, i: (i, 0), "tile", g) for a, g in zip(arrs, grads)]


def _vec_op(v, grad=True):
    return Op(v, (1, v.shape[1]), lambda o, i: (0, 0), "param", grad)


def rmsnorm_fwd(h, g, name):
    T, D = h.shape
    tm = _pick(T, 512, 256, 128)
    f = lambda x, gg: (_rms(x, gg),)
    return tmap(f, (1, T // tm), _row_ops([h], tm) + [_vec_op(g)],
                [((T, D), _ACT, (tm, D), lambda o, i: (i, 0))], name)[0]


def rmsnorm_bwd(h, g, dhn, dres, name):
    T, D = h.shape
    tm = _pick(T, 512, 256, 128)
    f = lambda x, gg: (_rms(x, gg), x)
    return tmap_vjp(f, (1, T // tm), _row_ops([h], tm) + [_vec_op(g)], _row_ops([dhn, dres], tm), name)


def mlp_fwd(h, g_row, w_in, w_out, tag, cq=None):
    hn = rmsnorm_fwd(h, g_row, f"{tag}_norm")
    a, r2 = matmul(hn, w_in, "nn", (F32, _ACT), f"{tag}_in", b_cols4=True, cq=cq,
                   epilogue=lambda acc: (acc, jnp.square(jnp.maximum(acc, 0.0))))
    out = matmul(r2, w_out, "nn", F32, f"{tag}_out", cq=cq, epilogue=lambda acc, hh: (acc + hh,), extras=(h,))
    return out, (h, hn, a, r2)


def mlp_bwd(dout, saved, g_row, w_in, w_out, tag, cq=None):
    h, hn, a, r2 = saved
    da = matmul(dout, w_out, "nt", _ACT, f"{tag}_dact", cq=cq,
                epilogue=lambda acc, aa: (acc * (2.0 * jnp.maximum(aa, 0.0)),), extras=(a,))
    dw_out = matmul(r2, dout, "tn", GRAD_DT, f"{tag}_dwout", cq=cq)
    dw_in = matmul(hn, da, "tn", GRAD_DT, f"{tag}_dwin", out_cols4=True, cq=cq)
    dhn = matmul(da, w_in, "nt", F32, f"{tag}_dhn", b_cols4=True, cq=cq)
    dh, dg = rmsnorm_bwd(h, g_row, dhn, dout, f"{tag}_dnorm")
    return dh, dg, dw_in, dw_out


def _split3(x):
    hi = x.astype(BF16)
    r = x - hi.astype(F32)
    mid = r.astype(BF16)
    lo = (r - mid.astype(F32)).astype(BF16)
    return hi, mid, lo


def _cumdot(x, tri):
    return sum(lax.dot_general(p, tri, NN, preferred_element_type=F32) for p in _split3(x))


def _iotas():
    row = lax.broadcasted_iota(jnp.int32, (CHUNK, CHUNK), 0)
    col = lax.broadcasted_iota(jnp.int32, (CHUNK, CHUNK), 1)
    return row, col


_TQ = 256
_TK = 256
_DEAD = -88.0


def _sb_block(q, kblk, q0, k0, scale, row, col):
    z = _dot(q, kblk, NT) * scale
    e = jnp.exp(-jnp.abs(z))
    den = 1.0 + e
    sp = jnp.maximum(z, 0.0) + jnp.log(den)
    mask = (col + k0) < (row + q0)
    lg = jnp.where(mask, -sp, 0.0)
    beta = jnp.where(z >= 0, 1.0, e) / den
    return z, mask, lg, beta


def _attn_iotas(tq, tk):
    row = lax.broadcasted_iota(jnp.int32, (tq, tk), 0)
    col = lax.broadcasted_iota(jnp.int32, (tq, tk), 1)
    r2 = lax.broadcasted_iota(jnp.int32, (tk, tk), 0)
    c2 = lax.broadcasted_iota(jnp.int32, (tk, tk), 1)
    return row, col, r2, c2


def attn_fwd(qn, kn, v, name):
    T, W = qn.shape
    tq, tk = _pick(T, _TQ, CHUNK), _pick(T, _TK, CHUNK)
    H, NQ, per = W // LANES, T // tq, tq // tk
    assert tq % tk == 0 and T // tk <= LANES
    scale = LANES ** -0.5

    def body(q_ref, k_ref, v_ref, o_ref, r_ref, acc_ref, run_ref):
        qi = pl.program_id(1)
        q = q_ref[...]
        row, col, r2, c2 = _attn_iotas(tq, tk)
        suffix = (r2 >= c2).astype(_MXU)
        lane_q = lax.broadcasted_iota(jnp.int32, (tq, LANES), 1)
        acc_ref[...] = jnp.zeros_like(acc_ref)
        run_ref[...] = jnp.zeros_like(run_ref)
        r_ref[...] = jnp.full(r_ref.shape, -1e30, F32)

        def step(carry):
            kb, _ = carry
            off = pl.multiple_of(kb * tk, tk)
            run = run_ref[...]
            z, mask, lg, _ = _sb_block(q, k_ref[pl.ds(off, tk), :], qi * tq, off, scale, row, col)
            r_ref[...] = jnp.where(lane_q == kb, run, r_ref[...])
            cl = _cumdot(lg, suffix) + run
            a = jnp.exp(jnp.where(mask, z + cl, -1e30))
            acc_ref[...] += _dot(a, v_ref[pl.ds(off, tk), :])
            run = run + jnp.sum(lg, axis=1, keepdims=True)
            run_ref[...] = run
            return kb - 1, jnp.max(run) > _DEAD

        lax.while_loop(lambda c: (c[0] >= 0) & c[1], step, ((qi + 1) * per - 1, True))
        o_ref[...] = acc_ref[...].astype(o_ref.dtype)

    qspec = pl.BlockSpec((tq, LANES), lambda h, i: (i, h))
    kvspec = pl.BlockSpec((T, LANES), lambda h, i: (0, h))
    return pl.pallas_call(
        body,
        out_shape=(jax.ShapeDtypeStruct((T, W), _ACT), jax.ShapeDtypeStruct((H, T, LANES), F32)),
        grid=(H, NQ),
        in_specs=[qspec, kvspec, kvspec],
        out_specs=(qspec, pl.BlockSpec((None, tq, LANES), lambda h, i: (h, i, 0))),
        scratch_shapes=[pltpu.VMEM((tq, LANES), F32), pltpu.VMEM((tq, 1), F32)],
        compiler_params=_params(("parallel", "parallel")),
        name=name,
    )(qn, kn, v)


def attn_bwd(qn, kn, v, do, rblk, name):
    T, W = qn.shape
    tq, tk = _pick(T, _TQ, CHUNK), _pick(T, _TK, CHUNK)
    H, NQ, per = W // LANES, T // tq, tq // tk
    scale = LANES ** -0.5

    def body(q_ref, k_ref, v_ref, do_ref, r_ref, dq_ref, dk_ref, dv_ref, g_ref):
        qi = pl.program_id(1)

        @pl.when(qi == 0)
        def _():
            dk_ref[...] = jnp.zeros_like(dk_ref)
            dv_ref[...] = jnp.zeros_like(dv_ref)

        q = q_ref[...]
        dout = do_ref[...]
        rt = r_ref[...]
        row, col, r2, c2 = _attn_iotas(tq, tk)
        suffix = (r2 >= c2).astype(_MXU)
        prefix = (r2 <= c2).astype(_MXU)
        kend = (qi + 1) * per - 1
        lane = lax.broadcasted_iota(jnp.int32, (1, LANES), 1)
        lane_q = lax.broadcasted_iota(jnp.int32, (tq, LANES), 1)
        unvisited = (jnp.max(rt, axis=0, keepdims=True) < -1e29) & (lane <= kend)
        start = jnp.sum(unvisited.astype(jnp.int32))

        dq_ref[...] = jnp.zeros_like(dq_ref)
        g_ref[...] = jnp.zeros_like(g_ref)

        @pl.loop(start, kend + 1)
        def _(kb):
            gsum = g_ref[...]
            off = pl.multiple_of(kb * tk, tk)
            kblk = k_ref[pl.ds(off, tk), :]
            vblk = v_ref[pl.ds(off, tk), :]
            z, mask, lg, beta = _sb_block(q, kblk, qi * tq, off, scale, row, col)
            run = jnp.sum(jnp.where(lane_q == kb, rt, 0.0), axis=1, keepdims=True)
            cl = _cumdot(lg, suffix) + run
            a = jnp.exp(jnp.where(mask, z + cl, -1e30))
            e = _dot(dout, vblk, NT) * a
            f = _cumdot(e, prefix) + gsum
            dz = jnp.where(mask, e - beta * f, 0.0) * scale
            dk_ref[pl.ds(off, tk), :] += _dot(dz, q, TN)
            dv_ref[pl.ds(off, tk), :] += _dot(a, dout, TN)
            dq_ref[...] += _dot(dz, kblk)
            g_ref[...] = gsum + jnp.sum(e, axis=1, keepdims=True)

    qspec = pl.BlockSpec((tq, LANES), lambda h, i: (i, h))
    kvspec = pl.BlockSpec((T, LANES), lambda h, i: (0, h))
    big = jax.ShapeDtypeStruct((T, W), F32)
    return pl.pallas_call(
        body,
        out_shape=(big, big, big),
        grid=(H, NQ),
        in_specs=[qspec, kvspec, kvspec, qspec, pl.BlockSpec((None, tq, LANES), lambda h, i: (h, i, 0))],
        out_specs=(qspec, kvspec, kvspec),
        scratch_shapes=[pltpu.VMEM((tq, 1), F32)],
        compiler_params=_params(("parallel", "arbitrary")),
        name=name,
    )(qn, kn, v, do, rblk)


def _qk_ops(qkv, qg, kg, tm, grad):
    T, W3 = qkv.shape
    H, NT_ = W3 // (3 * LANES), T // tm
    W = H * LANES

    def part(p):
        return Op(qkv, (tm, LANES), lambda o, n: (lax.rem(n, NT_), p * H + lax.div(n, NT_)), "tile", grad,
                  gshape=(T, W), gimap=lambda o, n: (lax.rem(n, NT_), lax.div(n, NT_)))

    vec = lambda g: Op(g, (1, LANES), lambda o, n: (0, 0), "param", grad)
    return [part(0), part(1), part(2), vec(qg), vec(kg)], (1, H * NT_), H, NT_, W


def _qk_f(q, k, v, qg, kg):
    return _rms(q, qg), _rms(k, kg), v


def qknorm_fwd(qkv, qg, kg, name):
    T = qkv.shape[0]
    tm = _pick(T, 512, 256, 128)
    ins, grid, H, NT_, W = _qk_ops(qkv, qg, kg, tm, False)
    out = ((T, W), _ACT, (tm, LANES), lambda o, n: (lax.rem(n, NT_), lax.div(n, NT_)))
    return tmap(_qk_f, grid, ins, [out, out, out], name)


def qknorm_bwd(qkv, qg, kg, dq, dk, dv, name):
    T = qkv.shape[0]
    tm = _pick(T, 512, 256, 128)
    ins, grid, H, NT_, W = _qk_ops(qkv, qg, kg, tm, True)
    cts = [Op(c, (tm, LANES), lambda o, n: (lax.rem(n, NT_), lax.div(n, NT_))) for c in (dq, dk, dv)]
    return tmap_vjp(_qk_f, grid, ins, cts, name, grad_dtypes={0: _ACT, 1: _ACT, 2: _ACT})


def sb_fwd(h, g_row, w_qkv, qg, kg, w_o, tag, cq=None):
    hn = rmsnorm_fwd(h, g_row, f"{tag}_norm")
    qkv = matmul(hn, w_qkv, "nn", F32, f"{tag}_qkv", b_cols4=True, cq=cq)
    qn, kn, v = qknorm_fwd(qkv, qg, kg, f"{tag}_qknorm")
    o, rblk = attn_fwd(qn, kn, v, f"{tag}_attn")
    out = matmul(o, w_o, "nn", F32, f"{tag}_wo", cq=cq, epilogue=lambda acc, hh: (acc + hh,), extras=(h,))
    return out, (h, hn, qkv, qn, kn, v, o, rblk)


def sb_bwd(dout, saved, g_row, w_qkv, qg, kg, w_o, tag, cq=None):
    h, hn, qkv, qn, kn, v, o, rblk = saved
    do = matmul(dout, w_o, "nt", _ACT, f"{tag}_do")
    dw_o = matmul(o, dout, "tn", GRAD_DT, f"{tag}_dwo")
    dqn, dkn, dv = attn_bwd(qn, kn, v, do, rblk, f"{tag}_dattn")
    dq, dk, dvv, dqg, dkg = qknorm_bwd(qkv, qg, kg, dqn, dkn, dv, f"{tag}_dqknorm")
    dqkv = jnp.concatenate([dq, dk, dvv], axis=1)
    dw_qkv = matmul(hn, dqkv, "tn", GRAD_DT, f"{tag}_dwqkv", out_cols4=True, cq=cq)
    dhn = matmul(dqkv, w_qkv, "nt", F32, f"{tag}_dhn", b_cols4=True, cq=cq)
    dh, dg = rmsnorm_bwd(h, g_row, dhn, dout, f"{tag}_dnorm")
    return dh, dg, dw_qkv, dqg, dkg, dw_o


@functools.partial(jax.custom_vjp, nondiff_argnums=(2,))
def _dotv(a, b, mode):
    return _dot(a, b, {"nn": NN, "nt": NT, "tn": TN}[mode])


def _dotv_fwd(a, b, mode):
    return _dotv(a, b, mode), (a, b)


def _dotv_bwd(mode, res, g):
    a, b = res
    if mode == "nn":
        return _dot(g, b, NT), _dot(a, g, TN)
    if mode == "nt":
        return _dot(g, b, NN), _dot(g, a, TN)
    return _dot(b, g, NT), _dot(a, g, NN)


_dotv.defvjp(_dotv_fwd, _dotv_bwd)


def _gelu(x):
    return 0.5 * x * (1.0 + lax.erf(x * (2.0 ** -0.5)))


def _gm1_f(au, av, vg):
    return _gelu(au), _rms(_gelu(av), vg)


def _gm1_ops(a, vg, tm, grad):
    T, W2 = a.shape
    W = W2 // 2
    part = lambda p: Op(a, (tm, W), lambda o, i: (i, p), "tile", grad, gshape=(T, W), gimap=lambda o, i: (i, 0))
    return [part(0), part(1), _vec_op(vg, grad)], (1, T // tm), W


_GM_ROWS = 1024


def _gm_specs(T, W, G):
    rows = _pick(T, _GM_ROWS, 512, 256, CHUNK)
    blk = pl.BlockSpec((rows, LANES), lambda g, c: (c, g))
    wspec = pl.BlockSpec((None, CHUNK, CHUNK), lambda g, c: (g, 0, 0))
    bspec = pl.BlockSpec((None, CHUNK, 1), lambda g, c: (g, 0, 0))
    return rows, blk, wspec, bspec, (G, T // rows)


def gm_mix_fwd(u, vn, ws, bcol, name):
    T, W = u.shape
    rows, blk, wspec, bspec, grid = _gm_specs(T, W, W // LANES)

    def body(u_ref, v_ref, w_ref, b_ref, y_ref):
        r, c = _iotas()
        w = jnp.where(r >= c, w_ref[...], 0.0).astype(_MXU)
        for k in range(rows // CHUNK):
            sl = pl.ds(k * CHUNK, CHUNK)
            y_ref[sl, :] = (u_ref[sl, :] * (_dot(w, v_ref[sl, :]) + b_ref[...])).astype(y_ref.dtype)

    return pl.pallas_call(body, out_shape=jax.ShapeDtypeStruct((T, W), _ACT), grid=grid,
                          in_specs=[blk, blk, wspec, bspec], out_specs=blk,
                          compiler_params=_params(("parallel", "parallel")), name=name)(u, vn, ws, bcol)


def gm_mix_bwd(u, vn, ws, bcol, dy, name):
    T, W = u.shape
    G = W // LANES
    rows, blk, wspec, bspec, grid = _gm_specs(T, W, G)

    def body(u_ref, v_ref, w_ref, b_ref, dy_ref, du_ref, dv_ref, dw_ref, db_ref):
        r, c = _iotas()
        tri = r >= c
        w = jnp.where(tri, w_ref[...], 0.0).astype(_MXU)
        dw = jnp.zeros((CHUNK, CHUNK), F32)
        db = jnp.zeros((CHUNK, 1), F32)
        for k in range(rows // CHUNK):
            sl = pl.ds(k * CHUNK, CHUNK)
            v = v_ref[sl, :]
            g = dy_ref[sl, :]
            du_ref[sl, :] = g * (_dot(w, v) + b_ref[...])
            dm = g * u_ref[sl, :]
            dv_ref[sl, :] = _dot(w, dm, TN)
            dw = dw + _dot(dm, v, NT)
            db = db + jnp.sum(dm, axis=1, keepdims=True)
        dw = jnp.where(tri, dw, 0.0)

        @pl.when(pl.program_id(1) == 0)
        def _():
            dw_ref[...] = dw
            db_ref[...] = db

        @pl.when(pl.program_id(1) != 0)
        def _():
            dw_ref[...] += dw
            db_ref[...] += db

    big = jax.ShapeDtypeStruct((T, W), F32)
    return pl.pallas_call(
        body,
        out_shape=(big, big, jax.ShapeDtypeStruct((G, CHUNK, CHUNK), F32), jax.ShapeDtypeStruct((G, CHUNK, 1), F32)),
        grid=grid, in_specs=[blk, blk, wspec, bspec, blk], out_specs=(blk, blk, wspec, bspec),
        compiler_params=_params(("parallel", "arbitrary")), name=name)(u, vn, ws, bcol, dy)


def gm_fwd(h, g_row, w_in, vg, ws, bcol, w_o, tag, cq=None):
    T = h.shape[0]
    tm = _pick(T, 256, 128)
    hn = rmsnorm_fwd(h, g_row, f"{tag}_norm")
    a = matmul(hn, w_in, "nn", F32, f"{tag}_in", b_cols4=True, cq=cq)
    ins, grid, W = _gm1_ops(a, vg, tm, False)
    rows = lambda dt: ((T, W), dt, (tm, W), lambda o, i: (i, 0))
    u, vn = tmap(_gm1_f, grid, ins, [rows(F32), rows(_ACT)], f"{tag}_act")
    y = gm_mix_fwd(u, vn, ws, bcol, f"{tag}_mix")
    out = matmul(y, w_o, "nn", F32, f"{tag}_wo", cq=cq, epilogue=lambda acc, hh: (acc + hh,), extras=(h,))
    return out, (h, hn, a, u, vn, y)


def gm_bwd(dout, saved, g_row, w_in, vg, ws, bcol, w_o, tag, cq=None):
    h, hn, a, u, vn, y = saved
    T = h.shape[0]
    tm = _pick(T, 256, 128)
    dy = matmul(dout, w_o, "nt", F32, f"{tag}_dy")
    dw_o = matmul(y, dout, "tn", GRAD_DT, f"{tag}_dwo")
    du, dvn, dws, dbcol = gm_mix_bwd(u, vn, ws, bcol, dy, f"{tag}_dmix")
    ins, grid, W = _gm1_ops(a, vg, tm, True)
    dau, dav, dvg = tmap_vjp(_gm1_f, grid, ins, _row_ops([du, dvn], tm), f"{tag}_dact", grad_dtypes={0: _ACT, 1: _ACT})
    da = jnp.concatenate([dau, dav], axis=1)
    dw_in = matmul(hn, da, "tn", GRAD_DT, f"{tag}_dwin", out_cols4=True)
    dhn = matmul(da, w_in, "nt", F32, f"{tag}_dhn", b_cols4=True)
    dh, dg = rmsnorm_bwd(h, g_row, dhn, dout, f"{tag}_dnorm")
    return dh, dg, dw_in, dvg, dws, dbcol, dw_o


@jax.custom_vjp
def _softplus(x):
    return jnp.maximum(x, 0.0) + jnp.log(1.0 + jnp.exp(-jnp.abs(x)))


_softplus.defvjp(lambda x: (_softplus(x), x), lambda x, g: (g * lax.logistic(x),))


def _silu(x):
    return x * lax.logistic(x)


def _shift_impl(x, s, down):
    n = x.shape[0]
    r = lax.broadcasted_iota(jnp.int32, x.shape, 0)
    if down:
        return jnp.where(r >= s, pltpu.roll(x, s, 0), 0.0)
    return jnp.where(r < n - s, pltpu.roll(x, n - s, 0), 0.0)


@functools.partial(jax.custom_vjp, nondiff_argnums=(1,))
def _shift_down(x, s):
    return _shift_impl(x, s, True)


_shift_down.defvjp(lambda x, s: (_shift_impl(x, s, True), None), lambda s, _, g: (_shift_impl(g, s, False),))


def _conv_f(x, w, b):
    k_id = lax.broadcasted_iota(jnp.int32, w.shape, 0)
    y = b + jnp.sum(jnp.where(k_id == SSD_CONV - 1, w, 0.0), axis=0, keepdims=True) * x
    for k in range(SSD_CONV - 1):
        wk = jnp.sum(jnp.where(k_id == k, w, 0.0), axis=0, keepdims=True)
        y = y + wk * _shift_down(x, SSD_CONV - 1 - k)
    return (_silu(y),)


def _conv_ops(zx, conv_w, conv_b, wi, grad):
    T = zx.shape[0]
    cd = conv_w.shape[1]
    cw = LANES
    off = wi // cw
    return [Op(zx, (T, cw), lambda o, j: (0, off + j), "tile", grad, gshape=(T, cd), gimap=lambda o, j: (0, j)),
            Op(conv_w, (SSD_CONV, cw), lambda o, j: (0, j), "tile", grad),
            Op(conv_b, (1, cw), lambda o, j: (0, j), "tile", grad)], (1, cd // cw), (T, cw)


def _dt_f(dtr, bias):
    return (_softplus(dtr + bias),)


def _cumdot_left(tri, x):
    return sum(lax.dot_general(tri, p, NN, preferred_element_type=F32) for p in _split3(x))


@jax.custom_vjp
def _cumsum_rows(x):
    row, col = _iotas()
    return _cumdot_left((row >= col).astype(_MXU), x)


def _cumsum_rows_bwd(_, g):
    row, col = _iotas()
    return (_cumdot_left((row <= col).astype(_MXU), g),)


_cumsum_rows.defvjp(lambda x: (_cumsum_rows(x), None), _cumsum_rows_bwd)


def _ssd_chunk(xps, dt, bm, cm, sps, alog, hid_base):
    row, col = _iotas()
    half = SSD_HEAD_DIM
    rcol = lax.broadcasted_iota(jnp.int32, (CHUNK, 1), 0)
    colpick = lambda m, hid: jnp.sum(jnp.where(col == hid, m, 0.0), axis=1, keepdims=True)
    rowpick = lambda m, hid: jnp.sum(jnp.where(row == hid, m, 0.0), axis=0, keepdims=True)
    last = lambda v: jnp.sum(jnp.where(rcol == CHUNK - 1, v, 0.0), axis=0, keepdims=True)
    acum = _cumsum_rows(dt * (-jnp.exp(alog)))
    acum_t = acum.T
    cb = _dotv(cm, bm, "nt")
    tri = row >= col
    lo = col < half
    ys, snews = [], []
    for p, (xp, sp) in enumerate(zip(xps, sps)):
        h0, h1 = hid_base + 2 * p, hid_base + 2 * p + 1
        ac0, ac1 = colpick(acum, h0), colpick(acum, h1)
        m0 = cb * jnp.exp(jnp.where(tri, ac0 - rowpick(acum_t, h0), -1e30))
        m1 = cb * jnp.exp(jnp.where(tri, ac1 - rowpick(acum_t, h1), -1e30))
        xs = xp * jnp.where(lo, colpick(dt, h0), colpick(dt, h1))
        ydiag = jnp.where(lo, _dotv(m0, xs, "nn"), _dotv(m1, xs, "nn"))
        yoff = jnp.where(lo, jnp.exp(ac0), jnp.exp(ac1)) * _dotv(cm, sp, "nt")
        al0, al1 = last(ac0), last(ac1)
        xsd = xs * jnp.where(lo, jnp.exp(al0 - ac0), jnp.exp(al1 - ac1))
        snew = jnp.where(row < half, jnp.exp(al0), jnp.exp(al1)) * sp + _dotv(xsd, bm, "tn")
        ys.append(ydiag + yoff)
        snews.append(snew)
    return ys, snews


def _ssd_dims(xact, wi):
    T, cd = xact.shape
    G = (cd - wi) // (2 * SSD_STATE)
    hpg = wi // SSD_HEAD_DIM // G
    assert hpg % 2 == 0 and SSD_STATE == LANES
    return T, G, hpg, hpg // 2, T // CHUNK, wi // G


def ssd_scan_fwd(xact, dt, alog, wi, name):
    T, G, hpg, NP, NC, gw = _ssd_dims(xact, wi)
    bo, co = wi // LANES, wi // LANES + G

    def body(x_ref, b_ref, c_ref, dt_ref, al_ref, y_ref, st_ref, s_ref):
        g, c = pl.program_id(0), pl.program_id(1)

        @pl.when(c == 0)
        def _():
            s_ref[...] = jnp.zeros_like(s_ref)

        st_ref[...] = s_ref[...]
        xps = [x_ref[:, p * LANES:(p + 1) * LANES] for p in range(NP)]
        sps = [s_ref[p] for p in range(NP)]
        ys, snews = _ssd_chunk(xps, dt_ref[...], b_ref[...], c_ref[...], sps, al_ref[...], g * hpg)
        for p in range(NP):
            y_ref[:, p * LANES:(p + 1) * LANES] = ys[p]
            s_ref[p] = snews[p]

    return pl.pallas_call(
        body,
        out_shape=(jax.ShapeDtypeStruct((T, wi), F32), jax.ShapeDtypeStruct((G, NC, NP, LANES, SSD_STATE), F32)),
        grid=(G, NC),
        in_specs=[pl.BlockSpec((CHUNK, gw), lambda g, c: (c, g)),
                  pl.BlockSpec((CHUNK, LANES), lambda g, c: (c, bo + g)),
                  pl.BlockSpec((CHUNK, LANES), lambda g, c: (c, co + g)),
                  pl.BlockSpec((CHUNK, LANES), lambda g, c: (c, 0)),
                  pl.BlockSpec((1, LANES), lambda g, c: (0, 0))],
        out_specs=(pl.BlockSpec((CHUNK, gw), lambda g, c: (c, g)),
                   pl.BlockSpec((None, None, NP, LANES, SSD_STATE), lambda g, c: (g, c, 0, 0, 0))),
        scratch_shapes=[pltpu.VMEM((NP, LANES, SSD_STATE), F32)],
        compiler_params=_params(("parallel", "arbitrary")),
        name=name,
    )(xact, xact, xact, dt, alog)


def ssd_scan_bwd(xact, dt, alog, states, dy, wi, name):
    T, G, hpg, NP, NC, gw = _ssd_dims(xact, wi)
    bo, co = wi // LANES, wi // LANES + G
    rev = lambda c: NC - 1 - c

    def body(x_ref, b_ref, c_ref, dt_ref, al_ref, st_ref, dy_ref, dx_ref, db_ref, dc_ref, ddt_ref, dal_ref, ds_ref):
        g, c = pl.program_id(0), pl.program_id(1)

        @pl.when(c == 0)
        def _():
            ds_ref[...] = jnp.zeros_like(ds_ref)

        xps = [x_ref[:, p * LANES:(p + 1) * LANES] for p in range(NP)]
        sps = [st_ref[p] for p in range(NP)]
        f = lambda xps_, dt_, bm_, cm_, sps_, al_: _ssd_chunk(xps_, dt_, bm_, cm_, sps_, al_, g * hpg)
        _, vjp = jax.vjp(f, xps, dt_ref[...], b_ref[...], c_ref[...], sps, al_ref[...])
        dys = [dy_ref[:, p * LANES:(p + 1) * LANES] for p in range(NP)]
        dxps, ddt, dbm, dcm, dsps, dal = vjp((dys, [ds_ref[p] for p in range(NP)]))
        for p in range(NP):
            dx_ref[:, p * LANES:(p + 1) * LANES] = dxps[p]
            ds_ref[p] = dsps[p]
        db_ref[...] = dbm
        dc_ref[...] = dcm
        ddt_ref[...] = ddt

        @pl.when(c == 0)
        def _():
            dal_ref[...] = dal

        @pl.when(c != 0)
        def _():
            dal_ref[...] += dal

    gb = G * SSD_STATE
    return pl.pallas_call(
        body,
        out_shape=(jax.ShapeDtypeStruct((T, wi), F32), jax.ShapeDtypeStruct((T, gb), F32), jax.ShapeDtypeStruct((T, gb), F32),
                   jax.ShapeDtypeStruct((G, T, LANES), F32), jax.ShapeDtypeStruct((G, 1, LANES), F32)),
        grid=(G, NC),
        in_specs=[pl.BlockSpec((CHUNK, gw), lambda g, c: (rev(c), g)),
                  pl.BlockSpec((CHUNK, LANES), lambda g, c: (rev(c), bo + g)),
                  pl.BlockSpec((CHUNK, LANES), lambda g, c: (rev(c), co + g)),
                  pl.BlockSpec((CHUNK, LANES), lambda g, c: (rev(c), 0)),
                  pl.BlockSpec((1, LANES), lambda g, c: (0, 0)),
                  pl.BlockSpec((None, None, NP, LANES, SSD_STATE), lambda g, c: (g, rev(c), 0, 0, 0)),
                  pl.BlockSpec((CHUNK, gw), lambda g, c: (rev(c), g))],
        out_specs=(pl.BlockSpec((CHUNK, gw), lambda g, c: (rev(c), g)),
                   pl.BlockSpec((CHUNK, LANES), lambda g, c: (rev(c), g)),
                   pl.BlockSpec((CHUNK, LANES), lambda g, c: (rev(c), g)),
                   pl.BlockSpec((None, CHUNK, LANES), lambda g, c: (g, rev(c), 0)),
                   pl.BlockSpec((None, 1, LANES), lambda g, c: (g, 0, 0))),
        scratch_shapes=[pltpu.VMEM((NP, LANES, SSD_STATE), F32)],
        compiler_params=_params(("parallel", "arbitrary")),
        name=name,
    )(xact, xact, xact, dt, alog, states, dy)


def _post_f(y, x, z, dcol, ng):
    return (_rms((y + dcol * x) * _silu(z), ng),)


def _post_ops(yssd, xact, zx, dcol, ng, G, tm, grad):
    T, wi = yssd.shape
    gw = wi // G
    blk = lambda a: Op(a, (tm, gw), lambda g, i: (i, g), "tile", grad, gshape=(T, wi))
    vec = lambda v: Op(v, (1, gw), lambda g, i: (0, g), "param", grad)
    return [blk(yssd), blk(xact), blk(zx), vec(dcol), vec(ng)], (G, T // tm), gw


def ssd_fwd(h, g_row, w_zx, w_dt, conv_w, conv_b, dtb, alog, dcol, ng, w_o, tag, cq=None):
    T = h.shape[0]
    wi = ng.shape[1]
    tm = _pick(T, 256, 128)
    hn = rmsnorm_fwd(h, g_row, f"{tag}_norm")
    zx = matmul(hn, w_zx, "nn", F32, f"{tag}_inzx", cq=cq)
    dtr = matmul(hn, w_dt, "nn", F32, f"{tag}_indt")
    ins, grid, blk = _conv_ops(zx, conv_w, conv_b, wi, False)
    cd = conv_w.shape[1]
    xact = tmap(_conv_f, grid, ins, [((T, cd), F32, blk, lambda o, j: (0, j))], f"{tag}_conv")[0]
    dt = tmap(_dt_f, (1, T // tm), _row_ops([dtr], tm) + [_vec_op(dtb)],
              [((T, LANES), F32, (tm, LANES), lambda o, i: (i, 0))], f"{tag}_dt")[0]
    yssd, states = ssd_scan_fwd(xact, dt, alog, wi, f"{tag}_scan")
    G = states.shape[0]
    ins, grid, gw = _post_ops(yssd, xact, zx, dcol, ng, G, tm, False)
    yn = tmap(_post_f, grid, ins, [((T, wi), _ACT, (tm, gw), lambda g, i: (i, g))], f"{tag}_post")[0]
    out = matmul(yn, w_o, "nn", F32, f"{tag}_wo", cq=cq, epilogue=lambda acc, hh: (acc + hh,), extras=(h,))
    return out, (h, hn, zx, dtr, xact, dt, yssd, states, yn)


def ssd_bwd(dout, saved, g_row, w_zx, w_dt, conv_w, conv_b, dtb, alog, dcol, ng, w_o, tag, cq=None):
    h, hn, zx, dtr, xact, dt, yssd, states, yn = saved
    T = h.shape[0]
    wi = ng.shape[1]
    tm = _pick(T, 256, 128)
    G = states.shape[0]
    dyn = matmul(dout, w_o, "nt", F32, f"{tag}_dyn")
    dw_o = matmul(yn, dout, "tn", GRAD_DT, f"{tag}_dwo")
    ins, grid, gw = _post_ops(yssd, xact, zx, dcol, ng, G, tm, True)
    dyssd, dxi_skip, dz, ddcol, dng = tmap_vjp(_post_f, grid, ins, [Op(dyn, (tm, gw), lambda g, i: (i, g))],
                                                f"{tag}_dpost", grad_dtypes={2: _ACT})
    dxi, dbm, dcm, ddt_g, dalog_g = ssd_scan_bwd(xact, dt, alog, states, dyssd, wi, f"{tag}_dscan")
    dxact = jnp.concatenate([dxi + dxi_skip, dbm, dcm], axis=1)
    ddt = jnp.sum(ddt_g, axis=0)
    dalog = jnp.sum(dalog_g, axis=0)
    ins, grid, blk = _conv_ops(zx, conv_w, conv_b, wi, True)
    dxbc, dconv_w, dconv_b = tmap_vjp(_conv_f, grid, ins, [Op(dxact, blk, lambda o, j: (0, j))], f"{tag}_dconv",
                                      grad_dtypes={0: _ACT})
    ddtr, ddtb = tmap_vjp(_dt_f, (1, T // tm), _row_ops([dtr], tm) + [_vec_op(dtb)], _row_ops([ddt], tm), f"{tag}_ddt",
                          grad_dtypes={0: _ACT})
    dzx = jnp.concatenate([dz, dxbc], axis=1)
    dw_zx = matmul(hn, dzx, "tn", GRAD_DT, f"{tag}_dwzx")
    dw_dt = matmul(hn, ddtr, "tn", GRAD_DT, f"{tag}_dwdt")
    dhn1 = matmul(ddtr, w_dt, "nt", F32, f"{tag}_dhn1")
    dhn = matmul(dzx, w_zx, "nt", F32, f"{tag}_dhn", epilogue=lambda acc, e: (acc + e,), extras=(dhn1,))
    dh, dg = rmsnorm_bwd(h, g_row, dhn, dout, f"{tag}_dnorm")
    return dh, dg, dw_zx, dw_dt, dconv_w, dconv_b, ddtb, dalog, ddcol, dng, dw_o


def loss_head(y, target, name):
    T, D = y.shape
    tm = _pick(T, 512, 256, 128)

    def body(y_ref, t_ref, dy_ref, part_ref):
        d = y_ref[...] - t_ref[...]
        dy_ref[...] = d * (1.0 / D)
        s = jnp.sum(d * d, axis=0, keepdims=True) * (0.5 / D)

        @pl.when(pl.program_id(0) == 0)
        def _():
            part_ref[...] = s

        @pl.when(pl.program_id(0) != 0)
        def _():
            part_ref[...] += s

    rows = pl.BlockSpec((tm, D), lambda i: (i, 0))
    return pl.pallas_call(
        body,
        out_shape=(jax.ShapeDtypeStruct((T, D), F32), jax.ShapeDtypeStruct((1, D), F32)),
        grid=(T // tm,),
        in_specs=[rows, rows],
        out_specs=(rows, pl.BlockSpec((1, D), lambda i: (0, 0))),
        compiler_params=_params(("arbitrary",)),
        name=name,
    )(y, target)


def _row_tile(R, C, itemsize=4, target=1 << 20):
    for t in (1024, 512, 256, 128, 64, 32, 16, 8):
        if R % t == 0 and t * C * itemsize <= target:
            return t
    return R


def adamw(w, g, m, v, name):
    R, C = w.shape
    tr = _row_tile(R, C)
    c1 = 1.0 - ADAM_B1 ** ADAM_STEP
    c2 = 1.0 - ADAM_B2 ** ADAM_STEP

    def body(w_ref, g_ref, m_ref, v_ref, d_ref, nm_ref, nv_ref):
        gg = g_ref[...]
        nm = ADAM_B1 * m_ref[...] + (1.0 - ADAM_B1) * gg
        nv = ADAM_B2 * v_ref[...] + (1.0 - ADAM_B2) * jnp.square(gg)
        d_ref[...] = -ADAM_LR * ((nm / c1) / (jnp.sqrt(nv / c2) + ADAM_EPS) + ADAM_WD * w_ref[...])
        nm_ref[...] = nm
        nv_ref[...] = nv

    spec = pl.BlockSpec((tr, C), lambda i: (i, 0))
    sds = jax.ShapeDtypeStruct((R, C), F32)
    return pl.pallas_call(body, out_shape=(sds, sds, sds), grid=(R // tr,), in_specs=[spec] * 4, out_specs=(spec,) * 3,
                          compiler_params=_params(("parallel",)), name=name)(w, g, m, v)


def pair_sum(gfull, recv, name):
    _, _, R, C = gfull.shape
    tr = _row_tile(R, C, 2)

    def body(g_ref, p_ref, o_ref):
        c = lax.axis_index("c")
        o_ref[...] = (g_ref[c].astype(F32) + p_ref[...].astype(F32)).astype(o_ref.dtype)

    return pl.pallas_call(
        body,
        out_shape=jax.ShapeDtypeStruct((4, R, C), gfull.dtype),
        grid=(4, R // tr),
        in_specs=[pl.BlockSpec((None, 2, tr, C), lambda p, i: (p, 0, i, 0)), pl.BlockSpec((None, tr, C), lambda p, i: (p, i, 0))],
        out_specs=pl.BlockSpec((None, tr, C), lambda p, i: (p, i, 0)),
        compiler_params=_params(("parallel", "parallel")),
        name=name,
    )(gfull, recv)


def chip_sum(s, recv, name):
    _, R, C = s.shape
    tr = _row_tile(R, C, 2, 1 << 19)

    def body(c_ref, s_ref, p_ref, o_ref):
        x, y, _ = _xyc()
        acc = s_ref[2 * x + y].astype(F32)
        for r in range(3):
            acc = acc + p_ref[r].astype(F32)
        o_ref[...] = acc

    grid_spec = pltpu.PrefetchScalarGridSpec(
        num_scalar_prefetch=1,
        grid=(R // tr,),
        in_specs=[pl.BlockSpec((4, tr, C), lambda i, c: (0, i, 0)), pl.BlockSpec((3, tr, C), lambda i, c: (0, i, 0))],
        out_specs=pl.BlockSpec((None, tr, C), lambda i, c: (c[0], i, 0)),
    )
    return pl.pallas_call(
        body,
        out_shape=jax.ShapeDtypeStruct((2, R, C), F32),
        grid_spec=grid_spec,
        compiler_params=_params(("arbitrary",)),
        name=name,
    )(lax.axis_index("c").reshape(1).astype(jnp.int32), s, recv)


def sum8(g, name):
    _, R, C = g.shape

    def body(g_ref, o_ref):
        acc = g_ref[0]
        for d in range(1, 8):
            acc = acc + g_ref[d]
        o_ref[...] = acc

    return pl.pallas_call(body, out_shape=jax.ShapeDtypeStruct((R, C), F32), name=name,
                          compiler_params=pltpu.CompilerParams(vmem_limit_bytes=_VMEM_LIMIT))(g)


def gather_small(x, name):
    R, C = x.shape

    def body(x_ref, out_ref, send_sems, recv_sems, local_sem):
        x_, y_, c_ = _xyc()
        me, sibling = (x_, y_, c_), (x_, y_, 1 - c_)
        chips = [(_flip(x_, fx), _flip(y_, fy)) for fx, fy in _REL]
        slot = lambda px, py, pc: out_ref.at[4 * px + 2 * py + pc]

        def copy(k, block, to, src=None):
            return _remote(slot(*block) if src is None else src, slot(*block), send_sems.at[k], recv_sems.at[k], to)

        mine = pltpu.make_async_copy(x_ref, slot(*me), local_sem)
        mine.start()
        first = [copy(0, me, sibling, src=x_ref)]
        first += [copy(1 + j, me, (*chip, c_), src=x_ref) for j, chip in enumerate(chips)]
        for cp in first:
            cp.start()
        passed = [copy(4 + j, (*chip, c_), sibling) for j, chip in enumerate(chips)]
        for j, chip in enumerate(chips):
            copy(1 + j, (*chip, c_), me).wait_recv()
            passed[j].start()
        copy(0, sibling, me).wait_recv()
        for j, chip in enumerate(chips):
            copy(4 + j, (*chip, 1 - c_), me).wait_recv()
        for cp in first + passed:
            cp.wait_send()
        mine.wait()

    return pl.pallas_call(
        body,
        out_shape=jax.ShapeDtypeStruct((8, R, C), x.dtype),
        in_specs=[pl.BlockSpec(memory_space=pltpu.VMEM)],
        out_specs=pl.BlockSpec(memory_space=pltpu.VMEM),
        scratch_shapes=[pltpu.SemaphoreType.DMA((7,)), pltpu.SemaphoreType.DMA((7,)), pltpu.SemaphoreType.DMA],
        compiler_params=pltpu.CompilerParams(vmem_limit_bytes=_VMEM_LIMIT),
        name=name,
    )(x)


def gather_weights(halves, name):
    n = len(halves)

    def body(*refs):
        ins, outs = refs[:n], refs[n:2 * n]
        send_sems, recv_sems = refs[2 * n:]
        x_, y_, c_ = _xyc()
        sibling = (x_, y_, 1 - c_)
        chips = [(_flip(x_, fx), _flip(y_, fy)) for fx, fy in _REL]
        started = []
        for i in range(n):
            own, dst = ins[i].at[c_], outs[i].at[2 * x_ + y_, c_]
            for r, chip in enumerate(chips):
                started.append(_remote(own, dst, send_sems.at[6 * i + r], recv_sems.at[6 * i + r], (*chip, c_)))
                started[-1].start()
        for i in range(n):
            for r, (px, py) in enumerate(chips):
                blk = outs[i].at[2 * px + py, c_]
                _remote(blk, blk, send_sems.at[6 * i + r], recv_sems.at[6 * i + r], sibling).wait_recv()
                started.append(_remote(blk, blk, send_sems.at[6 * i + 3 + r], recv_sems.at[6 * i + 3 + r], sibling))
                started[-1].start()
        for i in range(n):
            for r, (px, py) in enumerate(chips):
                blk = outs[i].at[2 * px + py, 1 - c_]
                _remote(blk, blk, send_sems.at[6 * i + 3 + r], recv_sems.at[6 * i + 3 + r], sibling).wait_recv()
        for cp in started:
            cp.wait_send()

    return pl.pallas_call(
        body,
        out_shape=tuple(jax.ShapeDtypeStruct((4,) + h.shape, h.dtype) for h in halves),
        in_specs=[_ANY] * n,
        out_specs=tuple(_ANY for _ in halves),
        scratch_shapes=[pltpu.SemaphoreType.DMA((6 * n,)), pltpu.SemaphoreType.DMA((6 * n,))],
        name=name,
    )(*halves)


def share_weights(gathered, name):
    n = len(gathered)

    def body(*refs):
        outs = refs[n:2 * n]
        send_sems, recv_sems = refs[2 * n:]
        x_, y_, c_ = _xyc()
        sibling = (x_, y_, 1 - c_)
        chips = [(_flip(x_, fx), _flip(y_, fy)) for fx, fy in _REL]
        sent = []
        for i in range(n):
            for q, (px, py) in enumerate(chips):
                blk = outs[i].at[2 * px + py, c_]
                sent.append(_remote(blk, blk, send_sems.at[3 * i + q], recv_sems.at[3 * i + q], sibling))
                sent[-1].start()
        for i in range(n):
            for q, (px, py) in enumerate(chips):
                blk = outs[i].at[2 * px + py, 1 - c_]
                _remote(blk, blk, send_sems.at[3 * i + q], recv_sems.at[3 * i + q], sibling).wait_recv()
        for cp in sent:
            cp.wait_send()

    return pl.pallas_call(
        body,
        out_shape=tuple(jax.ShapeDtypeStruct(g.shape, g.dtype) for g in gathered),
        in_specs=[_ANY] * n,
        out_specs=tuple(_ANY for _ in gathered),
        input_output_aliases={i: i for i in range(n)},
        scratch_shapes=[pltpu.SemaphoreType.DMA((3 * n,)), pltpu.SemaphoreType.DMA((3 * n,))],
        name=name,
    )(*gathered)


def swap_halves(gfulls, name):
    n = len(gfulls)

    def body(*refs):
        ins, outs = refs[:n], refs[n:2 * n]
        send_sems, recv_sems = refs[2 * n:]
        x_, y_, c_ = _xyc()
        sibling = (x_, y_, 1 - c_)
        started = []
        for i in range(n):
            for p in range(4):
                started.append(_remote(ins[i].at[p, 1 - c_], outs[i].at[p], send_sems.at[4 * i + p], recv_sems.at[4 * i + p], sibling))
                started[-1].start()
        for cp in started:
            cp.wait()

    return pl.pallas_call(
        body,
        out_shape=tuple(jax.ShapeDtypeStruct((4,) + g.shape[2:], g.dtype) for g in gfulls),
        in_specs=[_ANY] * n,
        out_specs=tuple(_ANY for _ in gfulls),
        scratch_shapes=[pltpu.SemaphoreType.DMA((4 * n,)), pltpu.SemaphoreType.DMA((4 * n,))],
        name=name,
    )(*gfulls)


def scatter_chips(sums, name):
    n = len(sums)

    def body(*refs):
        ins, outs = refs[:n], refs[n:2 * n]
        send_sems, recv_sems = refs[2 * n:]
        x_, y_, c_ = _xyc()
        chips = [(_flip(x_, fx), _flip(y_, fy)) for fx, fy in _REL]
        started = []
        for i in range(n):
            for r, (px, py) in enumerate(chips):
                started.append(_remote(ins[i].at[2 * px + py], outs[i].at[r], send_sems.at[3 * i + r], recv_sems.at[3 * i + r], (px, py, c_)))
                started[-1].start()
        for cp in started:
            cp.wait()

    return pl.pallas_call(
        body,
        out_shape=tuple(jax.ShapeDtypeStruct((3,) + s.shape[1:], s.dtype) for s in sums),
        in_specs=[_ANY] * n,
        out_specs=tuple(_ANY for _ in sums),
        scratch_shapes=[pltpu.SemaphoreType.DMA((3 * n,)), pltpu.SemaphoreType.DMA((3 * n,))],
        name=name,
    )(*sums)


_JOIN_CHUNK_BYTES = 4 << 20


def join_halves(bufs, name):
    n = len(bufs)
    chunks = []
    for b in bufs:
        _, R, C = b.shape
        k = 1
        while k < 8 and R % (2 * k * 8) == 0 and R * C * 4 // k > _JOIN_CHUNK_BYTES:
            k *= 2
        chunks.append(k)
    base = [sum(chunks[:i]) for i in range(n)]
    total = sum(chunks)

    def body(*refs):
        outs = refs[n:2 * n]
        send_sems, recv_sems = refs[2 * n:]
        x_, y_, c_ = _xyc()
        sibling = (x_, y_, 1 - c_)
        sent = []
        for i in range(n):
            rc = bufs[i].shape[1] // chunks[i]
            for q in range(chunks[i]):
                blk = outs[i].at[c_, pl.ds(q * rc, rc)]
                sent.append(_remote(blk, blk, send_sems.at[base[i] + q], recv_sems.at[base[i] + q], sibling))
                sent[-1].start()
        for i in range(n):
            rc = bufs[i].shape[1] // chunks[i]
            for q in range(chunks[i]):
                blk = outs[i].at[1 - c_, pl.ds(q * rc, rc)]
                _remote(blk, blk, send_sems.at[base[i] + q], recv_sems.at[base[i] + q], sibling).wait_recv()
        for cp in sent:
            cp.wait_send()

    return pl.pallas_call(
        body,
        out_shape=tuple(jax.ShapeDtypeStruct(b.shape, b.dtype) for b in bufs),
        in_specs=[_ANY] * n,
        out_specs=tuple(_ANY for _ in bufs),
        input_output_aliases={i: i for i in range(n)},
        scratch_shapes=[pltpu.SemaphoreType.DMA((total,)), pltpu.SemaphoreType.DMA((total,))],
        name=name,
    )(*bufs)


def _halves(a):
    return a.reshape(2, -1, a.shape[-1])


_BIG = ("sb_w_qkv", "sb_w_o", "gm_w_in", "gm_w_o", "ssd_w_in", "ssd_w_o", "mlp_w_in", "mlp_w_out")
_COLS = ("sb_w_qkv", "gm_w_in", "ssd_w_in", "mlp_w_in")
_SMALL = ("norm_mix_g", "norm_mlp_g", "sb_q_norm_g", "sb_k_norm_g", "gm_v_norm_g", "gm_w_s", "gm_b_s",
          "ssd_conv_w", "ssd_conv_b", "ssd_dt_bias", "ssd_a_log", "ssd_d", "ssd_norm_g")
_SMALL_SHARDED = ("ssd_conv_w", "ssd_conv_b", "ssd_norm_g")
_WEIGHTS = ("norm_mix_g", "norm_mlp_g", "sb_w_qkv", "sb_q_norm_g", "sb_k_norm_g", "sb_w_o", "gm_w_in", "gm_v_norm_g",
            "gm_w_s", "gm_b_s", "gm_w_o", "ssd_w_in", "ssd_conv_w", "ssd_conv_b", "ssd_dt_bias", "ssd_a_log", "ssd_d",
            "ssd_norm_g", "ssd_w_o", "mlp_w_in", "mlp_w_out")


def _pack(arrs):
    flat = jnp.concatenate([a.reshape(-1).astype(F32) for a in arrs])
    n = flat.shape[0]
    tot = -(-n // (8 * LANES)) * 8 * LANES
    return jnp.pad(flat, (0, tot - n)).reshape(-1, LANES)


def _unpack(buf, shapes):
    flat = buf.reshape(-1)
    out, o = [], 0
    for s in shapes:
        n = math.prod(s)
        out.append(flat[o:o + n].reshape(s))
        o += n
    return out


_CARRIER_RANK = {0: (3, 0, 2, 1), 1: (3, 2, 0, 1), 2: (3, 0, 2, 1)}


def _layer_arrays(i):
    kind, j = i % 3, i // 3
    mix = (("sb_w_qkv", "sb_w_o"), ("gm_w_in", "gm_w_o"), ("ssd_w_in", "ssd_w_o"))[kind]
    return [(mix[0], j), (mix[1], j), ("mlp_w_in", i), ("mlp_w_out", i)]


def _step(x, w, target, m, v):
    depth = w["norm_mix_g"].shape[0]
    xc, yc, _ = _xyc()
    chip = 2 * xc + yc
    Hs = w["ssd_dt_bias"].shape[1]
    wi = 4 * w["ssd_norm_g"].shape[1]
    cd = 4 * w["ssd_conv_b"].shape[1]

    halves = {(k, l): _halves(w[k][l].astype(_MXU)) for i in range(depth) for (k, l) in _layer_arrays(i)}
    size = lambda key: math.prod(halves[key].shape)

    def finish(key, g):
        k, l = key
        g = g.reshape((4,) + w[k].shape[1:])
        g = lax.dynamic_update_slice(g, w[k][l].astype(_MXU)[None], (chip, 0, 0))
        if k == "ssd_w_in":
            full = g.transpose(1, 0, 2).reshape(g.shape[1], -1)
            return full[:, :wi + cd], jnp.pad(full[:, wi + cd:], ((0, 0), (0, LANES - Hs)))
        return g if k in _COLS else g.reshape(-1, g.shape[-1])

    small_sh = gather_small(_pack([w[k] for k in _SMALL_SHARDED]), "gather_small_weights")
    parts = [_unpack(small_sh[2 * j], [w[k].shape for k in _SMALL_SHARDED]) for j in range(4)]
    conv_w = jnp.concatenate([p[0][0] for p in parts], axis=1)
    conv_b = jnp.concatenate([p[1] for p in parts], axis=1)
    ssd_ng = jnp.concatenate([p[2] for p in parts], axis=1)
    padh = lambda a: jnp.pad(a, ((0, 0), (0, LANES - Hs)))
    dtb, alog = padh(w["ssd_dt_bias"]), padh(w["ssd_a_log"])
    dcol = jnp.repeat(w["ssd_d"], SSD_HEAD_DIM, axis=1)
    bcol = w["gm_b_s"][0][:, :, None]

    keys0 = _layer_arrays(0)
    ready = dict(zip(keys0, gather_weights([halves[k] for k in keys0], "gather_weights_0")))

    h = x[0]
    tape = []
    for i in range(depth):
        kind, j = i % 3, i // 3
        keys = _layer_arrays(i)
        wl = [finish(k, ready[k]) for k in keys]
        cq = Carriers()
        if i + 1 < depth:
            nxt = sorted(_layer_arrays(i + 1), key=size, reverse=True)
            order = [None] * 4
            for rank, key in zip(_CARRIER_RANK[kind], nxt):
                order[rank] = key
            for key in order:
                cq.add("gather", halves[key], key)
        gmix = w["norm_mix_g"][i:i + 1]
        if kind == 0:
            args = (gmix, wl[0], w["sb_q_norm_g"][j:j + 1], w["sb_k_norm_g"][j:j + 1], wl[1], f"sb{j}")
            h, sv = sb_fwd(h, *args, cq=cq)
        elif kind == 1:
            args = (gmix, wl[0], w["gm_v_norm_g"][j:j + 1], w["gm_w_s"][j], bcol, wl[1], f"gm{j}")
            h, sv = gm_fwd(h, *args, cq=cq)
        else:
            args = (gmix, wl[0][0], wl[0][1], conv_w, conv_b, dtb, alog, dcol, ssd_ng, wl[1], f"ssd{j}")
            h, sv = ssd_fwd(h, *args, cq=cq)
        margs = (w["norm_mlp_g"][i:i + 1], wl[2], wl[3], f"mlp{i}")
        h, msv = mlp_fwd(h, *margs, cq=cq)
        tape.append((kind, j, args, sv, margs, msv))
        if i + 1 < depth:
            assert not cq.pending
            nk_ = _layer_arrays(i + 1)
            shared = share_weights([cq.done[k] for k in nk_], f"share_weights_{i + 1}")
            ready.update(zip(nk_, shared))

    dh, loss_cols = loss_head(h, target[0], "loss_head")

    gsmall = {k: [None] * w[k].shape[0] for k in ("norm_mix_g", "norm_mlp_g", "sb_q_norm_g", "sb_k_norm_g")}
    pairs, from_chips = {}, {}

    def reduce_pairs(keys_, grads_, tag_):
        gl = [g.reshape(4, 2, -1, g.shape[-1]).astype(GRAD_DT) for g in grads_]
        got = swap_halves(gl, f"grad_swap_halves_{tag_}")
        for key, g, r in zip(keys_, gl, got):
            pairs[key] = pair_sum(g, r, f"pair_sum_{key[0]}_{key[1]}")

    cq = Carriers()
    for i in reversed(range(depth)):
        kind, j, args, sv, margs, msv = tape[i]
        keys = _layer_arrays(i)
        dh, gsmall["norm_mlp_g"][i], d_mlp_in, d_mlp_out = mlp_bwd(dh, msv, *margs, cq=cq)
        assert not cq.pending
        from_chips.update(cq.done)
        mix_cq = Carriers()
        if i == 0:
            reduce_pairs(keys[2:], (d_mlp_in, d_mlp_out), "0_mlp")
            for key in keys[2:]:
                mix_cq.add("scatter", pairs[key], key)
        if kind == 0:
            dh, gsmall["norm_mix_g"][i], d_in, gsmall["sb_q_norm_g"][j], gsmall["sb_k_norm_g"][j], d_out = sb_bwd(dh, sv, *args, cq=mix_cq)
        elif kind == 1:
            dh, gsmall["norm_mix_g"][i], d_in, d_vg, d_ws, d_bcol, d_out = gm_bwd(dh, sv, *args)
            gsmall["gm_v_norm_g"], gsmall["gm_w_s"], gsmall["gm_b_s"] = d_vg, d_ws[None], d_bcol[None, :, :, 0]
        else:
            dh, gsmall["norm_mix_g"][i], d_zx, d_dt, d_cw, d_cb, d_dtb, d_al, d_dcol, d_ng, d_out = ssd_bwd(dh, sv, *args)
            d_full = jnp.concatenate([d_zx, d_dt[:, :Hs]], axis=1)
            d_in = d_full.reshape(d_full.shape[0], 4, -1).transpose(1, 0, 2)
            gsmall["ssd_conv_w"], gsmall["ssd_conv_b"], gsmall["ssd_norm_g"] = d_cw[None], d_cb, d_ng
            gsmall["ssd_dt_bias"], gsmall["ssd_a_log"] = d_dtb[:, :Hs], d_al[:, :Hs]
            gsmall["ssd_d"] = jnp.sum(d_dcol.reshape(Hs, SSD_HEAD_DIM), axis=1)[None]
        assert not mix_cq.pending
        from_chips.update(mix_cq.done)
        cq = Carriers()
        if i > 0:
            reduce_pairs(keys, (d_in, d_out, d_mlp_in, d_mlp_out), str(i))
            for key in sorted(keys, key=size, reverse=True):
                cq.add("scatter", pairs[key], key)
        else:
            reduce_pairs(keys[:2], (d_in, d_out), "0_mix")
            from_chips.update(zip(keys[:2], scatter_chips([pairs[k] for k in keys[:2]], "grad_scatter_chips_0")))
    for k in gsmall:
        if isinstance(gsmall[k], list):
            gsmall[k] = jnp.concatenate(gsmall[k], axis=0)
    grad_x = dh[None]

    allkeys = [key for i in range(depth) for key in _layer_arrays(i)]
    mine = [chip_sum(pairs[key], from_chips[key], f"chip_sum_{key[0]}_{key[1]}") for key in allkeys]
    joined = dict(zip(allkeys, join_halves(mine, "grad_join_halves")))
    grads, deltas, new_m, new_v = {}, {}, {}, {}
    for k in _BIG:
        g = jnp.stack([joined[(k, l)].reshape(w[k].shape[1:]) for l in range(w[k].shape[0])])
        C = w[k].shape[-1]
        d_, m_, v_ = adamw(w[k].reshape(-1, C), g.reshape(-1, C), m[k].reshape(-1, C), v[k].reshape(-1, C), f"adamw_{k}")
        grads[k], deltas[k], new_m[k], new_v[k] = (a.reshape(w[k].shape) for a in (g, d_, m_, v_))

    full_shapes = [gsmall[k].shape for k in _SMALL] + [(1,)]
    loss_local = jnp.sum(loss_cols).reshape(1)
    red = sum8(gather_small(_pack([gsmall[k] for k in _SMALL] + [loss_local]), "gather_small_grads"), "sum_small_grads")
    red = _unpack(red, full_shapes)
    loss = red[-1][0]
    gsm = dict(zip(_SMALL, red[:-1]))
    for k in _SMALL_SHARDED:
        n = w[k].shape[-1]
        gsm[k] = lax.dynamic_slice_in_dim(gsm[k], chip * n, n, axis=gsm[k].ndim - 1)
    shapes = [w[k].shape for k in _SMALL]
    packed = [_pack([d[k] for k in _SMALL]) for d in (w, gsm, m, v)]
    outs = adamw(*packed, "adamw_small")
    for k, g_, d_, m_, v_ in zip(_SMALL, [gsm[k] for k in _SMALL], *[_unpack(o, shapes) for o in outs]):
        grads[k], deltas[k], new_m[k], new_v[k] = g_.reshape(w[k].shape), d_, m_, v_

    return (loss, grad_x, *[grads[k] for k in _WEIGHTS], *[deltas[k] for k in _WEIGHTS],
            *[new_m[k] for k in _WEIGHTS], *[new_v[k] for k in _WEIGHTS])


def kernel(x, norm_mix_g, norm_mlp_g, sb_w_qkv, sb_q_norm_g, sb_k_norm_g, sb_w_o, gm_w_in, gm_v_norm_g, gm_w_s, gm_b_s, gm_w_o, ssd_w_in, ssd_conv_w, ssd_conv_b, ssd_dt_bias, ssd_a_log, ssd_d, ssd_norm_g, ssd_w_o, mlp_w_in, mlp_w_out, loss_target, m_norm_mix_g, m_norm_mlp_g, m_sb_w_qkv, m_sb_q_norm_g, m_sb_k_norm_g, m_sb_w_o, m_gm_w_in, m_gm_v_norm_g, m_gm_w_s, m_gm_b_s, m_gm_w_o, m_ssd_w_in, m_ssd_conv_w, m_ssd_conv_b, m_ssd_dt_bias, m_ssd_a_log, m_ssd_d, m_ssd_norm_g, m_ssd_w_o, m_mlp_w_in, m_mlp_w_out, v_norm_mix_g, v_norm_mlp_g, v_sb_w_qkv, v_sb_q_norm_g, v_sb_k_norm_g, v_sb_w_o, v_gm_w_in, v_gm_v_norm_g, v_gm_w_s, v_gm_b_s, v_gm_w_o, v_ssd_w_in, v_ssd_conv_w, v_ssd_conv_b, v_ssd_dt_bias, v_ssd_a_log, v_ssd_d, v_ssd_norm_g, v_ssd_w_o, v_mlp_w_in, v_mlp_w_out):
    w = dict(zip(_WEIGHTS, (norm_mix_g, norm_mlp_g, sb_w_qkv, sb_q_norm_g, sb_k_norm_g, sb_w_o, gm_w_in, gm_v_norm_g, gm_w_s,
                            gm_b_s, gm_w_o, ssd_w_in, ssd_conv_w, ssd_conv_b, ssd_dt_bias, ssd_a_log, ssd_d, ssd_norm_g,
                            ssd_w_o, mlp_w_in, mlp_w_out)))
    m = dict(zip(_WEIGHTS, (m_norm_mix_g, m_norm_mlp_g, m_sb_w_qkv, m_sb_q_norm_g, m_sb_k_norm_g, m_sb_w_o, m_gm_w_in,
                            m_gm_v_norm_g, m_gm_w_s, m_gm_b_s, m_gm_w_o, m_ssd_w_in, m_ssd_conv_w, m_ssd_conv_b,
                            m_ssd_dt_bias, m_ssd_a_log, m_ssd_d, m_ssd_norm_g, m_ssd_w_o, m_mlp_w_in, m_mlp_w_out)))
    v = dict(zip(_WEIGHTS, (v_norm_mix_g, v_norm_mlp_g, v_sb_w_qkv, v_sb_q_norm_g, v_sb_k_norm_g, v_sb_w_o, v_gm_w_in,
                            v_gm_v_norm_g, v_gm_w_s, v_gm_b_s, v_gm_w_o, v_ssd_w_in, v_ssd_conv_w, v_ssd_conv_b,
                            v_ssd_dt_bias, v_ssd_a_log, v_ssd_d, v_ssd_norm_g, v_ssd_w_o, v_mlp_w_in, v_mlp_w_out)))
    return _step(x, w, loss_target, m, v)
```

```python
import functools
import math

import jax
import jax.numpy as jnp
from jax import lax
from jax.experimental import pallas as pl
from jax.experimental.pallas import tpu as pltpu

F32 = jnp.float32
BF16 = jnp.bfloat16
_MXU = jnp.bfloat16
_ACT = jnp.bfloat16
GRAD_DT = jnp.bfloat16
_VMEM_LIMIT = 56 * 1024 * 1024
_MATMUL_VMEM_BUDGET = 44 * 1024 * 1024
EPS = 1e-6
LANES = 128
CHUNK = 128
SSD_HEAD_DIM = 64
SSD_STATE = 128
SSD_CONV = 4
ADAM_LR, ADAM_B1, ADAM_B2, ADAM_EPS, ADAM_WD, ADAM_STEP = 1e-3, 0.9, 0.999, 1e-8, 0.01, 10
MESH = pl.DeviceIdType.MESH

NN = (((1,), (0,)), ((), ()))
NT = (((1,), (1,)), ((), ()))
TN = (((0,), (0,)), ((), ()))

_ANY = pl.BlockSpec(memory_space=pl.ANY)
_REL = ((1, 0), (0, 1), (1, 1))


def _dot(a, b, dims=NN):
    return lax.dot_general(a.astype(_MXU), b.astype(_MXU), dims, preferred_element_type=F32)


def _params(sem):
    return pltpu.CompilerParams(dimension_semantics=sem, vmem_limit_bytes=_VMEM_LIMIT)


def _pick(n, *cands):
    for c in cands:
        if n % c == 0:
            return c
    return n


def _xyc():
    return lax.axis_index("x"), lax.axis_index("y"), lax.axis_index("c")


def _flip(v, f):
    return 1 - v if f else v


def _remote(src, dst, ssem, rsem, dev):
    return pltpu.make_async_remote_copy(src_ref=src, dst_ref=dst, send_sem=ssem, recv_sem=rsem, device_id=dev,
                                        device_id_type=MESH)


class Carriers:
    def __init__(self):
        self.pending, self.done = [], {}

    def add(self, kind, src, tag):
        self.pending.append([(kind, src, tag)])

    def add_group(self, transfers):
        self.pending.append(list(transfers))


def matmul(a, b, mode, out_dtypes, name, epilogue=None, extras=(), b_cols4=False, out_cols4=False, cq=None):
    ash = a.shape
    bsh = (b.shape[1], 4 * b.shape[2]) if b_cols4 else b.shape
    if mode == "nn":
        (M, K), (K2, N) = ash, bsh
    elif mode == "nt":
        (M, K), (N, K2) = ash, bsh
    else:
        (K, M), (K2, N) = ash, bsh
    assert K == K2, (mode, a.shape, b.shape)
    nsh = N // 4 if (out_cols4 or (b_cols4 and mode == "nn")) else None
    ksh = K // 4 if (b_cols4 and mode == "nt") else None
    single = not isinstance(out_dtypes, (tuple, list))
    odt = (out_dtypes,) if single else tuple(out_dtypes)
    n_ex, n_out = len(extras), len(odt)

    def fits(sh, top):
        whole = [sh] if sh is not None and sh <= top and sh % LANES == 0 else []
        return whole + [c for c in (2048, 1024, 512, 256, 128) if c <= top and (sh is None or sh % c == 0)] + ([] if sh is None else [sh])

    tn = _pick(N, *fits(nsh, 1536))
    tk = _pick(K, *fits(ksh, 2048))

    def vmem_bytes(tm_):
        per_step = tm_ * tk * a.dtype.itemsize + tk * tn * b.dtype.itemsize + n_ex * tm_ * tn * 4
        per_step += sum(tm_ * tn * jnp.dtype(d).itemsize for d in odt)
        return 2 * per_step + tm_ * tn * 4

    tm = next((c for c in (1024, 512, 256, 128) if M % c == 0 and vmem_bytes(c) <= _MATMUL_VMEM_BUDGET), M)
    ni, nj, nk = M // tm, N // tn, K // tk
    dims = {"nn": NN, "nt": NT, "tn": TN}[mode]
    carry = cq.pending.pop(0) if (cq is not None and cq.pending) else []
    nc = len(carry)

    if mode == "tn":
        a_spec = pl.BlockSpec((tk, tm), lambda i, j, k: (k, i))
    else:
        a_spec = pl.BlockSpec((tm, tk), lambda i, j, k: (i, k))
    if b_cols4 and mode == "nn":
        b_spec = pl.BlockSpec((None, tk, tn), lambda i, j, k: (lax.div(j * tn, nsh), k, lax.div(lax.rem(j * tn, nsh), tn)))
    elif b_cols4:
        b_spec = pl.BlockSpec((None, tn, tk), lambda i, j, k: (lax.div(k * tk, ksh), j, lax.div(lax.rem(k * tk, ksh), tk)))
    elif mode == "nt":
        b_spec = pl.BlockSpec((tn, tk), lambda i, j, k: (j, k))
    else:
        b_spec = pl.BlockSpec((tk, tn), lambda i, j, k: (k, j))
    mn_spec = pl.BlockSpec((tm, tn), lambda i, j, k: (i, j))
    if out_cols4:
        o_spec = pl.BlockSpec((None, tm, tn), lambda i, j, k: (lax.div(j * tn, nsh), i, lax.div(lax.rem(j * tn, nsh), tn)))
        o_shape = (4, M, N // 4)
    else:
        o_spec, o_shape = mn_spec, (M, N)

    def body(*refs):
        a_ref, b_ref = refs[0], refs[1]
        ex = refs[2:2 + n_ex]
        pos = 2 + n_ex
        src_refs = refs[pos:pos + nc]
        pos += nc
        outs = refs[pos:pos + n_out]
        pos += n_out
        dst_refs = refs[pos:pos + nc]
        pos += nc
        acc = refs[pos]
        i, j, k = pl.program_id(0), pl.program_id(1), pl.program_id(2)

        if carry:
            ssem, rsem = refs[pos + 1], refs[pos + 2]
            x_, y_, c_ = _xyc()
            chips = [(_flip(x_, fx), _flip(y_, fy)) for fx, fy in _REL]

            def copies(arriving):
                out = []
                for t, (kind, _, _) in enumerate(carry):
                    src_ref, dst_ref = src_refs[t], dst_refs[t]
                    for r, (px, py) in enumerate(chips):
                        if kind == "gather":
                            s_, d_ = src_ref.at[c_], dst_ref.at[(2 * px + py) if arriving else (2 * x_ + y_), c_]
                        else:
                            s_, d_ = src_ref.at[2 * px + py], dst_ref.at[r]
                        out.append(_remote(s_, d_, ssem.at[3 * t + r], rsem.at[3 * t + r], (px, py, c_)))
                return out

            @pl.when((i == 0) & (j == 0) & (k == 0))
            def _():
                for send in copies(False):
                    send.start()

        @pl.when(k == 0)
        def _():
            acc[...] = jnp.zeros_like(acc)

        part = _dot(a_ref[...], b_ref[...], dims)

        @pl.when(k < nk - 1)
        def _():
            acc[...] += part

        @pl.when(k == nk - 1)
        def _():
            r = acc[...] + part
            res = (r,) if epilogue is None else epilogue(r, *[e[...] for e in ex])
            for o, v in zip(outs, res):
                o[...] = v.astype(o.dtype)

        if carry:
            @pl.when((i == ni - 1) & (j == nj - 1) & (k == nk - 1))
            def _():
                for arrive in copies(True):
                    arrive.wait_recv()
                for send in copies(False):
                    send.wait_send()

    in_specs = [a_spec, b_spec] + [mn_spec] * n_ex
    out_shape = [jax.ShapeDtypeStruct(o_shape, d) for d in odt]
    out_specs = [o_spec for _ in odt]
    scratch = [pltpu.VMEM((tm, tn), F32)]
    operands = [a, b, *extras]
    for kind, src, _ in carry:
        in_specs.append(_ANY)
        operands.append(src)
        out_shape.append(jax.ShapeDtypeStruct(((4,) + src.shape) if kind == "gather" else ((3,) + src.shape[1:]), src.dtype))
        out_specs.append(_ANY)
    if carry:
        scratch += [pltpu.SemaphoreType.DMA((3 * nc,)), pltpu.SemaphoreType.DMA((3 * nc,))]
    out = pl.pallas_call(
        body,
        out_shape=tuple(out_shape),
        grid=(ni, nj, nk),
        in_specs=in_specs,
        out_specs=tuple(out_specs),
        scratch_shapes=scratch,
        compiler_params=_params(("arbitrary", "arbitrary", "arbitrary") if carry else ("parallel", "parallel", "arbitrary")),
        name=name,
    )(*operands)
    if carry:
        for (_, _, tag), filled in zip(carry, out[n_out:]):
            cq.done[tag] = filled
        out = out[:n_out]
    return out[0] if single else out


class Op:
    def __init__(self, arr, block, imap, kind="tile", grad=True, gshape=None, gimap=None):
        self.arr, self.block, self.imap, self.kind, self.grad = arr, block, imap, kind, grad
        self.gshape = gshape or arr.shape
        self.gimap = gimap or imap

    def spec(self):
        return pl.BlockSpec(self.block, self.imap)


def tmap(f, grid, ins, outs, name):
    n_in = len(ins)

    def body(*refs):
        res = f(*[r[...] for r in refs[:n_in]])
        for o, v in zip(refs[n_in:], res):
            o[...] = v.astype(o.dtype)

    return pl.pallas_call(
        body,
        out_shape=tuple(jax.ShapeDtypeStruct(s, d) for s, d, _, _ in outs),
        grid=grid,
        in_specs=[o.spec() for o in ins],
        out_specs=tuple(pl.BlockSpec(b, m) for _, _, b, m in outs),
        compiler_params=_params(("parallel", "parallel")),
        name=name,
    )(*[o.arr for o in ins])


def tmap_vjp(f, grid, ins, cts, name, grad_dtypes=None):
    n_in, n_ct = len(ins), len(cts)
    gidx = [i for i, o in enumerate(ins) if o.grad]
    gdt = grad_dtypes or {}

    def body(*refs):
        in_refs, ct_refs, g_refs = refs[:n_in], refs[n_in:n_in + n_ct], refs[n_in + n_ct:]
        vals = [r[...] for r in in_refs]

        def g_only(*diff):
            full = list(vals)
            for i, v in zip(gidx, diff):
                full[i] = v
            return f(*full)

        res, vjp = jax.vjp(g_only, *[vals[i].astype(F32) for i in gidx])
        grads = vjp(tuple(c[...].astype(r.dtype) for c, r in zip(ct_refs, res)))
        inner = pl.program_id(1)
        for i, g, gr in zip(gidx, grads, g_refs):
            if ins[i].kind == "tile":
                gr[...] = g.astype(gr.dtype)
            else:
                @pl.when(inner == 0)
                def _(gr=gr, g=g):
                    gr[...] = g.astype(gr.dtype)

                @pl.when(inner != 0)
                def _(gr=gr, g=g):
                    gr[...] += g.astype(gr.dtype)

    out_shape = tuple(jax.ShapeDtypeStruct(ins[i].gshape, gdt.get(i, F32)) for i in gidx)
    return pl.pallas_call(
        body,
        out_shape=out_shape,
        grid=grid,
        in_specs=[o.spec() for o in ins] + [o.spec() for o in cts],
        out_specs=tuple(pl.BlockSpec(ins[i].block, ins[i].gimap) for i in gidx),
        compiler_params=_params(("parallel", "arbitrary")),
        name=name,
    )(*[o.arr for o in ins], *[o.arr for o in cts])


def _rms(x, g):
    return x * lax.rsqrt(jnp.mean(x * x, axis=-1, keepdims=True) + EPS) * g


def _row_ops(arrs, tm, grads=None):
    grads = grads or [True] * len(arrs)
    return [Op(a, (tm, a.shape[1]), lambda o, i: (i, 0), "tile", g) for a, g in zip(arrs, grads)]


def _vec_op(v, grad=True):
    return Op(v, (1, v.shape[1]), lambda o, i: (0, 0), "param", grad)


def rmsnorm_fwd(h, g, name):
    T, D = h.shape
    tm = _pick(T, 512, 256, 128)
    f = lambda x, gg: (_rms(x, gg),)
    return tmap(f, (1, T // tm), _row_ops([h], tm) + [_vec_op(g)],
                [((T, D), _ACT, (tm, D), lambda o, i: (i, 0))], name)[0]


def rmsnorm_bwd(h, g, dhn, dres, name):
    T, D = h.shape
    tm = _pick(T, 512, 256, 128)
    f = lambda x, gg: (_rms(x, gg), x)
    return tmap_vjp(f, (1, T // tm), _row_ops([h], tm) + [_vec_op(g)], _row_ops([dhn, dres], tm), name)


def mlp_fwd(h, g_row, w_in, w_out, tag, cq=None):
    hn = rmsnorm_fwd(h, g_row, f"{tag}_norm")
    a, r2 = matmul(hn, w_in, "nn", (F32, _ACT), f"{tag}_in", b_cols4=True, cq=cq,
                   epilogue=lambda acc: (acc, jnp.square(jnp.maximum(acc, 0.0))))
    out = matmul(r2, w_out, "nn", F32, f"{tag}_out", cq=cq, epilogue=lambda acc, hh: (acc + hh,), extras=(h,))
    return out, (h, hn, a, r2)


def mlp_bwd(dout, saved, g_row, w_in, w_out, tag, cq=None):
    h, hn, a, r2 = saved
    da = matmul(dout, w_out, "nt", _ACT, f"{tag}_dact", cq=cq,
                epilogue=lambda acc, aa: (acc * (2.0 * jnp.maximum(aa, 0.0)),), extras=(a,))
    dw_out = matmul(r2, dout, "tn", GRAD_DT, f"{tag}_dwout", cq=cq)
    dw_in = matmul(hn, da, "tn", GRAD_DT, f"{tag}_dwin", out_cols4=True, cq=cq)
    dhn = matmul(da, w_in, "nt", F32, f"{tag}_dhn", b_cols4=True, cq=cq)
    dh, dg = rmsnorm_bwd(h, g_row, dhn, dout, f"{tag}_dnorm")
    return dh, dg, dw_in, dw_out


def _split3(x):
    hi = x.astype(BF16)
    r = x - hi.astype(F32)
    mid = r.astype(BF16)
    lo = (r - mid.astype(F32)).astype(BF16)
    return hi, mid, lo


def _cumdot(x, tri):
    return sum(lax.dot_general(p, tri, NN, preferred_element_type=F32) for p in _split3(x))


def _iotas():
    row = lax.broadcasted_iota(jnp.int32, (CHUNK, CHUNK), 0)
    col = lax.broadcasted_iota(jnp.int32, (CHUNK, CHUNK), 1)
    return row, col


_TQ = 256
_TK = 256
_DEAD = -88.0


def _sb_block(q, kblk, q0, k0, scale, row, col):
    z = _dot(q, kblk, NT) * scale
    e = jnp.exp(-jnp.abs(z))
    den = 1.0 + e
    sp = jnp.maximum(z, 0.0) + jnp.log(den)
    mask = (col + k0) < (row + q0)
    lg = jnp.where(mask, -sp, 0.0)
    beta = jnp.where(z >= 0, 1.0, e) / den
    return z, mask, lg, beta


def _attn_iotas(tq, tk):
    row = lax.broadcasted_iota(jnp.int32, (tq, tk), 0)
    col = lax.broadcasted_iota(jnp.int32, (tq, tk), 1)
    r2 = lax.broadcasted_iota(jnp.int32, (tk, tk), 0)
    c2 = lax.broadcasted_iota(jnp.int32, (tk, tk), 1)
    return row, col, r2, c2


def attn_fwd(qn, kn, v, name):
    T, W = qn.shape
    tq, tk = _pick(T, _TQ, CHUNK), _pick(T, _TK, CHUNK)
    H, NQ, per = W // LANES, T // tq, tq // tk
    assert tq % tk == 0 and T // tk <= LANES
    scale = LANES ** -0.5

    def body(q_ref, k_ref, v_ref, o_ref, r_ref, acc_ref, run_ref):
        qi = pl.program_id(1)
        q = q_ref[...]
        row, col, r2, c2 = _attn_iotas(tq, tk)
        suffix = (r2 >= c2).astype(_MXU)
        lane_q = lax.broadcasted_iota(jnp.int32, (tq, LANES), 1)
        acc_ref[...] = jnp.zeros_like(acc_ref)
        run_ref[...] = jnp.zeros_like(run_ref)
        r_ref[...] = jnp.full(r_ref.shape, -1e30, F32)

        def step(carry):
            kb, _ = carry
            off = pl.multiple_of(kb * tk, tk)
            run = run_ref[...]
            z, mask, lg, _ = _sb_block(q, k_ref[pl.ds(off, tk), :], qi * tq, off, scale, row, col)
            r_ref[...] = jnp.where(lane_q == kb, run, r_ref[...])
            cl = _cumdot(lg, suffix) + run
            a = jnp.exp(jnp.where(mask, z + cl, -1e30))
            acc_ref[...] += _dot(a, v_ref[pl.ds(off, tk), :])
            run = run + jnp.sum(lg, axis=1, keepdims=True)
            run_ref[...] = run
            return kb - 1, jnp.max(run) > _DEAD

        lax.while_loop(lambda c: (c[0] >= 0) & c[1], step, ((qi + 1) * per - 1, True))
        o_ref[...] = acc_ref[...].astype(o_ref.dtype)

    qspec = pl.BlockSpec((tq, LANES), lambda h, i: (i, h))
    kvspec = pl.BlockSpec((T, LANES), lambda h, i: (0, h))
    return pl.pallas_call(
        body,
        out_shape=(jax.ShapeDtypeStruct((T, W), _ACT), jax.ShapeDtypeStruct((H, T, LANES), F32)),
        grid=(H, NQ),
        in_specs=[qspec, kvspec, kvspec],
        out_specs=(qspec, pl.BlockSpec((None, tq, LANES), lambda h, i: (h, i, 0))),
        scratch_shapes=[pltpu.VMEM((tq, LANES), F32), pltpu.VMEM((tq, 1), F32)],
        compiler_params=_params(("parallel", "parallel")),
        name=name,
    )(qn, kn, v)


def attn_bwd(qn, kn, v, do, rblk, name):
    T, W = qn.shape
    tq, tk = _pick(T, _TQ, CHUNK), _pick(T, _TK, CHUNK)
    H, NQ, per = W // LANES, T // tq, tq // tk
    scale = LANES ** -0.5

    def body(q_ref, k_ref, v_ref, do_ref, r_ref, dq_ref, dk_ref, dv_ref, g_ref):
        qi = pl.program_id(1)

        @pl.when(qi == 0)
        def _():
            dk_ref[...] = jnp.zeros_like(dk_ref)
            dv_ref[...] = jnp.zeros_like(dv_ref)

        q = q_ref[...]
        dout = do_ref[...]
        rt = r_ref[...]
        row, col, r2, c2 = _attn_iotas(tq, tk)
        suffix = (r2 >= c2).astype(_MXU)
        prefix = (r2 <= c2).astype(_MXU)
        kend = (qi + 1) * per - 1
        lane = lax.broadcasted_iota(jnp.int32, (1, LANES), 1)
        lane_q = lax.broadcasted_iota(jnp.int32, (tq, LANES), 1)
        unvisited = (jnp.max(rt, axis=0, keepdims=True) < -1e29) & (lane <= kend)
        start = jnp.sum(unvisited.astype(jnp.int32))

        dq_ref[...] = jnp.zeros_like(dq_ref)
        g_ref[...] = jnp.zeros_like(g_ref)

        @pl.loop(start, kend + 1)
        def _(kb):
            gsum = g_ref[...]
            off = pl.multiple_of(kb * tk, tk)
            kblk = k_ref[pl.ds(off, tk), :]
            vblk = v_ref[pl.ds(off, tk), :]
            z, mask, lg, beta = _sb_block(q, kblk, qi * tq, off, scale, row, col)
            run = jnp.sum(jnp.where(lane_q == kb, rt, 0.0), axis=1, keepdims=True)
            cl = _cumdot(lg, suffix) + run
            a = jnp.exp(jnp.where(mask, z + cl, -1e30))
            e = _dot(dout, vblk, NT) * a
            f = _cumdot(e, prefix) + gsum
            dz = jnp.where(mask, e - beta * f, 0.0) * scale
            dk_ref[pl.ds(off, tk), :] += _dot(dz, q, TN)
            dv_ref[pl.ds(off, tk), :] += _dot(a, dout, TN)
            dq_ref[...] += _dot(dz, kblk)
            g_ref[...] = gsum + jnp.sum(e, axis=1, keepdims=True)

    qspec = pl.BlockSpec((tq, LANES), lambda h, i: (i, h))
    kvspec = pl.BlockSpec((T, LANES), lambda h, i: (0, h))
    big = jax.ShapeDtypeStruct((T, W), F32)
    return pl.pallas_call(
        body,
        out_shape=(big, big, big),
        grid=(H, NQ),
        in_specs=[qspec, kvspec, kvspec, qspec, pl.BlockSpec((None, tq, LANES), lambda h, i: (h, i, 0))],
        out_specs=(qspec, kvspec, kvspec),
        scratch_shapes=[pltpu.VMEM((tq, 1), F32)],
        compiler_params=_params(("parallel", "arbitrary")),
        name=name,
    )(qn, kn, v, do, rblk)


def _qk_ops(qkv, qg, kg, tm, grad):
    T, W3 = qkv.shape
    H, NT_ = W3 // (3 * LANES), T // tm
    W = H * LANES

    def part(p):
        return Op(qkv, (tm, LANES), lambda o, n: (lax.rem(n, NT_), p * H + lax.div(n, NT_)), "tile", grad,
                  gshape=(T, W), gimap=lambda o, n: (lax.rem(n, NT_), lax.div(n, NT_)))

    vec = lambda g: Op(g, (1, LANES), lambda o, n: (0, 0), "param", grad)
    return [part(0), part(1), part(2), vec(qg), vec(kg)], (1, H * NT_), H, NT_, W


def _qk_f(q, k, v, qg, kg):
    return _rms(q, qg), _rms(k, kg), v


def qknorm_fwd(qkv, qg, kg, name):
    T = qkv.shape[0]
    tm = _pick(T, 512, 256, 128)
    ins, grid, H, NT_, W = _qk_ops(qkv, qg, kg, tm, False)
    out = ((T, W), _ACT, (tm, LANES), lambda o, n: (lax.rem(n, NT_), lax.div(n, NT_)))
    return tmap(_qk_f, grid, ins, [out, out, out], name)


def qknorm_bwd(qkv, qg, kg, dq, dk, dv, name):
    T = qkv.shape[0]
    tm = _pick(T, 512, 256, 128)
    ins, grid, H, NT_, W = _qk_ops(qkv, qg, kg, tm, True)
    cts = [Op(c, (tm, LANES), lambda o, n: (lax.rem(n, NT_), lax.div(n, NT_))) for c in (dq, dk, dv)]
    return tmap_vjp(_qk_f, grid, ins, cts, name, grad_dtypes={0: _ACT, 1: _ACT, 2: _ACT})


def sb_fwd(h, g_row, w_qkv, qg, kg, w_o, tag, cq=None):
    hn = rmsnorm_fwd(h, g_row, f"{tag}_norm")
    qkv = matmul(hn, w_qkv, "nn", F32, f"{tag}_qkv", b_cols4=True, cq=cq)
    qn, kn, v = qknorm_fwd(qkv, qg, kg, f"{tag}_qknorm")
    o, rblk = attn_fwd(qn, kn, v, f"{tag}_attn")
    out = matmul(o, w_o, "nn", F32, f"{tag}_wo", cq=cq, epilogue=lambda acc, hh: (acc + hh,), extras=(h,))
    return out, (h, hn, qkv, qn, kn, v, o, rblk)


def sb_bwd(dout, saved, g_row, w_qkv, qg, kg, w_o, tag, cq=None):
    h, hn, qkv, qn, kn, v, o, rblk = saved
    do = matmul(dout, w_o, "nt", _ACT, f"{tag}_do")
    dw_o = matmul(o, dout, "tn", GRAD_DT, f"{tag}_dwo")
    dqn, dkn, dv = attn_bwd(qn, kn, v, do, rblk, f"{tag}_dattn")
    dq, dk, dvv, dqg, dkg = qknorm_bwd(qkv, qg, kg, dqn, dkn, dv, f"{tag}_dqknorm")
    dqkv = jnp.concatenate([dq, dk, dvv], axis=1)
    dw_qkv = matmul(hn, dqkv, "tn", GRAD_DT, f"{tag}_dwqkv", out_cols4=True, cq=cq)
    dhn = matmul(dqkv, w_qkv, "nt", F32, f"{tag}_dhn", b_cols4=True, cq=cq)
    dh, dg = rmsnorm_bwd(h, g_row, dhn, dout, f"{tag}_dnorm")
    return dh, dg, dw_qkv, dqg, dkg, dw_o


@functools.partial(jax.custom_vjp, nondiff_argnums=(2,))
def _dotv(a, b, mode):
    return _dot(a, b, {"nn": NN, "nt": NT, "tn": TN}[mode])


def _dotv_fwd(a, b, mode):
    return _dotv(a, b, mode), (a, b)


def _dotv_bwd(mode, res, g):
    a, b = res
    if mode == "nn":
        return _dot(g, b, NT), _dot(a, g, TN)
    if mode == "nt":
        return _dot(g, b, NN), _dot(g, a, TN)
    return _dot(b, g, NT), _dot(a, g, NN)


_dotv.defvjp(_dotv_fwd, _dotv_bwd)


def _gelu(x):
    return 0.5 * x * (1.0 + lax.erf(x * (2.0 ** -0.5)))


def _gm1_f(au, av, vg):
    return _gelu(au), _rms(_gelu(av), vg)


def _gm1_ops(a, vg, tm, grad):
    T, W2 = a.shape
    W = W2 // 2
    part = lambda p: Op(a, (tm, W), lambda o, i: (i, p), "tile", grad, gshape=(T, W), gimap=lambda o, i: (i, 0))
    return [part(0), part(1), _vec_op(vg, grad)], (1, T // tm), W


_GM_ROWS = 1024


def _gm_specs(T, W, G):
    rows = _pick(T, _GM_ROWS, 512, 256, CHUNK)
    blk = pl.BlockSpec((rows, LANES), lambda g, c: (c, g))
    wspec = pl.BlockSpec((None, CHUNK, CHUNK), lambda g, c: (g, 0, 0))
    bspec = pl.BlockSpec((None, CHUNK, 1), lambda g, c: (g, 0, 0))
    return rows, blk, wspec, bspec, (G, T // rows)


def gm_mix_fwd(u, vn, ws, bcol, name):
    T, W = u.shape
    rows, blk, wspec, bspec, grid = _gm_specs(T, W, W // LANES)

    def body(u_ref, v_ref, w_ref, b_ref, y_ref):
        r, c = _iotas()
        w = jnp.where(r >= c, w_ref[...], 0.0).astype(_MXU)
        for k in range(rows // CHUNK):
            sl = pl.ds(k * CHUNK, CHUNK)
            y_ref[sl, :] = (u_ref[sl, :] * (_dot(w, v_ref[sl, :]) + b_ref[...])).astype(y_ref.dtype)

    return pl.pallas_call(body, out_shape=jax.ShapeDtypeStruct((T, W), _ACT), grid=grid,
                          in_specs=[blk, blk, wspec, bspec], out_specs=blk,
                          compiler_params=_params(("parallel", "parallel")), name=name)(u, vn, ws, bcol)


def gm_mix_bwd(u, vn, ws, bcol, dy, name):
    T, W = u.shape
    G = W // LANES
    rows, blk, wspec, bspec, grid = _gm_specs(T, W, G)

    def body(u_ref, v_ref, w_ref, b_ref, dy_ref, du_ref, dv_ref, dw_ref, db_ref):
        r, c = _iotas()
        tri = r >= c
        w = jnp.where(tri, w_ref[...], 0.0).astype(_MXU)
        dw = jnp.zeros((CHUNK, CHUNK), F32)
        db = jnp.zeros((CHUNK, 1), F32)
        for k in range(rows // CHUNK):
            sl = pl.ds(k * CHUNK, CHUNK)
            v = v_ref[sl, :]
            g = dy_ref[sl, :]
            du_ref[sl, :] = g * (_dot(w, v) + b_ref[...])
            dm = g * u_ref[sl, :]
            dv_ref[sl, :] = _dot(w, dm, TN)
            dw = dw + _dot(dm, v, NT)
            db = db + jnp.sum(dm, axis=1, keepdims=True)
        dw = jnp.where(tri, dw, 0.0)

        @pl.when(pl.program_id(1) == 0)
        def _():
            dw_ref[...] = dw
            db_ref[...] = db

        @pl.when(pl.program_id(1) != 0)
        def _():
            dw_ref[...] += dw
            db_ref[...] += db

    big = jax.ShapeDtypeStruct((T, W), F32)
    return pl.pallas_call(
        body,
        out_shape=(big, big, jax.ShapeDtypeStruct((G, CHUNK, CHUNK), F32), jax.ShapeDtypeStruct((G, CHUNK, 1), F32)),
        grid=grid, in_specs=[blk, blk, wspec, bspec, blk], out_specs=(blk, blk, wspec, bspec),
        compiler_params=_params(("parallel", "arbitrary")), name=name)(u, vn, ws, bcol, dy)


def gm_fwd(h, g_row, w_in, vg, ws, bcol, w_o, tag, cq=None):
    T = h.shape[0]
    tm = _pick(T, 256, 128)
    hn = rmsnorm_fwd(h, g_row, f"{tag}_norm")
    a = matmul(hn, w_in, "nn", F32, f"{tag}_in", b_cols4=True, cq=cq)
    ins, grid, W = _gm1_ops(a, vg, tm, False)
    rows = lambda dt: ((T, W), dt, (tm, W), lambda o, i: (i, 0))
    u, vn = tmap(_gm1_f, grid, ins, [rows(F32), rows(_ACT)], f"{tag}_act")
    y = gm_mix_fwd(u, vn, ws, bcol, f"{tag}_mix")
    out = matmul(y, w_o, "nn", F32, f"{tag}_wo", cq=cq, epilogue=lambda acc, hh: (acc + hh,), extras=(h,))
    return out, (h, hn, a, u, vn, y)


def gm_bwd(dout, saved, g_row, w_in, vg, ws, bcol, w_o, tag, cq=None):
    h, hn, a, u, vn, y = saved
    T = h.shape[0]
    tm = _pick(T, 256, 128)
    dy = matmul(dout, w_o, "nt", F32, f"{tag}_dy")
    dw_o = matmul(y, dout, "tn", GRAD_DT, f"{tag}_dwo")
    du, dvn, dws, dbcol = gm_mix_bwd(u, vn, ws, bcol, dy, f"{tag}_dmix")
    ins, grid, W = _gm1_ops(a, vg, tm, True)
    dau, dav, dvg = tmap_vjp(_gm1_f, grid, ins, _row_ops([du, dvn], tm), f"{tag}_dact", grad_dtypes={0: _ACT, 1: _ACT})
    da = jnp.concatenate([dau, dav], axis=1)
    dw_in = matmul(hn, da, "tn", GRAD_DT, f"{tag}_dwin", out_cols4=True)
    dhn = matmul(da, w_in, "nt", F32, f"{tag}_dhn", b_cols4=True)
    dh, dg = rmsnorm_bwd(h, g_row, dhn, dout, f"{tag}_dnorm")
    return dh, dg, dw_in, dvg, dws, dbcol, dw_o


@jax.custom_vjp
def _softplus(x):
    return jnp.maximum(x, 0.0) + jnp.log(1.0 + jnp.exp(-jnp.abs(x)))


_softplus.defvjp(lambda x: (_softplus(x), x), lambda x, g: (g * lax.logistic(x),))


def _silu(x):
    return x * lax.logistic(x)


def _shift_impl(x, s, down):
    n = x.shape[0]
    r = lax.broadcasted_iota(jnp.int32, x.shape, 0)
    if down:
        return jnp.where(r >= s, pltpu.roll(x, s, 0), 0.0)
    return jnp.where(r < n - s, pltpu.roll(x, n - s, 0), 0.0)


@functools.partial(jax.custom_vjp, nondiff_argnums=(1,))
def _shift_down(x, s):
    return _shift_impl(x, s, True)


_shift_down.defvjp(lambda x, s: (_shift_impl(x, s, True), None), lambda s, _, g: (_shift_impl(g, s, False),))


def _conv_f(x, w, b):
    k_id = lax.broadcasted_iota(jnp.int32, w.shape, 0)
    y = b + jnp.sum(jnp.where(k_id == SSD_CONV - 1, w, 0.0), axis=0, keepdims=True) * x
    for k in range(SSD_CONV - 1):
        wk = jnp.sum(jnp.where(k_id == k, w, 0.0), axis=0, keepdims=True)
        y = y + wk * _shift_down(x, SSD_CONV - 1 - k)
    return (_silu(y),)


def _conv_ops(zx, conv_w, conv_b, wi, grad):
    T = zx.shape[0]
    cd = conv_w.shape[1]
    cw = LANES
    off = wi // cw
    return [Op(zx, (T, cw), lambda o, j: (0, off + j), "tile", grad, gshape=(T, cd), gimap=lambda o, j: (0, j)),
            Op(conv_w, (SSD_CONV, cw), lambda o, j: (0, j), "tile", grad),
            Op(conv_b, (1, cw), lambda o, j: (0, j), "tile", grad)], (1, cd // cw), (T, cw)


def _dt_f(dtr, bias):
    return (_softplus(dtr + bias),)


def _cumdot_left(tri, x):
    return sum(lax.dot_general(tri, p, NN, preferred_element_type=F32) for p in _split3(x))


@jax.custom_vjp
def _cumsum_rows(x):
    row, col = _iotas()
    return _cumdot_left((row >= col).astype(_MXU), x)


def _cumsum_rows_bwd(_, g):
    row, col = _iotas()
    return (_cumdot_left((row <= col).astype(_MXU), g),)


_cumsum_rows.defvjp(lambda x: (_cumsum_rows(x), None), _cumsum_rows_bwd)


def _ssd_chunk(xps, dt, bm, cm, sps, alog, hid_base):
    row, col = _iotas()
    half = SSD_HEAD_DIM
    rcol = lax.broadcasted_iota(jnp.int32, (CHUNK, 1), 0)
    colpick = lambda m, hid: jnp.sum(jnp.where(col == hid, m, 0.0), axis=1, keepdims=True)
    rowpick = lambda m, hid: jnp.sum(jnp.where(row == hid, m, 0.0), axis=0, keepdims=True)
    last = lambda v: jnp.sum(jnp.where(rcol == CHUNK - 1, v, 0.0), axis=0, keepdims=True)
    acum = _cumsum_rows(dt * (-jnp.exp(alog)))
    acum_t = acum.T
    cb = _dotv(cm, bm, "nt")
    tri = row >= col
    lo = col < half
    ys, snews = [], []
    for p, (xp, sp) in enumerate(zip(xps, sps)):
        h0, h1 = hid_base + 2 * p, hid_base + 2 * p + 1
        ac0, ac1 = colpick(acum, h0), colpick(acum, h1)
        m0 = cb * jnp.exp(jnp.where(tri, ac0 - rowpick(acum_t, h0), -1e30))
        m1 = cb * jnp.exp(jnp.where(tri, ac1 - rowpick(acum_t, h1), -1e30))
        xs = xp * jnp.where(lo, colpick(dt, h0), colpick(dt, h1))
        ydiag = jnp.where(lo, _dotv(m0, xs, "nn"), _dotv(m1, xs, "nn"))
        yoff = jnp.where(lo, jnp.exp(ac0), jnp.exp(ac1)) * _dotv(cm, sp, "nt")
        al0, al1 = last(ac0), last(ac1)
        xsd = xs * jnp.where(lo, jnp.exp(al0 - ac0), jnp.exp(al1 - ac1))
        snew = jnp.where(row < half, jnp.exp(al0), jnp.exp(al1)) * sp + _dotv(xsd, bm, "tn")
        ys.append(ydiag + yoff)
        snews.append(snew)
    return ys, snews


def _ssd_dims(xact, wi):
    T, cd = xact.shape
    G = (cd - wi) // (2 * SSD_STATE)
    hpg = wi // SSD_HEAD_DIM // G
    assert hpg % 2 == 0 and SSD_STATE == LANES
    return T, G, hpg, hpg // 2, T // CHUNK, wi // G


def ssd_scan_fwd(xact, dt, alog, wi, name):
    T, G, hpg, NP, NC, gw = _ssd_dims(xact, wi)
    bo, co = wi // LANES, wi // LANES + G

    def body(x_ref, b_ref, c_ref, dt_ref, al_ref, y_ref, st_ref, s_ref):
        g, c = pl.program_id(0), pl.program_id(1)

        @pl.when(c == 0)
        def _():
            s_ref[...] = jnp.zeros_like(s_ref)

        st_ref[...] = s_ref[...]
        xps = [x_ref[:, p * LANES:(p + 1) * LANES] for p in range(NP)]
        sps = [s_ref[p] for p in range(NP)]
        ys, snews = _ssd_chunk(xps, dt_ref[...], b_ref[...], c_ref[...], sps, al_ref[...], g * hpg)
        for p in range(NP):
            y_ref[:, p * LANES:(p + 1) * LANES] = ys[p]
            s_ref[p] = snews[p]

    return pl.pallas_call(
        body,
        out_shape=(jax.ShapeDtypeStruct((T, wi), F32), jax.ShapeDtypeStruct((G, NC, NP, LANES, SSD_STATE), F32)),
        grid=(G, NC),
        in_specs=[pl.BlockSpec((CHUNK, gw), lambda g, c: (c, g)),
                  pl.BlockSpec((CHUNK, LANES), lambda g, c: (c, bo + g)),
                  pl.BlockSpec((CHUNK, LANES), lambda g, c: (c, co + g)),
                  pl.BlockSpec((CHUNK, LANES), lambda g, c: (c, 0)),
                  pl.BlockSpec((1, LANES), lambda g, c: (0, 0))],
        out_specs=(pl.BlockSpec((CHUNK, gw), lambda g, c: (c, g)),
                   pl.BlockSpec((None, None, NP, LANES, SSD_STATE), lambda g, c: (g, c, 0, 0, 0))),
        scratch_shapes=[pltpu.VMEM((NP, LANES, SSD_STATE), F32)],
        compiler_params=_params(("parallel", "arbitrary")),
        name=name,
    )(xact, xact, xact, dt, alog)


def ssd_scan_bwd(xact, dt, alog, states, dy, wi, name):
    T, G, hpg, NP, NC, gw = _ssd_dims(xact, wi)
    bo, co = wi // LANES, wi // LANES + G
    rev = lambda c: NC - 1 - c

    def body(x_ref, b_ref, c_ref, dt_ref, al_ref, st_ref, dy_ref, dx_ref, db_ref, dc_ref, ddt_ref, dal_ref, ds_ref):
        g, c = pl.program_id(0), pl.program_id(1)

        @pl.when(c == 0)
        def _():
            ds_ref[...] = jnp.zeros_like(ds_ref)

        xps = [x_ref[:, p * LANES:(p + 1) * LANES] for p in range(NP)]
        sps = [st_ref[p] for p in range(NP)]
        f = lambda xps_, dt_, bm_, cm_, sps_, al_: _ssd_chunk(xps_, dt_, bm_, cm_, sps_, al_, g * hpg)
        _, vjp = jax.vjp(f, xps, dt_ref[...], b_ref[...], c_ref[...], sps, al_ref[...])
        dys = [dy_ref[:, p * LANES:(p + 1) * LANES] for p in range(NP)]
        dxps, ddt, dbm, dcm, dsps, dal = vjp((dys, [ds_ref[p] for p in range(NP)]))
        for p in range(NP):
            dx_ref[:, p * LANES:(p + 1) * LANES] = dxps[p]
            ds_ref[p] = dsps[p]
        db_ref[...] = dbm
        dc_ref[...] = dcm
        ddt_ref[...] = ddt

        @pl.when(c == 0)
        def _():
            dal_ref[...] = dal

        @pl.when(c != 0)
        def _():
            dal_ref[...] += dal

    gb = G * SSD_STATE
    return pl.pallas_call(
        body,
        out_shape=(jax.ShapeDtypeStruct((T, wi), F32), jax.ShapeDtypeStruct((T, gb), F32), jax.ShapeDtypeStruct((T, gb), F32),
                   jax.ShapeDtypeStruct((G, T, LANES), F32), jax.ShapeDtypeStruct((G, 1, LANES), F32)),
        grid=(G, NC),
        in_specs=[pl.BlockSpec((CHUNK, gw), lambda g, c: (rev(c), g)),
                  pl.BlockSpec((CHUNK, LANES), lambda g, c: (rev(c), bo + g)),
                  pl.BlockSpec((CHUNK, LANES), lambda g, c: (rev(c), co + g)),
                  pl.BlockSpec((CHUNK, LANES), lambda g, c: (rev(c), 0)),
                  pl.BlockSpec((1, LANES), lambda g, c: (0, 0)),
                  pl.BlockSpec((None, None, NP, LANES, SSD_STATE), lambda g, c: (g, rev(c), 0, 0, 0)),
                  pl.BlockSpec((CHUNK, gw), lambda g, c: (rev(c), g))],
        out_specs=(pl.BlockSpec((CHUNK, gw), lambda g, c: (rev(c), g)),
                   pl.BlockSpec((CHUNK, LANES), lambda g, c: (rev(c), g)),
                   pl.BlockSpec((CHUNK, LANES), lambda g, c: (rev(c), g)),
                   pl.BlockSpec((None, CHUNK, LANES), lambda g, c: (g, rev(c), 0)),
                   pl.BlockSpec((None, 1, LANES), lambda g, c: (g, 0, 0))),
        scratch_shapes=[pltpu.VMEM((NP, LANES, SSD_STATE), F32)],
        compiler_params=_params(("parallel", "arbitrary")),
        name=name,
    )(xact, xact, xact, dt, alog, states, dy)


def _post_f(y, x, z, dcol, ng):
    return (_rms((y + dcol * x) * _silu(z), ng),)


def _post_ops(yssd, xact, zx, dcol, ng, G, tm, grad):
    T, wi = yssd.shape
    gw = wi // G
    blk = lambda a: Op(a, (tm, gw), lambda g, i: (i, g), "tile", grad, gshape=(T, wi))
    vec = lambda v: Op(v, (1, gw), lambda g, i: (0, g), "param", grad)
    return [blk(yssd), blk(xact), blk(zx), vec(dcol), vec(ng)], (G, T // tm), gw


def ssd_fwd(h, g_row, w_zx, w_dt, conv_w, conv_b, dtb, alog, dcol, ng, w_o, tag, cq=None):
    T = h.shape[0]
    wi = ng.shape[1]
    tm = _pick(T, 256, 128)
    hn = rmsnorm_fwd(h, g_row, f"{tag}_norm")
    zx = matmul(hn, w_zx, "nn", F32, f"{tag}_inzx", cq=cq)
    dtr = matmul(hn, w_dt, "nn", F32, f"{tag}_indt")
    ins, grid, blk = _conv_ops(zx, conv_w, conv_b, wi, False)
    cd = conv_w.shape[1]
    xact = tmap(_conv_f, grid, ins, [((T, cd), F32, blk, lambda o, j: (0, j))], f"{tag}_conv")[0]
    dt = tmap(_dt_f, (1, T // tm), _row_ops([dtr], tm) + [_vec_op(dtb)],
              [((T, LANES), F32, (tm, LANES), lambda o, i: (i, 0))], f"{tag}_dt")[0]
    yssd, states = ssd_scan_fwd(xact, dt, alog, wi, f"{tag}_scan")
    G = states.shape[0]
    ins, grid, gw = _post_ops(yssd, xact, zx, dcol, ng, G, tm, False)
    yn = tmap(_post_f, grid, ins, [((T, wi), _ACT, (tm, gw), lambda g, i: (i, g))], f"{tag}_post")[0]
    out = matmul(yn, w_o, "nn", F32, f"{tag}_wo", cq=cq, epilogue=lambda acc, hh: (acc + hh,), extras=(h,))
    return out, (h, hn, zx, dtr, xact, dt, yssd, states, yn)


def ssd_bwd(dout, saved, g_row, w_zx, w_dt, conv_w, conv_b, dtb, alog, dcol, ng, w_o, tag, cq=None):
    h, hn, zx, dtr, xact, dt, yssd, states, yn = saved
    T = h.shape[0]
    wi = ng.shape[1]
    tm = _pick(T, 256, 128)
    G = states.shape[0]
    dyn = matmul(dout, w_o, "nt", F32, f"{tag}_dyn")
    dw_o = matmul(yn, dout, "tn", GRAD_DT, f"{tag}_dwo")
    ins, grid, gw = _post_ops(yssd, xact, zx, dcol, ng, G, tm, True)
    dyssd, dxi_skip, dz, ddcol, dng = tmap_vjp(_post_f, grid, ins, [Op(dyn, (tm, gw), lambda g, i: (i, g))],
                                                f"{tag}_dpost", grad_dtypes={2: _ACT})
    dxi, dbm, dcm, ddt_g, dalog_g = ssd_scan_bwd(xact, dt, alog, states, dyssd, wi, f"{tag}_dscan")
    dxact = jnp.concatenate([dxi + dxi_skip, dbm, dcm], axis=1)
    ddt = jnp.sum(ddt_g, axis=0)
    dalog = jnp.sum(dalog_g, axis=0)
    ins, grid, blk = _conv_ops(zx, conv_w, conv_b, wi, True)
    dxbc, dconv_w, dconv_b = tmap_vjp(_conv_f, grid, ins, [Op(dxact, blk, lambda o, j: (0, j))], f"{tag}_dconv",
                                      grad_dtypes={0: _ACT})
    ddtr, ddtb = tmap_vjp(_dt_f, (1, T // tm), _row_ops([dtr], tm) + [_vec_op(dtb)], _row_ops([ddt], tm), f"{tag}_ddt",
                          grad_dtypes={0: _ACT})
    dzx = jnp.concatenate([dz, dxbc], axis=1)
    dw_zx = matmul(hn, dzx, "tn", GRAD_DT, f"{tag}_dwzx")
    dw_dt = matmul(hn, ddtr, "tn", GRAD_DT, f"{tag}_dwdt")
    dhn1 = matmul(ddtr, w_dt, "nt", F32, f"{tag}_dhn1")
    dhn = matmul(dzx, w_zx, "nt", F32, f"{tag}_dhn", epilogue=lambda acc, e: (acc + e,), extras=(dhn1,))
    dh, dg = rmsnorm_bwd(h, g_row, dhn, dout, f"{tag}_dnorm")
    return dh, dg, dw_zx, dw_dt, dconv_w, dconv_b, ddtb, dalog, ddcol, dng, dw_o


def loss_head(y, target, name):
    T, D = y.shape
    tm = _pick(T, 512, 256, 128)

    def body(y_ref, t_ref, dy_ref, part_ref):
        d = y_ref[...] - t_ref[...]
        dy_ref[...] = d * (1.0 / D)
        s = jnp.sum(d * d, axis=0, keepdims=True) * (0.5 / D)

        @pl.when(pl.program_id(0) == 0)
        def _():
            part_ref[...] = s

        @pl.when(pl.program_id(0) != 0)
        def _():
            part_ref[...] += s

    rows = pl.BlockSpec((tm, D), lambda i: (i, 0))
    return pl.pallas_call(
        body,
        out_shape=(jax.ShapeDtypeStruct((T, D), F32), jax.ShapeDtypeStruct((1, D), F32)),
        grid=(T // tm,),
        in_specs=[rows, rows],
        out_specs=(rows, pl.BlockSpec((1, D), lambda i: (0, 0))),
        compiler_params=_params(("arbitrary",)),
        name=name,
    )(y, target)


def _row_tile(R, C, itemsize=4, target=1 << 20):
    for t in (1024, 512, 256, 128, 64, 32, 16, 8):
        if R % t == 0 and t * C * itemsize <= target:
            return t
    return R


def adamw(w, g, m, v, name):
    R, C = w.shape
    tr = _row_tile(R, C)
    c1 = 1.0 - ADAM_B1 ** ADAM_STEP
    c2 = 1.0 - ADAM_B2 ** ADAM_STEP

    def body(w_ref, g_ref, m_ref, v_ref, d_ref, nm_ref, nv_ref):
        gg = g_ref[...]
        nm = ADAM_B1 * m_ref[...] + (1.0 - ADAM_B1) * gg
        nv = ADAM_B2 * v_ref[...] + (1.0 - ADAM_B2) * jnp.square(gg)
        d_ref[...] = -ADAM_LR * ((nm / c1) / (jnp.sqrt(nv / c2) + ADAM_EPS) + ADAM_WD * w_ref[...])
        nm_ref[...] = nm
        nv_ref[...] = nv

    spec = pl.BlockSpec((tr, C), lambda i: (i, 0))
    sds = jax.ShapeDtypeStruct((R, C), F32)
    return pl.pallas_call(body, out_shape=(sds, sds, sds), grid=(R // tr,), in_specs=[spec] * 4, out_specs=(spec,) * 3,
                          compiler_params=_params(("parallel",)), name=name)(w, g, m, v)


def pair_sum(gfull, recv, name):
    _, _, R, C = gfull.shape
    tr = _row_tile(R, C, 2)

    def body(g_ref, p_ref, o_ref):
        c = lax.axis_index("c")
        o_ref[...] = (g_ref[c].astype(F32) + p_ref[...].astype(F32)).astype(o_ref.dtype)

    return pl.pallas_call(
        body,
        out_shape=jax.ShapeDtypeStruct((4, R, C), gfull.dtype),
        grid=(4, R // tr),
        in_specs=[pl.BlockSpec((None, 2, tr, C), lambda p, i: (p, 0, i, 0)), pl.BlockSpec((None, tr, C), lambda p, i: (p, i, 0))],
        out_specs=pl.BlockSpec((None, tr, C), lambda p, i: (p, i, 0)),
        compiler_params=_params(("parallel", "parallel")),
        name=name,
    )(gfull, recv)


def chip_sum(s, recv, name):
    _, R, C = s.shape
    tr = _row_tile(R, C, 2, 1 << 19)

    def body(c_ref, s_ref, p_ref, o_ref):
        x, y, _ = _xyc()
        acc = s_ref[2 * x + y].astype(F32)
        for r in range(3):
            acc = acc + p_ref[r].astype(F32)
        o_ref[...] = acc

    grid_spec = pltpu.PrefetchScalarGridSpec(
        num_scalar_prefetch=1,
        grid=(R // tr,),
        in_specs=[pl.BlockSpec((4, tr, C), lambda i, c: (0, i, 0)), pl.BlockSpec((3, tr, C), lambda i, c: (0, i, 0))],
        out_specs=pl.BlockSpec((None, tr, C), lambda i, c: (c[0], i, 0)),
    )
    return pl.pallas_call(
        body,
        out_shape=jax.ShapeDtypeStruct((2, R, C), F32),
        grid_spec=grid_spec,
        compiler_params=_params(("arbitrary",)),
        name=name,
    )(lax.axis_index("c").reshape(1).astype(jnp.int32), s, recv)


def sum8(g, name):
    _, R, C = g.shape

    def body(g_ref, o_ref):
        acc = g_ref[0]
        for d in range(1, 8):
            acc = acc + g_ref[d]
        o_ref[...] = acc

    return pl.pallas_call(body, out_shape=jax.ShapeDtypeStruct((R, C), F32), name=name,
                          compiler_params=pltpu.CompilerParams(vmem_limit_bytes=_VMEM_LIMIT))(g)


def gather_small(x, name):
    R, C = x.shape

    def body(x_ref, out_ref, send_sems, recv_sems, local_sem):
        x_, y_, c_ = _xyc()
        me, sibling = (x_, y_, c_), (x_, y_, 1 - c_)
        chips = [(_flip(x_, fx), _flip(y_, fy)) for fx, fy in _REL]
        slot = lambda px, py, pc: out_ref.at[4 * px + 2 * py + pc]

        def copy(k, block, to, src=None):
            return _remote(slot(*block) if src is None else src, slot(*block), send_sems.at[k], recv_sems.at[k], to)

        mine = pltpu.make_async_copy(x_ref, slot(*me), local_sem)
        mine.start()
        first = [copy(0, me, sibling, src=x_ref)]
        first += [copy(1 + j, me, (*chip, c_), src=x_ref) for j, chip in enumerate(chips)]
        for cp in first:
            cp.start()
        passed = [copy(4 + j, (*chip, c_), sibling) for j, chip in enumerate(chips)]
        for j, chip in enumerate(chips):
            copy(1 + j, (*chip, c_), me).wait_recv()
            passed[j].start()
        copy(0, sibling, me).wait_recv()
        for j, chip in enumerate(chips):
            copy(4 + j, (*chip, 1 - c_), me).wait_recv()
        for cp in first + passed:
            cp.wait_send()
        mine.wait()

    return pl.pallas_call(
        body,
        out_shape=jax.ShapeDtypeStruct((8, R, C), x.dtype),
        in_specs=[pl.BlockSpec(memory_space=pltpu.VMEM)],
        out_specs=pl.BlockSpec(memory_space=pltpu.VMEM),
        scratch_shapes=[pltpu.SemaphoreType.DMA((7,)), pltpu.SemaphoreType.DMA((7,)), pltpu.SemaphoreType.DMA],
        compiler_params=pltpu.CompilerParams(vmem_limit_bytes=_VMEM_LIMIT),
        name=name,
    )(x)


def gather_weights(halves, name):
    n = len(halves)

    def body(*refs):
        ins, outs = refs[:n], refs[n:2 * n]
        send_sems, recv_sems = refs[2 * n:]
        x_, y_, c_ = _xyc()
        sibling = (x_, y_, 1 - c_)
        chips = [(_flip(x_, fx), _flip(y_, fy)) for fx, fy in _REL]
        started = []
        for i in range(n):
            own, dst = ins[i].at[c_], outs[i].at[2 * x_ + y_, c_]
            for r, chip in enumerate(chips):
                started.append(_remote(own, dst, send_sems.at[6 * i + r], recv_sems.at[6 * i + r], (*chip, c_)))
                started[-1].start()
        for i in range(n):
            for r, (px, py) in enumerate(chips):
                blk = outs[i].at[2 * px + py, c_]
                _remote(blk, blk, send_sems.at[6 * i + r], recv_sems.at[6 * i + r], sibling).wait_recv()
                started.append(_remote(blk, blk, send_sems.at[6 * i + 3 + r], recv_sems.at[6 * i + 3 + r], sibling))
                started[-1].start()
        for i in range(n):
            for r, (px, py) in enumerate(chips):
                blk = outs[i].at[2 * px + py, 1 - c_]
                _remote(blk, blk, send_sems.at[6 * i + 3 + r], recv_sems.at[6 * i + 3 + r], sibling).wait_recv()
        for cp in started:
            cp.wait_send()

    return pl.pallas_call(
        body,
        out_shape=tuple(jax.ShapeDtypeStruct((4,) + h.shape, h.dtype) for h in halves),
        in_specs=[_ANY] * n,
        out_specs=tuple(_ANY for _ in halves),
        scratch_shapes=[pltpu.SemaphoreType.DMA((6 * n,)), pltpu.SemaphoreType.DMA((6 * n,))],
        name=name,
    )(*halves)


def share_weights(gathered, name):
    n = len(gathered)

    def body(*refs):
        outs = refs[n:2 * n]
        send_sems, recv_sems = refs[2 * n:]
        x_, y_, c_ = _xyc()
        sibling = (x_, y_, 1 - c_)
        chips = [(_flip(x_, fx), _flip(y_, fy)) for fx, fy in _REL]
        sent = []
        for i in range(n):
            for q, (px, py) in enumerate(chips):
                blk = outs[i].at[2 * px + py, c_]
                sent.append(_remote(blk, blk, send_sems.at[3 * i + q], recv_sems.at[3 * i + q], sibling))
                sent[-1].start()
        for i in range(n):
            for q, (px, py) in enumerate(chips):
                blk = outs[i].at[2 * px + py, 1 - c_]
                _remote(blk, blk, send_sems.at[3 * i + q], recv_sems.at[3 * i + q], sibling).wait_recv()
        for cp in sent:
            cp.wait_send()

    return pl.pallas_call(
        body,
        out_shape=tuple(jax.ShapeDtypeStruct(g.shape, g.dtype) for g in gathered),
        in_specs=[_ANY] * n,
        out_specs=tuple(_ANY for _ in gathered),
        input_output_aliases={i: i for i in range(n)},
        scratch_shapes=[pltpu.SemaphoreType.DMA((3 * n,)), pltpu.SemaphoreType.DMA((3 * n,))],
        name=name,
    )(*gathered)


def swap_halves(gfulls, name):
    n = len(gfulls)

    def body(*refs):
        ins, outs = refs[:n], refs[n:2 * n]
        send_sems, recv_sems = refs[2 * n:]
        x_, y_, c_ = _xyc()
        sibling = (x_, y_, 1 - c_)
        started = []
        for i in range(n):
            for p in range(4):
                started.append(_remote(ins[i].at[p, 1 - c_], outs[i].at[p], send_sems.at[4 * i + p], recv_sems.at[4 * i + p], sibling))
                started[-1].start()
        for cp in started:
            cp.wait()

    return pl.pallas_call(
        body,
        out_shape=tuple(jax.ShapeDtypeStruct((4,) + g.shape[2:], g.dtype) for g in gfulls),
        in_specs=[_ANY] * n,
        out_specs=tuple(_ANY for _ in gfulls),
        scratch_shapes=[pltpu.SemaphoreType.DMA((4 * n,)), pltpu.SemaphoreType.DMA((4 * n,))],
        name=name,
    )(*gfulls)


def scatter_chips(sums, name):
    n = len(sums)

    def body(*refs):
        ins, outs = refs[:n], refs[n:2 * n]
        send_sems, recv_sems = refs[2 * n:]
        x_, y_, c_ = _xyc()
        chips = [(_flip(x_, fx), _flip(y_, fy)) for fx, fy in _REL]
        started = []
        for i in range(n):
            for r, (px, py) in enumerate(chips):
                started.append(_remote(ins[i].at[2 * px + py], outs[i].at[r], send_sems.at[3 * i + r], recv_sems.at[3 * i + r], (px, py, c_)))
                started[-1].start()
        for cp in started:
            cp.wait()

    return pl.pallas_call(
        body,
        out_shape=tuple(jax.ShapeDtypeStruct((3,) + s.shape[1:], s.dtype) for s in sums),
        in_specs=[_ANY] * n,
        out_specs=tuple(_ANY for _ in sums),
        scratch_shapes=[pltpu.SemaphoreType.DMA((3 * n,)), pltpu.SemaphoreType.DMA((3 * n,))],
        name=name,
    )(*sums)


_JOIN_CHUNK_BYTES = 4 << 20


def join_halves(bufs, name):
    n = len(bufs)
    chunks = []
    for b in bufs:
        _, R, C = b.shape
        k = 1
        while k < 8 and R % (2 * k * 8) == 0 and R * C * 4 // k > _JOIN_CHUNK_BYTES:
            k *= 2
        chunks.append(k)
    base = [sum(chunks[:i]) for i in range(n)]
    total = sum(chunks)

    def body(*refs):
        outs = refs[n:2 * n]
        send_sems, recv_sems = refs[2 * n:]
        x_, y_, c_ = _xyc()
        sibling = (x_, y_, 1 - c_)
        sent = []
        for i in range(n):
            rc = bufs[i].shape[1] // chunks[i]
            for q in range(chunks[i]):
                blk = outs[i].at[c_, pl.ds(q * rc, rc)]
                sent.append(_remote(blk, blk, send_sems.at[base[i] + q], recv_sems.at[base[i] + q], sibling))
                sent[-1].start()
        for i in range(n):
            rc = bufs[i].shape[1] // chunks[i]
            for q in range(chunks[i]):
                blk = outs[i].at[1 - c_, pl.ds(q * rc, rc)]
                _remote(blk, blk, send_sems.at[base[i] + q], recv_sems.at[base[i] + q], sibling).wait_recv()
        for cp in sent:
            cp.wait_send()

    return pl.pallas_call(
        body,
        out_shape=tuple(jax.ShapeDtypeStruct(b.shape, b.dtype) for b in bufs),
        in_specs=[_ANY] * n,
        out_specs=tuple(_ANY for _ in bufs),
        input_output_aliases={i: i for i in range(n)},
        scratch_shapes=[pltpu.SemaphoreType.DMA((total,)), pltpu.SemaphoreType.DMA((total,))],
        name=name,
    )(*bufs)


def _halves(a):
    return a.reshape(2, -1, a.shape[-1])


_BIG = ("sb_w_qkv", "sb_w_o", "gm_w_in", "gm_w_o", "ssd_w_in", "ssd_w_o", "mlp_w_in", "mlp_w_out")
_COLS = ("sb_w_qkv", "gm_w_in", "ssd_w_in", "mlp_w_in")
_SMALL = ("norm_mix_g", "norm_mlp_g", "sb_q_norm_g", "sb_k_norm_g", "gm_v_norm_g", "gm_w_s", "gm_b_s",
          "ssd_conv_w", "ssd_conv_b", "ssd_dt_bias", "ssd_a_log", "ssd_d", "ssd_norm_g")
_SMALL_SHARDED = ("ssd_conv_w", "ssd_conv_b", "ssd_norm_g")
_WEIGHTS = ("norm_mix_g", "norm_mlp_g", "sb_w_qkv", "sb_q_norm_g", "sb_k_norm_g", "sb_w_o", "gm_w_in", "gm_v_norm_g",
            "gm_w_s", "gm_b_s", "gm_w_o", "ssd_w_in", "ssd_conv_w", "ssd_conv_b", "ssd_dt_bias", "ssd_a_log", "ssd_d",
            "ssd_norm_g", "ssd_w_o", "mlp_w_in", "mlp_w_out")


def _pack(arrs):
    flat = jnp.concatenate([a.reshape(-1).astype(F32) for a in arrs])
    n = flat.shape[0]
    tot = -(-n // (8 * LANES)) * 8 * LANES
    return jnp.pad(flat, (0, tot - n)).reshape(-1, LANES)


def _unpack(buf, shapes):
    flat = buf.reshape(-1)
    out, o = [], 0
    for s in shapes:
        n = math.prod(s)
        out.append(flat[o:o + n].reshape(s))
        o += n
    return out


_CARRIER_RANK = {0: (3, 0, 2, 1), 1: (3, 2, 0, 1), 2: (3, 0, 2, 1)}


def _layer_arrays(i):
    kind, j = i % 3, i // 3
    mix = (("sb_w_qkv", "sb_w_o"), ("gm_w_in", "gm_w_o"), ("ssd_w_in", "ssd_w_o"))[kind]
    return [(mix[0], j), (mix[1], j), ("mlp_w_in", i), ("mlp_w_out", i)]


def _step(x, w, target, m, v):
    depth = w["norm_mix_g"].shape[0]
    xc, yc, _ = _xyc()
    chip = 2 * xc + yc
    Hs = w["ssd_dt_bias"].shape[1]
    wi = 4 * w["ssd_norm_g"].shape[1]
    cd = 4 * w["ssd_conv_b"].shape[1]

    halves = {(k, l): _halves(w[k][l].astype(_MXU)) for i in range(depth) for (k, l) in _layer_arrays(i)}
    size = lambda key: math.prod(halves[key].shape)

    def finish(key, g):
        k, l = key
        g = g.reshape((4,) + w[k].shape[1:])
        g = lax.dynamic_update_slice(g, w[k][l].astype(_MXU)[None], (chip, 0, 0))
        if k == "ssd_w_in":
            full = g.transpose(1, 0, 2).reshape(g.shape[1], -1)
            return full[:, :wi + cd], jnp.pad(full[:, wi + cd:], ((0, 0), (0, LANES - Hs)))
        return g if k in _COLS else g.reshape(-1, g.shape[-1])

    small_sh = gather_small(_pack([w[k] for k in _SMALL_SHARDED]), "gather_small_weights")
    parts = [_unpack(small_sh[2 * j], [w[k].shape for k in _SMALL_SHARDED]) for j in range(4)]
    conv_w = jnp.concatenate([p[0][0] for p in parts], axis=1)
    conv_b = jnp.concatenate([p[1] for p in parts], axis=1)
    ssd_ng = jnp.concatenate([p[2] for p in parts], axis=1)
    padh = lambda a: jnp.pad(a, ((0, 0), (0, LANES - Hs)))
    dtb, alog = padh(w["ssd_dt_bias"]), padh(w["ssd_a_log"])
    dcol = jnp.repeat(w["ssd_d"], SSD_HEAD_DIM, axis=1)
    bcol = w["gm_b_s"][0][:, :, None]

    keys0 = _layer_arrays(0) if depth == 1 else _layer_arrays(0)[:2]
    ready = dict(zip(keys0, gather_weights([halves[k] for k in keys0], "gather_weights_0")))
    gather = lambda key: ("gather", halves[key], key)

    h = x[0]
    tape = []
    for i in range(depth):
        kind, j = i % 3, i // 3
        keys = _layer_arrays(i)
        wl = [finish(k, ready[k]) for k in keys[:2]]
        cq = Carriers()
        if i == 0 and depth > 1:
            n1 = _layer_arrays(1)
            for group in ([keys[2]], [keys[3]], [n1[2], n1[1]], [n1[3], n1[0]]):
                cq.add_group([gather(key) for key in group])
        elif i + 1 < depth:
            nxt = sorted(_layer_arrays(i + 1), key=size, reverse=True)
            order = [None] * 4
            for rank, key in zip(_CARRIER_RANK[kind], nxt):
                order[rank] = key
            for key in order:
                cq.add("gather", halves[key], key)
        gmix = w["norm_mix_g"][i:i + 1]
        if kind == 0:
            args = (gmix, wl[0], w["sb_q_norm_g"][j:j + 1], w["sb_k_norm_g"][j:j + 1], wl[1], f"sb{j}")
            h, sv = sb_fwd(h, *args, cq=cq)
        elif kind == 1:
            args = (gmix, wl[0], w["gm_v_norm_g"][j:j + 1], w["gm_w_s"][j], bcol, wl[1], f"gm{j}")
            h, sv = gm_fwd(h, *args, cq=cq)
        else:
            args = (gmix, wl[0][0], wl[0][1], conv_w, conv_b, dtb, alog, dcol, ssd_ng, wl[1], f"ssd{j}")
            h, sv = ssd_fwd(h, *args, cq=cq)
        if i == 0 and depth > 1:
            ready.update(zip(keys[2:], share_weights([cq.done[k] for k in keys[2:]], "share_weights_0")))
        wl += [finish(k, ready[k]) for k in keys[2:]]
        margs = (w["norm_mlp_g"][i:i + 1], wl[2], wl[3], f"mlp{i}")
        h, msv = mlp_fwd(h, *margs, cq=cq)
        tape.append((kind, j, args, sv, margs, msv))
        if i + 1 < depth:
            assert not cq.pending
            nk_ = _layer_arrays(i + 1)
            shared = share_weights([cq.done[k] for k in nk_], f"share_weights_{i + 1}")
            ready.update(zip(nk_, shared))

    dh, loss_cols = loss_head(h, target[0], "loss_head")

    gsmall = {k: [None] * w[k].shape[0] for k in ("norm_mix_g", "norm_mlp_g", "sb_q_norm_g", "sb_k_norm_g")}
    pairs, from_chips = {}, {}

    def reduce_pairs(keys_, grads_, tag_):
        gl = [g.reshape(4, 2, -1, g.shape[-1]).astype(GRAD_DT) for g in grads_]
        got = swap_halves(gl, f"grad_swap_halves_{tag_}")
        for key, g, r in zip(keys_, gl, got):
            pairs[key] = pair_sum(g, r, f"pair_sum_{key[0]}_{key[1]}")

    cq = Carriers()
    for i in reversed(range(depth)):
        kind, j, args, sv, margs, msv = tape[i]
        keys = _layer_arrays(i)
        dh, gsmall["norm_mlp_g"][i], d_mlp_in, d_mlp_out = mlp_bwd(dh, msv, *margs, cq=cq)
        assert not cq.pending
        from_chips.update(cq.done)
        mix_cq = Carriers()
        if i == 0:
            reduce_pairs(keys[2:], (d_mlp_in, d_mlp_out), "0_mlp")
            for key in keys[2:]:
                mix_cq.add("scatter", pairs[key], key)
        if kind == 0:
            dh, gsmall["norm_mix_g"][i], d_in, gsmall["sb_q_norm_g"][j], gsmall["sb_k_norm_g"][j], d_out = sb_bwd(dh, sv, *args, cq=mix_cq)
        elif kind == 1:
            dh, gsmall["norm_mix_g"][i], d_in, d_vg, d_ws, d_bcol, d_out = gm_bwd(dh, sv, *args)
            gsmall["gm_v_norm_g"], gsmall["gm_w_s"], gsmall["gm_b_s"] = d_vg, d_ws[None], d_bcol[None, :, :, 0]
        else:
            dh, gsmall["norm_mix_g"][i], d_zx, d_dt, d_cw, d_cb, d_dtb, d_al, d_dcol, d_ng, d_out = ssd_bwd(dh, sv, *args)
            d_full = jnp.concatenate([d_zx, d_dt[:, :Hs]], axis=1)
            d_in = d_full.reshape(d_full.shape[0], 4, -1).transpose(1, 0, 2)
            gsmall["ssd_conv_w"], gsmall["ssd_conv_b"], gsmall["ssd_norm_g"] = d_cw[None], d_cb, d_ng
            gsmall["ssd_dt_bias"], gsmall["ssd_a_log"] = d_dtb[:, :Hs], d_al[:, :Hs]
            gsmall["ssd_d"] = jnp.sum(d_dcol.reshape(Hs, SSD_HEAD_DIM), axis=1)[None]
        assert not mix_cq.pending
        from_chips.update(mix_cq.done)
        cq = Carriers()
        if i > 0:
            reduce_pairs(keys, (d_in, d_out, d_mlp_in, d_mlp_out), str(i))
            for key in sorted(keys, key=size, reverse=True):
                cq.add("scatter", pairs[key], key)
        else:
            reduce_pairs(keys[:2], (d_in, d_out), "0_mix")
            from_chips.update(zip(keys[:2], scatter_chips([pairs[k] for k in keys[:2]], "grad_scatter_chips_0")))
    for k in gsmall:
        if isinstance(gsmall[k], list):
            gsmall[k] = jnp.concatenate(gsmall[k], axis=0)
    grad_x = dh[None]

    allkeys = [key for i in range(depth) for key in _layer_arrays(i)]
    mine = [chip_sum(pairs[key], from_chips[key], f"chip_sum_{key[0]}_{key[1]}") for key in allkeys]
    joined = dict(zip(allkeys, join_halves(mine, "grad_join_halves")))
    grads, deltas, new_m, new_v = {}, {}, {}, {}
    for k in _BIG:
        g = jnp.stack([joined[(k, l)].reshape(w[k].shape[1:]) for l in range(w[k].shape[0])])
        C = w[k].shape[-1]
        d_, m_, v_ = adamw(w[k].reshape(-1, C), g.reshape(-1, C), m[k].reshape(-1, C), v[k].reshape(-1, C), f"adamw_{k}")
        grads[k], deltas[k], new_m[k], new_v[k] = (a.reshape(w[k].shape) for a in (g, d_, m_, v_))

    full_shapes = [gsmall[k].shape for k in _SMALL] + [(1,)]
    loss_local = jnp.sum(loss_cols).reshape(1)
    red = sum8(gather_small(_pack([gsmall[k] for k in _SMALL] + [loss_local]), "gather_small_grads"), "sum_small_grads")
    red = _unpack(red, full_shapes)
    loss = red[-1][0]
    gsm = dict(zip(_SMALL, red[:-1]))
    for k in _SMALL_SHARDED:
        n = w[k].shape[-1]
        gsm[k] = lax.dynamic_slice_in_dim(gsm[k], chip * n, n, axis=gsm[k].ndim - 1)
    shapes = [w[k].shape for k in _SMALL]
    packed = [_pack([d[k] for k in _SMALL]) for d in (w, gsm, m, v)]
    outs = adamw(*packed, "adamw_small")
    for k, g_, d_, m_, v_ in zip(_SMALL, [gsm[k] for k in _SMALL], *[_unpack(o, shapes) for o in outs]):
        grads[k], deltas[k], new_m[k], new_v[k] = g_.reshape(w[k].shape), d_, m_, v_

    return (loss, grad_x, *[grads[k] for k in _WEIGHTS], *[deltas[k] for k in _WEIGHTS],
            *[new_m[k] for k in _WEIGHTS], *[new_v[k] for k in _WEIGHTS])


def kernel(x, norm_mix_g, norm_mlp_g, sb_w_qkv, sb_q_norm_g, sb_k_norm_g, sb_w_o, gm_w_in, gm_v_norm_g, gm_w_s, gm_b_s, gm_w_o, ssd_w_in, ssd_conv_w, ssd_conv_b, ssd_dt_bias, ssd_a_log, ssd_d, ssd_norm_g, ssd_w_o, mlp_w_in, mlp_w_out, loss_target, m_norm_mix_g, m_norm_mlp_g, m_sb_w_qkv, m_sb_q_norm_g, m_sb_k_norm_g, m_sb_w_o, m_gm_w_in, m_gm_v_norm_g, m_gm_w_s, m_gm_b_s, m_gm_w_o, m_ssd_w_in, m_ssd_conv_w, m_ssd_conv_b, m_ssd_dt_bias, m_ssd_a_log, m_ssd_d, m_ssd_norm_g, m_ssd_w_o, m_mlp_w_in, m_mlp_w_out, v_norm_mix_g, v_norm_mlp_g, v_sb_w_qkv, v_sb_q_norm_g, v_sb_k_norm_g, v_sb_w_o, v_gm_w_in, v_gm_v_norm_g, v_gm_w_s, v_gm_b_s, v_gm_w_o, v_ssd_w_in, v_ssd_conv_w, v_ssd_conv_b, v_ssd_dt_bias, v_ssd_a_log, v_ssd_d, v_ssd_norm_g, v_ssd_w_o, v_mlp_w_in, v_mlp_w_out):
    w = dict(zip(_WEIGHTS, (norm_mix_g, norm_mlp_g, sb_w_qkv, sb_q_norm_g, sb_k_norm_g, sb_w_o, gm_w_in, gm_v_norm_g, gm_w_s,
                            gm_b_s, gm_w_o, ssd_w_in, ssd_conv_w, ssd_conv_b, ssd_dt_bias, ssd_a_log, ssd_d, ssd_norm_g,
                            ssd_w_o, mlp_w_in, mlp_w_out)))
    m = dict(zip(_WEIGHTS, (m_norm_mix_g, m_norm_mlp_g, m_sb_w_qkv, m_sb_q_norm_g, m_sb_k_norm_g, m_sb_w_o, m_gm_w_in,
                            m_gm_v_norm_g, m_gm_w_s, m_gm_b_s, m_gm_w_o, m_ssd_w_in, m_ssd_conv_w, m_ssd_conv_b,
                            m_ssd_dt_bias, m_ssd_a_log, m_ssd_d, m_ssd_norm_g, m_ssd_w_o, m_mlp_w_in, m_mlp_w_out)))
    v = dict(zip(_WEIGHTS, (v_norm_mix_g, v_norm_mlp_g, v_sb_w_qkv, v_sb_q_norm_g, v_sb_k_norm_g, v_sb_w_o, v_gm_w_in,
                            v_gm_v_norm_g, v_gm_w_s, v_gm_b_s, v_gm_w_o, v_ssd_w_in, v_ssd_conv_w, v_ssd_conv_b,
                            v_ssd_dt_bias, v_ssd_a_log, v_ssd_d, v_ssd_norm_g, v_ssd_w_o, v_mlp_w_in, v_mlp_w_out)))
    return _step(x, w, loss_target, m, v)
```

```python
import functools
import math

import jax
import jax.numpy as jnp
from jax import lax
from jax.experimental import pallas as pl
from jax.experimental.pallas import tpu as pltpu

F32 = jnp.float32
BF16 = jnp.bfloat16
_MXU = jnp.bfloat16
_ACT = jnp.bfloat16
GRAD_DT = jnp.bfloat16
_VMEM_LIMIT = 56 * 1024 * 1024
_MATMUL_VMEM_BUDGET = 44 * 1024 * 1024
EPS = 1e-6
LANES = 128
CHUNK = 128
SSD_HEAD_DIM = 64
SSD_STATE = 128
SSD_CONV = 4
ADAM_LR, ADAM_B1, ADAM_B2, ADAM_EPS, ADAM_WD, ADAM_STEP = 1e-3, 0.9, 0.999, 1e-8, 0.01, 10
MESH = pl.DeviceIdType.MESH

NN = (((1,), (0,)), ((), ()))
NT = (((1,), (1,)), ((), ()))
TN = (((0,), (0,)), ((), ()))

_ANY = pl.BlockSpec(memory_space=pl.ANY)
_REL = ((1, 0), (0, 1), (1, 1))


def _dot(a, b, dims=NN):
    return lax.dot_general(a.astype(_MXU), b.astype(_MXU), dims, preferred_element_type=F32)


def _params(sem):
    return pltpu.CompilerParams(dimension_semantics=sem, vmem_limit_bytes=_VMEM_LIMIT)


def _pick(n, *cands):
    for c in cands:
        if n % c == 0:
            return c
    return n


def _xyc():
    return lax.axis_index("x"), lax.axis_index("y"), lax.axis_index("c")


def _flip(v, f):
    return 1 - v if f else v


def _remote(src, dst, ssem, rsem, dev):
    return pltpu.make_async_remote_copy(src_ref=src, dst_ref=dst, send_sem=ssem, recv_sem=rsem, device_id=dev,
                                        device_id_type=MESH)


class Carriers:
    def __init__(self):
        self.pending, self.done = [], {}

    def add(self, kind, src, tag):
        self.pending.append([(kind, src, tag)])

    def add_group(self, transfers):
        self.pending.append(list(transfers))


def matmul(a, b, mode, out_dtypes, name, epilogue=None, extras=(), b_cols4=False, out_cols4=False, cq=None):
    ash = a.shape
    bsh = (b.shape[1], 4 * b.shape[2]) if b_cols4 else b.shape
    if mode == "nn":
        (M, K), (K2, N) = ash, bsh
    elif mode == "nt":
        (M, K), (N, K2) = ash, bsh
    else:
        (K, M), (K2, N) = ash, bsh
    assert K == K2, (mode, a.shape, b.shape)
    nsh = N // 4 if (out_cols4 or (b_cols4 and mode == "nn")) else None
    ksh = K // 4 if (b_cols4 and mode == "nt") else None
    single = not isinstance(out_dtypes, (tuple, list))
    odt = (out_dtypes,) if single else tuple(out_dtypes)
    n_ex, n_out = len(extras), len(odt)

    def fits(sh, top):
        whole = [sh] if sh is not None and sh <= top and sh % LANES == 0 else []
        return whole + [c for c in (2048, 1024, 512, 256, 128) if c <= top and (sh is None or sh % c == 0)] + ([] if sh is None else [sh])

    tn = _pick(N, *fits(nsh, 1536))
    tk = _pick(K, *fits(ksh, 2048))

    def vmem_bytes(tm_):
        per_step = tm_ * tk * a.dtype.itemsize + tk * tn * b.dtype.itemsize + n_ex * tm_ * tn * 4
        per_step += sum(tm_ * tn * jnp.dtype(d).itemsize for d in odt)
        return 2 * per_step + tm_ * tn * 4

    tm = next((c for c in (1024, 512, 256, 128) if M % c == 0 and vmem_bytes(c) <= _MATMUL_VMEM_BUDGET), M)
    ni, nj, nk = M // tm, N // tn, K // tk
    dims = {"nn": NN, "nt": NT, "tn": TN}[mode]
    carry = cq.pending.pop(0) if (cq is not None and cq.pending) else []
    nc = len(carry)

    if mode == "tn":
        a_spec = pl.BlockSpec((tk, tm), lambda i, j, k: (k, i))
    else:
        a_spec = pl.BlockSpec((tm, tk), lambda i, j, k: (i, k))
    if b_cols4 and mode == "nn":
        b_spec = pl.BlockSpec((None, tk, tn), lambda i, j, k: (lax.div(j * tn, nsh), k, lax.div(lax.rem(j * tn, nsh), tn)))
    elif b_cols4:
        b_spec = pl.BlockSpec((None, tn, tk), lambda i, j, k: (lax.div(k * tk, ksh), j, lax.div(lax.rem(k * tk, ksh), tk)))
    elif mode == "nt":
        b_spec = pl.BlockSpec((tn, tk), lambda i, j, k: (j, k))
    else:
        b_spec = pl.BlockSpec((tk, tn), lambda i, j, k: (k, j))
    mn_spec = pl.BlockSpec((tm, tn), lambda i, j, k: (i, j))
    if out_cols4:
        o_spec = pl.BlockSpec((None, tm, tn), lambda i, j, k: (lax.div(j * tn, nsh), i, lax.div(lax.rem(j * tn, nsh), tn)))
        o_shape = (4, M, N // 4)
    else:
        o_spec, o_shape = mn_spec, (M, N)

    def body(*refs):
        a_ref, b_ref = refs[0], refs[1]
        ex = refs[2:2 + n_ex]
        pos = 2 + n_ex
        src_refs = refs[pos:pos + nc]
        pos += nc
        outs = refs[pos:pos + n_out]
        pos += n_out
        dst_refs = refs[pos:pos + nc]
        pos += nc
        acc = refs[pos]
        i, j, k = pl.program_id(0), pl.program_id(1), pl.program_id(2)

        if carry:
            ssem, rsem = refs[pos + 1], refs[pos + 2]
            x_, y_, c_ = _xyc()
            chips = [(_flip(x_, fx), _flip(y_, fy)) for fx, fy in _REL]

            def copies(arriving):
                out = []
                for t, (kind, _, _) in enumerate(carry):
                    src_ref, dst_ref = src_refs[t], dst_refs[t]
                    for r, (px, py) in enumerate(chips):
                        if kind == "gather":
                            s_, d_ = src_ref.at[c_], dst_ref.at[(2 * px + py) if arriving else (2 * x_ + y_), c_]
                        else:
                            s_, d_ = src_ref.at[2 * px + py], dst_ref.at[r]
                        out.append(_remote(s_, d_, ssem.at[3 * t + r], rsem.at[3 * t + r], (px, py, c_)))
                return out

            @pl.when((i == 0) & (j == 0) & (k == 0))
            def _():
                for send in copies(False):
                    send.start()

        @pl.when(k == 0)
        def _():
            acc[...] = jnp.zeros_like(acc)

        part = _dot(a_ref[...], b_ref[...], dims)

        @pl.when(k < nk - 1)
        def _():
            acc[...] += part

        @pl.when(k == nk - 1)
        def _():
            r = acc[...] + part
            res = (r,) if epilogue is None else epilogue(r, *[e[...] for e in ex])
            for o, v in zip(outs, res):
                o[...] = v.astype(o.dtype)

        if carry:
            @pl.when((i == ni - 1) & (j == nj - 1) & (k == nk - 1))
            def _():
                for arrive in copies(True):
                    arrive.wait_recv()
                for send in copies(False):
                    send.wait_send()

    in_specs = [a_spec, b_spec] + [mn_spec] * n_ex
    out_shape = [jax.ShapeDtypeStruct(o_shape, d) for d in odt]
    out_specs = [o_spec for _ in odt]
    scratch = [pltpu.VMEM((tm, tn), F32)]
    operands = [a, b, *extras]
    for kind, src, _ in carry:
        in_specs.append(_ANY)
        operands.append(src)
        out_shape.append(jax.ShapeDtypeStruct(((4,) + src.shape) if kind == "gather" else ((3,) + src.shape[1:]), src.dtype))
        out_specs.append(_ANY)
    if carry:
        scratch += [pltpu.SemaphoreType.DMA((3 * nc,)), pltpu.SemaphoreType.DMA((3 * nc,))]
    out = pl.pallas_call(
        body,
        out_shape=tuple(out_shape),
        grid=(ni, nj, nk),
        in_specs=in_specs,
        out_specs=tuple(out_specs),
        scratch_shapes=scratch,
        compiler_params=_params(("arbitrary", "arbitrary", "arbitrary") if carry else ("parallel", "parallel", "arbitrary")),
        name=name,
    )(*operands)
    if carry:
        for (_, _, tag), filled in zip(carry, out[n_out:]):
            cq.done[tag] = filled
        out = out[:n_out]
    return out[0] if single else out


class Op:
    def __init__(self, arr, block, imap, kind="tile", grad=True, gshape=None, gimap=None):
        self.arr, self.block, self.imap, self.kind, self.grad = arr, block, imap, kind, grad
        self.gshape = gshape or arr.shape
        self.gimap = gimap or imap

    def spec(self):
        return pl.BlockSpec(self.block, self.imap)


def tmap(f, grid, ins, outs, name):
    n_in = len(ins)

    def body(*refs):
        res = f(*[r[...] for r in refs[:n_in]])
        for o, v in zip(refs[n_in:], res):
            o[...] = v.astype(o.dtype)

    return pl.pallas_call(
        body,
        out_shape=tuple(jax.ShapeDtypeStruct(s, d) for s, d, _, _ in outs),
        grid=grid,
        in_specs=[o.spec() for o in ins],
        out_specs=tuple(pl.BlockSpec(b, m) for _, _, b, m in outs),
        compiler_params=_params(("parallel", "parallel")),
        name=name,
    )(*[o.arr for o in ins])


def tmap_vjp(f, grid, ins, cts, name, grad_dtypes=None):
    n_in, n_ct = len(ins), len(cts)
    gidx = [i for i, o in enumerate(ins) if o.grad]
    gdt = grad_dtypes or {}

    def body(*refs):
        in_refs, ct_refs, g_refs = refs[:n_in], refs[n_in:n_in + n_ct], refs[n_in + n_ct:]
        vals = [r[...] for r in in_refs]

        def g_only(*diff):
            full = list(vals)
            for i, v in zip(gidx, diff):
                full[i] = v
            return f(*full)

        res, vjp = jax.vjp(g_only, *[vals[i].astype(F32) for i in gidx])
        grads = vjp(tuple(c[...].astype(r.dtype) for c, r in zip(ct_refs, res)))
        inner = pl.program_id(1)
        for i, g, gr in zip(gidx, grads, g_refs):
            if ins[i].kind == "tile":
                gr[...] = g.astype(gr.dtype)
            else:
                @pl.when(inner == 0)
                def _(gr=gr, g=g):
                    gr[...] = g.astype(gr.dtype)

                @pl.when(inner != 0)
                def _(gr=gr, g=g):
                    gr[...] += g.astype(gr.dtype)

    out_shape = tuple(jax.ShapeDtypeStruct(ins[i].gshape, gdt.get(i, F32)) for i in gidx)
    return pl.pallas_call(
        body,
        out_shape=out_shape,
        grid=grid,
        in_specs=[o.spec() for o in ins] + [o.spec() for o in cts],
        out_specs=tuple(pl.BlockSpec(ins[i].block, ins[i].gimap) for i in gidx),
        compiler_params=_params(("parallel", "arbitrary")),
        name=name,
    )(*[o.arr for o in ins], *[o.arr for o in cts])


def _rms(x, g):
    return x * lax.rsqrt(jnp.mean(x * x, axis=-1, keepdims=True) + EPS) * g


def _row_ops(arrs, tm, grads=None):
    grads = grads or [True] * len(arrs)
    return [Op(a, (tm, a.shape[1]), lambda o, i: (i, 0), "tile", g) for a, g in zip(arrs, grads)]


def _vec_op(v, grad=True):
    return Op(v, (1, v.shape[1]), lambda o, i: (0, 0), "param", grad)


def rmsnorm_fwd(h, g, name):
    T, D = h.shape
    tm = _pick(T, 512, 256, 128)
    f = lambda x, gg: (_rms(x, gg),)
    return tmap(f, (1, T // tm), _row_ops([h], tm) + [_vec_op(g)],
                [((T, D), _ACT, (tm, D), lambda o, i: (i, 0))], name)[0]


def rmsnorm_bwd(h, g, dhn, dres, name):
    T, D = h.shape
    tm = _pick(T, 512, 256, 128)
    f = lambda x, gg: (_rms(x, gg), x)
    return tmap_vjp(f, (1, T // tm), _row_ops([h], tm) + [_vec_op(g)], _row_ops([dhn, dres], tm), name)


def mlp_fwd(h, g_row, w_in, w_out, tag, cq=None):
    hn = rmsnorm_fwd(h, g_row, f"{tag}_norm")
    a, r2 = matmul(hn, w_in, "nn", (F32, _ACT), f"{tag}_in", b_cols4=True, cq=cq,
                   epilogue=lambda acc: (acc, jnp.square(jnp.maximum(acc, 0.0))))
    out = matmul(r2, w_out, "nn", F32, f"{tag}_out", cq=cq, epilogue=lambda acc, hh: (acc + hh,), extras=(h,))
    return out, (h, hn, a, r2)


def mlp_bwd(dout, saved, g_row, w_in, w_out, tag, cq=None):
    h, hn, a, r2 = saved
    da = matmul(dout, w_out, "nt", _ACT, f"{tag}_dact", cq=cq,
                epilogue=lambda acc, aa: (acc * (2.0 * jnp.maximum(aa, 0.0)),), extras=(a,))
    dw_out = matmul(r2, dout, "tn", GRAD_DT, f"{tag}_dwout", cq=cq)
    dw_in = matmul(hn, da, "tn", GRAD_DT, f"{tag}_dwin", out_cols4=True, cq=cq)
    dhn = matmul(da, w_in, "nt", F32, f"{tag}_dhn", b_cols4=True, cq=cq)
    dh, dg = rmsnorm_bwd(h, g_row, dhn, dout, f"{tag}_dnorm")
    return dh, dg, dw_in, dw_out


def _split3(x):
    hi = x.astype(BF16)
    r = x - hi.astype(F32)
    mid = r.astype(BF16)
    lo = (r - mid.astype(F32)).astype(BF16)
    return hi, mid, lo


_ATTN_PIECES = 2


def _cumdot(x, tri, pieces=3):
    return sum(lax.dot_general(p, tri, NN, preferred_element_type=F32) for p in _split3(x)[:pieces])


def _iotas():
    row = lax.broadcasted_iota(jnp.int32, (CHUNK, CHUNK), 0)
    col = lax.broadcasted_iota(jnp.int32, (CHUNK, CHUNK), 1)
    return row, col


_TQ = 256
_TK = 256
_DEAD = -88.0


def _sb_block(q, kblk, q0, k0, scale, row, col):
    z = _dot(q, kblk, NT) * scale
    e = jnp.exp(-jnp.abs(z))
    den = 1.0 + e
    sp = jnp.maximum(z, 0.0) + jnp.log(den)
    mask = (col + k0) < (row + q0)
    lg = jnp.where(mask, -sp, 0.0)
    beta = jnp.where(z >= 0, 1.0, e) / den
    return z, mask, lg, beta


def _attn_iotas(tq, tk):
    row = lax.broadcasted_iota(jnp.int32, (tq, tk), 0)
    col = lax.broadcasted_iota(jnp.int32, (tq, tk), 1)
    r2 = lax.broadcasted_iota(jnp.int32, (tk, tk), 0)
    c2 = lax.broadcasted_iota(jnp.int32, (tk, tk), 1)
    return row, col, r2, c2


def attn_fwd(qn, kn, v, name):
    T, W = qn.shape
    tq, tk = _pick(T, _TQ, CHUNK), _pick(T, _TK, CHUNK)
    H, NQ, per = W // LANES, T // tq, tq // tk
    assert tq % tk == 0 and T // tk <= LANES
    scale = LANES ** -0.5

    def body(q_ref, k_ref, v_ref, o_ref, r_ref, acc_ref, run_ref):
        qi = pl.program_id(1)
        q = q_ref[...]
        row, col, r2, c2 = _attn_iotas(tq, tk)
        suffix = (r2 >= c2).astype(_MXU)
        lane_q = lax.broadcasted_iota(jnp.int32, (tq, LANES), 1)
        acc_ref[...] = jnp.zeros_like(acc_ref)
        run_ref[...] = jnp.zeros_like(run_ref)
        r_ref[...] = jnp.full(r_ref.shape, -1e30, F32)

        def step(carry):
            kb, _ = carry
            off = pl.multiple_of(kb * tk, tk)
            run = run_ref[...]
            z, mask, lg, _ = _sb_block(q, k_ref[pl.ds(off, tk), :], qi * tq, off, scale, row, col)
            r_ref[...] = jnp.where(lane_q == kb, run, r_ref[...])
            cl = _cumdot(lg, suffix, _ATTN_PIECES) + run
            a = jnp.exp(jnp.where(mask, z + cl, -1e30))
            acc_ref[...] += _dot(a, v_ref[pl.ds(off, tk), :])
            run = run + jnp.sum(lg, axis=1, keepdims=True)
            run_ref[...] = run
            return kb - 1, jnp.max(run) > _DEAD

        lax.while_loop(lambda c: (c[0] >= 0) & c[1], step, ((qi + 1) * per - 1, True))
        o_ref[...] = acc_ref[...].astype(o_ref.dtype)

    qspec = pl.BlockSpec((tq, LANES), lambda h, i: (i, h))
    kvspec = pl.BlockSpec((T, LANES), lambda h, i: (0, h))
    return pl.pallas_call(
        body,
        out_shape=(jax.ShapeDtypeStruct((T, W), _ACT), jax.ShapeDtypeStruct((H, T, LANES), F32)),
        grid=(H, NQ),
        in_specs=[qspec, kvspec, kvspec],
        out_specs=(qspec, pl.BlockSpec((None, tq, LANES), lambda h, i: (h, i, 0))),
        scratch_shapes=[pltpu.VMEM((tq, LANES), F32), pltpu.VMEM((tq, 1), F32)],
        compiler_params=_params(("parallel", "parallel")),
        name=name,
    )(qn, kn, v)


def attn_bwd(qn, kn, v, do, rblk, name):
    T, W = qn.shape
    tq, tk = _pick(T, _TQ, CHUNK), _pick(T, _TK, CHUNK)
    H, NQ, per = W // LANES, T // tq, tq // tk
    scale = LANES ** -0.5

    def body(q_ref, k_ref, v_ref, do_ref, r_ref, dq_ref, dk_ref, dv_ref, g_ref):
        qi = pl.program_id(1)

        @pl.when(qi == 0)
        def _():
            dk_ref[...] = jnp.zeros_like(dk_ref)
            dv_ref[...] = jnp.zeros_like(dv_ref)

        q = q_ref[...]
        dout = do_ref[...]
        rt = r_ref[...]
        row, col, r2, c2 = _attn_iotas(tq, tk)
        suffix = (r2 >= c2).astype(_MXU)
        prefix = (r2 <= c2).astype(_MXU)
        kend = (qi + 1) * per - 1
        lane = lax.broadcasted_iota(jnp.int32, (1, LANES), 1)
        lane_q = lax.broadcasted_iota(jnp.int32, (tq, LANES), 1)
        unvisited = (jnp.max(rt, axis=0, keepdims=True) < -1e29) & (lane <= kend)
        start = jnp.sum(unvisited.astype(jnp.int32))

        dq_ref[...] = jnp.zeros_like(dq_ref)
        g_ref[...] = jnp.zeros_like(g_ref)

        @pl.loop(start, kend + 1)
        def _(kb):
            gsum = g_ref[...]
            off = pl.multiple_of(kb * tk, tk)
            kblk = k_ref[pl.ds(off, tk), :]
            vblk = v_ref[pl.ds(off, tk), :]
            z, mask, lg, beta = _sb_block(q, kblk, qi * tq, off, scale, row, col)
            run = jnp.sum(jnp.where(lane_q == kb, rt, 0.0), axis=1, keepdims=True)
            cl = _cumdot(lg, suffix, _ATTN_PIECES) + run
            a = jnp.exp(jnp.where(mask, z + cl, -1e30))
            e = _dot(dout, vblk, NT) * a
            f = _cumdot(e, prefix, _ATTN_PIECES) + gsum
            dz = jnp.where(mask, e - beta * f, 0.0) * scale
            dk_ref[pl.ds(off, tk), :] += _dot(dz, q, TN)
            dv_ref[pl.ds(off, tk), :] += _dot(a, dout, TN)
            dq_ref[...] += _dot(dz, kblk)
            g_ref[...] = gsum + jnp.sum(e, axis=1, keepdims=True)

    qspec = pl.BlockSpec((tq, LANES), lambda h, i: (i, h))
    kvspec = pl.BlockSpec((T, LANES), lambda h, i: (0, h))
    big = jax.ShapeDtypeStruct((T, W), F32)
    return pl.pallas_call(
        body,
        out_shape=(big, big, big),
        grid=(H, NQ),
        in_specs=[qspec, kvspec, kvspec, qspec, pl.BlockSpec((None, tq, LANES), lambda h, i: (h, i, 0))],
        out_specs=(qspec, kvspec, kvspec),
        scratch_shapes=[pltpu.VMEM((tq, 1), F32)],
        compiler_params=_params(("parallel", "arbitrary")),
        name=name,
    )(qn, kn, v, do, rblk)


def _qk_ops(qkv, qg, kg, tm, grad):
    T, W3 = qkv.shape
    H, NT_ = W3 // (3 * LANES), T // tm
    W = H * LANES

    def part(p):
        return Op(qkv, (tm, LANES), lambda o, n: (lax.rem(n, NT_), p * H + lax.div(n, NT_)), "tile", grad,
                  gshape=(T, W), gimap=lambda o, n: (lax.rem(n, NT_), lax.div(n, NT_)))

    vec = lambda g: Op(g, (1, LANES), lambda o, n: (0, 0), "param", grad)
    return [part(0), part(1), part(2), vec(qg), vec(kg)], (1, H * NT_), H, NT_, W


def _qk_f(q, k, v, qg, kg):
    return _rms(q, qg), _rms(k, kg), v


def qknorm_fwd(qkv, qg, kg, name):
    T = qkv.shape[0]
    tm = _pick(T, 512, 256, 128)
    ins, grid, H, NT_, W = _qk_ops(qkv, qg, kg, tm, False)
    out = ((T, W), _ACT, (tm, LANES), lambda o, n: (lax.rem(n, NT_), lax.div(n, NT_)))
    return tmap(_qk_f, grid, ins, [out, out, out], name)


def qknorm_bwd(qkv, qg, kg, dq, dk, dv, name):
    T = qkv.shape[0]
    tm = _pick(T, 512, 256, 128)
    ins, grid, H, NT_, W = _qk_ops(qkv, qg, kg, tm, True)
    cts = [Op(c, (tm, LANES), lambda o, n: (lax.rem(n, NT_), lax.div(n, NT_))) for c in (dq, dk, dv)]
    return tmap_vjp(_qk_f, grid, ins, cts, name, grad_dtypes={0: _ACT, 1: _ACT, 2: _ACT})


def sb_fwd(h, g_row, w_qkv, qg, kg, w_o, tag, cq=None):
    hn = rmsnorm_fwd(h, g_row, f"{tag}_norm")
    qkv = matmul(hn, w_qkv, "nn", F32, f"{tag}_qkv", b_cols4=True, cq=cq)
    qn, kn, v = qknorm_fwd(qkv, qg, kg, f"{tag}_qknorm")
    o, rblk = attn_fwd(qn, kn, v, f"{tag}_attn")
    out = matmul(o, w_o, "nn", F32, f"{tag}_wo", cq=cq, epilogue=lambda acc, hh: (acc + hh,), extras=(h,))
    return out, (h, hn, qkv, qn, kn, v, o, rblk)


def sb_bwd(dout, saved, g_row, w_qkv, qg, kg, w_o, tag, cq=None):
    h, hn, qkv, qn, kn, v, o, rblk = saved
    do = matmul(dout, w_o, "nt", _ACT, f"{tag}_do")
    dw_o = matmul(o, dout, "tn", GRAD_DT, f"{tag}_dwo")
    dqn, dkn, dv = attn_bwd(qn, kn, v, do, rblk, f"{tag}_dattn")
    dq, dk, dvv, dqg, dkg = qknorm_bwd(qkv, qg, kg, dqn, dkn, dv, f"{tag}_dqknorm")
    dqkv = jnp.concatenate([dq, dk, dvv], axis=1)
    dw_qkv = matmul(hn, dqkv, "tn", GRAD_DT, f"{tag}_dwqkv", out_cols4=True, cq=cq)
    dhn = matmul(dqkv, w_qkv, "nt", F32, f"{tag}_dhn", b_cols4=True, cq=cq)
    dh, dg = rmsnorm_bwd(h, g_row, dhn, dout, f"{tag}_dnorm")
    return dh, dg, dw_qkv, dqg, dkg, dw_o


@functools.partial(jax.custom_vjp, nondiff_argnums=(2,))
def _dotv(a, b, mode):
    return _dot(a, b, {"nn": NN, "nt": NT, "tn": TN}[mode])


def _dotv_fwd(a, b, mode):
    return _dotv(a, b, mode), (a, b)


def _dotv_bwd(mode, res, g):
    a, b = res
    if mode == "nn":
        return _dot(g, b, NT), _dot(a, g, TN)
    if mode == "nt":
        return _dot(g, b, NN), _dot(g, a, TN)
    return _dot(b, g, NT), _dot(a, g, NN)


_dotv.defvjp(_dotv_fwd, _dotv_bwd)


def _gelu(x):
    return 0.5 * x * (1.0 + lax.erf(x * (2.0 ** -0.5)))


def _gm1_f(au, av, vg):
    return _gelu(au), _rms(_gelu(av), vg)


def _gm1_ops(a, vg, tm, grad):
    T, W2 = a.shape
    W = W2 // 2
    part = lambda p: Op(a, (tm, W), lambda o, i: (i, p), "tile", grad, gshape=(T, W), gimap=lambda o, i: (i, 0))
    return [part(0), part(1), _vec_op(vg, grad)], (1, T // tm), W


_GM_ROWS = 1024


def _gm_specs(T, W, G):
    rows = _pick(T, _GM_ROWS, 512, 256, CHUNK)
    blk = pl.BlockSpec((rows, LANES), lambda g, c: (c, g))
    wspec = pl.BlockSpec((None, CHUNK, CHUNK), lambda g, c: (g, 0, 0))
    bspec = pl.BlockSpec((None, CHUNK, 1), lambda g, c: (g, 0, 0))
    return rows, blk, wspec, bspec, (G, T // rows)


def gm_mix_fwd(u, vn, ws, bcol, name):
    T, W = u.shape
    rows, blk, wspec, bspec, grid = _gm_specs(T, W, W // LANES)

    def body(u_ref, v_ref, w_ref, b_ref, y_ref):
        r, c = _iotas()
        w = jnp.where(r >= c, w_ref[...], 0.0).astype(_MXU)
        for k in range(rows // CHUNK):
            sl = pl.ds(k * CHUNK, CHUNK)
            y_ref[sl, :] = (u_ref[sl, :] * (_dot(w, v_ref[sl, :]) + b_ref[...])).astype(y_ref.dtype)

    return pl.pallas_call(body, out_shape=jax.ShapeDtypeStruct((T, W), _ACT), grid=grid,
                          in_specs=[blk, blk, wspec, bspec], out_specs=blk,
                          compiler_params=_params(("parallel", "parallel")), name=name)(u, vn, ws, bcol)


def gm_mix_bwd(u, vn, ws, bcol, dy, name):
    T, W = u.shape
    G = W // LANES
    rows, blk, wspec, bspec, grid = _gm_specs(T, W, G)

    def body(u_ref, v_ref, w_ref, b_ref, dy_ref, du_ref, dv_ref, dw_ref, db_ref):
        r, c = _iotas()
        tri = r >= c
        w = jnp.where(tri, w_ref[...], 0.0).astype(_MXU)
        dw = jnp.zeros((CHUNK, CHUNK), F32)
        db = jnp.zeros((CHUNK, 1), F32)
        for k in range(rows // CHUNK):
            sl = pl.ds(k * CHUNK, CHUNK)
            v = v_ref[sl, :]
            g = dy_ref[sl, :]
            du_ref[sl, :] = g * (_dot(w, v) + b_ref[...])
            dm = g * u_ref[sl, :]
            dv_ref[sl, :] = _dot(w, dm, TN)
            dw = dw + _dot(dm, v, NT)
            db = db + jnp.sum(dm, axis=1, keepdims=True)
        dw = jnp.where(tri, dw, 0.0)

        @pl.when(pl.program_id(1) == 0)
        def _():
            dw_ref[...] = dw
            db_ref[...] = db

        @pl.when(pl.program_id(1) != 0)
        def _():
            dw_ref[...] += dw
            db_ref[...] += db

    big = jax.ShapeDtypeStruct((T, W), F32)
    return pl.pallas_call(
        body,
        out_shape=(big, big, jax.ShapeDtypeStruct((G, CHUNK, CHUNK), F32), jax.ShapeDtypeStruct((G, CHUNK, 1), F32)),
        grid=grid, in_specs=[blk, blk, wspec, bspec, blk], out_specs=(blk, blk, wspec, bspec),
        compiler_params=_params(("parallel", "arbitrary")), name=name)(u, vn, ws, bcol, dy)


def gm_fwd(h, g_row, w_in, vg, ws, bcol, w_o, tag, cq=None):
    T = h.shape[0]
    tm = _pick(T, 256, 128)
    hn = rmsnorm_fwd(h, g_row, f"{tag}_norm")
    a = matmul(hn, w_in, "nn", F32, f"{tag}_in", b_cols4=True, cq=cq)
    ins, grid, W = _gm1_ops(a, vg, tm, False)
    rows = lambda dt: ((T, W), dt, (tm, W), lambda o, i: (i, 0))
    u, vn = tmap(_gm1_f, grid, ins, [rows(F32), rows(_ACT)], f"{tag}_act")
    y = gm_mix_fwd(u, vn, ws, bcol, f"{tag}_mix")
    out = matmul(y, w_o, "nn", F32, f"{tag}_wo", cq=cq, epilogue=lambda acc, hh: (acc + hh,), extras=(h,))
    return out, (h, hn, a, u, vn, y)


def gm_bwd(dout, saved, g_row, w_in, vg, ws, bcol, w_o, tag, cq=None):
    h, hn, a, u, vn, y = saved
    T = h.shape[0]
    tm = _pick(T, 256, 128)
    dy = matmul(dout, w_o, "nt", F32, f"{tag}_dy")
    dw_o = matmul(y, dout, "tn", GRAD_DT, f"{tag}_dwo")
    du, dvn, dws, dbcol = gm_mix_bwd(u, vn, ws, bcol, dy, f"{tag}_dmix")
    ins, grid, W = _gm1_ops(a, vg, tm, True)
    dau, dav, dvg = tmap_vjp(_gm1_f, grid, ins, _row_ops([du, dvn], tm), f"{tag}_dact", grad_dtypes={0: _ACT, 1: _ACT})
    da = jnp.concatenate([dau, dav], axis=1)
    dw_in = matmul(hn, da, "tn", GRAD_DT, f"{tag}_dwin", out_cols4=True)
    dhn = matmul(da, w_in, "nt", F32, f"{tag}_dhn", b_cols4=True)
    dh, dg = rmsnorm_bwd(h, g_row, dhn, dout, f"{tag}_dnorm")
    return dh, dg, dw_in, dvg, dws, dbcol, dw_o


@jax.custom_vjp
def _softplus(x):
    return jnp.maximum(x, 0.0) + jnp.log(1.0 + jnp.exp(-jnp.abs(x)))


_softplus.defvjp(lambda x: (_softplus(x), x), lambda x, g: (g * lax.logistic(x),))


def _silu(x):
    return x * lax.logistic(x)


def _shift_impl(x, s, down):
    n = x.shape[0]
    r = lax.broadcasted_iota(jnp.int32, x.shape, 0)
    if down:
        return jnp.where(r >= s, pltpu.roll(x, s, 0), 0.0)
    return jnp.where(r < n - s, pltpu.roll(x, n - s, 0), 0.0)


@functools.partial(jax.custom_vjp, nondiff_argnums=(1,))
def _shift_down(x, s):
    return _shift_impl(x, s, True)


_shift_down.defvjp(lambda x, s: (_shift_impl(x, s, True), None), lambda s, _, g: (_shift_impl(g, s, False),))


def _conv_f(x, w, b):
    k_id = lax.broadcasted_iota(jnp.int32, w.shape, 0)
    y = b + jnp.sum(jnp.where(k_id == SSD_CONV - 1, w, 0.0), axis=0, keepdims=True) * x
    for k in range(SSD_CONV - 1):
        wk = jnp.sum(jnp.where(k_id == k, w, 0.0), axis=0, keepdims=True)
        y = y + wk * _shift_down(x, SSD_CONV - 1 - k)
    return (_silu(y),)


def _conv_ops(zx, conv_w, conv_b, wi, grad):
    T = zx.shape[0]
    cd = conv_w.shape[1]
    cw = LANES
    off = wi // cw
    return [Op(zx, (T, cw), lambda o, j: (0, off + j), "tile", grad, gshape=(T, cd), gimap=lambda o, j: (0, j)),
            Op(conv_w, (SSD_CONV, cw), lambda o, j: (0, j), "tile", grad),
            Op(conv_b, (1, cw), lambda o, j: (0, j), "tile", grad)], (1, cd // cw), (T, cw)


def _dt_f(dtr, bias):
    return (_softplus(dtr + bias),)


def _cumdot_left(tri, x):
    return sum(lax.dot_general(tri, p, NN, preferred_element_type=F32) for p in _split3(x))


@jax.custom_vjp
def _cumsum_rows(x):
    row, col = _iotas()
    return _cumdot_left((row >= col).astype(_MXU), x)


def _cumsum_rows_bwd(_, g):
    row, col = _iotas()
    return (_cumdot_left((row <= col).astype(_MXU), g),)


_cumsum_rows.defvjp(lambda x: (_cumsum_rows(x), None), _cumsum_rows_bwd)


def _ssd_chunk(xps, dt, bm, cm, sps, alog, hid_base):
    row, col = _iotas()
    half = SSD_HEAD_DIM
    rcol = lax.broadcasted_iota(jnp.int32, (CHUNK, 1), 0)
    colpick = lambda m, hid: jnp.sum(jnp.where(col == hid, m, 0.0), axis=1, keepdims=True)
    rowpick = lambda m, hid: jnp.sum(jnp.where(row == hid, m, 0.0), axis=0, keepdims=True)
    last = lambda v: jnp.sum(jnp.where(rcol == CHUNK - 1, v, 0.0), axis=0, keepdims=True)
    acum = _cumsum_rows(dt * (-jnp.exp(alog)))
    acum_t = acum.T
    cb = _dotv(cm, bm, "nt")
    tri = row >= col
    lo = col < half
    ys, snews = [], []
    for p, (xp, sp) in enumerate(zip(xps, sps)):
        h0, h1 = hid_base + 2 * p, hid_base + 2 * p + 1
        ac0, ac1 = colpick(acum, h0), colpick(acum, h1)
        m0 = cb * jnp.exp(jnp.where(tri, ac0 - rowpick(acum_t, h0), -1e30))
        m1 = cb * jnp.exp(jnp.where(tri, ac1 - rowpick(acum_t, h1), -1e30))
        xs = xp * jnp.where(lo, colpick(dt, h0), colpick(dt, h1))
        ydiag = jnp.where(lo, _dotv(m0, xs, "nn"), _dotv(m1, xs, "nn"))
        yoff = jnp.where(lo, jnp.exp(ac0), jnp.exp(ac1)) * _dotv(cm, sp, "nt")
        al0, al1 = last(ac0), last(ac1)
        xsd = xs * jnp.where(lo, jnp.exp(al0 - ac0), jnp.exp(al1 - ac1))
        snew = jnp.where(row < half, jnp.exp(al0), jnp.exp(al1)) * sp + _dotv(xsd, bm, "tn")
        ys.append(ydiag + yoff)
        snews.append(snew)
    return ys, snews


def _ssd_dims(xact, wi):
    T, cd = xact.shape
    G = (cd - wi) // (2 * SSD_STATE)
    hpg = wi // SSD_HEAD_DIM // G
    assert hpg % 2 == 0 and SSD_STATE == LANES
    return T, G, hpg, hpg // 2, T // CHUNK, wi // G


def ssd_scan_fwd(xact, dt, alog, wi, name):
    T, G, hpg, NP, NC, gw = _ssd_dims(xact, wi)
    bo, co = wi // LANES, wi // LANES + G

    def body(x_ref, b_ref, c_ref, dt_ref, al_ref, y_ref, st_ref, s_ref):
        g, c = pl.program_id(0), pl.program_id(1)

        @pl.when(c == 0)
        def _():
            s_ref[...] = jnp.zeros_like(s_ref)

        st_ref[...] = s_ref[...]
        xps = [x_ref[:, p * LANES:(p + 1) * LANES] for p in range(NP)]
        sps = [s_ref[p] for p in range(NP)]
        ys, snews = _ssd_chunk(xps, dt_ref[...], b_ref[...], c_ref[...], sps, al_ref[...], g * hpg)
        for p in range(NP):
            y_ref[:, p * LANES:(p + 1) * LANES] = ys[p]
            s_ref[p] = snews[p]

    return pl.pallas_call(
        body,
        out_shape=(jax.ShapeDtypeStruct((T, wi), F32), jax.ShapeDtypeStruct((G, NC, NP, LANES, SSD_STATE), F32)),
        grid=(G, NC),
        in_specs=[pl.BlockSpec((CHUNK, gw), lambda g, c: (c, g)),
                  pl.BlockSpec((CHUNK, LANES), lambda g, c: (c, bo + g)),
                  pl.BlockSpec((CHUNK, LANES), lambda g, c: (c, co + g)),
                  pl.BlockSpec((CHUNK, LANES), lambda g, c: (c, 0)),
                  pl.BlockSpec((1, LANES), lambda g, c: (0, 0))],
        out_specs=(pl.BlockSpec((CHUNK, gw), lambda g, c: (c, g)),
                   pl.BlockSpec((None, None, NP, LANES, SSD_STATE), lambda g, c: (g, c, 0, 0, 0))),
        scratch_shapes=[pltpu.VMEM((NP, LANES, SSD_STATE), F32)],
        compiler_params=_params(("parallel", "arbitrary")),
        name=name,
    )(xact, xact, xact, dt, alog)


def ssd_scan_bwd(xact, dt, alog, states, dy, wi, name):
    T, G, hpg, NP, NC, gw = _ssd_dims(xact, wi)
    bo, co = wi // LANES, wi // LANES + G
    rev = lambda c: NC - 1 - c

    def body(x_ref, b_ref, c_ref, dt_ref, al_ref, st_ref, dy_ref, dx_ref, db_ref, dc_ref, ddt_ref, dal_ref, ds_ref):
        g, c = pl.program_id(0), pl.program_id(1)

        @pl.when(c == 0)
        def _():
            ds_ref[...] = jnp.zeros_like(ds_ref)

        xps = [x_ref[:, p * LANES:(p + 1) * LANES] for p in range(NP)]
        sps = [st_ref[p] for p in range(NP)]
        f = lambda xps_, dt_, bm_, cm_, sps_, al_: _ssd_chunk(xps_, dt_, bm_, cm_, sps_, al_, g * hpg)
        _, vjp = jax.vjp(f, xps, dt_ref[...], b_ref[...], c_ref[...], sps, al_ref[...])
        dys = [dy_ref[:, p * LANES:(p + 1) * LANES] for p in range(NP)]
        dxps, ddt, dbm, dcm, dsps, dal = vjp((dys, [ds_ref[p] for p in range(NP)]))
        for p in range(NP):
            dx_ref[:, p * LANES:(p + 1) * LANES] = dxps[p]
            ds_ref[p] = dsps[p]
        db_ref[...] = dbm
        dc_ref[...] = dcm
        ddt_ref[...] = ddt

        @pl.when(c == 0)
        def _():
            dal_ref[...] = dal

        @pl.when(c != 0)
        def _():
            dal_ref[...] += dal

    gb = G * SSD_STATE
    return pl.pallas_call(
        body,
        out_shape=(jax.ShapeDtypeStruct((T, wi), F32), jax.ShapeDtypeStruct((T, gb), F32), jax.ShapeDtypeStruct((T, gb), F32),
                   jax.ShapeDtypeStruct((G, T, LANES), F32), jax.ShapeDtypeStruct((G, 1, LANES), F32)),
        grid=(G, NC),
        in_specs=[pl.BlockSpec((CHUNK, gw), lambda g, c: (rev(c), g)),
                  pl.BlockSpec((CHUNK, LANES), lambda g, c: (rev(c), bo + g)),
                  pl.BlockSpec((CHUNK, LANES), lambda g, c: (rev(c), co + g)),
                  pl.BlockSpec((CHUNK, LANES), lambda g, c: (rev(c), 0)),
                  pl.BlockSpec((1, LANES), lambda g, c: (0, 0)),
                  pl.BlockSpec((None, None, NP, LANES, SSD_STATE), lambda g, c: (g, rev(c), 0, 0, 0)),
                  pl.BlockSpec((CHUNK, gw), lambda g, c: (rev(c), g))],
        out_specs=(pl.BlockSpec((CHUNK, gw), lambda g, c: (rev(c), g)),
                   pl.BlockSpec((CHUNK, LANES), lambda g, c: (rev(c), g)),
                   pl.BlockSpec((CHUNK, LANES), lambda g, c: (rev(c), g)),
                   pl.BlockSpec((None, CHUNK, LANES), lambda g, c: (g, rev(c), 0)),
                   pl.BlockSpec((None, 1, LANES), lambda g, c: (g, 0, 0))),
        scratch_shapes=[pltpu.VMEM((NP, LANES, SSD_STATE), F32)],
        compiler_params=_params(("parallel", "arbitrary")),
        name=name,
    )(xact, xact, xact, dt, alog, states, dy)


def _post_f(y, x, z, dcol, ng):
    return (_rms((y + dcol * x) * _silu(z), ng),)


def _post_ops(yssd, xact, zx, dcol, ng, G, tm, grad):
    T, wi = yssd.shape
    gw = wi // G
    blk = lambda a: Op(a, (tm, gw), lambda g, i: (i, g), "tile", grad, gshape=(T, wi))
    vec = lambda v: Op(v, (1, gw), lambda g, i: (0, g), "param", grad)
    return [blk(yssd), blk(xact), blk(zx), vec(dcol), vec(ng)], (G, T // tm), gw


def ssd_fwd(h, g_row, w_zx, w_dt, conv_w, conv_b, dtb, alog, dcol, ng, w_o, tag, cq=None):
    T = h.shape[0]
    wi = ng.shape[1]
    tm = _pick(T, 256, 128)
    hn = rmsnorm_fwd(h, g_row, f"{tag}_norm")
    zx = matmul(hn, w_zx, "nn", F32, f"{tag}_inzx", cq=cq)
    dtr = matmul(hn, w_dt, "nn", F32, f"{tag}_indt")
    ins, grid, blk = _conv_ops(zx, conv_w, conv_b, wi, False)
    cd = conv_w.shape[1]
    xact = tmap(_conv_f, grid, ins, [((T, cd), F32, blk, lambda o, j: (0, j))], f"{tag}_conv")[0]
    dt = tmap(_dt_f, (1, T // tm), _row_ops([dtr], tm) + [_vec_op(dtb)],
              [((T, LANES), F32, (tm, LANES), lambda o, i: (i, 0))], f"{tag}_dt")[0]
    yssd, states = ssd_scan_fwd(xact, dt, alog, wi, f"{tag}_scan")
    G = states.shape[0]
    ins, grid, gw = _post_ops(yssd, xact, zx, dcol, ng, G, tm, False)
    yn = tmap(_post_f, grid, ins, [((T, wi), _ACT, (tm, gw), lambda g, i: (i, g))], f"{tag}_post")[0]
    out = matmul(yn, w_o, "nn", F32, f"{tag}_wo", cq=cq, epilogue=lambda acc, hh: (acc + hh,), extras=(h,))
    return out, (h, hn, zx, dtr, xact, dt, yssd, states, yn)


def ssd_bwd(dout, saved, g_row, w_zx, w_dt, conv_w, conv_b, dtb, alog, dcol, ng, w_o, tag, cq=None):
    h, hn, zx, dtr, xact, dt, yssd, states, yn = saved
    T = h.shape[0]
    wi = ng.shape[1]
    tm = _pick(T, 256, 128)
    G = states.shape[0]
    dyn = matmul(dout, w_o, "nt", F32, f"{tag}_dyn")
    dw_o = matmul(yn, dout, "tn", GRAD_DT, f"{tag}_dwo")
    ins, grid, gw = _post_ops(yssd, xact, zx, dcol, ng, G, tm, True)
    dyssd, dxi_skip, dz, ddcol, dng = tmap_vjp(_post_f, grid, ins, [Op(dyn, (tm, gw), lambda g, i: (i, g))],
                                                f"{tag}_dpost", grad_dtypes={2: _ACT})
    dxi, dbm, dcm, ddt_g, dalog_g = ssd_scan_bwd(xact, dt, alog, states, dyssd, wi, f"{tag}_dscan")
    dxact = jnp.concatenate([dxi + dxi_skip, dbm, dcm], axis=1)
    ddt = jnp.sum(ddt_g, axis=0)
    dalog = jnp.sum(dalog_g, axis=0)
    ins, grid, blk = _conv_ops(zx, conv_w, conv_b, wi, True)
    dxbc, dconv_w, dconv_b = tmap_vjp(_conv_f, grid, ins, [Op(dxact, blk, lambda o, j: (0, j))], f"{tag}_dconv",
                                      grad_dtypes={0: _ACT})
    ddtr, ddtb = tmap_vjp(_dt_f, (1, T // tm), _row_ops([dtr], tm) + [_vec_op(dtb)], _row_ops([ddt], tm), f"{tag}_ddt",
                          grad_dtypes={0: _ACT})
    dzx = jnp.concatenate([dz, dxbc], axis=1)
    dw_zx = matmul(hn, dzx, "tn", GRAD_DT, f"{tag}_dwzx")
    dw_dt = matmul(hn, ddtr, "tn", GRAD_DT, f"{tag}_dwdt")
    dhn1 = matmul(ddtr, w_dt, "nt", F32, f"{tag}_dhn1")
    dhn = matmul(dzx, w_zx, "nt", F32, f"{tag}_dhn", epilogue=lambda acc, e: (acc + e,), extras=(dhn1,))
    dh, dg = rmsnorm_bwd(h, g_row, dhn, dout, f"{tag}_dnorm")
    return dh, dg, dw_zx, dw_dt, dconv_w, dconv_b, ddtb, dalog, ddcol, dng, dw_o


def loss_head(y, target, name):
    T, D = y.shape
    tm = _pick(T, 512, 256, 128)

    def body(y_ref, t_ref, dy_ref, part_ref):
        d = y_ref[...] - t_ref[...]
        dy_ref[...] = d * (1.0 / D)
        s = jnp.sum(d * d, axis=0, keepdims=True) * (0.5 / D)

        @pl.when(pl.program_id(0) == 0)
        def _():
            part_ref[...] = s

        @pl.when(pl.program_id(0) != 0)
        def _():
            part_ref[...] += s

    rows = pl.BlockSpec((tm, D), lambda i: (i, 0))
    return pl.pallas_call(
        body,
        out_shape=(jax.ShapeDtypeStruct((T, D), F32), jax.ShapeDtypeStruct((1, D), F32)),
        grid=(T // tm,),
        in_specs=[rows, rows],
        out_specs=(rows, pl.BlockSpec((1, D), lambda i: (0, 0))),
        compiler_params=_params(("arbitrary",)),
        name=name,
    )(y, target)


def _row_tile(R, C, itemsize=4, target=1 << 20):
    for t in (1024, 512, 256, 128, 64, 32, 16, 8):
        if R % t == 0 and t * C * itemsize <= target:
            return t
    return R


def adamw(w, g, m, v, name):
    R, C = w.shape
    tr = _row_tile(R, C)
    c1 = 1.0 - ADAM_B1 ** ADAM_STEP
    c2 = 1.0 - ADAM_B2 ** ADAM_STEP

    def body(w_ref, g_ref, m_ref, v_ref, d_ref, nm_ref, nv_ref):
        gg = g_ref[...]
        nm = ADAM_B1 * m_ref[...] + (1.0 - ADAM_B1) * gg
        nv = ADAM_B2 * v_ref[...] + (1.0 - ADAM_B2) * jnp.square(gg)
        d_ref[...] = -ADAM_LR * ((nm / c1) / (jnp.sqrt(nv / c2) + ADAM_EPS) + ADAM_WD * w_ref[...])
        nm_ref[...] = nm
        nv_ref[...] = nv

    spec = pl.BlockSpec((tr, C), lambda i: (i, 0))
    sds = jax.ShapeDtypeStruct((R, C), F32)
    return pl.pallas_call(body, out_shape=(sds, sds, sds), grid=(R // tr,), in_specs=[spec] * 4, out_specs=(spec,) * 3,
                          compiler_params=_params(("parallel",)), name=name)(w, g, m, v)


def pair_sum(gfull, recv, name):
    _, _, R, C = gfull.shape
    tr = _row_tile(R, C, 2)

    def body(g_ref, p_ref, o_ref):
        c = lax.axis_index("c")
        o_ref[...] = (g_ref[c].astype(F32) + p_ref[...].astype(F32)).astype(o_ref.dtype)

    return pl.pallas_call(
        body,
        out_shape=jax.ShapeDtypeStruct((4, R, C), gfull.dtype),
        grid=(4, R // tr),
        in_specs=[pl.BlockSpec((None, 2, tr, C), lambda p, i: (p, 0, i, 0)), pl.BlockSpec((None, tr, C), lambda p, i: (p, i, 0))],
        out_specs=pl.BlockSpec((None, tr, C), lambda p, i: (p, i, 0)),
        compiler_params=_params(("parallel", "parallel")),
        name=name,
    )(gfull, recv)


def chip_sum(s, recv, name):
    _, R, C = s.shape
    tr = _row_tile(R, C, 2, 1 << 19)

    def body(c_ref, s_ref, p_ref, o_ref):
        x, y, _ = _xyc()
        acc = s_ref[2 * x + y].astype(F32)
        for r in range(3):
            acc = acc + p_ref[r].astype(F32)
        o_ref[...] = acc

    grid_spec = pltpu.PrefetchScalarGridSpec(
        num_scalar_prefetch=1,
        grid=(R // tr,),
        in_specs=[pl.BlockSpec((4, tr, C), lambda i, c: (0, i, 0)), pl.BlockSpec((3, tr, C), lambda i, c: (0, i, 0))],
        out_specs=pl.BlockSpec((None, tr, C), lambda i, c: (c[0], i, 0)),
    )
    return pl.pallas_call(
        body,
        out_shape=jax.ShapeDtypeStruct((2, R, C), F32),
        grid_spec=grid_spec,
        compiler_params=_params(("arbitrary",)),
        name=name,
    )(lax.axis_index("c").reshape(1).astype(jnp.int32), s, recv)


def sum8(g, name):
    _, R, C = g.shape

    def body(g_ref, o_ref):
        acc = g_ref[0]
        for d in range(1, 8):
            acc = acc + g_ref[d]
        o_ref[...] = acc

    return pl.pallas_call(body, out_shape=jax.ShapeDtypeStruct((R, C), F32), name=name,
                          compiler_params=pltpu.CompilerParams(vmem_limit_bytes=_VMEM_LIMIT))(g)


def gather_small(x, name):
    R, C = x.shape

    def body(x_ref, out_ref, send_sems, recv_sems, local_sem):
        x_, y_, c_ = _xyc()
        me, sibling = (x_, y_, c_), (x_, y_, 1 - c_)
        chips = [(_flip(x_, fx), _flip(y_, fy)) for fx, fy in _REL]
        slot = lambda px, py, pc: out_ref.at[4 * px + 2 * py + pc]

        def copy(k, block, to, src=None):
            return _remote(slot(*block) if src is None else src, slot(*block), send_sems.at[k], recv_sems.at[k], to)

        mine = pltpu.make_async_copy(x_ref, slot(*me), local_sem)
        mine.start()
        first = [copy(0, me, sibling, src=x_ref)]
        first += [copy(1 + j, me, (*chip, c_), src=x_ref) for j, chip in enumerate(chips)]
        for cp in first:
            cp.start()
        passed = [copy(4 + j, (*chip, c_), sibling) for j, chip in enumerate(chips)]
        for j, chip in enumerate(chips):
            copy(1 + j, (*chip, c_), me).wait_recv()
            passed[j].start()
        copy(0, sibling, me).wait_recv()
        for j, chip in enumerate(chips):
            copy(4 + j, (*chip, 1 - c_), me).wait_recv()
        for cp in first + passed:
            cp.wait_send()
        mine.wait()

    return pl.pallas_call(
        body,
        out_shape=jax.ShapeDtypeStruct((8, R, C), x.dtype),
        in_specs=[pl.BlockSpec(memory_space=pltpu.VMEM)],
        out_specs=pl.BlockSpec(memory_space=pltpu.VMEM),
        scratch_shapes=[pltpu.SemaphoreType.DMA((7,)), pltpu.SemaphoreType.DMA((7,)), pltpu.SemaphoreType.DMA],
        compiler_params=pltpu.CompilerParams(vmem_limit_bytes=_VMEM_LIMIT),
        name=name,
    )(x)


def gather_weights(halves, name):
    n = len(halves)

    def body(*refs):
        ins, outs = refs[:n], refs[n:2 * n]
        send_sems, recv_sems = refs[2 * n:]
        x_, y_, c_ = _xyc()
        sibling = (x_, y_, 1 - c_)
        chips = [(_flip(x_, fx), _flip(y_, fy)) for fx, fy in _REL]
        started = []
        for i in range(n):
            own, dst = ins[i].at[c_], outs[i].at[2 * x_ + y_, c_]
            for r, chip in enumerate(chips):
                started.append(_remote(own, dst, send_sems.at[6 * i + r], recv_sems.at[6 * i + r], (*chip, c_)))
                started[-1].start()
        for i in range(n):
            for r, (px, py) in enumerate(chips):
                blk = outs[i].at[2 * px + py, c_]
                _remote(blk, blk, send_sems.at[6 * i + r], recv_sems.at[6 * i + r], sibling).wait_recv()
                started.append(_remote(blk, blk, send_sems.at[6 * i + 3 + r], recv_sems.at[6 * i + 3 + r], sibling))
                started[-1].start()
        for i in range(n):
            for r, (px, py) in enumerate(chips):
                blk = outs[i].at[2 * px + py, 1 - c_]
                _remote(blk, blk, send_sems.at[6 * i + 3 + r], recv_sems.at[6 * i + 3 + r], sibling).wait_recv()
        for cp in started:
            cp.wait_send()

    return pl.pallas_call(
        body,
        out_shape=tuple(jax.ShapeDtypeStruct((4,) + h.shape, h.dtype) for h in halves),
        in_specs=[_ANY] * n,
        out_specs=tuple(_ANY for _ in halves),
        scratch_shapes=[pltpu.SemaphoreType.DMA((6 * n,)), pltpu.SemaphoreType.DMA((6 * n,))],
        name=name,
    )(*halves)


def share_weights(gathered, name):
    n = len(gathered)

    def body(*refs):
        outs = refs[n:2 * n]
        send_sems, recv_sems = refs[2 * n:]
        x_, y_, c_ = _xyc()
        sibling = (x_, y_, 1 - c_)
        chips = [(_flip(x_, fx), _flip(y_, fy)) for fx, fy in _REL]
        sent = []
        for i in range(n):
            for q, (px, py) in enumerate(chips):
                blk = outs[i].at[2 * px + py, c_]
                sent.append(_remote(blk, blk, send_sems.at[3 * i + q], recv_sems.at[3 * i + q], sibling))
                sent[-1].start()
        for i in range(n):
            for q, (px, py) in enumerate(chips):
                blk = outs[i].at[2 * px + py, 1 - c_]
                _remote(blk, blk, send_sems.at[3 * i + q], recv_sems.at[3 * i + q], sibling).wait_recv()
        for cp in sent:
            cp.wait_send()

    return pl.pallas_call(
        body,
        out_shape=tuple(jax.ShapeDtypeStruct(g.shape, g.dtype) for g in gathered),
        in_specs=[_ANY] * n,
        out_specs=tuple(_ANY for _ in gathered),
        input_output_aliases={i: i for i in range(n)},
        scratch_shapes=[pltpu.SemaphoreType.DMA((3 * n,)), pltpu.SemaphoreType.DMA((3 * n,))],
        name=name,
    )(*gathered)


def swap_halves(gfulls, name):
    n = len(gfulls)

    def body(*refs):
        ins, outs = refs[:n], refs[n:2 * n]
        send_sems, recv_sems = refs[2 * n:]
        x_, y_, c_ = _xyc()
        sibling = (x_, y_, 1 - c_)
        started = []
        for i in range(n):
            for p in range(4):
                started.append(_remote(ins[i].at[p, 1 - c_], outs[i].at[p], send_sems.at[4 * i + p], recv_sems.at[4 * i + p], sibling))
                started[-1].start()
        for cp in started:
            cp.wait()

    return pl.pallas_call(
        body,
        out_shape=tuple(jax.ShapeDtypeStruct((4,) + g.shape[2:], g.dtype) for g in gfulls),
        in_specs=[_ANY] * n,
        out_specs=tuple(_ANY for _ in gfulls),
        scratch_shapes=[pltpu.SemaphoreType.DMA((4 * n,)), pltpu.SemaphoreType.DMA((4 * n,))],
        name=name,
    )(*gfulls)


def scatter_chips(sums, name):
    n = len(sums)

    def body(*refs):
        ins, outs = refs[:n], refs[n:2 * n]
        send_sems, recv_sems = refs[2 * n:]
        x_, y_, c_ = _xyc()
        chips = [(_flip(x_, fx), _flip(y_, fy)) for fx, fy in _REL]
        started = []
        for i in range(n):
            for r, (px, py) in enumerate(chips):
                started.append(_remote(ins[i].at[2 * px + py], outs[i].at[r], send_sems.at[3 * i + r], recv_sems.at[3 * i + r], (px, py, c_)))
                started[-1].start()
        for cp in started:
            cp.wait()

    return pl.pallas_call(
        body,
        out_shape=tuple(jax.ShapeDtypeStruct((3,) + s.shape[1:], s.dtype) for s in sums),
        in_specs=[_ANY] * n,
        out_specs=tuple(_ANY for _ in sums),
        scratch_shapes=[pltpu.SemaphoreType.DMA((3 * n,)), pltpu.SemaphoreType.DMA((3 * n,))],
        name=name,
    )(*sums)


_JOIN_CHUNK_BYTES = 4 << 20


def join_halves(bufs, name):
    n = len(bufs)
    chunks = []
    for b in bufs:
        _, R, C = b.shape
        k = 1
        while k < 8 and R % (2 * k * 8) == 0 and R * C * 4 // k > _JOIN_CHUNK_BYTES:
            k *= 2
        chunks.append(k)
    base = [sum(chunks[:i]) for i in range(n)]
    total = sum(chunks)

    def body(*refs):
        outs = refs[n:2 * n]
        send_sems, recv_sems = refs[2 * n:]
        x_, y_, c_ = _xyc()
        sibling = (x_, y_, 1 - c_)
        sent = []
        for i in range(n):
            rc = bufs[i].shape[1] // chunks[i]
            for q in range(chunks[i]):
                blk = outs[i].at[c_, pl.ds(q * rc, rc)]
                sent.append(_remote(blk, blk, send_sems.at[base[i] + q], recv_sems.at[base[i] + q], sibling))
                sent[-1].start()
        for i in range(n):
            rc = bufs[i].shape[1] // chunks[i]
            for q in range(chunks[i]):
                blk = outs[i].at[1 - c_, pl.ds(q * rc, rc)]
                _remote(blk, blk, send_sems.at[base[i] + q], recv_sems.at[base[i] + q], sibling).wait_recv()
        for cp in sent:
            cp.wait_send()

    return pl.pallas_call(
        body,
        out_shape=tuple(jax.ShapeDtypeStruct(b.shape, b.dtype) for b in bufs),
        in_specs=[_ANY] * n,
        out_specs=tuple(_ANY for _ in bufs),
        input_output_aliases={i: i for i in range(n)},
        scratch_shapes=[pltpu.SemaphoreType.DMA((total,)), pltpu.SemaphoreType.DMA((total,))],
        name=name,
    )(*bufs)


def _halves(a):
    return a.reshape(2, -1, a.shape[-1])


_BIG = ("sb_w_qkv", "sb_w_o", "gm_w_in", "gm_w_o", "ssd_w_in", "ssd_w_o", "mlp_w_in", "mlp_w_out")
_COLS = ("sb_w_qkv", "gm_w_in", "ssd_w_in", "mlp_w_in")
_SMALL = ("norm_mix_g", "norm_mlp_g", "sb_q_norm_g", "sb_k_norm_g", "gm_v_norm_g", "gm_w_s", "gm_b_s",
          "ssd_conv_w", "ssd_conv_b", "ssd_dt_bias", "ssd_a_log", "ssd_d", "ssd_norm_g")
_SMALL_SHARDED = ("ssd_conv_w", "ssd_conv_b", "ssd_norm_g")
_WEIGHTS = ("norm_mix_g", "norm_mlp_g", "sb_w_qkv", "sb_q_norm_g", "sb_k_norm_g", "sb_w_o", "gm_w_in", "gm_v_norm_g",
            "gm_w_s", "gm_b_s", "gm_w_o", "ssd_w_in", "ssd_conv_w", "ssd_conv_b", "ssd_dt_bias", "ssd_a_log", "ssd_d",
            "ssd_norm_g", "ssd_w_o", "mlp_w_in", "mlp_w_out")


def _pack(arrs):
    flat = jnp.concatenate([a.reshape(-1).astype(F32) for a in arrs])
    n = flat.shape[0]
    tot = -(-n // (8 * LANES)) * 8 * LANES
    return jnp.pad(flat, (0, tot - n)).reshape(-1, LANES)


def _unpack(buf, shapes):
    flat = buf.reshape(-1)
    out, o = [], 0
    for s in shapes:
        n = math.prod(s)
        out.append(flat[o:o + n].reshape(s))
        o += n
    return out


_CARRIER_RANK = {0: (3, 0, 2, 1), 1: (3, 2, 0, 1), 2: (3, 0, 2, 1)}


def _layer_arrays(i):
    kind, j = i % 3, i // 3
    mix = (("sb_w_qkv", "sb_w_o"), ("gm_w_in", "gm_w_o"), ("ssd_w_in", "ssd_w_o"))[kind]
    return [(mix[0], j), (mix[1], j), ("mlp_w_in", i), ("mlp_w_out", i)]


def _step(x, w, target, m, v):
    depth = w["norm_mix_g"].shape[0]
    xc, yc, _ = _xyc()
    chip = 2 * xc + yc
    Hs = w["ssd_dt_bias"].shape[1]
    wi = 4 * w["ssd_norm_g"].shape[1]
    cd = 4 * w["ssd_conv_b"].shape[1]

    halves = {(k, l): _halves(w[k][l].astype(_MXU)) for i in range(depth) for (k, l) in _layer_arrays(i)}
    size = lambda key: math.prod(halves[key].shape)

    def finish(key, g):
        k, l = key
        g = g.reshape((4,) + w[k].shape[1:])
        g = lax.dynamic_update_slice(g, w[k][l].astype(_MXU)[None], (chip, 0, 0))
        if k == "ssd_w_in":
            full = g.transpose(1, 0, 2).reshape(g.shape[1], -1)
            return full[:, :wi + cd], jnp.pad(full[:, wi + cd:], ((0, 0), (0, LANES - Hs)))
        return g if k in _COLS else g.reshape(-1, g.shape[-1])

    small_sh = gather_small(_pack([w[k] for k in _SMALL_SHARDED]), "gather_small_weights")
    parts = [_unpack(small_sh[2 * j], [w[k].shape for k in _SMALL_SHARDED]) for j in range(4)]
    conv_w = jnp.concatenate([p[0][0] for p in parts], axis=1)
    conv_b = jnp.concatenate([p[1] for p in parts], axis=1)
    ssd_ng = jnp.concatenate([p[2] for p in parts], axis=1)
    padh = lambda a: jnp.pad(a, ((0, 0), (0, LANES - Hs)))
    dtb, alog = padh(w["ssd_dt_bias"]), padh(w["ssd_a_log"])
    dcol = jnp.repeat(w["ssd_d"], SSD_HEAD_DIM, axis=1)
    bcol = w["gm_b_s"][0][:, :, None]

    keys0 = _layer_arrays(0) if depth == 1 else _layer_arrays(0)[:2]
    ready = dict(zip(keys0, gather_weights([halves[k] for k in keys0], "gather_weights_0")))
    gather = lambda key: ("gather", halves[key], key)

    h = x[0]
    tape = []
    for i in range(depth):
        kind, j = i % 3, i // 3
        keys = _layer_arrays(i)
        wl = [finish(k, ready[k]) for k in keys[:2]]
        cq = Carriers()
        if i == 0 and depth > 1:
            n1 = _layer_arrays(1)
            for group in ([keys[2]], [keys[3]], [n1[2], n1[1]], [n1[3], n1[0]]):
                cq.add_group([gather(key) for key in group])
        elif i + 1 < depth:
            nxt = sorted(_layer_arrays(i + 1), key=size, reverse=True)
            order = [None] * 4
            for rank, key in zip(_CARRIER_RANK[kind], nxt):
                order[rank] = key
            for key in order:
                cq.add("gather", halves[key], key)
        gmix = w["norm_mix_g"][i:i + 1]
        if kind == 0:
            args = (gmix, wl[0], w["sb_q_norm_g"][j:j + 1], w["sb_k_norm_g"][j:j + 1], wl[1], f"sb{j}")
            h, sv = sb_fwd(h, *args, cq=cq)
        elif kind == 1:
            args = (gmix, wl[0], w["gm_v_norm_g"][j:j + 1], w["gm_w_s"][j], bcol, wl[1], f"gm{j}")
            h, sv = gm_fwd(h, *args, cq=cq)
        else:
            args = (gmix, wl[0][0], wl[0][1], conv_w, conv_b, dtb, alog, dcol, ssd_ng, wl[1], f"ssd{j}")
            h, sv = ssd_fwd(h, *args, cq=cq)
        if i == 0 and depth > 1:
            ready.update(zip(keys[2:], share_weights([cq.done[k] for k in keys[2:]], "share_weights_0")))
        wl += [finish(k, ready[k]) for k in keys[2:]]
        margs = (w["norm_mlp_g"][i:i + 1], wl[2], wl[3], f"mlp{i}")
        h, msv = mlp_fwd(h, *margs, cq=cq)
        tape.append((kind, j, args, sv, margs, msv))
        if i + 1 < depth:
            assert not cq.pending
            nk_ = _layer_arrays(i + 1)
            shared = share_weights([cq.done[k] for k in nk_], f"share_weights_{i + 1}")
            ready.update(zip(nk_, shared))

    dh, loss_cols = loss_head(h, target[0], "loss_head")

    gsmall = {k: [None] * w[k].shape[0] for k in ("norm_mix_g", "norm_mlp_g", "sb_q_norm_g", "sb_k_norm_g")}
    pairs, from_chips = {}, {}

    def reduce_pairs(keys_, grads_, tag_):
        gl = [g.reshape(4, 2, -1, g.shape[-1]).astype(GRAD_DT) for g in grads_]
        got = swap_halves(gl, f"grad_swap_halves_{tag_}")
        for key, g, r in zip(keys_, gl, got):
            pairs[key] = pair_sum(g, r, f"pair_sum_{key[0]}_{key[1]}")

    cq = Carriers()
    for i in reversed(range(depth)):
        kind, j, args, sv, margs, msv = tape[i]
        keys = _layer_arrays(i)
        dh, gsmall["norm_mlp_g"][i], d_mlp_in, d_mlp_out = mlp_bwd(dh, msv, *margs, cq=cq)
        assert not cq.pending
        from_chips.update(cq.done)
        mix_cq = Carriers()
        if i == 0:
            reduce_pairs(keys[2:], (d_mlp_in, d_mlp_out), "0_mlp")
            for key in keys[2:]:
                mix_cq.add("scatter", pairs[key], key)
        if kind == 0:
            dh, gsmall["norm_mix_g"][i], d_in, gsmall["sb_q_norm_g"][j], gsmall["sb_k_norm_g"][j], d_out = sb_bwd(dh, sv, *args, cq=mix_cq)
        elif kind == 1:
            dh, gsmall["norm_mix_g"][i], d_in, d_vg, d_ws, d_bcol, d_out = gm_bwd(dh, sv, *args)
            gsmall["gm_v_norm_g"], gsmall["gm_w_s"], gsmall["gm_b_s"] = d_vg, d_ws[None], d_bcol[None, :, :, 0]
        else:
            dh, gsmall["norm_mix_g"][i], d_zx, d_dt, d_cw, d_cb, d_dtb, d_al, d_dcol, d_ng, d_out = ssd_bwd(dh, sv, *args)
            d_full = jnp.concatenate([d_zx, d_dt[:, :Hs]], axis=1)
            d_in = d_full.reshape(d_full.shape[0], 4, -1).transpose(1, 0, 2)
            gsmall["ssd_conv_w"], gsmall["ssd_conv_b"], gsmall["ssd_norm_g"] = d_cw[None], d_cb, d_ng
            gsmall["ssd_dt_bias"], gsmall["ssd_a_log"] = d_dtb[:, :Hs], d_al[:, :Hs]
            gsmall["ssd_d"] = jnp.sum(d_dcol.reshape(Hs, SSD_HEAD_DIM), axis=1)[None]
        assert not mix_cq.pending
        from_chips.update(mix_cq.done)
        cq = Carriers()
        if i > 0:
            reduce_pairs(keys, (d_in, d_out, d_mlp_in, d_mlp_out), str(i))
            for key in sorted(keys, key=size, reverse=True):
                cq.add("scatter", pairs[key], key)
        else:
            reduce_pairs(keys[:2], (d_in, d_out), "0_mix")
            from_chips.update(zip(keys[:2], scatter_chips([pairs[k] for k in keys[:2]], "grad_scatter_chips_0")))
    for k in gsmall:
        if isinstance(gsmall[k], list):
            gsmall[k] = jnp.concatenate(gsmall[k], axis=0)
    grad_x = dh[None]

    allkeys = [key for i in range(depth) for key in _layer_arrays(i)]
    mine = [chip_sum(pairs[key], from_chips[key], f"chip_sum_{key[0]}_{key[1]}") for key in allkeys]
    joined = dict(zip(allkeys, join_halves(mine, "grad_join_halves")))
    grads, deltas, new_m, new_v = {}, {}, {}, {}
    for k in _BIG:
        g = jnp.stack([joined[(k, l)].reshape(w[k].shape[1:]) for l in range(w[k].shape[0])])
        C = w[k].shape[-1]
        d_, m_, v_ = adamw(w[k].reshape(-1, C), g.reshape(-1, C), m[k].reshape(-1, C), v[k].reshape(-1, C), f"adamw_{k}")
        grads[k], deltas[k], new_m[k], new_v[k] = (a.reshape(w[k].shape) for a in (g, d_, m_, v_))

    full_shapes = [gsmall[k].shape for k in _SMALL] + [(1,)]
    loss_local = jnp.sum(loss_cols).reshape(1)
    red = sum8(gather_small(_pack([gsmall[k] for k in _SMALL] + [loss_local]), "gather_small_grads"), "sum_small_grads")
    red = _unpack(red, full_shapes)
    loss = red[-1][0]
    gsm = dict(zip(_SMALL, red[:-1]))
    for k in _SMALL_SHARDED:
        n = w[k].shape[-1]
        gsm[k] = lax.dynamic_slice_in_dim(gsm[k], chip * n, n, axis=gsm[k].ndim - 1)
    shapes = [w[k].shape for k in _SMALL]
    packed = [_pack([d[k] for k in _SMALL]) for d in (w, gsm, m, v)]
    outs = adamw(*packed, "adamw_small")
    for k, g_, d_, m_, v_ in zip(_SMALL, [gsm[k] for k in _SMALL], *[_unpack(o, shapes) for o in outs]):
        grads[k], deltas[k], new_m[k], new_v[k] = g_.reshape(w[k].shape), d_, m_, v_

    return (loss, grad_x, *[grads[k] for k in _WEIGHTS], *[deltas[k] for k in _WEIGHTS],
            *[new_m[k] for k in _WEIGHTS], *[new_v[k] for k in _WEIGHTS])


def kernel(x, norm_mix_g, norm_mlp_g, sb_w_qkv, sb_q_norm_g, sb_k_norm_g, sb_w_o, gm_w_in, gm_v_norm_g, gm_w_s, gm_b_s, gm_w_o, ssd_w_in, ssd_conv_w, ssd_conv_b, ssd_dt_bias, ssd_a_log, ssd_d, ssd_norm_g, ssd_w_o, mlp_w_in, mlp_w_out, loss_target, m_norm_mix_g, m_norm_mlp_g, m_sb_w_qkv, m_sb_q_norm_g, m_sb_k_norm_g, m_sb_w_o, m_gm_w_in, m_gm_v_norm_g, m_gm_w_s, m_gm_b_s, m_gm_w_o, m_ssd_w_in, m_ssd_conv_w, m_ssd_conv_b, m_ssd_dt_bias, m_ssd_a_log, m_ssd_d, m_ssd_norm_g, m_ssd_w_o, m_mlp_w_in, m_mlp_w_out, v_norm_mix_g, v_norm_mlp_g, v_sb_w_qkv, v_sb_q_norm_g, v_sb_k_norm_g, v_sb_w_o, v_gm_w_in, v_gm_v_norm_g, v_gm_w_s, v_gm_b_s, v_gm_w_o, v_ssd_w_in, v_ssd_conv_w, v_ssd_conv_b, v_ssd_dt_bias, v_ssd_a_log, v_ssd_d, v_ssd_norm_g, v_ssd_w_o, v_mlp_w_in, v_mlp_w_out):
    w = dict(zip(_WEIGHTS, (norm_mix_g, norm_mlp_g, sb_w_qkv, sb_q_norm_g, sb_k_norm_g, sb_w_o, gm_w_in, gm_v_norm_g, gm_w_s,
                            gm_b_s, gm_w_o, ssd_w_in, ssd_conv_w, ssd_conv_b, ssd_dt_bias, ssd_a_log, ssd_d, ssd_norm_g,
                            ssd_w_o, mlp_w_in, mlp_w_out)))
    m = dict(zip(_WEIGHTS, (m_norm_mix_g, m_norm_mlp_g, m_sb_w_qkv, m_sb_q_norm_g, m_sb_k_norm_g, m_sb_w_o, m_gm_w_in,
                            m_gm_v_norm_g, m_gm_w_s, m_gm_b_s, m_gm_w_o, m_ssd_w_in, m_ssd_conv_w, m_ssd_conv_b,
                            m_ssd_dt_bias, m_ssd_a_log, m_ssd_d, m_ssd_norm_g, m_ssd_w_o, m_mlp_w_in, m_mlp_w_out)))
    v = dict(zip(_WEIGHTS, (v_norm_mix_g, v_norm_mlp_g, v_sb_w_qkv, v_sb_q_norm_g, v_sb_k_norm_g, v_sb_w_o, v_gm_w_in,
                            v_gm_v_norm_g, v_gm_w_s, v_gm_b_s, v_gm_w_o, v_ssd_w_in, v_ssd_conv_w, v_ssd_conv_b,
                            v_ssd_dt_bias, v_ssd_a_log, v_ssd_d, v_ssd_norm_g, v_ssd_w_o, v_mlp_w_in, v_mlp_w_out)))
    return _step(x, w, loss_target, m, v)
```
